```python
import math
import jax
import jax.numpy as jnp
from jax import lax
import numpy as np

D_MODEL = 1024
BATCH = 4
SEQ = 4096
DEPTH = 2

GRID_W = 64
CTX_LEN = 256
EPS = 1e-6
F32 = jnp.float32

POOL_WINDOWS = (2, 4, 8, 16)
D_POOL = D_MODEL
POOL_GROUP = D_POOL // len(POOL_WINDOWS)

D_MLSTM = D_MODEL
MLSTM_HEADS = 4
MLSTM_HEAD_DIM = D_MLSTM // MLSTM_HEADS
MLSTM_CHUNK = 64
QK_CONV = 4

D_SSD = D_MODEL
SSD_HEAD_DIM = 64
SSD_HEADS = D_SSD // SSD_HEAD_DIM
SSD_GROUPS = 4
SSD_STATE = 128
SSD_CHUNK = 64
SSD_CONV = 4
D_XBC = D_SSD + 2 * SSD_GROUPS * SSD_STATE

N_BRANCH = 3
D_BRANCH = D_MODEL

N_EXPERT_GROUPS = 4
EXPERTS_PER_GROUP = 8
N_EXPERTS = N_EXPERT_GROUPS * EXPERTS_PER_GROUP
TOP_K_IN_GROUP = 2
D_EXPERT = 512
MOE_BLOCK = 256

IN_SPLITS = (D_POOL,
             D_MLSTM, D_MLSTM, D_MLSTM, D_MLSTM,
             4 * MLSTM_HEADS,
             D_SSD, D_SSD,
             SSD_GROUPS * SSD_STATE, SSD_GROUPS * SSD_STATE,
             2 * SSD_HEADS,
             N_BRANCH * D_MODEL)
D_IN = sum(IN_SPLITS)

kernel_name = 'hybrid_pool_mlstm_ssd_hmoe_diffusion'


def rmsnorm(x, g):
    xf = x.astype(F32)
    y = xf * lax.rsqrt(jnp.mean(xf * xf, axis=-1, keepdims=True) + EPS)
    return (y * g.astype(F32)).astype(x.dtype)


def modulate(h, shift, scale):
    return h * (1 + scale) + shift


def split_cols(u, sizes):
    out, start = [], 0
    for s in sizes:
        out.append(u[..., start:start + s])
        start += s
    return out


def flip(t):
    return jnp.flip(t, axis=1)


def to_colmajor(t):
    b, l = t.shape[:2]
    rows = l // GRID_W
    return t.reshape(b, rows, GRID_W, *t.shape[2:]).swapaxes(1, 2).reshape(t.shape)


def to_rowmajor(t):
    b, l = t.shape[:2]
    rows = l // GRID_W
    return t.reshape(b, GRID_W, rows, *t.shape[2:]).swapaxes(1, 2).reshape(t.shape)


def short_conv(x, w, b):
    k = w.shape[0]
    l = x.shape[1]
    xp = jnp.pad(x, ((0, 0), (k // 2, k - 1 - k // 2), (0, 0)))
    y = b + w[0] * xp[:, 0:l]
    for j in range(1, k):
        y = y + w[j] * xp[:, j:j + l]
    return y


def window_mean_minus_self(x, w):
    l = x.shape[-2]
    t = np.arange(l)
    lo = np.clip(t - w // 2, 0, l - 1)
    hi = np.clip(t - w // 2 + w - 1, 0, l - 1)
    cnt = jnp.asarray((hi - lo + 1).astype(np.float32))[:, None]
    xf = x.astype(F32)
    cs = jnp.cumsum(xf, axis=-2)
    cs = jnp.concatenate([jnp.zeros_like(cs[..., :1, :]), cs], axis=-2)
    s = jnp.take(cs, hi + 1, axis=-2) - jnp.take(cs, lo, axis=-2)
    return s / cnt - xf


def pool_mixer(a, pool_w, pool_scale, grid):
    b, l, cw = a.shape
    if grid:
        a = a.reshape(b, l // GRID_W, GRID_W, cw)
    groups = jnp.split(a, len(POOL_WINDOWS), axis=-1)
    pooled = jnp.stack([window_mean_minus_self(g, w) for g, w in zip(groups, POOL_WINDOWS)], axis=-2)
    y = jnp.einsum('...gc,gcd->...gd', pooled, pool_w.astype(F32)).reshape(b, l, cw)
    return (y * pool_scale.astype(F32)).astype(a.dtype)


def mlstm_inputs(q, k, v, gates, conv_w, conv_b, gate_b):
    b, l, _ = q.shape
    qk = jax.nn.silu(short_conv(jnp.concatenate([q, k], axis=-1), conv_w, conv_b))
    q, k = jnp.split(qk, 2, axis=-1)
    heads = lambda t: t.astype(F32).reshape(b, l, MLSTM_HEADS, MLSTM_HEAD_DIM)
    q, k, v = heads(q), heads(k) * (MLSTM_HEAD_DIM ** -0.5), heads(v)
    g = gates.astype(F32).reshape(b, l, 4, MLSTM_HEADS) + gate_b.astype(F32)
    fwd = (g[:, :, 0], jax.nn.log_sigmoid(g[:, :, 2]))
    bwd = (g[:, :, 1], jax.nn.log_sigmoid(g[:, :, 3]))
    return q, k, v, fwd, bwd


def mlstm_scan(q, k, v, ig, lf, state):
    b, l, h, dh = q.shape
    nc = l // MLSTM_CHUNK
    causal = jnp.tril(jnp.ones((MLSTM_CHUNK, MLSTM_CHUNK), bool))

    def chunks(t):
        return jnp.moveaxis(t.reshape(b, nc, MLSTM_CHUNK, *t.shape[2:]), 1, 0)

    def step(carry, inp):
        c_st, n_st, m_st = carry
        qc, kc, vc, ic, fc = inp
        cum = jnp.cumsum(fc, axis=1)
        dmat = cum[:, :, None, :] - cum[:, None, :, :] + ic[:, None, :, :]
        dmat = jnp.where(causal[None, :, :, None], dmat, -jnp.inf)
        g = cum + m_st[:, None, :]
        m_t = jnp.maximum(g, dmat.max(axis=2))
        w = jnp.exp(dmat - m_t[:, :, None, :])
        sg = jnp.exp(g - m_t)
        a = w * jnp.einsum('bthd,bshd->btsh', qc, kc)
        num = jnp.einsum('btsh,bshe->bthe', a, vc) + sg[..., None] * jnp.einsum('bthd,bhde->bthe', qc, c_st)
        den = a.sum(axis=2) + sg * jnp.einsum('bthd,bhd->bth', qc, n_st)
        h_out = num / jnp.maximum(jnp.abs(den), jnp.exp(-m_t))[..., None]
        last = cum[:, -1]
        wl_log = last[:, None] - cum + ic
        m_new = jnp.maximum(last + m_st, wl_log.max(axis=1))
        wl = jnp.exp(wl_log - m_new[:, None])
        decay = jnp.exp(last + m_st - m_new)
        c_st = decay[..., None, None] * c_st + jnp.einsum('bsh,bshd,bshe->bhde', wl, kc, vc)
        n_st = decay[..., None] * n_st + jnp.einsum('bsh,bshd->bhd', wl, kc)
        return (c_st, n_st, m_new), h_out

    state, hs = lax.scan(step, state, (chunks(q), chunks(k), chunks(v), chunks(ig), chunks(lf)))
    return jnp.moveaxis(hs, 0, 1).reshape(b, l, h, dh), state


def mlstm_out(h, o, norm_g):
    b, l = o.shape[:2]
    hn = h * lax.rsqrt(jnp.mean(h * h, axis=-1, keepdims=True) + EPS)
    hn = hn.reshape(b, l, D_MLSTM) * norm_g.astype(F32)
    return (jax.nn.sigmoid(o.astype(F32)) * hn).astype(o.dtype)


def ssd_inputs(xs, bs, cs, dt_raw, conv_w, conv_b, dt_bias):
    b, l, _ = xs.shape
    xbc = jax.nn.silu(short_conv(jnp.concatenate([xs, bs, cs], axis=-1), conv_w, conv_b))
    xs, bs, cs = split_cols(xbc, (D_SSD, SSD_GROUPS * SSD_STATE, SSD_GROUPS * SSD_STATE))
    xh = xs.astype(F32).reshape(b, l, SSD_HEADS, SSD_HEAD_DIM)
    bg = bs.astype(F32).reshape(b, l, SSD_GROUPS, SSD_STATE)
    cg = cs.astype(F32).reshape(b, l, SSD_GROUPS, SSD_STATE)
    dt = jax.nn.softplus(dt_raw.astype(F32).reshape(b, l, 2, SSD_HEADS) + dt_bias.astype(F32))
    return xh, bg, cg, dt[:, :, 0], dt[:, :, 1]


def ssd_scan(x, dt, a_neg, bm, cm, s0):
    b, l, h, p = x.shape
    g, n = bm.shape[2:]
    hg = h // g
    q = SSD_CHUNK
    nc = l // q
    xc = x.reshape(b, nc, q, g, hg, p)
    dtc = dt.reshape(b, nc, q, g, hg)
    bc = bm.reshape(b, nc, q, g, n)
    cc = cm.reshape(b, nc, q, g, n)
    acs = jnp.cumsum(dtc * a_neg.reshape(g, hg), axis=2)
    causal = jnp.tril(jnp.ones((q, q), bool))
    seg = acs[:, :, :, None] - acs[:, :, None, :]
    lmat = jnp.exp(jnp.where(causal[:, :, None, None], seg, -jnp.inf))
    cb = jnp.einsum('bctgn,bcsgn->bctsg', cc, bc)
    xdt = xc * dtc[..., None]
    y_intra = jnp.einsum('bctsg,bctsgh,bcsghp->bctghp', cb, lmat, xdt)
    decay_to_end = jnp.exp(acs[:, :, -1:] - acs)
    chunk_states = jnp.einsum('bcsgn,bcsgh,bcsghp->bcghpn', bc, decay_to_end, xdt)
    chunk_decay = jnp.exp(acs[:, :, -1])

    def step(s, inp):
        st, dec = inp
        return dec[..., None, None] * s + st, s

    s_final, s_starts = lax.scan(step, s0.reshape(b, g, hg, p, n),
                                 (jnp.moveaxis(chunk_states, 1, 0), jnp.moveaxis(chunk_decay, 1, 0)))
    s_starts = jnp.moveaxis(s_starts, 0, 1)
    y_inter = jnp.einsum('bctgn,bcghpn,bctgh->bctghp', cc, s_starts, jnp.exp(acs))
    return (y_intra + y_inter).reshape(b, l, h, p), s_final.reshape(b, h, p, n)


def ssd_out(y, xh, d_skip, z, norm_g):
    b, l = z.shape[:2]
    y = (y + d_skip[:, None] * xh).reshape(b, l, D_SSD)
    return rmsnorm(y * jax.nn.silu(z.astype(F32)), norm_g).astype(z.dtype)


def merge_branches(branches, gate_pre, w_branch, w_out):
    br = jnp.stack(branches, axis=-2)
    proj = jnp.einsum('blid,ide->blie', br, w_branch)
    gate = jax.nn.sigmoid(gate_pre.astype(F32)).reshape(proj.shape).astype(proj.dtype)
    return (gate * proj).sum(axis=-2) @ w_out


def token_mixer(hc, hl, keep_ctx, w_in, pool_w, pool_scale, mlstm_conv_w, mlstm_conv_b, mlstm_gate_b,
                mlstm_norm_g, ssd_conv_w, ssd_conv_b, ssd_dt_bias, ssd_a_log, ssd_d, ssd_norm_g,
                w_branch, w_out):
    b = hl.shape[0]
    uc = hc @ w_in
    ul = hl @ w_in
    pa_c, q_c, k_c, v_c, o_c, ga_c, sx_c, sz_c, sb_c, sc_c, sdt_c, mg_c = split_cols(uc, IN_SPLITS)
    pa_l, q_l, k_l, v_l, o_l, ga_l, sx_l, sz_l, sb_l, sc_l, sdt_l, mg_l = split_cols(ul, IN_SPLITS)

    pool_l = pool_mixer(pa_l, pool_w, pool_scale, True)

    qc, kc, vc, gfc, gbc = mlstm_inputs(q_c, k_c, v_c, ga_c, mlstm_conv_w, mlstm_conv_b, mlstm_gate_b)
    ql, kl, vl, gfl, gbl = mlstm_inputs(to_colmajor(q_l), to_colmajor(k_l), to_colmajor(v_l),
                                        to_colmajor(ga_l), mlstm_conv_w, mlstm_conv_b, mlstm_gate_b)
    zero_m = (jnp.zeros((b, MLSTM_HEADS, MLSTM_HEAD_DIM, MLSTM_HEAD_DIM), F32),
              jnp.zeros((b, MLSTM_HEADS, MLSTM_HEAD_DIM), F32),
              jnp.zeros((b, MLSTM_HEADS), F32))
    h_cf, st_f = mlstm_scan(qc, kc, vc, gfc[0], gfc[1], zero_m)
    h_cb, st_b = mlstm_scan(flip(qc), flip(kc), flip(vc), flip(gbc[0]), flip(gbc[1]), zero_m)
    h_lf, _ = mlstm_scan(ql, kl, vl, gfl[0], gfl[1], st_f)
    h_lb, _ = mlstm_scan(flip(ql), flip(kl), flip(vl), flip(gbl[0]), flip(gbl[1]), st_b)
    mlstm_l = mlstm_out(to_rowmajor(h_lf + flip(h_lb)), o_l, mlstm_norm_g)

    a_neg = -jnp.exp(ssd_a_log.astype(F32))
    d_skip = ssd_d.astype(F32)
    xc_, bc_, cc_, dtf_c, dtb_c = ssd_inputs(sx_c, sb_c, sc_c, sdt_c, ssd_conv_w, ssd_conv_b, ssd_dt_bias)
    xl_, bl_, cl_, dtf_l, dtb_l = ssd_inputs(sx_l, sb_l, sc_l, sdt_l, ssd_conv_w, ssd_conv_b, ssd_dt_bias)
    zero_s = jnp.zeros((b, SSD_HEADS, SSD_HEAD_DIM, SSD_STATE), F32)
    y_cf, s_f = ssd_scan(xc_, dtf_c, a_neg[0], bc_, cc_, zero_s)
    y_cb, s_b = ssd_scan(flip(xc_), flip(dtb_c), a_neg[1], flip(bc_), flip(cc_), zero_s)
    y_lf, _ = ssd_scan(xl_, dtf_l, a_neg[0], bl_, cl_, s_f)
    y_lb, _ = ssd_scan(flip(xl_), flip(dtb_l), a_neg[1], flip(bl_), flip(cl_), s_b)
    ssd_l = ssd_out(y_lf + flip(y_lb), xl_, d_skip, sz_l, ssd_norm_g)

    out_l = merge_branches((pool_l, mlstm_l, ssd_l), mg_l, w_branch, w_out)
    if not keep_ctx:
        return None, out_l
    pool_c = pool_mixer(pa_c, pool_w, pool_scale, False)
    mlstm_c = mlstm_out(h_cf + flip(h_cb), o_c, mlstm_norm_g)
    ssd_c = ssd_out(y_cf + flip(y_cb), xc_, d_skip, sz_c, ssd_norm_g)
    out_c = merge_branches((pool_c, mlstm_c, ssd_c), mg_c, w_branch, w_out)
    return out_c, out_l


def hier_moe(h, w_rg, b_rg, w_re, b_re, w_gate, w_up, w_down):
    b, l, d = h.shape
    n_tok = b * l
    t = h.reshape(n_tok, d)
    lg = (t @ w_rg).astype(F32) + b_rg.astype(F32)
    grp_p = jax.nn.softmax(lg, axis=-1)
    _, grp = lax.top_k(lg, 1)
    p_grp = jnp.take_along_axis(grp_p, grp, axis=-1)
    le = ((t @ w_re).astype(F32) + b_re.astype(F32)).reshape(n_tok, N_EXPERT_GROUPS, EXPERTS_PER_GROUP)
    le = jnp.take_along_axis(le, grp[:, :, None], axis=1)[:, 0]
    top_v, top_i = lax.top_k(le, TOP_K_IN_GROUP)
    weight = p_grp * jax.nn.softmax(top_v, axis=-1)
    expert = grp * EXPERTS_PER_GROUP + top_i

    n_asg = n_tok * TOP_K_IN_GROUP
    flat_e = expert.reshape(-1)
    flat_w = weight.reshape(-1)
    flat_tok = jnp.repeat(jnp.arange(n_tok, dtype=jnp.int32), TOP_K_IN_GROUP)
    order = jnp.argsort(flat_e)
    se = flat_e[order]
    counts = jnp.bincount(flat_e, length=N_EXPERTS)
    starts = jnp.cumsum(counts) - counts
    padded = (counts + MOE_BLOCK - 1) // MOE_BLOCK * MOE_BLOCK
    pends = jnp.cumsum(padded)
    pstarts = pends - padded
    dest = pstarts[se] + (jnp.arange(n_asg, dtype=jnp.int32) - starts[se])
    cap = -(-n_asg // MOE_BLOCK) * MOE_BLOCK + N_EXPERTS * MOE_BLOCK
    n_blocks = cap // MOE_BLOCK
    buf_tok = jnp.full((cap,), n_tok, jnp.int32).at[dest].set(flat_tok[order])
    buf_w = jnp.zeros((cap,), F32).at[dest].set(flat_w[order])
    block_expert = jnp.clip(jnp.searchsorted(pends, jnp.arange(n_blocks) * MOE_BLOCK, side='right'),
                            0, N_EXPERTS - 1)
    t_pad = jnp.concatenate([t, jnp.zeros((1, d), t.dtype)], axis=0)
    xb = t_pad[buf_tok].reshape(n_blocks, MOE_BLOCK, d)

    def expert_block(args):
        xblk, e = args
        a = jax.nn.silu(xblk @ w_gate[e]) * (xblk @ w_up[e])
        return a @ w_down[e]

    yb = lax.map(expert_block, (xb, block_expert)).reshape(cap, d)
    y = jax.ops.segment_sum(yb * buf_w[:, None].astype(yb.dtype), buf_tok, num_segments=n_tok + 1)[:n_tok]
    return y.reshape(b, l, d)


def setup_inputs(seed: int = 0) -> dict:
    key = jax.random.key(seed)
    ks = jax.random.split(key, 32)
    nrm = lambda k, shape, s: jax.random.normal(k, shape, F32) * s
    D = D_MODEL
    dt0 = jnp.exp(jax.random.uniform(ks[17], (DEPTH, 2, SSD_HEADS), F32, math.log(1e-3), math.log(1e-1)))
    return {
        'x': nrm(ks[0], (BATCH, SEQ, D), 1.0),
        'c': nrm(ks[1], (BATCH, D), 1.0),
        'ctx': nrm(ks[2], (BATCH, CTX_LEN, D), 1.0),
        'c_ctx': nrm(ks[3], (D,), 1.0),
        'w_ada': nrm(ks[4], (DEPTH, D, 6 * D), 0.5 * D ** -0.5),
        'b_ada': nrm(ks[5], (DEPTH, 6 * D), 0.02),
        'g_norm_mix': 1.0 + nrm(ks[6], (DEPTH, D), 0.05),
        'g_norm_ffn': 1.0 + nrm(ks[7], (DEPTH, D), 0.05),
        'w_in': nrm(ks[8], (DEPTH, D, D_IN), D ** -0.5),
        'pool_w': nrm(ks[9], (DEPTH, len(POOL_WINDOWS), POOL_GROUP, POOL_GROUP), POOL_GROUP ** -0.5),
        'pool_scale': 1.0 + nrm(ks[10], (DEPTH, D_POOL), 0.05),
        'mlstm_conv_w': nrm(ks[11], (DEPTH, QK_CONV, 2 * D_MLSTM), QK_CONV ** -0.5),
        'mlstm_conv_b': nrm(ks[12], (DEPTH, 2 * D_MLSTM), 0.02),
        'mlstm_gate_b': jnp.concatenate([nrm(ks[13], (DEPTH, 2, MLSTM_HEADS), 0.1),
                                         3.0 + 3.0 * jax.random.uniform(ks[14], (DEPTH, 2, MLSTM_HEADS), F32)],
                                        axis=1),
        'mlstm_norm_g': 1.0 + nrm(ks[15], (DEPTH, D_MLSTM), 0.05),
        'ssd_conv_w': nrm(ks[16], (DEPTH, SSD_CONV, D_XBC), SSD_CONV ** -0.5),
        'ssd_conv_b': nrm(ks[18], (DEPTH, D_XBC), 0.02),
        'ssd_dt_bias': dt0 + jnp.log(-jnp.expm1(-dt0)),
        'ssd_a_log': jnp.log(jax.random.uniform(ks[19], (DEPTH, 2, SSD_HEADS), F32, 1.0, 16.0)),
        'ssd_d': 1.0 + nrm(ks[20], (DEPTH, SSD_HEADS), 0.1),
        'ssd_norm_g': 1.0 + nrm(ks[21], (DEPTH, D_SSD), 0.05),
        'w_branch': nrm(ks[22], (DEPTH, N_BRANCH, D_BRANCH, D), D_BRANCH ** -0.5),
        'w_out': nrm(ks[23], (DEPTH, D, D), D ** -0.5),
        'w_route_group': nrm(ks[24], (DEPTH, D, N_EXPERT_GROUPS), D ** -0.5),
        'b_route_group': nrm(ks[25], (DEPTH, N_EXPERT_GROUPS), 0.01),
        'w_route_expert': nrm(ks[26], (DEPTH, D, N_EXPERTS), D ** -0.5),
        'b_route_expert': nrm(ks[27], (DEPTH, N_EXPERTS), 0.01),
        'w_exp_gate': nrm(ks[28], (DEPTH, N_EXPERTS, D, D_EXPERT), D ** -0.5),
        'w_exp_up': nrm(ks[29], (DEPTH, N_EXPERTS, D, D_EXPERT), D ** -0.5),
        'w_exp_down': nrm(ks[30], (DEPTH, N_EXPERTS, D_EXPERT, D), D_EXPERT ** -0.5),
        'g_final': 1.0 + nrm(ks[31], (D,), 0.05),
    }


def reference(x, c, ctx, c_ctx, w_ada, b_ada, g_norm_mix, g_norm_ffn, w_in, pool_w, pool_scale,
              mlstm_conv_w, mlstm_conv_b, mlstm_gate_b, mlstm_norm_g, ssd_conv_w, ssd_conv_b,
              ssd_dt_bias, ssd_a_log, ssd_d, ssd_norm_g, w_branch, w_out, w_route_group, b_route_group,
              w_route_expert, b_route_expert, w_exp_gate, w_exp_up, w_exp_down, g_final):
    silu_c = jax.nn.silu(c)
    silu_cc = jax.nn.silu(c_ctx)
    lc = ctx.shape[1]
    xl, xc = x, ctx
    for layer in range(DEPTH):
        keep_ctx = layer < DEPTH - 1
        mod_l = [m[:, None, :] for m in jnp.split(silu_c @ w_ada[layer] + b_ada[layer], 6, axis=-1)]
        mod_c = jnp.split(silu_cc @ w_ada[layer] + b_ada[layer], 6, axis=-1)
        hl = modulate(rmsnorm(xl, g_norm_mix[layer]), mod_l[0], mod_l[1])
        hc = modulate(rmsnorm(xc, g_norm_mix[layer]), mod_c[0], mod_c[1])
        oc, ol = token_mixer(hc, hl, keep_ctx, w_in[layer], pool_w[layer], pool_scale[layer],
                             mlstm_conv_w[layer], mlstm_conv_b[layer], mlstm_gate_b[layer], mlstm_norm_g[layer],
                             ssd_conv_w[layer], ssd_conv_b[layer], ssd_dt_bias[layer], ssd_a_log[layer],
                             ssd_d[layer], ssd_norm_g[layer], w_branch[layer], w_out[layer])
        xl = xl + mod_l[2] * ol
        hl = modulate(rmsnorm(xl, g_norm_ffn[layer]), mod_l[3], mod_l[4])
        moe_w = (w_route_group[layer], b_route_group[layer], w_route_expert[layer], b_route_expert[layer],
                 w_exp_gate[layer], w_exp_up[layer], w_exp_down[layer])
        if keep_ctx:
            xc = xc + mod_c[2] * oc
            hc = modulate(rmsnorm(xc, g_norm_ffn[layer]), mod_c[3], mod_c[4])
            y = hier_moe(jnp.concatenate([hc, hl], axis=1), *moe_w)
            xc = xc + mod_c[5] * y[:, :lc]
            xl = xl + mod_l[5] * y[:, lc:]
        else:
            xl = xl + mod_l[5] * hier_moe(hl, *moe_w)
    return rmsnorm(xl, g_final)
```

```python
import functools
import math

import jax
import jax.numpy as jnp
import numpy as np
from jax import lax
from jax.experimental import pallas as pl
from jax.experimental.pallas import tpu as pltpu

F32 = jnp.float32
BF16 = jnp.bfloat16

D = 1024
BATCH = 4
SEQ = 4096
CTX = 256
DEPTH = 2
GRID_W = 64
EPS = 1e-6

TB = 256
SEQ_BLOCKS = (SEQ + CTX) // TB
T_BATCH = SEQ + CTX
N_TOK = BATCH * T_BATCH
N_BLOCKS = N_TOK // TB

POOL_WINDOWS = (2, 4, 8, 16)
HEADS_M = 4
DH_M = 256
SSD_HEADS = 16
SSD_P = 64
SSD_GROUPS = 4
SSD_N = 128
N_GROUPS_E = 4
EPG = 8
N_EXPERTS = 32
D_EXPERT = 512
MOE_BLOCK = 256
SMALL = 128

VMEM_LIMIT = 56 * 1024 * 1024


def _cparams(n_axes):
    return pltpu.CompilerParams(dimension_semantics=("arbitrary",) * n_axes,
                                vmem_limit_bytes=VMEM_LIMIT)


def _dot(a, b):
    return jnp.dot(a, b, preferred_element_type=F32)


def _split2(x):
    hi = x.astype(BF16)
    lo = (x - hi.astype(F32)).astype(BF16)
    return hi, lo


def _split3(x):
    hi = x.astype(BF16)
    r = x - hi.astype(F32)
    mid = r.astype(BF16)
    lo = (r - mid.astype(F32)).astype(BF16)
    return hi, mid, lo


def _sigmoid(x):
    return 1.0 / (1.0 + jnp.exp(-x))


def _silu(x):
    return x * _sigmoid(x)


def _log1pexp_negabs(x):
    return jnp.log(1.0 + jnp.exp(-jnp.abs(x)))


def _log_sigmoid(x):
    return jnp.minimum(x, 0.0) - _log1pexp_negabs(x)


def _softplus(x):
    return jnp.maximum(x, 0.0) + _log1pexp_negabs(x)


def _rms(x):
    return x * lax.rsqrt(jnp.mean(x * x, axis=-1, keepdims=True) + EPS)


def _mod_row(j):
    return jnp.where(j % SEQ_BLOCKS == 0, BATCH, j // SEQ_BLOCKS)


def _ada_kernel(c_ref, w_ref, b_ref, o_ref):
    c = c_ref[...]
    s_hi, s_lo = _split2(_silu(c))
    w_hi, w_lo = _split2(w_ref[...])
    o_ref[...] = _dot(s_hi, w_hi) + _dot(s_lo, w_hi) + _dot(s_hi, w_lo) + b_ref[...]


def _ada_table(cvec, w_ada, b_ada):
    tn = 1024
    return pl.pallas_call(
        _ada_kernel,
        grid=(DEPTH, 6 * D // tn),
        in_specs=[pl.BlockSpec((8, D), lambda l, j: (0, 0)),
                  pl.BlockSpec((None, D, tn), lambda l, j: (l, 0, j)),
                  pl.BlockSpec((None, 1, tn), lambda l, j: (l, 0, j))],
        out_specs=pl.BlockSpec((None, 8, tn), lambda l, j: (l, 0, j)),
        out_shape=jax.ShapeDtypeStruct((DEPTH, 8, 6 * D), F32),
        compiler_params=_cparams(2),
        name="ada_table",
    )(cvec, w_ada, b_ada.reshape(DEPTH, 1, 6 * D))


def _norm_mod_small(x, g_ref, sh_ref, sc_ref, wsh_ref, wsl_ref, h_ref, small_ref):
    h = _rms(x) * g_ref[...]
    h = h * (1.0 + sc_ref[...]) + sh_ref[...]
    h_hi, h_lo = _split2(h)
    h_ref[...] = h_hi
    small_ref[...] = _dot(h_hi, wsh_ref[...]) + _dot(h_lo, wsh_ref[...]) + _dot(h_hi, wsl_ref[...])


def _norm_kernel(x_ref, g_ref, sh_ref, sc_ref, wsh_ref, wsl_ref, h_ref, small_ref):
    _norm_mod_small(x_ref[...], g_ref, sh_ref, sc_ref, wsh_ref, wsl_ref, h_ref, small_ref)


def _resnorm_kernel(x_ref, ya_ref, yb_ref, gate_ref, g_ref, sh_ref, sc_ref, wsh_ref, wsl_ref,
                    xo_ref, h_ref, small_ref):
    x = x_ref[...] + gate_ref[...] * (ya_ref[...].astype(F32) + yb_ref[...].astype(F32))
    xo_ref[...] = x
    _norm_mod_small(x, g_ref, sh_ref, sc_ref, wsh_ref, wsl_ref, h_ref, small_ref)


def _mod_spec(chunk):
    return pl.BlockSpec((None, 1, D), lambda j: (_mod_row(j), 0, chunk))


def _tok_spec(width=D, col=0):
    return pl.BlockSpec((TB, width), lambda j: (j, col))


def _const_spec(shape):
    nd = len(shape)
    return pl.BlockSpec(shape, lambda j: (0,) * nd)


def _norm_call(x, g, mods3, ws_hi, ws_lo):
    return pl.pallas_call(
        _norm_kernel,
        grid=(N_BLOCKS,),
        in_specs=[_tok_spec(), _const_spec((1, D)), _mod_spec(0), _mod_spec(1),
                  _const_spec((D, SMALL)), _const_spec((D, SMALL))],
        out_specs=[_tok_spec(), _tok_spec(SMALL)],
        out_shape=[jax.ShapeDtypeStruct((N_TOK, D), BF16), jax.ShapeDtypeStruct((N_TOK, SMALL), F32)],
        compiler_params=_cparams(1),
        name="norm_mod",
    )(x, g, mods3, mods3, ws_hi, ws_lo)


def _resnorm_call(x, ya, yb, mods3_prev, g, mods3, ws_hi, ws_lo):
    return pl.pallas_call(
        _resnorm_kernel,
        grid=(N_BLOCKS,),
        in_specs=[_tok_spec(), _tok_spec(), _tok_spec(),
                  pl.BlockSpec((None, 1, D), lambda j: (_mod_row(j), 0, 5)),
                  _const_spec((1, D)), _mod_spec(0), _mod_spec(1),
                  _const_spec((D, SMALL)), _const_spec((D, SMALL))],
        out_specs=[_tok_spec(), _tok_spec(), _tok_spec(SMALL)],
        out_shape=[jax.ShapeDtypeStruct((N_TOK, D), F32), jax.ShapeDtypeStruct((N_TOK, D), BF16),
                   jax.ShapeDtypeStruct((N_TOK, SMALL), F32)],
        compiler_params=_cparams(1),
        name="residual_norm_mod",
    )(x, ya, yb, mods3_prev, g, mods3, mods3, ws_hi, ws_lo)


def _final_kernel(x_ref, ya_ref, yb_ref, gate_ref, g_ref, o_ref):
    x = x_ref[...] + gate_ref[...] * (ya_ref[...].astype(F32) + yb_ref[...].astype(F32))
    o_ref[...] = _rms(x) * g_ref[...]


def _final_call(x, ya, yb, mods3, g_final):
    lat = lambda b, s: (b * SEQ_BLOCKS + 1 + s, 0)
    spec = pl.BlockSpec((TB, D), lat)
    return pl.pallas_call(
        _final_kernel,
        grid=(BATCH, SEQ // TB),
        in_specs=[spec, spec, spec,
                  pl.BlockSpec((None, 1, D), lambda b, s: (b, 0, 5)),
                  pl.BlockSpec((1, D), lambda b, s: (0, 0))],
        out_specs=pl.BlockSpec((TB, D), lambda b, s: (b * (SEQ // TB) + s, 0)),
        out_shape=jax.ShapeDtypeStruct((BATCH * SEQ, D), F32),
        compiler_params=_cparams(2),
        name="final_norm",
    )(x, ya, yb, mods3, g_final)


def _mm_kernel(h_ref, w_ref, o_ref):
    o_ref[...] = _dot(h_ref[...], w_ref[...]).astype(o_ref.dtype)


def _mm_call(h, w, name):
    n = w.shape[1]
    tm, tn = 1024, 512
    return pl.pallas_call(
        _mm_kernel,
        grid=(n // tn, N_TOK // tm),
        in_specs=[pl.BlockSpec((tm, D), lambda j, i: (i, 0)),
                  pl.BlockSpec((D, tn), lambda j, i: (0, j))],
        out_specs=pl.BlockSpec((tm, tn), lambda j, i: (i, j)),
        out_shape=jax.ShapeDtypeStruct((N_TOK, n), BF16),
        compiler_params=_cparams(2),
        name=name,
    )(h, w)


CONV_HALO = 16


def _conv_kernel(x_ref, w_ref, b_ref, post_ref, o_ref):
    w = w_ref[...]
    bias = b_ref[...]
    post = post_ref[...]
    row = lax.broadcasted_iota(jnp.int32, (TB, 1), 0)
    for k in range(SEQ_BLOCKS):
        r0 = k * TB
        lo = max(r0 - CONV_HALO, 0)
        hi = min(r0 + TB + CONV_HALO, T_BATCH)
        off = r0 - lo
        n = hi - lo
        ext = x_ref[lo:hi, :].astype(F32)
        xm2 = pltpu.roll(ext, 2, 0)[off:off + TB]
        xm1 = pltpu.roll(ext, 1, 0)[off:off + TB]
        x0 = ext[off:off + TB]
        xp1 = pltpu.roll(ext, n - 1, 0)[off:off + TB]
        if k in (0, 1):
            xm2 = jnp.where(row >= 2, xm2, 0.0)
            xm1 = jnp.where(row >= 1, xm1, 0.0)
        if k in (0, SEQ_BLOCKS - 1):
            xp1 = jnp.where(row <= TB - 2, xp1, 0.0)
        y = bias + w[0:1] * xm2 + w[1:2] * xm1 + w[2:3] * x0 + w[3:4] * xp1
        o_ref[r0:r0 + TB, :] = (_silu(y) * post).astype(o_ref.dtype)


def _conv_call(u, col0, width, w, b, post, name):
    tc = 512
    cb0 = col0 // tc
    return pl.pallas_call(
        _conv_kernel,
        grid=(BATCH, width // tc),
        in_specs=[pl.BlockSpec((T_BATCH, tc), lambda bi, c: (bi, cb0 + c)),
                  pl.BlockSpec((4, tc), lambda bi, c: (0, c)),
                  pl.BlockSpec((1, tc), lambda bi, c: (0, c)),
                  pl.BlockSpec((1, tc), lambda bi, c: (0, c))],
        out_specs=pl.BlockSpec((T_BATCH, tc), lambda bi, c: (bi, c)),
        out_shape=jax.ShapeDtypeStruct((N_TOK, width), BF16),
        compiler_params=_cparams(2),
        name=name,
    )(u, w, b, post)


def _tri_dot_cols(tri, x):
    a, b, c = _split3(x)
    return _dot(tri, a) + _dot(tri, b) + _dot(tri, c)


def _tri_dot_rows(x, tri):
    a, b, c = _split3(x)
    return _dot(a, tri) + _dot(b, tri) + _dot(c, tri)


def _gate_kernel(pre_ref, pret_ref, bc_ref, br_ref, ac_ref, ar_ref, bwdc_ref, bwdr_ref, tril_ref, triu_ref,
                 valc_ref, cumc_ref, valr_ref, cumr_ref, *, ssd):
    tril = tril_ref[...]
    triu = triu_ref[...]

    def act(v):
        if ssd:
            val = _softplus(v)
            return val, val
        return v, _log_sigmoid(v)

    vc, dc = act(pre_ref[...] + bc_ref[...])
    dc = dc * ac_ref[...]
    valc_ref[...] = vc
    cumc_ref[...] = jnp.where(bwdc_ref[...] > 0.5, _tri_dot_cols(triu, dc), _tri_dot_cols(tril, dc))

    vr, dr = act(pret_ref[...] + br_ref[...])
    dr = dr * ar_ref[...]
    valr_ref[...] = vr
    cumr_ref[...] = jnp.where(bwdr_ref[...] > 0.5, _tri_dot_rows(dr, tril), _tri_dot_rows(dr, triu))


def _gate_call(small, bias, scale, bwd, ssd, name):
    small_t = small.reshape(N_BLOCKS, TB, SMALL)[:, :, :64].transpose(0, 2, 1)
    tri = np.tril(np.ones((TB, TB), np.float32))
    tril = jnp.asarray(tri, BF16)
    triu = jnp.asarray(tri.T, BF16)
    col = lambda v: v.reshape(1, SMALL).astype(F32)
    rowv = lambda v: v[:64].reshape(64, 1).astype(F32)
    cs = lambda shape: pl.BlockSpec(shape, lambda j: (0,) * len(shape))
    return pl.pallas_call(
        functools.partial(_gate_kernel, ssd=ssd),
        grid=(N_BLOCKS,),
        in_specs=[_tok_spec(SMALL), pl.BlockSpec((None, 64, TB), lambda j: (j, 0, 0)),
                  cs((1, SMALL)), cs((64, 1)), cs((1, SMALL)), cs((64, 1)), cs((1, SMALL)), cs((64, 1)),
                  cs((TB, TB)), cs((TB, TB))],
        out_specs=[_tok_spec(SMALL), _tok_spec(SMALL),
                   pl.BlockSpec((None, 64, TB), lambda j: (j, 0, 0)),
                   pl.BlockSpec((None, 64, TB), lambda j: (j, 0, 0))],
        out_shape=[jax.ShapeDtypeStruct((N_TOK, SMALL), F32), jax.ShapeDtypeStruct((N_TOK, SMALL), F32),
                   jax.ShapeDtypeStruct((N_BLOCKS, 64, TB), F32), jax.ShapeDtypeStruct((N_BLOCKS, 64, TB), F32)],
        compiler_params=_cparams(1),
        name=name,
    )(small, small_t, col(bias), rowv(bias), col(scale), rowv(scale), col(bwd), rowv(bwd), tril, triu)


def _scan_block(d):
    if d == 0:
        return lambda b, s: b * SEQ_BLOCKS + s
    return lambda b, s: b * SEQ_BLOCKS + jnp.where(s == 0, 0, SEQ_BLOCKS - s)


def _causal_mask(d):
    t = lax.broadcasted_iota(jnp.int32, (TB, TB), 0)
    s = lax.broadcasted_iota(jnp.int32, (TB, TB), 1)
    return (s <= t) if d == 0 else (s >= t)


def _mlstm_kernel(q_ref, k_ref, v_ref, valc_ref, cumc_ref, valr_ref, cumr_ref, o_ref,
                  c_ref, n_ref, m_ref, *, d):
    @pl.when(pl.program_id(1) == 0)
    def _():
        c_ref[...] = jnp.zeros_like(c_ref)
        n_ref[...] = jnp.zeros_like(n_ref)
        m_ref[...] = jnp.zeros_like(m_ref)

    mask = _causal_mask(d)
    end = TB - 1 if d == 0 else 0
    for h in range(HEADS_M):
        p = d * HEADS_M + h
        cs = slice(h * DH_M, (h + 1) * DH_M)
        q = q_ref[:, cs]
        k = k_ref[:, cs]
        v = v_ref[:, cs]
        i_c = valc_ref[:, p:p + 1]
        cum_c = cumc_ref[:, 8 + p:9 + p]
        i_r = valr_ref[p:p + 1, :]
        cum_r = cumr_ref[8 + p:9 + p, :]
        total = cum_r[:, end:end + 1]
        m_prev = m_ref[h]
        c_st = c_ref[h]
        n_st = n_ref[h]

        dmat = jnp.where(mask, cum_c - cum_r + i_r, -jnp.inf)
        g = cum_c + m_prev
        m_t = jnp.maximum(g, jnp.max(dmat, axis=1, keepdims=True))
        w = jnp.exp(dmat - m_t)
        sg = jnp.exp(g - m_t)
        a = w * lax.dot_general(q, k, (((1,), (1,)), ((), ())), preferred_element_type=F32)
        num = _dot(a.astype(BF16), v) + sg * _dot(q, c_st.astype(BF16))
        den = jnp.sum(a, axis=1, keepdims=True) + sg * jnp.sum(q.astype(F32) * n_st, axis=1, keepdims=True)
        o_ref[:, cs] = (num / jnp.maximum(jnp.abs(den), jnp.exp(-m_t))).astype(o_ref.dtype)

        wl_log = total - cum_c + i_c
        m_new = jnp.maximum(total + m_prev, jnp.max(wl_log, axis=0, keepdims=True))
        wl = jnp.exp(wl_log - m_new)
        decay = jnp.exp(total + m_prev - m_new)
        kf = k.astype(F32)
        wv = (wl * v.astype(F32)).astype(BF16)
        c_ref[h] = decay * c_st + lax.dot_general(k, wv, (((0,), (0,)), ((), ())), preferred_element_type=F32)
        n_ref[h] = decay * n_st + jnp.sum(wl * kf, axis=0, keepdims=True)
        m_ref[h] = m_new


def _mlstm_call(qk, ucm, valc, cumc, valr, cumr, d):
    blk = _scan_block(d)
    tok = lambda col: pl.BlockSpec((TB, D), lambda b, s: (blk(b, s), col))
    small = pl.BlockSpec((TB, SMALL), lambda b, s: (blk(b, s), 0))
    rows = pl.BlockSpec((None, 64, TB), lambda b, s: (blk(b, s), 0, 0))
    return pl.pallas_call(
        functools.partial(_mlstm_kernel, d=d),
        grid=(BATCH, SEQ_BLOCKS),
        in_specs=[tok(0), tok(1), tok(2), small, small, rows, rows],
        out_specs=tok(0),
        out_shape=jax.ShapeDtypeStruct((N_TOK, D), BF16),
        scratch_shapes=[pltpu.VMEM((HEADS_M, DH_M, DH_M), F32), pltpu.VMEM((HEADS_M, 1, DH_M), F32),
                        pltpu.VMEM((HEADS_M, 1, 1), F32)],
        compiler_params=_cparams(2),
        name="mlstm_fwd" if d == 0 else "mlstm_bwd",
    )(qk, qk, ucm, valc, cumc, valr, cumr)


def _expand_heads(cols):
    lane = lax.broadcasted_iota(jnp.int32, (TB, 4 * SSD_P), 1)
    return jnp.where(lane < SSD_P, cols[0],
                     jnp.where(lane < 2 * SSD_P, cols[1], jnp.where(lane < 3 * SSD_P, cols[2], cols[3])))


def _ssd_kernel(x_ref, b_ref, c_ref, dtc_ref, acsc_ref, acsr_ref, o_ref, s_ref, *, d):
    @pl.when(pl.program_id(1) == 0)
    def _():
        s_ref[...] = jnp.zeros_like(s_ref)

    mask = _causal_mask(d)
    end = TB - 1 if d == 0 else 0
    lane = lax.broadcasted_iota(jnp.int32, (TB, 4 * SSD_P), 1)
    lane1 = lax.broadcasted_iota(jnp.int32, (1, 4 * SSD_P), 1)
    for g in range(SSD_GROUPS):
        xg = x_ref[:, g * 256:(g + 1) * 256].astype(F32)
        bg = b_ref[:, g * SSD_N:(g + 1) * SSD_N]
        cg = c_ref[:, g * SSD_N:(g + 1) * SSD_N]
        s_prev = s_ref[g]
        lanes = [SSD_HEADS + d * SSD_HEADS + g * 4 + hh for hh in range(4)]
        dt_cols = [dtc_ref[:, p:p + 1] for p in lanes]
        acs_cols = [acsc_ref[:, p:p + 1] for p in lanes]
        acs_rows = [acsr_ref[p:p + 1, :] for p in lanes]
        tots = [r[:, end:end + 1] for r in acs_rows]

        xdt = xg * _expand_heads(dt_cols)
        xdt_b = xdt.astype(BF16)
        cb = lax.dot_general(cg, bg, (((1,), (1,)), ((), ())), preferred_element_type=F32)
        y = jnp.zeros((TB, 4 * SSD_P), F32)
        for hh in range(4):
            lm = jnp.exp(jnp.where(mask, acs_cols[hh] - acs_rows[hh], -jnp.inf))
            yh = _dot((cb * lm).astype(BF16), xdt_b)
            y = jnp.where((lane >= hh * SSD_P) & (lane < (hh + 1) * SSD_P), yh, y)
        y = y + _dot(cg, s_prev.astype(BF16)) * _expand_heads([jnp.exp(a) for a in acs_cols])
        o_ref[:, g * 256:(g + 1) * 256] = y.astype(o_ref.dtype)

        to_end = _expand_heads([jnp.exp(tots[hh] - acs_cols[hh]) for hh in range(4)])
        st = lax.dot_general(bg, (xdt * to_end).astype(BF16), (((0,), (0,)), ((), ())),
                             preferred_element_type=F32)
        cd = [jnp.exp(t) for t in tots]
        chunk_decay = jnp.where(lane1 < SSD_P, cd[0],
                                jnp.where(lane1 < 2 * SSD_P, cd[1], jnp.where(lane1 < 3 * SSD_P, cd[2], cd[3])))
        s_ref[g] = chunk_decay * s_prev + st


def _ssd_call(xbc, dtc, acsc, acsr, d):
    blk = _scan_block(d)
    small = pl.BlockSpec((TB, SMALL), lambda b, s: (blk(b, s), 0))
    rows = pl.BlockSpec((None, 64, TB), lambda b, s: (blk(b, s), 0, 0))
    return pl.pallas_call(
        functools.partial(_ssd_kernel, d=d),
        grid=(BATCH, SEQ_BLOCKS),
        in_specs=[pl.BlockSpec((TB, D), lambda b, s: (blk(b, s), 0)),
                  pl.BlockSpec((TB, 512), lambda b, s: (blk(b, s), 2)),
                  pl.BlockSpec((TB, 512), lambda b, s: (blk(b, s), 3)),
                  small, small, rows],
        out_specs=pl.BlockSpec((TB, D), lambda b, s: (blk(b, s), 0)),
        out_shape=jax.ShapeDtypeStruct((N_TOK, D), BF16),
        scratch_shapes=[pltpu.VMEM((SSD_GROUPS, SSD_N, 4 * SSD_P), F32)],
        compiler_params=_cparams(2),
        name="ssd_fwd" if d == 0 else "ssd_bwd",
    )(xbc, xbc, xbc, dtc, acsc, acsr)


def _pool_tables():
    masks = np.zeros((2, 4, TB, TB), np.float32)
    inv = np.zeros((2, 4, TB, 1), np.float32)
    for kind, length in ((0, GRID_W), (1, CTX)):
        for gi, w in enumerate(POOL_WINDOWS):
            for r in range(TB):
                base, c = (r // length) * length, r % length
                lo = min(max(c - w // 2, 0), length - 1)
                hi = min(max(c - w // 2 + w - 1, 0), length - 1)
                masks[kind, gi, r, base + lo:base + hi + 1] = 1.0
                inv[kind, gi, r, 0] = 1.0 / (hi - lo + 1)
    return jnp.asarray(masks, BF16), jnp.asarray(inv, F32)


def _merge_kernel(x_ref, pa_ref, o_ref, z_ref, mg0_ref, mg1_ref, mg2_ref, hf_ref, hb_ref, yf_ref, yb_ref, xs_ref,
                  pmask_ref, pinv_ref, poolw_ref, pscale_ref, mng_ref, dsk_ref, sng_ref, wbr_ref, wout_ref,
                  gate_ref, sh_ref, sc_ref, gffn_ref, wrh_ref, wrl_ref, brt_ref,
                  xo_ref, h2_ref, lg_ref):
    parts = []
    for g in range(4):
        a_g = pa_ref[:, g * 256:(g + 1) * 256]
        pooled = _dot(pmask_ref[g], a_g) * pinv_ref[g] - a_g.astype(F32)
        parts.append(_dot(pooled.astype(BF16), poolw_ref[g]))
    pool = jnp.concatenate(parts, axis=1) * pscale_ref[...]

    hs = hf_ref[...].astype(F32) + hb_ref[...].astype(F32)
    hn = jnp.concatenate([_rms(hs[:, h * DH_M:(h + 1) * DH_M]) for h in range(HEADS_M)], axis=1)
    ml = _sigmoid(o_ref[...].astype(F32)) * (hn * mng_ref[...])

    y = yf_ref[...].astype(F32) + yb_ref[...].astype(F32) + dsk_ref[...] * xs_ref[...].astype(F32)
    sl = _rms(y * _silu(z_ref[...].astype(F32))) * sng_ref[...]

    acc = _sigmoid(mg0_ref[...].astype(F32)) * _dot(pool.astype(BF16), wbr_ref[0])
    acc = acc + _sigmoid(mg1_ref[...].astype(F32)) * _dot(ml.astype(BF16), wbr_ref[1])
    acc = acc + _sigmoid(mg2_ref[...].astype(F32)) * _dot(sl.astype(BF16), wbr_ref[2])
    xn = x_ref[...] + gate_ref[...] * _dot(acc.astype(BF16), wout_ref[...])
    xo_ref[...] = xn

    h2 = (_rms(xn) * gffn_ref[...]) * (1.0 + sc_ref[...]) + sh_ref[...]
    hi, lo = _split2(h2)
    h2_ref[...] = hi
    lg_ref[...] = _dot(hi, wrh_ref[...]) + _dot(lo, wrh_ref[...]) + _dot(hi, wrl_ref[...]) + brt_ref[...]


def _merge_call(x, urm, hf, hb, yf, yb, xbc, pmask, pinv, poolw, pscale, mng, dsk, sng, wbr, wout,
                mods3, gffn, wr_hi, wr_lo, br):
    kind = lambda j: jnp.where(j % SEQ_BLOCKS == 0, 1, 0)
    vec = _const_spec((1, D))
    return pl.pallas_call(
        _merge_kernel,
        grid=(N_BLOCKS,),
        in_specs=[_tok_spec(),
                  _tok_spec(D, 0), _tok_spec(D, 1), _tok_spec(D, 4),
                  _tok_spec(D, 5), _tok_spec(D, 6), _tok_spec(D, 7),
                  _tok_spec(), _tok_spec(), _tok_spec(), _tok_spec(),
                  _tok_spec(D, 0),
                  pl.BlockSpec((None, 4, TB, TB), lambda j: (kind(j), 0, 0, 0)),
                  pl.BlockSpec((None, 4, TB, 1), lambda j: (kind(j), 0, 0, 0)),
                  _const_spec((4, 256, 256)), vec, vec, vec, vec,
                  _const_spec((3, D, D)), _const_spec((D, D)),
                  _mod_spec(2), _mod_spec(3), _mod_spec(4), vec,
                  _const_spec((D, SMALL)), _const_spec((D, SMALL)), _const_spec((1, SMALL))],
        out_specs=[_tok_spec(), _tok_spec(), _tok_spec(SMALL)],
        out_shape=[jax.ShapeDtypeStruct((N_TOK, D), F32), jax.ShapeDtypeStruct((N_TOK, D), BF16),
                   jax.ShapeDtypeStruct((N_TOK, SMALL), F32)],
        compiler_params=_cparams(1),
        name="branch_merge",
    )(x, urm, urm, urm, urm, urm, urm, hf, hb, yf, yb, xbc, pmask, pinv, poolw, pscale, mng, dsk, sng,
      wbr, wout, mods3, mods3, mods3, gffn, wr_hi, wr_lo, br)


def _moe_kernel(be_ref, nu_ref, xb_ref, wg_ref, wu_ref, wd_ref, bw_ref, o_ref):
    i = pl.program_id(0)

    @pl.when(i < nu_ref[0])
    def _():
        x = xb_ref[...]
        gt = _dot(x, wg_ref[...].astype(BF16))
        up = _dot(x, wu_ref[...].astype(BF16))
        act = (_silu(gt) * up).astype(BF16)
        o_ref[...] = (_dot(act, wd_ref[...].astype(BF16)) * bw_ref[...]).astype(o_ref.dtype)

    @pl.when(i >= nu_ref[0])
    def _():
        o_ref[...] = jnp.zeros_like(o_ref)


def _moe_call(block_expert, n_used, xb, w_gate, w_up, w_down, buf_w):
    cap = xb.shape[0]
    grid_spec = pltpu.PrefetchScalarGridSpec(
        num_scalar_prefetch=2,
        grid=(cap // MOE_BLOCK,),
        in_specs=[pl.BlockSpec((MOE_BLOCK, D), lambda i, be, nu: (i, 0)),
                  pl.BlockSpec((None, D, D_EXPERT), lambda i, be, nu: (be[i], 0, 0)),
                  pl.BlockSpec((None, D, D_EXPERT), lambda i, be, nu: (be[i], 0, 0)),
                  pl.BlockSpec((None, D_EXPERT, D), lambda i, be, nu: (be[i], 0, 0)),
                  pl.BlockSpec((MOE_BLOCK, 1), lambda i, be, nu: (i, 0))],
        out_specs=pl.BlockSpec((MOE_BLOCK, D), lambda i, be, nu: (i, 0)),
    )
    return pl.pallas_call(
        _moe_kernel,
        grid_spec=grid_spec,
        out_shape=jax.ShapeDtypeStruct((cap, D), BF16),
        compiler_params=_cparams(1),
        name="moe_experts",
    )(block_expert, n_used, xb, w_gate, w_up, w_down, buf_w)


def _route(logits, b_unused=None):
    n_tok = logits.shape[0]
    lg = logits[:, :N_GROUPS_E]
    grp_p = jax.nn.softmax(lg, axis=-1)
    _, grp = lax.top_k(lg, 1)
    p_grp = jnp.take_along_axis(grp_p, grp, axis=-1)
    le = logits[:, N_GROUPS_E:N_GROUPS_E + N_EXPERTS].reshape(n_tok, N_GROUPS_E, EPG)
    le = jnp.take_along_axis(le, grp[:, :, None], axis=1)[:, 0]
    top_v, top_i = lax.top_k(le, 2)
    weight = p_grp * jax.nn.softmax(top_v, axis=-1)
    expert = grp * EPG + top_i

    n_asg = n_tok * 2
    flat_e = expert.reshape(-1).astype(jnp.int32)
    flat_w = weight.reshape(-1)
    flat_tok = jnp.repeat(jnp.arange(n_tok, dtype=jnp.int32), 2)
    order = jnp.argsort(flat_e)
    se = flat_e[order]
    counts = jnp.bincount(flat_e, length=N_EXPERTS)
    starts = jnp.cumsum(counts) - counts
    padded = (counts + MOE_BLOCK - 1) // MOE_BLOCK * MOE_BLOCK
    pends = jnp.cumsum(padded)
    pstarts = pends - padded
    dest = (pstarts[se] + (jnp.arange(n_asg, dtype=jnp.int32) - starts[se])).astype(jnp.int32)
    cap = -(-n_asg // MOE_BLOCK) * MOE_BLOCK + N_EXPERTS * MOE_BLOCK
    n_blocks = cap // MOE_BLOCK
    buf_tok = jnp.zeros((cap,), jnp.int32).at[dest].set(flat_tok[order])
    buf_w = jnp.zeros((cap,), F32).at[dest].set(flat_w[order])
    block_expert = jnp.clip(jnp.searchsorted(pends, jnp.arange(n_blocks) * MOE_BLOCK, side='right'),
                            0, N_EXPERTS - 1).astype(jnp.int32)
    n_used = (pends[-1] // MOE_BLOCK).astype(jnp.int32).reshape(1)
    pos = jnp.zeros((n_asg,), jnp.int32).at[order].set(dest).reshape(n_tok, 2)
    return buf_tok, buf_w.reshape(cap, 1), block_expert, n_used, pos


def _to_colmajor(t):
    c = t.shape[-1]
    t = t.reshape(BATCH, T_BATCH, c)
    lat = t[:, CTX:].reshape(BATCH, SEQ // GRID_W, GRID_W, c).swapaxes(1, 2).reshape(BATCH, SEQ, c)
    return jnp.concatenate([t[:, :CTX], lat], axis=1).reshape(N_TOK, c)


def _to_rowmajor(t):
    c = t.shape[-1]
    t = t.reshape(BATCH, T_BATCH, c)
    lat = t[:, CTX:].reshape(BATCH, GRID_W, SEQ // GRID_W, c).swapaxes(1, 2).reshape(BATCH, SEQ, c)
    return jnp.concatenate([t[:, :CTX], lat], axis=1).reshape(N_TOK, c)


def _lanes(*pieces):
    v = jnp.concatenate([jnp.asarray(p, F32).reshape(-1) for p in pieces])
    return jnp.pad(v, (0, SMALL - v.shape[0]))


def kernel(x, c, ctx, c_ctx, w_ada, b_ada, g_norm_mix, g_norm_ffn, w_in, pool_w, pool_scale, mlstm_conv_w,
           mlstm_conv_b, mlstm_gate_b, mlstm_norm_g, ssd_conv_w, ssd_conv_b, ssd_dt_bias, ssd_a_log, ssd_d,
           ssd_norm_g, w_branch, w_out, w_route_group, b_route_group, w_route_expert, b_route_expert,
           w_exp_gate, w_exp_up, w_exp_down, g_final):
    cvec = jnp.concatenate([c, c_ctx[None], jnp.zeros((3, D), F32)], axis=0)
    mods = _ada_table(cvec, w_ada, b_ada)
    xs = jnp.concatenate([ctx, x], axis=1).reshape(N_TOK, D)
    pmask, pinv = _pool_tables()
    row = lambda v: v.reshape(1, -1).astype(F32)

    h = small = None
    out = None
    for l in range(DEPTH):
        mods3 = mods[l].reshape(8, 1, 6 * D)
        wi = w_in[l]
        sp = np.cumsum([0, D, D, D, D, D, 16, D, D, 512, 512, 32, 3 * D])
        piece = lambda i: wi[:, sp[i]:sp[i + 1]]
        w_rm = jnp.concatenate([piece(0), piece(4), piece(6), piece(8), piece(9), piece(7), piece(11)],
                               axis=1).astype(BF16)
        w_cm = jnp.concatenate([piece(1), piece(2), piece(3)], axis=1).astype(BF16)
        w_small = jnp.pad(jnp.concatenate([piece(5), piece(10)], axis=1), ((0, 0), (0, SMALL - 48)))
        ws_hi = w_small.astype(BF16)
        ws_lo = (w_small - ws_hi.astype(F32)).astype(BF16)

        if l == 0:
            h, small = _norm_call(xs, row(g_norm_mix[l]), mods3, ws_hi, ws_lo)

        h_cm = _to_colmajor(h)
        small_cm = _to_colmajor(small)
        u_rm = _mm_call(h, w_rm, "in_proj_rowmajor")
        u_cm = _mm_call(h_cm, w_cm, "in_proj_colmajor")

        post_m = jnp.concatenate([jnp.ones((D,), F32), jnp.full((D,), DH_M ** -0.5, F32)]).reshape(1, 2 * D)
        qk = _conv_call(u_cm, 0, 2 * D, mlstm_conv_w[l], row(mlstm_conv_b[l]), post_m, "mlstm_conv")
        gate_bias = _lanes(mlstm_gate_b[l])
        gate_bwd = _lanes(jnp.zeros((12,)), jnp.ones((4,)))
        g_valc, g_cumc, g_valr, g_cumr = _gate_call(small_cm, gate_bias, jnp.ones((SMALL,), F32), gate_bwd,
                                                    False, "mlstm_gates")
        h_f = _mlstm_call(qk, u_cm, g_valc, g_cumc, g_valr, g_cumr, 0)
        h_b = _mlstm_call(qk, u_cm, g_valc, g_cumc, g_valr, g_cumr, 1)
        h_f = _to_rowmajor(h_f)
        h_b = _to_rowmajor(h_b)

        xbc = _conv_call(u_rm, 2 * D, 2 * D, ssd_conv_w[l], row(ssd_conv_b[l]), jnp.ones((1, 2 * D), F32),
                         "ssd_conv")
        a_neg = -jnp.exp(ssd_a_log[l].astype(F32))
        dt_bias = _lanes(jnp.zeros((16,)), ssd_dt_bias[l])
        dt_scale = _lanes(jnp.zeros((16,)), a_neg)
        dt_bwd = _lanes(jnp.zeros((32,)), jnp.ones((16,)))
        s_dtc, s_acsc, _, s_acsr = _gate_call(small, dt_bias, dt_scale, dt_bwd, True, "ssd_gates")
        y_f = _ssd_call(xbc, s_dtc, s_acsc, s_acsr, 0)
        y_b = _ssd_call(xbc, s_dtc, s_acsc, s_acsr, 1)

        w_r = jnp.pad(jnp.concatenate([w_route_group[l], w_route_expert[l]], axis=1),
                      ((0, 0), (0, SMALL - N_GROUPS_E - N_EXPERTS)))
        wr_hi = w_r.astype(BF16)
        wr_lo = (w_r - wr_hi.astype(F32)).astype(BF16)
        b_r = _lanes(b_route_group[l], b_route_expert[l]).reshape(1, SMALL)
        dsk = jnp.repeat(ssd_d[l].astype(F32), SSD_P).reshape(1, D)
        xs, h2, logits = _merge_call(
            xs, u_rm, h_f, h_b, y_f, y_b, xbc, pmask, pinv, pool_w[l].astype(BF16), row(pool_scale[l]),
            row(mlstm_norm_g[l]), dsk, row(ssd_norm_g[l]), w_branch[l].astype(BF16), w_out[l].astype(BF16),
            mods3, row(g_norm_ffn[l]), wr_hi, wr_lo, b_r)

        buf_tok, buf_w, block_expert, n_used, pos = _route(logits)
        xb = h2[buf_tok]
        yb = _moe_call(block_expert, n_used, xb, w_exp_gate[l], w_exp_up[l], w_exp_down[l], buf_w)
        y0 = yb[pos[:, 0]]
        y1 = yb[pos[:, 1]]

        if l + 1 < DEPTH:
            wi_n = w_in[l + 1]
            w_small_n = jnp.pad(jnp.concatenate([wi_n[:, sp[5]:sp[6]], wi_n[:, sp[10]:sp[11]]], axis=1),
                                ((0, 0), (0, SMALL - 48)))
            wsn_hi = w_small_n.astype(BF16)
            wsn_lo = (w_small_n - wsn_hi.astype(F32)).astype(BF16)
            mods3_n = mods[l + 1].reshape(8, 1, 6 * D)
            xs, h, small = _resnorm_call(xs, y0, y1, mods3, row(g_norm_mix[l + 1]), mods3_n, wsn_hi, wsn_lo)
        else:
            out = _final_call(xs, y0, y1, mods3, row(g_final))
    return out.reshape(BATCH, SEQ, D)
```

```python
import functools
import math

import jax
import jax.numpy as jnp
import numpy as np
from jax import lax
from jax.experimental import pallas as pl
from jax.experimental.pallas import tpu as pltpu

F32 = jnp.float32
BF16 = jnp.bfloat16

D = 1024
BATCH = 4
SEQ = 4096
CTX = 256
DEPTH = 2
GRID_W = 64
EPS = 1e-6

TB = 256
SEQ_BLOCKS = (SEQ + CTX) // TB
T_BATCH = SEQ + CTX
N_TOK = BATCH * T_BATCH
N_BLOCKS = N_TOK // TB

POOL_WINDOWS = (2, 4, 8, 16)
HEADS_M = 4
DH_M = 256
SSD_HEADS = 16
SSD_P = 64
SSD_GROUPS = 4
SSD_N = 128
N_GROUPS_E = 4
EPG = 8
N_EXPERTS = 32
D_EXPERT = 512
MOE_BLOCK = 256
SMALL = 128

VMEM_LIMIT = 56 * 1024 * 1024


def _cparams(n_axes):
    return pltpu.CompilerParams(dimension_semantics=("arbitrary",) * n_axes,
                                vmem_limit_bytes=VMEM_LIMIT)


def _dot(a, b):
    return jnp.dot(a, b, preferred_element_type=F32)


def _split2(x):
    hi = x.astype(BF16)
    lo = (x - hi.astype(F32)).astype(BF16)
    return hi, lo


def _split3(x):
    hi = x.astype(BF16)
    r = x - hi.astype(F32)
    mid = r.astype(BF16)
    lo = (r - mid.astype(F32)).astype(BF16)
    return hi, mid, lo


def _sigmoid(x):
    return 1.0 / (1.0 + jnp.exp(-x))


def _silu(x):
    return x * _sigmoid(x)


def _log1pexp_negabs(x):
    return jnp.log(1.0 + jnp.exp(-jnp.abs(x)))


def _log_sigmoid(x):
    return jnp.minimum(x, 0.0) - _log1pexp_negabs(x)


def _softplus(x):
    return jnp.maximum(x, 0.0) + _log1pexp_negabs(x)


def _rms(x):
    return x * lax.rsqrt(jnp.mean(x * x, axis=-1, keepdims=True) + EPS)


def _mod_row(j):
    return jnp.where(j % SEQ_BLOCKS == 0, BATCH, j // SEQ_BLOCKS)


def _ada_kernel(c_ref, w_ref, b_ref, o_ref):
    c = c_ref[...]
    s_hi, s_lo = _split2(_silu(c))
    w_hi, w_lo = _split2(w_ref[...])
    o_ref[...] = _dot(s_hi, w_hi) + _dot(s_lo, w_hi) + _dot(s_hi, w_lo) + b_ref[...]


def _ada_table(cvec, w_ada, b_ada):
    tn = 1024
    return pl.pallas_call(
        _ada_kernel,
        grid=(DEPTH, 6 * D // tn),
        in_specs=[pl.BlockSpec((8, D), lambda l, j: (0, 0)),
                  pl.BlockSpec((None, D, tn), lambda l, j: (l, 0, j)),
                  pl.BlockSpec((None, 1, tn), lambda l, j: (l, 0, j))],
        out_specs=pl.BlockSpec((None, 8, tn), lambda l, j: (l, 0, j)),
        out_shape=jax.ShapeDtypeStruct((DEPTH, 8, 6 * D), F32),
        compiler_params=_cparams(2),
        name="ada_table",
    )(cvec, w_ada, b_ada.reshape(DEPTH, 1, 6 * D))


def _norm_mod_small(x, g_ref, sh_ref, sc_ref, wsh_ref, wsl_ref, h_ref, small_ref):
    h = _rms(x) * g_ref[...]
    h = h * (1.0 + sc_ref[...]) + sh_ref[...]
    h_hi, h_lo = _split2(h)
    h_ref[...] = h_hi
    small_ref[...] = _dot(h_hi, wsh_ref[...]) + _dot(h_lo, wsh_ref[...]) + _dot(h_hi, wsl_ref[...])


def _norm_kernel(x_ref, g_ref, sh_ref, sc_ref, wsh_ref, wsl_ref, h_ref, small_ref):
    _norm_mod_small(x_ref[...], g_ref, sh_ref, sc_ref, wsh_ref, wsl_ref, h_ref, small_ref)


def _moe_residual(x_ref, ya_ref, yb_ref, rt_ref, gate_ref):
    y = rt_ref[:, 2:3] * ya_ref[...].astype(F32) + rt_ref[:, 3:4] * yb_ref[...].astype(F32)
    return x_ref[...] + gate_ref[...] * y


def _resnorm_kernel(x_ref, ya_ref, yb_ref, rt_ref, gate_ref, g_ref, sh_ref, sc_ref, wsh_ref, wsl_ref,
                    xo_ref, h_ref, small_ref):
    x = _moe_residual(x_ref, ya_ref, yb_ref, rt_ref, gate_ref)
    xo_ref[...] = x
    _norm_mod_small(x, g_ref, sh_ref, sc_ref, wsh_ref, wsl_ref, h_ref, small_ref)


def _mod_spec(chunk):
    return pl.BlockSpec((None, 1, D), lambda j: (_mod_row(j), 0, chunk))


def _tok_spec(width=D, col=0):
    return pl.BlockSpec((TB, width), lambda j: (j, col))


def _const_spec(shape):
    nd = len(shape)
    return pl.BlockSpec(shape, lambda j: (0,) * nd)


def _norm_call(x, g, mods3, ws_hi, ws_lo):
    return pl.pallas_call(
        _norm_kernel,
        grid=(N_BLOCKS,),
        in_specs=[_tok_spec(), _const_spec((1, D)), _mod_spec(0), _mod_spec(1),
                  _const_spec((D, SMALL)), _const_spec((D, SMALL))],
        out_specs=[_tok_spec(), _tok_spec(SMALL)],
        out_shape=[jax.ShapeDtypeStruct((N_TOK, D), BF16), jax.ShapeDtypeStruct((N_TOK, SMALL), F32)],
        compiler_params=_cparams(1),
        name="norm_mod",
    )(x, g, mods3, mods3, ws_hi, ws_lo)


def _resnorm_call(x, ya, yb, route, mods3_prev, g, mods3, ws_hi, ws_lo):
    return pl.pallas_call(
        _resnorm_kernel,
        grid=(N_BLOCKS,),
        in_specs=[_tok_spec(), _tok_spec(), _tok_spec(), _tok_spec(SMALL),
                  pl.BlockSpec((None, 1, D), lambda j: (_mod_row(j), 0, 5)),
                  _const_spec((1, D)), _mod_spec(0), _mod_spec(1),
                  _const_spec((D, SMALL)), _const_spec((D, SMALL))],
        out_specs=[_tok_spec(), _tok_spec(), _tok_spec(SMALL)],
        out_shape=[jax.ShapeDtypeStruct((N_TOK, D), F32), jax.ShapeDtypeStruct((N_TOK, D), BF16),
                   jax.ShapeDtypeStruct((N_TOK, SMALL), F32)],
        compiler_params=_cparams(1),
        name="residual_norm_mod",
    )(x, ya, yb, route, mods3_prev, g, mods3, mods3, ws_hi, ws_lo)


def _final_kernel(x_ref, ya_ref, yb_ref, rt_ref, gate_ref, g_ref, o_ref):
    o_ref[...] = _rms(_moe_residual(x_ref, ya_ref, yb_ref, rt_ref, gate_ref)) * g_ref[...]


def _final_call(x, ya, yb, route, mods3, g_final):
    lat = lambda b, s: (b * SEQ_BLOCKS + 1 + s, 0)
    spec = pl.BlockSpec((TB, D), lat)
    return pl.pallas_call(
        _final_kernel,
        grid=(BATCH, SEQ // TB),
        in_specs=[spec, spec, spec, pl.BlockSpec((TB, SMALL), lat),
                  pl.BlockSpec((None, 1, D), lambda b, s: (b, 0, 5)),
                  pl.BlockSpec((1, D), lambda b, s: (0, 0))],
        out_specs=pl.BlockSpec((TB, D), lambda b, s: (b * (SEQ // TB) + s, 0)),
        out_shape=jax.ShapeDtypeStruct((BATCH * SEQ, D), F32),
        compiler_params=_cparams(2),
        name="final_norm",
    )(x, ya, yb, route, mods3, g_final)


def _mm_kernel(h_ref, w_ref, o_ref):
    o_ref[...] = _dot(h_ref[...], w_ref[...]).astype(o_ref.dtype)


def _mm_call(h, w, name):
    n = w.shape[1]
    tm, tn = 1024, 512
    return pl.pallas_call(
        _mm_kernel,
        grid=(n // tn, N_TOK // tm),
        in_specs=[pl.BlockSpec((tm, D), lambda j, i: (i, 0)),
                  pl.BlockSpec((D, tn), lambda j, i: (0, j))],
        out_specs=pl.BlockSpec((tm, tn), lambda j, i: (i, j)),
        out_shape=jax.ShapeDtypeStruct((N_TOK, n), BF16),
        compiler_params=_cparams(2),
        name=name,
    )(h, w)


CONV_HALO = 16


def _conv_kernel(x_ref, w_ref, b_ref, post_ref, o_ref):
    w = w_ref[...]
    bias = b_ref[...]
    post = post_ref[...]
    row = lax.broadcasted_iota(jnp.int32, (TB, 1), 0)
    for k in range(SEQ_BLOCKS):
        r0 = k * TB
        lo = max(r0 - CONV_HALO, 0)
        hi = min(r0 + TB + CONV_HALO, T_BATCH)
        off = r0 - lo
        n = hi - lo
        ext = x_ref[lo:hi, :].astype(F32)
        xm2 = pltpu.roll(ext, 2, 0)[off:off + TB]
        xm1 = pltpu.roll(ext, 1, 0)[off:off + TB]
        x0 = ext[off:off + TB]
        xp1 = pltpu.roll(ext, n - 1, 0)[off:off + TB]
        if k in (0, 1):
            xm2 = jnp.where(row >= 2, xm2, 0.0)
            xm1 = jnp.where(row >= 1, xm1, 0.0)
        if k in (0, SEQ_BLOCKS - 1):
            xp1 = jnp.where(row <= TB - 2, xp1, 0.0)
        y = bias + w[0:1] * xm2 + w[1:2] * xm1 + w[2:3] * x0 + w[3:4] * xp1
        o_ref[r0:r0 + TB, :] = (_silu(y) * post).astype(o_ref.dtype)


def _conv_call(u, col0, width, w, b, post, name):
    tc = 512
    cb0 = col0 // tc
    return pl.pallas_call(
        _conv_kernel,
        grid=(BATCH, width // tc),
        in_specs=[pl.BlockSpec((T_BATCH, tc), lambda bi, c: (bi, cb0 + c)),
                  pl.BlockSpec((4, tc), lambda bi, c: (0, c)),
                  pl.BlockSpec((1, tc), lambda bi, c: (0, c)),
                  pl.BlockSpec((1, tc), lambda bi, c: (0, c))],
        out_specs=pl.BlockSpec((T_BATCH, tc), lambda bi, c: (bi, c)),
        out_shape=jax.ShapeDtypeStruct((N_TOK, width), BF16),
        compiler_params=_cparams(2),
        name=name,
    )(u, w, b, post)


def _tri_dot_cols(tri, x):
    a, b, c = _split3(x)
    return _dot(tri, a) + _dot(tri, b) + _dot(tri, c)


def _tri_dot_rows(x, tri):
    a, b, c = _split3(x)
    return _dot(a, tri) + _dot(b, tri) + _dot(c, tri)


def _gate_kernel(pre_ref, pret_ref, bc_ref, br_ref, ac_ref, ar_ref, bwdc_ref, bwdr_ref, tril_ref, triu_ref,
                 valc_ref, cumc_ref, valr_ref, cumr_ref, *, ssd):
    tril = tril_ref[...]
    triu = triu_ref[...]

    def act(v):
        if ssd:
            val = _softplus(v)
            return val, val
        return v, _log_sigmoid(v)

    vc, dc = act(pre_ref[...] + bc_ref[...])
    dc = dc * ac_ref[...]
    valc_ref[...] = vc
    cumc_ref[...] = jnp.where(bwdc_ref[...] > 0.5, _tri_dot_cols(triu, dc), _tri_dot_cols(tril, dc))

    vr, dr = act(pret_ref[...] + br_ref[...])
    dr = dr * ar_ref[...]
    valr_ref[...] = vr
    cumr_ref[...] = jnp.where(bwdr_ref[...] > 0.5, _tri_dot_rows(dr, tril), _tri_dot_rows(dr, triu))


def _gate_call(small, bias, scale, bwd, ssd, name):
    small_t = small.reshape(N_BLOCKS, TB, SMALL)[:, :, :64].transpose(0, 2, 1)
    tri = np.tril(np.ones((TB, TB), np.float32))
    tril = jnp.asarray(tri, BF16)
    triu = jnp.asarray(tri.T, BF16)
    col = lambda v: v.reshape(1, SMALL).astype(F32)
    rowv = lambda v: v[:64].reshape(64, 1).astype(F32)
    cs = lambda shape: pl.BlockSpec(shape, lambda j: (0,) * len(shape))
    return pl.pallas_call(
        functools.partial(_gate_kernel, ssd=ssd),
        grid=(N_BLOCKS,),
        in_specs=[_tok_spec(SMALL), pl.BlockSpec((None, 64, TB), lambda j: (j, 0, 0)),
                  cs((1, SMALL)), cs((64, 1)), cs((1, SMALL)), cs((64, 1)), cs((1, SMALL)), cs((64, 1)),
                  cs((TB, TB)), cs((TB, TB))],
        out_specs=[_tok_spec(SMALL), _tok_spec(SMALL),
                   pl.BlockSpec((None, 64, TB), lambda j: (j, 0, 0)),
                   pl.BlockSpec((None, 64, TB), lambda j: (j, 0, 0))],
        out_shape=[jax.ShapeDtypeStruct((N_TOK, SMALL), F32), jax.ShapeDtypeStruct((N_TOK, SMALL), F32),
                   jax.ShapeDtypeStruct((N_BLOCKS, 64, TB), F32), jax.ShapeDtypeStruct((N_BLOCKS, 64, TB), F32)],
        compiler_params=_cparams(1),
        name=name,
    )(small, small_t, col(bias), rowv(bias), col(scale), rowv(scale), col(bwd), rowv(bwd), tril, triu)


def _scan_block(d):
    if d == 0:
        return lambda b, s: b * SEQ_BLOCKS + s
    return lambda b, s: b * SEQ_BLOCKS + jnp.where(s == 0, 0, SEQ_BLOCKS - s)


def _causal_mask(d):
    t = lax.broadcasted_iota(jnp.int32, (TB, TB), 0)
    s = lax.broadcasted_iota(jnp.int32, (TB, TB), 1)
    return (s <= t) if d == 0 else (s >= t)


def _mlstm_kernel(q_ref, k_ref, v_ref, valc_ref, cumc_ref, valr_ref, cumr_ref, o_ref,
                  c_ref, n_ref, m_ref, *, d):
    @pl.when(pl.program_id(1) == 0)
    def _():
        c_ref[...] = jnp.zeros_like(c_ref)
        n_ref[...] = jnp.zeros_like(n_ref)
        m_ref[...] = jnp.zeros_like(m_ref)

    mask = _causal_mask(d)
    end = TB - 1 if d == 0 else 0
    for h in range(HEADS_M):
        p = d * HEADS_M + h
        cs = slice(h * DH_M, (h + 1) * DH_M)
        q = q_ref[:, cs]
        k = k_ref[:, cs]
        v = v_ref[:, cs]
        i_c = valc_ref[:, p:p + 1]
        cum_c = cumc_ref[:, 8 + p:9 + p]
        i_r = valr_ref[p:p + 1, :]
        cum_r = cumr_ref[8 + p:9 + p, :]
        total = cum_r[:, end:end + 1]
        m_prev = m_ref[h]
        c_st = c_ref[h]
        n_st = n_ref[h]

        dmat = jnp.where(mask, cum_c - cum_r + i_r, -jnp.inf)
        g = cum_c + m_prev
        m_t = jnp.maximum(g, jnp.max(dmat, axis=1, keepdims=True))
        w = jnp.exp(dmat - m_t)
        sg = jnp.exp(g - m_t)
        a = w * lax.dot_general(q, k, (((1,), (1,)), ((), ())), preferred_element_type=F32)
        num = _dot(a.astype(BF16), v) + sg * _dot(q, c_st.astype(BF16))
        den = jnp.sum(a, axis=1, keepdims=True) + sg * jnp.sum(q.astype(F32) * n_st, axis=1, keepdims=True)
        o_ref[:, cs] = (num / jnp.maximum(jnp.abs(den), jnp.exp(-m_t))).astype(o_ref.dtype)

        wl_log = total - cum_c + i_c
        m_new = jnp.maximum(total + m_prev, jnp.max(wl_log, axis=0, keepdims=True))
        wl = jnp.exp(wl_log - m_new)
        decay = jnp.exp(total + m_prev - m_new)
        kf = k.astype(F32)
        wv = (wl * v.astype(F32)).astype(BF16)
        c_ref[h] = decay * c_st + lax.dot_general(k, wv, (((0,), (0,)), ((), ())), preferred_element_type=F32)
        n_ref[h] = decay * n_st + jnp.sum(wl * kf, axis=0, keepdims=True)
        m_ref[h] = m_new


def _mlstm_call(qk, ucm, valc, cumc, valr, cumr, d):
    blk = _scan_block(d)
    tok = lambda col: pl.BlockSpec((TB, D), lambda b, s: (blk(b, s), col))
    small = pl.BlockSpec((TB, SMALL), lambda b, s: (blk(b, s), 0))
    rows = pl.BlockSpec((None, 64, TB), lambda b, s: (blk(b, s), 0, 0))
    return pl.pallas_call(
        functools.partial(_mlstm_kernel, d=d),
        grid=(BATCH, SEQ_BLOCKS),
        in_specs=[tok(0), tok(1), tok(2), small, small, rows, rows],
        out_specs=tok(0),
        out_shape=jax.ShapeDtypeStruct((N_TOK, D), BF16),
        scratch_shapes=[pltpu.VMEM((HEADS_M, DH_M, DH_M), F32), pltpu.VMEM((HEADS_M, 1, DH_M), F32),
                        pltpu.VMEM((HEADS_M, 1, 1), F32)],
        compiler_params=_cparams(2),
        name="mlstm_fwd" if d == 0 else "mlstm_bwd",
    )(qk, qk, ucm, valc, cumc, valr, cumr)


def _expand_heads(cols):
    lane = lax.broadcasted_iota(jnp.int32, (TB, 4 * SSD_P), 1)
    return jnp.where(lane < SSD_P, cols[0],
                     jnp.where(lane < 2 * SSD_P, cols[1], jnp.where(lane < 3 * SSD_P, cols[2], cols[3])))


def _ssd_kernel(x_ref, b_ref, c_ref, dtc_ref, acsc_ref, acsr_ref, o_ref, s_ref, *, d):
    @pl.when(pl.program_id(1) == 0)
    def _():
        s_ref[...] = jnp.zeros_like(s_ref)

    mask = _causal_mask(d)
    end = TB - 1 if d == 0 else 0
    lane = lax.broadcasted_iota(jnp.int32, (TB, 4 * SSD_P), 1)
    lane1 = lax.broadcasted_iota(jnp.int32, (1, 4 * SSD_P), 1)
    for g in range(SSD_GROUPS):
        xg = x_ref[:, g * 256:(g + 1) * 256].astype(F32)
        bg = b_ref[:, g * SSD_N:(g + 1) * SSD_N]
        cg = c_ref[:, g * SSD_N:(g + 1) * SSD_N]
        s_prev = s_ref[g]
        lanes = [SSD_HEADS + d * SSD_HEADS + g * 4 + hh for hh in range(4)]
        dt_cols = [dtc_ref[:, p:p + 1] for p in lanes]
        acs_cols = [acsc_ref[:, p:p + 1] for p in lanes]
        acs_rows = [acsr_ref[p:p + 1, :] for p in lanes]
        tots = [r[:, end:end + 1] for r in acs_rows]

        xdt = xg * _expand_heads(dt_cols)
        xdt_b = xdt.astype(BF16)
        cb = lax.dot_general(cg, bg, (((1,), (1,)), ((), ())), preferred_element_type=F32)
        y = jnp.zeros((TB, 4 * SSD_P), F32)
        for hh in range(4):
            lm = jnp.exp(jnp.where(mask, acs_cols[hh] - acs_rows[hh], -jnp.inf))
            yh = _dot((cb * lm).astype(BF16), xdt_b)
            y = jnp.where((lane >= hh * SSD_P) & (lane < (hh + 1) * SSD_P), yh, y)
        y = y + _dot(cg, s_prev.astype(BF16)) * _expand_heads([jnp.exp(a) for a in acs_cols])
        o_ref[:, g * 256:(g + 1) * 256] = y.astype(o_ref.dtype)

        to_end = _expand_heads([jnp.exp(tots[hh] - acs_cols[hh]) for hh in range(4)])
        st = lax.dot_general(bg, (xdt * to_end).astype(BF16), (((0,), (0,)), ((), ())),
                             preferred_element_type=F32)
        cd = [jnp.exp(t) for t in tots]
        chunk_decay = jnp.where(lane1 < SSD_P, cd[0],
                                jnp.where(lane1 < 2 * SSD_P, cd[1], jnp.where(lane1 < 3 * SSD_P, cd[2], cd[3])))
        s_ref[g] = chunk_decay * s_prev + st


def _ssd_call(xbc, dtc, acsc, acsr, d):
    blk = _scan_block(d)
    small = pl.BlockSpec((TB, SMALL), lambda b, s: (blk(b, s), 0))
    rows = pl.BlockSpec((None, 64, TB), lambda b, s: (blk(b, s), 0, 0))
    return pl.pallas_call(
        functools.partial(_ssd_kernel, d=d),
        grid=(BATCH, SEQ_BLOCKS),
        in_specs=[pl.BlockSpec((TB, D), lambda b, s: (blk(b, s), 0)),
                  pl.BlockSpec((TB, 512), lambda b, s: (blk(b, s), 2)),
                  pl.BlockSpec((TB, 512), lambda b, s: (blk(b, s), 3)),
                  small, small, rows],
        out_specs=pl.BlockSpec((TB, D), lambda b, s: (blk(b, s), 0)),
        out_shape=jax.ShapeDtypeStruct((N_TOK, D), BF16),
        scratch_shapes=[pltpu.VMEM((SSD_GROUPS, SSD_N, 4 * SSD_P), F32)],
        compiler_params=_cparams(2),
        name="ssd_fwd" if d == 0 else "ssd_bwd",
    )(xbc, xbc, xbc, dtc, acsc, acsr)


def _pool_tables():
    masks = np.zeros((2, 4, TB, TB), np.float32)
    inv = np.zeros((2, 4, TB, 1), np.float32)
    for kind, length in ((0, GRID_W), (1, CTX)):
        for gi, w in enumerate(POOL_WINDOWS):
            for r in range(TB):
                base, c = (r // length) * length, r % length
                lo = min(max(c - w // 2, 0), length - 1)
                hi = min(max(c - w // 2 + w - 1, 0), length - 1)
                masks[kind, gi, r, base + lo:base + hi + 1] = 1.0
                inv[kind, gi, r, 0] = 1.0 / (hi - lo + 1)
    return jnp.asarray(masks, BF16), jnp.asarray(inv, F32)


ROUTE_FIELDS = 6


def _route_block(logits, ltri_ref, rt_ref, cnt_ref, cnt_scr):
    @pl.when(pl.program_id(0) == 0)
    def _():
        cnt_scr[...] = jnp.zeros_like(cnt_scr)

    lane = lax.broadcasted_iota(jnp.int32, logits.shape, 1).astype(F32)

    def top1(valid):
        v = jnp.max(jnp.where(valid, logits, -jnp.inf), axis=1, keepdims=True)
        i = jnp.min(jnp.where(valid & (logits == v), lane, float(SMALL)), axis=1, keepdims=True)
        return v, i

    is_g = lane < N_GROUPS_E
    gm, grp = top1(is_g)
    p_grp = 1.0 / jnp.sum(jnp.where(is_g, jnp.exp(logits - gm), 0.0), axis=1, keepdims=True)
    lo_lane = N_GROUPS_E + grp * EPG
    in_grp = (lane >= lo_lane) & (lane < lo_lane + EPG)
    v1, i1 = top1(in_grp)
    v2, i2 = top1(in_grp & (lane != i1))
    t = jnp.exp(v2 - v1)
    w1 = p_grp / (1.0 + t)
    w2 = p_grp * t / (1.0 + t)
    e1 = i1 - N_GROUPS_E
    e2 = i2 - N_GROUPS_E

    oh1 = lane == e1
    oh2 = lane == e2
    oh = jnp.where(oh1 | oh2, 1.0, 0.0)
    before = _dot(ltri_ref[...], oh.astype(BF16)) + cnt_scr[...]
    r1 = jnp.sum(jnp.where(oh1, before, 0.0), axis=1, keepdims=True)
    r2 = jnp.sum(jnp.where(oh2, before, 0.0), axis=1, keepdims=True)
    cnt_scr[...] = cnt_scr[...] + jnp.sum(oh, axis=0, keepdims=True)
    cnt_ref[...] = cnt_scr[...]
    rt = jnp.zeros(logits.shape, F32)
    for k, val in enumerate((e1, e2, w1, w2, r1, r2)):
        rt = jnp.where(lane == k, val, rt)
    rt_ref[...] = rt


def _merge_kernel(x_ref, pa_ref, o_ref, z_ref, mg0_ref, mg1_ref, mg2_ref, hf_ref, hb_ref, yf_ref, yb_ref, xs_ref,
                  pmask_ref, pinv_ref, poolw_ref, pscale_ref, mng_ref, dsk_ref, sng_ref, wbr_ref, wout_ref,
                  gate_ref, sh_ref, sc_ref, gffn_ref, wrh_ref, wrl_ref, brt_ref, ltri_ref,
                  xo_ref, h2_ref, rt_ref, cnt_ref, cnt_scr):
    parts = []
    for g in range(4):
        a_g = pa_ref[:, g * 256:(g + 1) * 256]
        pooled = _dot(pmask_ref[g], a_g) * pinv_ref[g] - a_g.astype(F32)
        parts.append(_dot(pooled.astype(BF16), poolw_ref[g]))
    pool = jnp.concatenate(parts, axis=1) * pscale_ref[...]

    hs = hf_ref[...].astype(F32) + hb_ref[...].astype(F32)
    hn = jnp.concatenate([_rms(hs[:, h * DH_M:(h + 1) * DH_M]) for h in range(HEADS_M)], axis=1)
    ml = _sigmoid(o_ref[...].astype(F32)) * (hn * mng_ref[...])

    y = yf_ref[...].astype(F32) + yb_ref[...].astype(F32) + dsk_ref[...] * xs_ref[...].astype(F32)
    sl = _rms(y * _silu(z_ref[...].astype(F32))) * sng_ref[...]

    acc = _sigmoid(mg0_ref[...].astype(F32)) * _dot(pool.astype(BF16), wbr_ref[0])
    acc = acc + _sigmoid(mg1_ref[...].astype(F32)) * _dot(ml.astype(BF16), wbr_ref[1])
    acc = acc + _sigmoid(mg2_ref[...].astype(F32)) * _dot(sl.astype(BF16), wbr_ref[2])
    xn = x_ref[...] + gate_ref[...] * _dot(acc.astype(BF16), wout_ref[...])
    xo_ref[...] = xn

    h2 = (_rms(xn) * gffn_ref[...]) * (1.0 + sc_ref[...]) + sh_ref[...]
    hi, lo = _split2(h2)
    h2_ref[...] = hi
    logits = _dot(hi, wrh_ref[...]) + _dot(lo, wrh_ref[...]) + _dot(hi, wrl_ref[...]) + brt_ref[...]
    _route_block(logits, ltri_ref, rt_ref, cnt_ref, cnt_scr)


def _merge_call(x, urm, hf, hb, yf, yb, xbc, pmask, pinv, poolw, pscale, mng, dsk, sng, wbr, wout,
                mods3, gffn, wr_hi, wr_lo, br):
    ltri = jnp.asarray(np.tril(np.ones((TB, TB), np.float32), -1), BF16)
    kind = lambda j: jnp.where(j % SEQ_BLOCKS == 0, 1, 0)
    vec = _const_spec((1, D))
    return pl.pallas_call(
        _merge_kernel,
        grid=(N_BLOCKS,),
        in_specs=[_tok_spec(),
                  _tok_spec(D, 0), _tok_spec(D, 1), _tok_spec(D, 4),
                  _tok_spec(D, 5), _tok_spec(D, 6), _tok_spec(D, 7),
                  _tok_spec(), _tok_spec(), _tok_spec(), _tok_spec(),
                  _tok_spec(D, 0),
                  pl.BlockSpec((None, 4, TB, TB), lambda j: (kind(j), 0, 0, 0)),
                  pl.BlockSpec((None, 4, TB, 1), lambda j: (kind(j), 0, 0, 0)),
                  _const_spec((4, 256, 256)), vec, vec, vec, vec,
                  _const_spec((3, D, D)), _const_spec((D, D)),
                  _mod_spec(2), _mod_spec(3), _mod_spec(4), vec,
                  _const_spec((D, SMALL)), _const_spec((D, SMALL)), _const_spec((1, SMALL)),
                  _const_spec((TB, TB))],
        out_specs=[_tok_spec(), _tok_spec(), _tok_spec(SMALL), _const_spec((1, SMALL))],
        out_shape=[jax.ShapeDtypeStruct((N_TOK, D), F32), jax.ShapeDtypeStruct((N_TOK, D), BF16),
                   jax.ShapeDtypeStruct((N_TOK, SMALL), F32), jax.ShapeDtypeStruct((1, SMALL), F32)],
        scratch_shapes=[pltpu.VMEM((1, SMALL), F32)],
        compiler_params=_cparams(1),
        name="branch_merge",
    )(x, urm, urm, urm, urm, urm, urm, hf, hb, yf, yb, xbc, pmask, pinv, poolw, pscale, mng, dsk, sng,
      wbr, wout, mods3, mods3, mods3, gffn, wr_hi, wr_lo, br, ltri)


def _moe_kernel(be_ref, nu_ref, xb_ref, wg_ref, wu_ref, wd_ref, o_ref):
    i = pl.program_id(0)

    @pl.when(i < nu_ref[0])
    def _():
        x = xb_ref[...]
        gt = _dot(x, wg_ref[...].astype(BF16))
        up = _dot(x, wu_ref[...].astype(BF16))
        act = (_silu(gt) * up).astype(BF16)
        o_ref[...] = _dot(act, wd_ref[...].astype(BF16)).astype(o_ref.dtype)

    @pl.when(i >= nu_ref[0])
    def _():
        o_ref[...] = jnp.zeros_like(o_ref)


MOE_CAP = N_TOK * 2 + N_EXPERTS * MOE_BLOCK
MOE_NBLOCKS = MOE_CAP // MOE_BLOCK


def _moe_call(layer, block_expert, n_used, xb, w_gate, w_up, w_down):
    grid_spec = pltpu.PrefetchScalarGridSpec(
        num_scalar_prefetch=2,
        grid=(MOE_NBLOCKS,),
        in_specs=[pl.BlockSpec((MOE_BLOCK, D), lambda i, be, nu: (i, 0)),
                  pl.BlockSpec((None, None, D, D_EXPERT), lambda i, be, nu: (layer, be[i], 0, 0)),
                  pl.BlockSpec((None, None, D, D_EXPERT), lambda i, be, nu: (layer, be[i], 0, 0)),
                  pl.BlockSpec((None, None, D_EXPERT, D), lambda i, be, nu: (layer, be[i], 0, 0))],
        out_specs=pl.BlockSpec((MOE_BLOCK, D), lambda i, be, nu: (i, 0)),
    )
    return pl.pallas_call(
        _moe_kernel,
        grid_spec=grid_spec,
        out_shape=jax.ShapeDtypeStruct((MOE_CAP, D), BF16),
        compiler_params=_cparams(1),
        name="moe_experts",
    )(block_expert, n_used, xb, w_gate, w_up, w_down)


def _dispatch_plan(route, counts):
    cnt = counts[0, :N_EXPERTS].astype(jnp.int32)
    padded = (cnt + MOE_BLOCK - 1) // MOE_BLOCK * MOE_BLOCK
    pends = jnp.cumsum(padded)
    pstarts = pends - padded
    expert = route[:, 0:2].astype(jnp.int32)
    rank = route[:, 4:6].astype(jnp.int32)
    onehot = expert[:, :, None] == jnp.arange(N_EXPERTS, dtype=jnp.int32)
    pos = jnp.sum(jnp.where(onehot, pstarts, 0), axis=-1) + rank
    tok = jnp.broadcast_to(jnp.arange(N_TOK, dtype=jnp.int32)[:, None], (N_TOK, 2))
    buf_tok = jnp.zeros((MOE_CAP,), jnp.int32).at[pos.reshape(-1)].set(tok.reshape(-1))
    block_start = jnp.arange(MOE_NBLOCKS, dtype=jnp.int32) * MOE_BLOCK
    block_expert = jnp.minimum(jnp.sum((pends[None, :] <= block_start[:, None]).astype(jnp.int32), axis=1),
                               N_EXPERTS - 1)
    n_used = (pends[-1] // MOE_BLOCK).reshape(1)
    return buf_tok, block_expert, n_used, pos


def _to_colmajor(t):
    c = t.shape[-1]
    t = t.reshape(BATCH, T_BATCH, c)
    lat = t[:, CTX:].reshape(BATCH, SEQ // GRID_W, GRID_W, c).swapaxes(1, 2).reshape(BATCH, SEQ, c)
    return jnp.concatenate([t[:, :CTX], lat], axis=1).reshape(N_TOK, c)


def _to_rowmajor(t):
    c = t.shape[-1]
    t = t.reshape(BATCH, T_BATCH, c)
    lat = t[:, CTX:].reshape(BATCH, GRID_W, SEQ // GRID_W, c).swapaxes(1, 2).reshape(BATCH, SEQ, c)
    return jnp.concatenate([t[:, :CTX], lat], axis=1).reshape(N_TOK, c)


def _lanes(*pieces):
    v = jnp.concatenate([jnp.asarray(p, F32).reshape(-1) for p in pieces])
    return jnp.pad(v, (0, SMALL - v.shape[0]))


def kernel(x, c, ctx, c_ctx, w_ada, b_ada, g_norm_mix, g_norm_ffn, w_in, pool_w, pool_scale, mlstm_conv_w,
           mlstm_conv_b, mlstm_gate_b, mlstm_norm_g, ssd_conv_w, ssd_conv_b, ssd_dt_bias, ssd_a_log, ssd_d,
           ssd_norm_g, w_branch, w_out, w_route_group, b_route_group, w_route_expert, b_route_expert,
           w_exp_gate, w_exp_up, w_exp_down, g_final):
    cvec = jnp.concatenate([c, c_ctx[None], jnp.zeros((3, D), F32)], axis=0)
    mods = _ada_table(cvec, w_ada, b_ada)
    xs = jnp.concatenate([ctx, x], axis=1).reshape(N_TOK, D)
    pmask, pinv = _pool_tables()
    row = lambda v: v.reshape(1, -1).astype(F32)
    w_in_bf = w_in.astype(BF16)

    h = small = None
    out = None
    for l in range(DEPTH):
        mods3 = mods[l].reshape(8, 1, 6 * D)
        wi = w_in_bf[l]
        sp = np.cumsum([0, D, D, D, D, D, 16, D, D, 512, 512, 32, 3 * D])
        piece = lambda i: wi[:, sp[i]:sp[i + 1]]
        w_rm = jnp.concatenate([piece(0), piece(4), piece(6), piece(8), piece(9), piece(7), piece(11)],
                               axis=1)
        w_cm = wi[:, sp[1]:sp[4]]
        w_small = jnp.pad(jnp.concatenate([w_in[l][:, sp[5]:sp[6]], w_in[l][:, sp[10]:sp[11]]], axis=1),
                          ((0, 0), (0, SMALL - 48)))
        ws_hi = w_small.astype(BF16)
        ws_lo = (w_small - ws_hi.astype(F32)).astype(BF16)

        if l == 0:
            h, small = _norm_call(xs, row(g_norm_mix[l]), mods3, ws_hi, ws_lo)

        h_cm = _to_colmajor(h)
        small_cm = _to_colmajor(small)
        u_rm = _mm_call(h, w_rm, "in_proj_rowmajor")
        u_cm = _mm_call(h_cm, w_cm, "in_proj_colmajor")

        post_m = jnp.concatenate([jnp.ones((D,), F32), jnp.full((D,), DH_M ** -0.5, F32)]).reshape(1, 2 * D)
        qk = _conv_call(u_cm, 0, 2 * D, mlstm_conv_w[l], row(mlstm_conv_b[l]), post_m, "mlstm_conv")
        gate_bias = _lanes(mlstm_gate_b[l])
        gate_bwd = _lanes(jnp.zeros((12,)), jnp.ones((4,)))
        g_valc, g_cumc, g_valr, g_cumr = _gate_call(small_cm, gate_bias, jnp.ones((SMALL,), F32), gate_bwd,
                                                    False, "mlstm_gates")
        h_f = _mlstm_call(qk, u_cm, g_valc, g_cumc, g_valr, g_cumr, 0)
        h_b = _mlstm_call(qk, u_cm, g_valc, g_cumc, g_valr, g_cumr, 1)
        h_f = _to_rowmajor(h_f)
        h_b = _to_rowmajor(h_b)

        xbc = _conv_call(u_rm, 2 * D, 2 * D, ssd_conv_w[l], row(ssd_conv_b[l]), jnp.ones((1, 2 * D), F32),
                         "ssd_conv")
        a_neg = -jnp.exp(ssd_a_log[l].astype(F32))
        dt_bias = _lanes(jnp.zeros((16,)), ssd_dt_bias[l])
        dt_scale = _lanes(jnp.zeros((16,)), a_neg)
        dt_bwd = _lanes(jnp.zeros((32,)), jnp.ones((16,)))
        s_dtc, s_acsc, _, s_acsr = _gate_call(small, dt_bias, dt_scale, dt_bwd, True, "ssd_gates")
        y_f = _ssd_call(xbc, s_dtc, s_acsc, s_acsr, 0)
        y_b = _ssd_call(xbc, s_dtc, s_acsc, s_acsr, 1)

        w_r = jnp.pad(jnp.concatenate([w_route_group[l], w_route_expert[l]], axis=1),
                      ((0, 0), (0, SMALL - N_GROUPS_E - N_EXPERTS)))
        wr_hi = w_r.astype(BF16)
        wr_lo = (w_r - wr_hi.astype(F32)).astype(BF16)
        b_r = _lanes(b_route_group[l], b_route_expert[l]).reshape(1, SMALL)
        dsk = jnp.repeat(ssd_d[l].astype(F32), SSD_P).reshape(1, D)
        xs, h2, route, counts = _merge_call(
            xs, u_rm, h_f, h_b, y_f, y_b, xbc, pmask, pinv, pool_w[l].astype(BF16), row(pool_scale[l]),
            row(mlstm_norm_g[l]), dsk, row(ssd_norm_g[l]), w_branch[l].astype(BF16), w_out[l].astype(BF16),
            mods3, row(g_norm_ffn[l]), wr_hi, wr_lo, b_r)

        buf_tok, block_expert, n_used, pos = _dispatch_plan(route, counts)
        xb = h2[buf_tok]
        yb = _moe_call(l, block_expert, n_used, xb, w_exp_gate, w_exp_up, w_exp_down)
        y0 = yb[pos[:, 0]]
        y1 = yb[pos[:, 1]]

        if l + 1 < DEPTH:
            wi_n = w_in[l + 1]
            w_small_n = jnp.pad(jnp.concatenate([wi_n[:, sp[5]:sp[6]], wi_n[:, sp[10]:sp[11]]], axis=1),
                                ((0, 0), (0, SMALL - 48)))
            wsn_hi = w_small_n.astype(BF16)
            wsn_lo = (w_small_n - wsn_hi.astype(F32)).astype(BF16)
            mods3_n = mods[l + 1].reshape(8, 1, 6 * D)
            xs, h, small = _resnorm_call(xs, y0, y1, route, mods3, row(g_norm_mix[l + 1]), mods3_n,
                                         wsn_hi, wsn_lo)
        else:
            out = _final_call(xs, y0, y1, route, mods3, row(g_final))
    return out.reshape(BATCH, SEQ, D)
```

```python
import functools
import math

import jax
import jax.numpy as jnp
import numpy as np
from jax import lax
from jax.experimental import pallas as pl
from jax.experimental.pallas import tpu as pltpu

F32 = jnp.float32
BF16 = jnp.bfloat16

D = 1024
BATCH = 4
SEQ = 4096
CTX = 256
DEPTH = 2
GRID_W = 64
EPS = 1e-6

TB = 256
SEQ_BLOCKS = (SEQ + CTX) // TB
T_BATCH = SEQ + CTX
N_TOK = BATCH * T_BATCH
N_BLOCKS = N_TOK // TB

POOL_WINDOWS = (2, 4, 8, 16)
HEADS_M = 4
DH_M = 256
SSD_HEADS = 16
SSD_P = 64
SSD_GROUPS = 4
SSD_N = 128
N_GROUPS_E = 4
EPG = 8
N_EXPERTS = 32
D_EXPERT = 512
MOE_BLOCK = 256
SMALL = 128

VMEM_LIMIT = 56 * 1024 * 1024


def _cparams(n_axes):
    return pltpu.CompilerParams(dimension_semantics=("arbitrary",) * n_axes,
                                vmem_limit_bytes=VMEM_LIMIT)


def _dot(a, b):
    return jnp.dot(a, b, preferred_element_type=F32)


def _split2(x):
    hi = x.astype(BF16)
    lo = (x - hi.astype(F32)).astype(BF16)
    return hi, lo


def _split3(x):
    hi = x.astype(BF16)
    r = x - hi.astype(F32)
    mid = r.astype(BF16)
    lo = (r - mid.astype(F32)).astype(BF16)
    return hi, mid, lo


def _sigmoid(x):
    return 1.0 / (1.0 + jnp.exp(-x))


def _silu(x):
    return x * _sigmoid(x)


def _log1pexp_negabs(x):
    return jnp.log(1.0 + jnp.exp(-jnp.abs(x)))


def _log_sigmoid(x):
    return jnp.minimum(x, 0.0) - _log1pexp_negabs(x)


def _softplus(x):
    return jnp.maximum(x, 0.0) + _log1pexp_negabs(x)


def _rms(x):
    return x * lax.rsqrt(jnp.mean(x * x, axis=-1, keepdims=True) + EPS)


def _mod_row(j):
    return jnp.where(j % SEQ_BLOCKS == 0, BATCH, j // SEQ_BLOCKS)


def _ada_kernel(c_ref, w_ref, b_ref, o_ref):
    c = c_ref[...]
    s_hi, s_lo = _split2(_silu(c))
    w_hi, w_lo = _split2(w_ref[...])
    o_ref[...] = _dot(s_hi, w_hi) + _dot(s_lo, w_hi) + _dot(s_hi, w_lo) + b_ref[...]


def _ada_table(cvec, w_ada, b_ada):
    tn = 1024
    return pl.pallas_call(
        _ada_kernel,
        grid=(DEPTH, 6 * D // tn),
        in_specs=[pl.BlockSpec((8, D), lambda l, j: (0, 0)),
                  pl.BlockSpec((None, D, tn), lambda l, j: (l, 0, j)),
                  pl.BlockSpec((None, 1, tn), lambda l, j: (l, 0, j))],
        out_specs=pl.BlockSpec((None, 8, tn), lambda l, j: (l, 0, j)),
        out_shape=jax.ShapeDtypeStruct((DEPTH, 8, 6 * D), F32),
        compiler_params=_cparams(2),
        name="ada_table",
    )(cvec, w_ada, b_ada.reshape(DEPTH, 1, 6 * D))


def _norm_mod_small(x, g_ref, sh_ref, sc_ref, wsh_ref, wsl_ref, h_ref, small_ref):
    h = _rms(x) * g_ref[...]
    h = h * (1.0 + sc_ref[...]) + sh_ref[...]
    h_hi, h_lo = _split2(h)
    h_ref[...] = h_hi
    small_ref[...] = _dot(h_hi, wsh_ref[...]) + _dot(h_lo, wsh_ref[...]) + _dot(h_hi, wsl_ref[...])


def _norm_kernel(x_ref, g_ref, sh_ref, sc_ref, wsh_ref, wsl_ref, h_ref, small_ref):
    _norm_mod_small(x_ref[...], g_ref, sh_ref, sc_ref, wsh_ref, wsl_ref, h_ref, small_ref)


def _moe_residual(x_ref, ya_ref, yb_ref, rt_ref, gate_ref):
    y = rt_ref[:, 2:3] * ya_ref[...].astype(F32) + rt_ref[:, 3:4] * yb_ref[...].astype(F32)
    return x_ref[...] + gate_ref[...] * y


def _resnorm_kernel(x_ref, ya_ref, yb_ref, rt_ref, gate_ref, g_ref, sh_ref, sc_ref, wsh_ref, wsl_ref,
                    xo_ref, h_ref, small_ref):
    x = _moe_residual(x_ref, ya_ref, yb_ref, rt_ref, gate_ref)
    xo_ref[...] = x
    _norm_mod_small(x, g_ref, sh_ref, sc_ref, wsh_ref, wsl_ref, h_ref, small_ref)


def _mod_spec(chunk):
    return pl.BlockSpec((None, 1, D), lambda j: (_mod_row(j), 0, chunk))


def _tok_spec(width=D, col=0):
    return pl.BlockSpec((TB, width), lambda j: (j, col))


def _const_spec(shape):
    nd = len(shape)
    return pl.BlockSpec(shape, lambda j: (0,) * nd)


def _norm_call(x, g, mods3, ws_hi, ws_lo):
    return pl.pallas_call(
        _norm_kernel,
        grid=(N_BLOCKS,),
        in_specs=[_tok_spec(), _const_spec((1, D)), _mod_spec(0), _mod_spec(1),
                  _const_spec((D, SMALL)), _const_spec((D, SMALL))],
        out_specs=[_tok_spec(), _tok_spec(SMALL)],
        out_shape=[jax.ShapeDtypeStruct((N_TOK, D), BF16), jax.ShapeDtypeStruct((N_TOK, SMALL), F32)],
        compiler_params=_cparams(1),
        name="norm_mod",
    )(x, g, mods3, mods3, ws_hi, ws_lo)


def _resnorm_call(x, ya, yb, route, mods3_prev, g, mods3, ws_hi, ws_lo):
    return pl.pallas_call(
        _resnorm_kernel,
        grid=(N_BLOCKS,),
        in_specs=[_tok_spec(), _tok_spec(), _tok_spec(), _tok_spec(SMALL),
                  pl.BlockSpec((None, 1, D), lambda j: (_mod_row(j), 0, 5)),
                  _const_spec((1, D)), _mod_spec(0), _mod_spec(1),
                  _const_spec((D, SMALL)), _const_spec((D, SMALL))],
        out_specs=[_tok_spec(), _tok_spec(), _tok_spec(SMALL)],
        out_shape=[jax.ShapeDtypeStruct((N_TOK, D), F32), jax.ShapeDtypeStruct((N_TOK, D), BF16),
                   jax.ShapeDtypeStruct((N_TOK, SMALL), F32)],
        compiler_params=_cparams(1),
        name="residual_norm_mod",
    )(x, ya, yb, route, mods3_prev, g, mods3, mods3, ws_hi, ws_lo)


def _final_kernel(x_ref, ya_ref, yb_ref, rt_ref, gate_ref, g_ref, o_ref):
    o_ref[...] = _rms(_moe_residual(x_ref, ya_ref, yb_ref, rt_ref, gate_ref)) * g_ref[...]


def _final_call(x, ya, yb, route, mods3, g_final):
    lat = lambda b, s: (b * SEQ_BLOCKS + 1 + s, 0)
    spec = pl.BlockSpec((TB, D), lat)
    return pl.pallas_call(
        _final_kernel,
        grid=(BATCH, SEQ // TB),
        in_specs=[spec, spec, spec, pl.BlockSpec((TB, SMALL), lat),
                  pl.BlockSpec((None, 1, D), lambda b, s: (b, 0, 5)),
                  pl.BlockSpec((1, D), lambda b, s: (0, 0))],
        out_specs=pl.BlockSpec((TB, D), lambda b, s: (b * (SEQ // TB) + s, 0)),
        out_shape=jax.ShapeDtypeStruct((BATCH * SEQ, D), F32),
        compiler_params=_cparams(2),
        name="final_norm",
    )(x, ya, yb, route, mods3, g_final)


def _mm_kernel(h_ref, w_ref, o_ref):
    o_ref[...] = _dot(h_ref[...], w_ref[...]).astype(o_ref.dtype)


def _mm_call(h, w, name):
    n = w.shape[1]
    tm, tn = 1024, 512
    return pl.pallas_call(
        _mm_kernel,
        grid=(n // tn, N_TOK // tm),
        in_specs=[pl.BlockSpec((tm, D), lambda j, i: (i, 0)),
                  pl.BlockSpec((D, tn), lambda j, i: (0, j))],
        out_specs=pl.BlockSpec((tm, tn), lambda j, i: (i, j)),
        out_shape=jax.ShapeDtypeStruct((N_TOK, n), BF16),
        compiler_params=_cparams(2),
        name=name,
    )(h, w)


CONV_HALO = 16


def _conv_kernel(x_ref, w_ref, b_ref, post_ref, o_ref):
    w = w_ref[...]
    bias = b_ref[...]
    post = post_ref[...]
    row = lax.broadcasted_iota(jnp.int32, (TB, 1), 0)
    for k in range(SEQ_BLOCKS):
        r0 = k * TB
        lo = max(r0 - CONV_HALO, 0)
        hi = min(r0 + TB + CONV_HALO, T_BATCH)
        off = r0 - lo
        n = hi - lo
        ext = x_ref[lo:hi, :].astype(F32)
        xm2 = pltpu.roll(ext, 2, 0)[off:off + TB]
        xm1 = pltpu.roll(ext, 1, 0)[off:off + TB]
        x0 = ext[off:off + TB]
        xp1 = pltpu.roll(ext, n - 1, 0)[off:off + TB]
        if k in (0, 1):
            xm2 = jnp.where(row >= 2, xm2, 0.0)
            xm1 = jnp.where(row >= 1, xm1, 0.0)
        if k in (0, SEQ_BLOCKS - 1):
            xp1 = jnp.where(row <= TB - 2, xp1, 0.0)
        y = bias + w[0:1] * xm2 + w[1:2] * xm1 + w[2:3] * x0 + w[3:4] * xp1
        o_ref[r0:r0 + TB, :] = (_silu(y) * post).astype(o_ref.dtype)


def _conv_call(u, col0, width, w, b, post, name):
    tc = 512
    cb0 = col0 // tc
    return pl.pallas_call(
        _conv_kernel,
        grid=(BATCH, width // tc),
        in_specs=[pl.BlockSpec((T_BATCH, tc), lambda bi, c: (bi, cb0 + c)),
                  pl.BlockSpec((4, tc), lambda bi, c: (0, c)),
                  pl.BlockSpec((1, tc), lambda bi, c: (0, c)),
                  pl.BlockSpec((1, tc), lambda bi, c: (0, c))],
        out_specs=pl.BlockSpec((T_BATCH, tc), lambda bi, c: (bi, c)),
        out_shape=jax.ShapeDtypeStruct((N_TOK, width), BF16),
        compiler_params=_cparams(2),
        name=name,
    )(u, w, b, post)


def _tri_dot_cols(tri, x):
    a, b, c = _split3(x)
    return _dot(tri, a) + _dot(tri, b) + _dot(tri, c)


def _tri_dot_rows(x, tri):
    a, b, c = _split3(x)
    return _dot(a, tri) + _dot(b, tri) + _dot(c, tri)


def _gate_kernel(pre_ref, pret_ref, bc_ref, br_ref, ac_ref, ar_ref, bwdc_ref, bwdr_ref, tril_ref, triu_ref,
                 valc_ref, cumc_ref, valr_ref, cumr_ref, *, ssd):
    tril = tril_ref[...]
    triu = triu_ref[...]

    def act(v):
        if ssd:
            val = _softplus(v)
            return val, val
        return v, _log_sigmoid(v)

    vc, dc = act(pre_ref[...] + bc_ref[...])
    dc = dc * ac_ref[...]
    valc_ref[...] = vc
    cumc_ref[...] = jnp.where(bwdc_ref[...] > 0.5, _tri_dot_cols(triu, dc), _tri_dot_cols(tril, dc))

    vr, dr = act(pret_ref[...] + br_ref[...])
    dr = dr * ar_ref[...]
    valr_ref[...] = vr
    cumr_ref[...] = jnp.where(bwdr_ref[...] > 0.5, _tri_dot_rows(dr, tril), _tri_dot_rows(dr, triu))


def _gate_call(small, bias, scale, bwd, ssd, name):
    small_t = small.reshape(N_BLOCKS, TB, SMALL)[:, :, :64].transpose(0, 2, 1)
    tri = np.tril(np.ones((TB, TB), np.float32))
    tril = jnp.asarray(tri, BF16)
    triu = jnp.asarray(tri.T, BF16)
    col = lambda v: v.reshape(1, SMALL).astype(F32)
    rowv = lambda v: v[:64].reshape(64, 1).astype(F32)
    cs = lambda shape: pl.BlockSpec(shape, lambda j: (0,) * len(shape))
    return pl.pallas_call(
        functools.partial(_gate_kernel, ssd=ssd),
        grid=(N_BLOCKS,),
        in_specs=[_tok_spec(SMALL), pl.BlockSpec((None, 64, TB), lambda j: (j, 0, 0)),
                  cs((1, SMALL)), cs((64, 1)), cs((1, SMALL)), cs((64, 1)), cs((1, SMALL)), cs((64, 1)),
                  cs((TB, TB)), cs((TB, TB))],
        out_specs=[_tok_spec(SMALL), _tok_spec(SMALL),
                   pl.BlockSpec((None, 64, TB), lambda j: (j, 0, 0)),
                   pl.BlockSpec((None, 64, TB), lambda j: (j, 0, 0))],
        out_shape=[jax.ShapeDtypeStruct((N_TOK, SMALL), F32), jax.ShapeDtypeStruct((N_TOK, SMALL), F32),
                   jax.ShapeDtypeStruct((N_BLOCKS, 64, TB), F32), jax.ShapeDtypeStruct((N_BLOCKS, 64, TB), F32)],
        compiler_params=_cparams(1),
        name=name,
    )(small, small_t, col(bias), rowv(bias), col(scale), rowv(scale), col(bwd), rowv(bwd), tril, triu)


def _scan_block(d):
    if d == 0:
        return lambda b, s: b * SEQ_BLOCKS + s
    return lambda b, s: b * SEQ_BLOCKS + jnp.where(s == 0, 0, SEQ_BLOCKS - s)


def _causal_mask(d):
    t = lax.broadcasted_iota(jnp.int32, (TB, TB), 0)
    s = lax.broadcasted_iota(jnp.int32, (TB, TB), 1)
    return (s <= t) if d == 0 else (s >= t)


def _mlstm_kernel(q_ref, k_ref, v_ref, valc_ref, cumc_ref, valr_ref, cumr_ref, o_ref,
                  c_ref, n_ref, m_ref, *, d):
    @pl.when(pl.program_id(1) == 0)
    def _():
        c_ref[...] = jnp.zeros_like(c_ref)
        n_ref[...] = jnp.zeros_like(n_ref)
        m_ref[...] = jnp.zeros_like(m_ref)

    mask = _causal_mask(d)
    end = TB - 1 if d == 0 else 0
    for h in range(HEADS_M):
        p = d * HEADS_M + h
        cs = slice(h * DH_M, (h + 1) * DH_M)
        q = q_ref[:, cs]
        k = k_ref[:, cs]
        v = v_ref[:, cs]
        i_c = valc_ref[:, p:p + 1]
        cum_c = cumc_ref[:, 8 + p:9 + p]
        i_r = valr_ref[p:p + 1, :]
        cum_r = cumr_ref[8 + p:9 + p, :]
        total = cum_r[:, end:end + 1]
        m_prev = m_ref[h]
        c_st = c_ref[h]
        n_st = n_ref[h]

        dmat = jnp.where(mask, cum_c - cum_r + i_r, -jnp.inf)
        g = cum_c + m_prev
        m_t = jnp.maximum(g, jnp.max(dmat, axis=1, keepdims=True))
        w = jnp.exp(dmat - m_t)
        sg = jnp.exp(g - m_t)
        a = w * lax.dot_general(q, k, (((1,), (1,)), ((), ())), preferred_element_type=F32)
        num = _dot(a.astype(BF16), v) + sg * _dot(q, c_st.astype(BF16))
        den = jnp.sum(a, axis=1, keepdims=True) + sg * jnp.sum(q.astype(F32) * n_st, axis=1, keepdims=True)
        o_ref[:, cs] = (num / jnp.maximum(jnp.abs(den), jnp.exp(-m_t))).astype(o_ref.dtype)

        wl_log = total - cum_c + i_c
        m_new = jnp.maximum(total + m_prev, jnp.max(wl_log, axis=0, keepdims=True))
        wl = jnp.exp(wl_log - m_new)
        decay = jnp.exp(total + m_prev - m_new)
        kf = k.astype(F32)
        wv = (wl * v.astype(F32)).astype(BF16)
        c_ref[h] = decay * c_st + lax.dot_general(k, wv, (((0,), (0,)), ((), ())), preferred_element_type=F32)
        n_ref[h] = decay * n_st + jnp.sum(wl * kf, axis=0, keepdims=True)
        m_ref[h] = m_new


def _mlstm_call(qk, ucm, valc, cumc, valr, cumr, d):
    blk = _scan_block(d)
    tok = lambda col: pl.BlockSpec((TB, D), lambda b, s: (blk(b, s), col))
    small = pl.BlockSpec((TB, SMALL), lambda b, s: (blk(b, s), 0))
    rows = pl.BlockSpec((None, 64, TB), lambda b, s: (blk(b, s), 0, 0))
    return pl.pallas_call(
        functools.partial(_mlstm_kernel, d=d),
        grid=(BATCH, SEQ_BLOCKS),
        in_specs=[tok(0), tok(1), tok(2), small, small, rows, rows],
        out_specs=tok(0),
        out_shape=jax.ShapeDtypeStruct((N_TOK, D), BF16),
        scratch_shapes=[pltpu.VMEM((HEADS_M, DH_M, DH_M), F32), pltpu.VMEM((HEADS_M, 1, DH_M), F32),
                        pltpu.VMEM((HEADS_M, 1, 1), F32)],
        compiler_params=_cparams(2),
        name="mlstm_fwd" if d == 0 else "mlstm_bwd",
    )(qk, qk, ucm, valc, cumc, valr, cumr)


def _expand_heads(cols):
    lane = lax.broadcasted_iota(jnp.int32, (TB, 4 * SSD_P), 1)
    return jnp.where(lane < SSD_P, cols[0],
                     jnp.where(lane < 2 * SSD_P, cols[1], jnp.where(lane < 3 * SSD_P, cols[2], cols[3])))


def _ssd_kernel(x_ref, b_ref, c_ref, dtc_ref, acsc_ref, acsr_ref, o_ref, s_ref, *, d):
    @pl.when(pl.program_id(1) == 0)
    def _():
        s_ref[...] = jnp.zeros_like(s_ref)

    mask = _causal_mask(d)
    end = TB - 1 if d == 0 else 0
    lane = lax.broadcasted_iota(jnp.int32, (TB, 4 * SSD_P), 1)
    lane1 = lax.broadcasted_iota(jnp.int32, (1, 4 * SSD_P), 1)
    for g in range(SSD_GROUPS):
        xg = x_ref[:, g * 256:(g + 1) * 256].astype(F32)
        bg = b_ref[:, g * SSD_N:(g + 1) * SSD_N]
        cg = c_ref[:, g * SSD_N:(g + 1) * SSD_N]
        s_prev = s_ref[g]
        lanes = [SSD_HEADS + d * SSD_HEADS + g * 4 + hh for hh in range(4)]
        dt_cols = [dtc_ref[:, p:p + 1] for p in lanes]
        acs_cols = [acsc_ref[:, p:p + 1] for p in lanes]
        acs_rows = [acsr_ref[p:p + 1, :] for p in lanes]
        tots = [r[:, end:end + 1] for r in acs_rows]

        xdt = xg * _expand_heads(dt_cols)
        xdt_b = xdt.astype(BF16)
        cb = lax.dot_general(cg, bg, (((1,), (1,)), ((), ())), preferred_element_type=F32)
        y = jnp.zeros((TB, 4 * SSD_P), F32)
        for hh in range(4):
            lm = jnp.exp(jnp.where(mask, acs_cols[hh] - acs_rows[hh], -jnp.inf))
            yh = _dot((cb * lm).astype(BF16), xdt_b)
            y = jnp.where((lane >= hh * SSD_P) & (lane < (hh + 1) * SSD_P), yh, y)
        y = y + _dot(cg, s_prev.astype(BF16)) * _expand_heads([jnp.exp(a) for a in acs_cols])
        o_ref[:, g * 256:(g + 1) * 256] = y.astype(o_ref.dtype)

        to_end = _expand_heads([jnp.exp(tots[hh] - acs_cols[hh]) for hh in range(4)])
        st = lax.dot_general(bg, (xdt * to_end).astype(BF16), (((0,), (0,)), ((), ())),
                             preferred_element_type=F32)
        cd = [jnp.exp(t) for t in tots]
        chunk_decay = jnp.where(lane1 < SSD_P, cd[0],
                                jnp.where(lane1 < 2 * SSD_P, cd[1], jnp.where(lane1 < 3 * SSD_P, cd[2], cd[3])))
        s_ref[g] = chunk_decay * s_prev + st


def _ssd_call(xbc, dtc, acsc, acsr, d):
    blk = _scan_block(d)
    small = pl.BlockSpec((TB, SMALL), lambda b, s: (blk(b, s), 0))
    rows = pl.BlockSpec((None, 64, TB), lambda b, s: (blk(b, s), 0, 0))
    return pl.pallas_call(
        functools.partial(_ssd_kernel, d=d),
        grid=(BATCH, SEQ_BLOCKS),
        in_specs=[pl.BlockSpec((TB, D), lambda b, s: (blk(b, s), 0)),
                  pl.BlockSpec((TB, 512), lambda b, s: (blk(b, s), 2)),
                  pl.BlockSpec((TB, 512), lambda b, s: (blk(b, s), 3)),
                  small, small, rows],
        out_specs=pl.BlockSpec((TB, D), lambda b, s: (blk(b, s), 0)),
        out_shape=jax.ShapeDtypeStruct((N_TOK, D), BF16),
        scratch_shapes=[pltpu.VMEM((SSD_GROUPS, SSD_N, 4 * SSD_P), F32)],
        compiler_params=_cparams(2),
        name="ssd_fwd" if d == 0 else "ssd_bwd",
    )(xbc, xbc, xbc, dtc, acsc, acsr)


def _pool_tables():
    masks = np.zeros((2, 4, TB, TB), np.float32)
    inv = np.zeros((2, 4, TB, 1), np.float32)
    for kind, length in ((0, GRID_W), (1, CTX)):
        for gi, w in enumerate(POOL_WINDOWS):
            for r in range(TB):
                base, c = (r // length) * length, r % length
                lo = min(max(c - w // 2, 0), length - 1)
                hi = min(max(c - w // 2 + w - 1, 0), length - 1)
                masks[kind, gi, r, base + lo:base + hi + 1] = 1.0
                inv[kind, gi, r, 0] = 1.0 / (hi - lo + 1)
    return jnp.asarray(masks, BF16), jnp.asarray(inv, F32)


ROUTE_FIELDS = 6


def _route_block(logits, ltri_ref, rt_ref, cnt_ref, cnt_scr):
    @pl.when(pl.program_id(0) == 0)
    def _():
        cnt_scr[...] = jnp.zeros_like(cnt_scr)

    lane = lax.broadcasted_iota(jnp.int32, logits.shape, 1).astype(F32)

    def top1(valid):
        v = jnp.max(jnp.where(valid, logits, -jnp.inf), axis=1, keepdims=True)
        i = jnp.min(jnp.where(valid & (logits == v), lane, float(SMALL)), axis=1, keepdims=True)
        return v, i

    is_g = lane < N_GROUPS_E
    gm, grp = top1(is_g)
    p_grp = 1.0 / jnp.sum(jnp.where(is_g, jnp.exp(logits - gm), 0.0), axis=1, keepdims=True)
    lo_lane = N_GROUPS_E + grp * EPG
    in_grp = (lane >= lo_lane) & (lane < lo_lane + EPG)
    v1, i1 = top1(in_grp)
    v2, i2 = top1(in_grp & (lane != i1))
    t = jnp.exp(v2 - v1)
    w1 = p_grp / (1.0 + t)
    w2 = p_grp * t / (1.0 + t)
    e1 = i1 - N_GROUPS_E
    e2 = i2 - N_GROUPS_E

    oh1 = lane == e1
    oh2 = lane == e2
    oh = jnp.where(oh1 | oh2, 1.0, 0.0)
    before = _dot(ltri_ref[...], oh.astype(BF16)) + cnt_scr[...]
    r1 = jnp.sum(jnp.where(oh1, before, 0.0), axis=1, keepdims=True)
    r2 = jnp.sum(jnp.where(oh2, before, 0.0), axis=1, keepdims=True)
    cnt_scr[...] = cnt_scr[...] + jnp.sum(oh, axis=0, keepdims=True)
    cnt_ref[...] = cnt_scr[...]
    rt = jnp.zeros(logits.shape, F32)
    for k, val in enumerate((e1, e2, w1, w2, r1, r2)):
        rt = jnp.where(lane == k, val, rt)
    rt_ref[...] = rt


def _merge_kernel(x_ref, pa_ref, o_ref, z_ref, mg0_ref, mg1_ref, mg2_ref, hf_ref, hb_ref, yf_ref, yb_ref, xs_ref,
                  pmask_ref, pinv_ref, poolw_ref, pscale_ref, mng_ref, dsk_ref, sng_ref, wbr_ref, wout_ref,
                  gate_ref, sh_ref, sc_ref, gffn_ref, wrh_ref, wrl_ref, brt_ref, ltri_ref,
                  xo_ref, h2_ref, rt_ref, cnt_ref, cnt_scr):
    parts = []
    for g in range(4):
        a_g = pa_ref[:, g * 256:(g + 1) * 256]
        pooled = _dot(pmask_ref[g], a_g) * pinv_ref[g] - a_g.astype(F32)
        parts.append(_dot(pooled.astype(BF16), poolw_ref[g]))
    pool = jnp.concatenate(parts, axis=1) * pscale_ref[...]

    hs = hf_ref[...].astype(F32) + hb_ref[...].astype(F32)
    hn = jnp.concatenate([_rms(hs[:, h * DH_M:(h + 1) * DH_M]) for h in range(HEADS_M)], axis=1)
    ml = _sigmoid(o_ref[...].astype(F32)) * (hn * mng_ref[...])

    y = yf_ref[...].astype(F32) + yb_ref[...].astype(F32) + dsk_ref[...] * xs_ref[...].astype(F32)
    sl = _rms(y * _silu(z_ref[...].astype(F32))) * sng_ref[...]

    acc = _sigmoid(mg0_ref[...].astype(F32)) * _dot(pool.astype(BF16), wbr_ref[0])
    acc = acc + _sigmoid(mg1_ref[...].astype(F32)) * _dot(ml.astype(BF16), wbr_ref[1])
    acc = acc + _sigmoid(mg2_ref[...].astype(F32)) * _dot(sl.astype(BF16), wbr_ref[2])
    xn = x_ref[...] + gate_ref[...] * _dot(acc.astype(BF16), wout_ref[...])
    xo_ref[...] = xn

    h2 = (_rms(xn) * gffn_ref[...]) * (1.0 + sc_ref[...]) + sh_ref[...]
    hi, lo = _split2(h2)
    h2_ref[...] = hi.astype(F32)
    logits = _dot(hi, wrh_ref[...]) + _dot(lo, wrh_ref[...]) + _dot(hi, wrl_ref[...]) + brt_ref[...]
    _route_block(logits, ltri_ref, rt_ref, cnt_ref, cnt_scr)


def _merge_call(x, urm, hf, hb, yf, yb, xbc, pmask, pinv, poolw, pscale, mng, dsk, sng, wbr, wout,
                mods3, gffn, wr_hi, wr_lo, br):
    ltri = jnp.asarray(np.tril(np.ones((TB, TB), np.float32), -1), BF16)
    kind = lambda j: jnp.where(j % SEQ_BLOCKS == 0, 1, 0)
    vec = _const_spec((1, D))
    return pl.pallas_call(
        _merge_kernel,
        grid=(N_BLOCKS,),
        in_specs=[_tok_spec(),
                  _tok_spec(D, 0), _tok_spec(D, 1), _tok_spec(D, 4),
                  _tok_spec(D, 5), _tok_spec(D, 6), _tok_spec(D, 7),
                  _tok_spec(), _tok_spec(), _tok_spec(), _tok_spec(),
                  _tok_spec(D, 0),
                  pl.BlockSpec((None, 4, TB, TB), lambda j: (kind(j), 0, 0, 0)),
                  pl.BlockSpec((None, 4, TB, 1), lambda j: (kind(j), 0, 0, 0)),
                  _const_spec((4, 256, 256)), vec, vec, vec, vec,
                  _const_spec((3, D, D)), _const_spec((D, D)),
                  _mod_spec(2), _mod_spec(3), _mod_spec(4), vec,
                  _const_spec((D, SMALL)), _const_spec((D, SMALL)), _const_spec((1, SMALL)),
                  _const_spec((TB, TB))],
        out_specs=[_tok_spec(), _tok_spec(), _tok_spec(SMALL), _const_spec((1, SMALL))],
        out_shape=[jax.ShapeDtypeStruct((N_TOK, D), F32), jax.ShapeDtypeStruct((N_TOK, D), F32),
                   jax.ShapeDtypeStruct((N_TOK, SMALL), F32), jax.ShapeDtypeStruct((1, SMALL), F32)],
        scratch_shapes=[pltpu.VMEM((1, SMALL), F32)],
        compiler_params=_cparams(1),
        name="branch_merge",
    )(x, urm, urm, urm, urm, urm, urm, hf, hb, yf, yb, xbc, pmask, pinv, poolw, pscale, mng, dsk, sng,
      wbr, wout, mods3, mods3, mods3, gffn, wr_hi, wr_lo, br, ltri)


def _moe_kernel(be_ref, nu_ref, tok_ref, h_hbm, wg_ref, wu_ref, wd_ref, o_ref, xbuf, sem):
    i = pl.program_id(0)
    n_used = nu_ref[0]
    slot = i % 2

    def gather_copy(block, s, r):
        tok = tok_ref[block * MOE_BLOCK + r]
        return pltpu.make_async_copy(h_hbm.at[pl.ds(tok, 1)], xbuf.at[s, pl.ds(r, 1)], sem.at[s])

    def start_gather(block, s):
        for r in range(MOE_BLOCK):
            gather_copy(block, s, r).start()

    @pl.when((i == 0) & (n_used > 0))
    def _():
        start_gather(0, 0)

    @pl.when(i + 1 < n_used)
    def _():
        start_gather(i + 1, 1 - slot)

    @pl.when(i < n_used)
    def _():
        pltpu.make_async_copy(xbuf.at[slot], xbuf.at[slot], sem.at[slot]).wait()
        x = xbuf[slot].astype(BF16)
        gt = _dot(x, wg_ref[...].astype(BF16))
        up = _dot(x, wu_ref[...].astype(BF16))
        act = (_silu(gt) * up).astype(BF16)
        o_ref[...] = _dot(act, wd_ref[...].astype(BF16)).astype(o_ref.dtype)

    @pl.when(i >= nu_ref[0])
    def _():
        o_ref[...] = jnp.zeros_like(o_ref)


MOE_CAP = N_TOK * 2 + N_EXPERTS * MOE_BLOCK
MOE_NBLOCKS = MOE_CAP // MOE_BLOCK


def _moe_call(layer, block_expert, n_used, buf_tok, h2, w_gate, w_up, w_down):
    grid_spec = pltpu.PrefetchScalarGridSpec(
        num_scalar_prefetch=3,
        grid=(MOE_NBLOCKS,),
        in_specs=[pl.BlockSpec(memory_space=pl.ANY),
                  pl.BlockSpec((None, None, D, D_EXPERT), lambda i, be, nu, bt: (layer, be[i], 0, 0)),
                  pl.BlockSpec((None, None, D, D_EXPERT), lambda i, be, nu, bt: (layer, be[i], 0, 0)),
                  pl.BlockSpec((None, None, D_EXPERT, D), lambda i, be, nu, bt: (layer, be[i], 0, 0))],
        out_specs=pl.BlockSpec((MOE_BLOCK, D), lambda i, be, nu, bt: (i, 0)),
        scratch_shapes=[pltpu.VMEM((2, MOE_BLOCK, D), F32), pltpu.SemaphoreType.DMA((2,))],
    )
    return pl.pallas_call(
        _moe_kernel,
        grid_spec=grid_spec,
        out_shape=jax.ShapeDtypeStruct((MOE_CAP, D), BF16),
        compiler_params=_cparams(1),
        name="moe_experts",
    )(block_expert, n_used, buf_tok, h2, w_gate, w_up, w_down)


def _dispatch_plan(route, counts):
    cnt = counts[0, :N_EXPERTS].astype(jnp.int32)
    padded = (cnt + MOE_BLOCK - 1) // MOE_BLOCK * MOE_BLOCK
    pends = jnp.cumsum(padded)
    pstarts = pends - padded
    expert = route[:, 0:2].astype(jnp.int32)
    rank = route[:, 4:6].astype(jnp.int32)
    onehot = expert[:, :, None] == jnp.arange(N_EXPERTS, dtype=jnp.int32)
    pos = jnp.sum(jnp.where(onehot, pstarts, 0), axis=-1) + rank
    tok = jnp.broadcast_to(jnp.arange(N_TOK, dtype=jnp.int32)[:, None], (N_TOK, 2))
    buf_tok = jnp.zeros((MOE_CAP,), jnp.int32).at[pos.reshape(-1)].set(tok.reshape(-1))
    block_start = jnp.arange(MOE_NBLOCKS, dtype=jnp.int32) * MOE_BLOCK
    block_expert = jnp.minimum(jnp.sum((pends[None, :] <= block_start[:, None]).astype(jnp.int32), axis=1),
                               N_EXPERTS - 1)
    n_used = (pends[-1] // MOE_BLOCK).reshape(1)
    return buf_tok, block_expert, n_used, pos


def _to_colmajor(t):
    c = t.shape[-1]
    t = t.reshape(BATCH, T_BATCH, c)
    lat = t[:, CTX:].reshape(BATCH, SEQ // GRID_W, GRID_W, c).swapaxes(1, 2).reshape(BATCH, SEQ, c)
    return jnp.concatenate([t[:, :CTX], lat], axis=1).reshape(N_TOK, c)


def _to_rowmajor(t):
    c = t.shape[-1]
    t = t.reshape(BATCH, T_BATCH, c)
    lat = t[:, CTX:].reshape(BATCH, GRID_W, SEQ // GRID_W, c).swapaxes(1, 2).reshape(BATCH, SEQ, c)
    return jnp.concatenate([t[:, :CTX], lat], axis=1).reshape(N_TOK, c)


def _lanes(*pieces):
    v = jnp.concatenate([jnp.asarray(p, F32).reshape(-1) for p in pieces])
    return jnp.pad(v, (0, SMALL - v.shape[0]))


def kernel(x, c, ctx, c_ctx, w_ada, b_ada, g_norm_mix, g_norm_ffn, w_in, pool_w, pool_scale, mlstm_conv_w,
           mlstm_conv_b, mlstm_gate_b, mlstm_norm_g, ssd_conv_w, ssd_conv_b, ssd_dt_bias, ssd_a_log, ssd_d,
           ssd_norm_g, w_branch, w_out, w_route_group, b_route_group, w_route_expert, b_route_expert,
           w_exp_gate, w_exp_up, w_exp_down, g_final):
    cvec = jnp.concatenate([c, c_ctx[None], jnp.zeros((3, D), F32)], axis=0)
    mods = _ada_table(cvec, w_ada, b_ada)
    xs = jnp.concatenate([ctx, x], axis=1).reshape(N_TOK, D)
    pmask, pinv = _pool_tables()
    row = lambda v: v.reshape(1, -1).astype(F32)
    w_in_bf = w_in.astype(BF16)

    h = small = None
    out = None
    for l in range(DEPTH):
        mods3 = mods[l].reshape(8, 1, 6 * D)
        wi = w_in_bf[l]
        sp = np.cumsum([0, D, D, D, D, D, 16, D, D, 512, 512, 32, 3 * D])
        piece = lambda i: wi[:, sp[i]:sp[i + 1]]
        w_rm = jnp.concatenate([piece(0), piece(4), piece(6), piece(8), piece(9), piece(7), piece(11)],
                               axis=1)
        w_cm = wi[:, sp[1]:sp[4]]
        w_small = jnp.pad(jnp.concatenate([w_in[l][:, sp[5]:sp[6]], w_in[l][:, sp[10]:sp[11]]], axis=1),
                          ((0, 0), (0, SMALL - 48)))
        ws_hi = w_small.astype(BF16)
        ws_lo = (w_small - ws_hi.astype(F32)).astype(BF16)

        if l == 0:
            h, small = _norm_call(xs, row(g_norm_mix[l]), mods3, ws_hi, ws_lo)

        h_cm = _to_colmajor(h)
        small_cm = _to_colmajor(small)
        u_rm = _mm_call(h, w_rm, "in_proj_rowmajor")
        u_cm = _mm_call(h_cm, w_cm, "in_proj_colmajor")

        post_m = jnp.concatenate([jnp.ones((D,), F32), jnp.full((D,), DH_M ** -0.5, F32)]).reshape(1, 2 * D)
        qk = _conv_call(u_cm, 0, 2 * D, mlstm_conv_w[l], row(mlstm_conv_b[l]), post_m, "mlstm_conv")
        gate_bias = _lanes(mlstm_gate_b[l])
        gate_bwd = _lanes(jnp.zeros((12,)), jnp.ones((4,)))
        g_valc, g_cumc, g_valr, g_cumr = _gate_call(small_cm, gate_bias, jnp.ones((SMALL,), F32), gate_bwd,
                                                    False, "mlstm_gates")
        h_f = _mlstm_call(qk, u_cm, g_valc, g_cumc, g_valr, g_cumr, 0)
        h_b = _mlstm_call(qk, u_cm, g_valc, g_cumc, g_valr, g_cumr, 1)
        h_f = _to_rowmajor(h_f)
        h_b = _to_rowmajor(h_b)

        xbc = _conv_call(u_rm, 2 * D, 2 * D, ssd_conv_w[l], row(ssd_conv_b[l]), jnp.ones((1, 2 * D), F32),
                         "ssd_conv")
        a_neg = -jnp.exp(ssd_a_log[l].astype(F32))
        dt_bias = _lanes(jnp.zeros((16,)), ssd_dt_bias[l])
        dt_scale = _lanes(jnp.zeros((16,)), a_neg)
        dt_bwd = _lanes(jnp.zeros((32,)), jnp.ones((16,)))
        s_dtc, s_acsc, _, s_acsr = _gate_call(small, dt_bias, dt_scale, dt_bwd, True, "ssd_gates")
        y_f = _ssd_call(xbc, s_dtc, s_acsc, s_acsr, 0)
        y_b = _ssd_call(xbc, s_dtc, s_acsc, s_acsr, 1)

        w_r = jnp.pad(jnp.concatenate([w_route_group[l], w_route_expert[l]], axis=1),
                      ((0, 0), (0, SMALL - N_GROUPS_E - N_EXPERTS)))
        wr_hi = w_r.astype(BF16)
        wr_lo = (w_r - wr_hi.astype(F32)).astype(BF16)
        b_r = _lanes(b_route_group[l], b_route_expert[l]).reshape(1, SMALL)
        dsk = jnp.repeat(ssd_d[l].astype(F32), SSD_P).reshape(1, D)
        xs, h2, route, counts = _merge_call(
            xs, u_rm, h_f, h_b, y_f, y_b, xbc, pmask, pinv, pool_w[l].astype(BF16), row(pool_scale[l]),
            row(mlstm_norm_g[l]), dsk, row(ssd_norm_g[l]), w_branch[l].astype(BF16), w_out[l].astype(BF16),
            mods3, row(g_norm_ffn[l]), wr_hi, wr_lo, b_r)

        buf_tok, block_expert, n_used, pos = _dispatch_plan(route, counts)
        yb = _moe_call(l, block_expert, n_used, buf_tok, h2, w_exp_gate, w_exp_up, w_exp_down)
        y0 = yb[pos[:, 0]]
        y1 = yb[pos[:, 1]]

        if l + 1 < DEPTH:
            wi_n = w_in[l + 1]
            w_small_n = jnp.pad(jnp.concatenate([wi_n[:, sp[5]:sp[6]], wi_n[:, sp[10]:sp[11]]], axis=1),
                                ((0, 0), (0, SMALL - 48)))
            wsn_hi = w_small_n.astype(BF16)
            wsn_lo = (w_small_n - wsn_hi.astype(F32)).astype(BF16)
            mods3_n = mods[l + 1].reshape(8, 1, 6 * D)
            xs, h, small = _resnorm_call(xs, y0, y1, route, mods3, row(g_norm_mix[l + 1]), mods3_n,
                                         wsn_hi, wsn_lo)
        else:
            out = _final_call(xs, y0, y1, route, mods3, row(g_final))
    return out.reshape(BATCH, SEQ, D)
```

```python
import functools
import math

import jax
import jax.numpy as jnp
import numpy as np
from jax import lax
from jax.experimental import pallas as pl
from jax.experimental.pallas import tpu as pltpu

F32 = jnp.float32
BF16 = jnp.bfloat16

D = 1024
BATCH = 4
SEQ = 4096
CTX = 256
DEPTH = 2
GRID_W = 64
EPS = 1e-6

TB = 256
SEQ_BLOCKS = (SEQ + CTX) // TB
T_BATCH = SEQ + CTX
N_TOK = BATCH * T_BATCH
N_BLOCKS = N_TOK // TB

POOL_WINDOWS = (2, 4, 8, 16)
HEADS_M = 4
DH_M = 256
SSD_HEADS = 16
SSD_P = 64
SSD_GROUPS = 4
SSD_N = 128
N_GROUPS_E = 4
EPG = 8
N_EXPERTS = 32
D_EXPERT = 512
MOE_BLOCK = 256
SMALL = 128
ROW_TILES = D // 128

VMEM_LIMIT = 56 * 1024 * 1024


def _cparams(n_axes):
    return pltpu.CompilerParams(dimension_semantics=("arbitrary",) * n_axes,
                                vmem_limit_bytes=VMEM_LIMIT)


def _dot(a, b):
    return jnp.dot(a, b, preferred_element_type=F32)


def _split2(x):
    hi = x.astype(BF16)
    lo = (x - hi.astype(F32)).astype(BF16)
    return hi, lo


def _split3(x):
    hi = x.astype(BF16)
    r = x - hi.astype(F32)
    mid = r.astype(BF16)
    lo = (r - mid.astype(F32)).astype(BF16)
    return hi, mid, lo


def _sigmoid(x):
    return 1.0 / (1.0 + jnp.exp(-x))


def _silu(x):
    return x * _sigmoid(x)


def _log1pexp_negabs(x):
    return jnp.log(1.0 + jnp.exp(-jnp.abs(x)))


def _log_sigmoid(x):
    return jnp.minimum(x, 0.0) - _log1pexp_negabs(x)


def _softplus(x):
    return jnp.maximum(x, 0.0) + _log1pexp_negabs(x)


def _rms(x):
    return x * lax.rsqrt(jnp.mean(x * x, axis=-1, keepdims=True) + EPS)


def _mod_row(j):
    return jnp.where(j % SEQ_BLOCKS == 0, BATCH, j // SEQ_BLOCKS)


def _ada_kernel(c_ref, w_ref, b_ref, o_ref):
    c = c_ref[...]
    s_hi, s_lo = _split2(_silu(c))
    w_hi, w_lo = _split2(w_ref[...])
    o_ref[...] = _dot(s_hi, w_hi) + _dot(s_lo, w_hi) + _dot(s_hi, w_lo) + b_ref[...]


def _ada_table(cvec, w_ada, b_ada):
    tn = 1024
    return pl.pallas_call(
        _ada_kernel,
        grid=(DEPTH, 6 * D // tn),
        in_specs=[pl.BlockSpec((8, D), lambda l, j: (0, 0)),
                  pl.BlockSpec((None, D, tn), lambda l, j: (l, 0, j)),
                  pl.BlockSpec((None, 1, tn), lambda l, j: (l, 0, j))],
        out_specs=pl.BlockSpec((None, 8, tn), lambda l, j: (l, 0, j)),
        out_shape=jax.ShapeDtypeStruct((DEPTH, 8, 6 * D), F32),
        compiler_params=_cparams(2),
        name="ada_table",
    )(cvec, w_ada, b_ada.reshape(DEPTH, 1, 6 * D))


def _norm_mod_small(x, g_ref, sh_ref, sc_ref, wsh_ref, wsl_ref, h_ref, small_ref):
    h = _rms(x) * g_ref[...]
    h = h * (1.0 + sc_ref[...]) + sh_ref[...]
    h_hi, h_lo = _split2(h)
    h_ref[...] = h_hi
    small_ref[...] = _dot(h_hi, wsh_ref[...]) + _dot(h_lo, wsh_ref[...]) + _dot(h_hi, wsl_ref[...])


def _norm_kernel(x_ref, g_ref, sh_ref, sc_ref, wsh_ref, wsl_ref, h_ref, small_ref):
    _norm_mod_small(x_ref[...], g_ref, sh_ref, sc_ref, wsh_ref, wsl_ref, h_ref, small_ref)


def _moe_residual(x_ref, ya_ref, yb_ref, rt_ref, gate_ref):
    y = rt_ref[:, 2:3] * ya_ref[...].astype(F32) + rt_ref[:, 3:4] * yb_ref[...].astype(F32)
    return x_ref[...] + gate_ref[...] * y


def _resnorm_kernel(x_ref, ya_ref, yb_ref, rt_ref, gate_ref, g_ref, sh_ref, sc_ref, wsh_ref, wsl_ref,
                    xo_ref, h_ref, small_ref):
    x = _moe_residual(x_ref, ya_ref, yb_ref, rt_ref, gate_ref)
    xo_ref[...] = x
    _norm_mod_small(x, g_ref, sh_ref, sc_ref, wsh_ref, wsl_ref, h_ref, small_ref)


def _mod_spec(chunk):
    return pl.BlockSpec((None, 1, D), lambda j: (_mod_row(j), 0, chunk))


def _tok_spec(width=D, col=0):
    return pl.BlockSpec((TB, width), lambda j: (j, col))


def _const_spec(shape):
    nd = len(shape)
    return pl.BlockSpec(shape, lambda j: (0,) * nd)


def _norm_call(x, g, mods3, ws_hi, ws_lo):
    return pl.pallas_call(
        _norm_kernel,
        grid=(N_BLOCKS,),
        in_specs=[_tok_spec(), _const_spec((1, D)), _mod_spec(0), _mod_spec(1),
                  _const_spec((D, SMALL)), _const_spec((D, SMALL))],
        out_specs=[_tok_spec(), _tok_spec(SMALL)],
        out_shape=[jax.ShapeDtypeStruct((N_TOK, D), BF16), jax.ShapeDtypeStruct((N_TOK, SMALL), F32)],
        compiler_params=_cparams(1),
        name="norm_mod",
    )(x, g, mods3, mods3, ws_hi, ws_lo)


def _resnorm_call(x, ya, yb, route, mods3_prev, g, mods3, ws_hi, ws_lo):
    return pl.pallas_call(
        _resnorm_kernel,
        grid=(N_BLOCKS,),
        in_specs=[_tok_spec(), _tok_spec(), _tok_spec(), _tok_spec(SMALL),
                  pl.BlockSpec((None, 1, D), lambda j: (_mod_row(j), 0, 5)),
                  _const_spec((1, D)), _mod_spec(0), _mod_spec(1),
                  _const_spec((D, SMALL)), _const_spec((D, SMALL))],
        out_specs=[_tok_spec(), _tok_spec(), _tok_spec(SMALL)],
        out_shape=[jax.ShapeDtypeStruct((N_TOK, D), F32), jax.ShapeDtypeStruct((N_TOK, D), BF16),
                   jax.ShapeDtypeStruct((N_TOK, SMALL), F32)],
        compiler_params=_cparams(1),
        name="residual_norm_mod",
    )(x, ya, yb, route, mods3_prev, g, mods3, mods3, ws_hi, ws_lo)


def _final_kernel(x_ref, ya_ref, yb_ref, rt_ref, gate_ref, g_ref, o_ref):
    o_ref[...] = _rms(_moe_residual(x_ref, ya_ref, yb_ref, rt_ref, gate_ref)) * g_ref[...]


def _final_call(x, ya, yb, route, mods3, g_final):
    lat = lambda b, s: (b * SEQ_BLOCKS + 1 + s, 0)
    spec = pl.BlockSpec((TB, D), lat)
    return pl.pallas_call(
        _final_kernel,
        grid=(BATCH, SEQ // TB),
        in_specs=[spec, spec, spec, pl.BlockSpec((TB, SMALL), lat),
                  pl.BlockSpec((None, 1, D), lambda b, s: (b, 0, 5)),
                  pl.BlockSpec((1, D), lambda b, s: (0, 0))],
        out_specs=pl.BlockSpec((TB, D), lambda b, s: (b * (SEQ // TB) + s, 0)),
        out_shape=jax.ShapeDtypeStruct((BATCH * SEQ, D), F32),
        compiler_params=_cparams(2),
        name="final_norm",
    )(x, ya, yb, route, mods3, g_final)


def _mm_kernel(h_ref, w_ref, o_ref):
    o_ref[...] = _dot(h_ref[...], w_ref[...]).astype(o_ref.dtype)


def _mm_call(h, w, name):
    n = w.shape[1]
    tm, tn = 1024, 512
    return pl.pallas_call(
        _mm_kernel,
        grid=(n // tn, N_TOK // tm),
        in_specs=[pl.BlockSpec((tm, D), lambda j, i: (i, 0)),
                  pl.BlockSpec((D, tn), lambda j, i: (0, j))],
        out_specs=pl.BlockSpec((tm, tn), lambda j, i: (i, j)),
        out_shape=jax.ShapeDtypeStruct((N_TOK, n), BF16),
        compiler_params=_cparams(2),
        name=name,
    )(h, w)


CONV_HALO = 16


def _conv_kernel(x_ref, w_ref, b_ref, post_ref, o_ref):
    w = w_ref[...]
    bias = b_ref[...]
    post = post_ref[...]
    row = lax.broadcasted_iota(jnp.int32, (TB, 1), 0)
    for k in range(SEQ_BLOCKS):
        r0 = k * TB
        lo = max(r0 - CONV_HALO, 0)
        hi = min(r0 + TB + CONV_HALO, T_BATCH)
        off = r0 - lo
        n = hi - lo
        ext = x_ref[lo:hi, :].astype(F32)
        xm2 = pltpu.roll(ext, 2, 0)[off:off + TB]
        xm1 = pltpu.roll(ext, 1, 0)[off:off + TB]
        x0 = ext[off:off + TB]
        xp1 = pltpu.roll(ext, n - 1, 0)[off:off + TB]
        if k in (0, 1):
            xm2 = jnp.where(row >= 2, xm2, 0.0)
            xm1 = jnp.where(row >= 1, xm1, 0.0)
        if k in (0, SEQ_BLOCKS - 1):
            xp1 = jnp.where(row <= TB - 2, xp1, 0.0)
        y = bias + w[0:1] * xm2 + w[1:2] * xm1 + w[2:3] * x0 + w[3:4] * xp1
        o_ref[r0:r0 + TB, :] = (_silu(y) * post).astype(o_ref.dtype)


def _conv_call(u, col0, width, w, b, post, name):
    tc = 512
    cb0 = col0 // tc
    return pl.pallas_call(
        _conv_kernel,
        grid=(BATCH, width // tc),
        in_specs=[pl.BlockSpec((T_BATCH, tc), lambda bi, c: (bi, cb0 + c)),
                  pl.BlockSpec((4, tc), lambda bi, c: (0, c)),
                  pl.BlockSpec((1, tc), lambda bi, c: (0, c)),
                  pl.BlockSpec((1, tc), lambda bi, c: (0, c))],
        out_specs=pl.BlockSpec((T_BATCH, tc), lambda bi, c: (bi, c)),
        out_shape=jax.ShapeDtypeStruct((N_TOK, width), BF16),
        compiler_params=_cparams(2),
        name=name,
    )(u, w, b, post)


def _tri_dot_cols(tri, x):
    a, b, c = _split3(x)
    return _dot(tri, a) + _dot(tri, b) + _dot(tri, c)


def _tri_dot_rows(x, tri):
    a, b, c = _split3(x)
    return _dot(a, tri) + _dot(b, tri) + _dot(c, tri)


def _gate_kernel(pre_ref, pret_ref, bc_ref, br_ref, ac_ref, ar_ref, bwdc_ref, bwdr_ref, tril_ref, triu_ref,
                 valc_ref, cumc_ref, valr_ref, cumr_ref, *, ssd):
    tril = tril_ref[...]
    triu = triu_ref[...]

    def act(v):
        if ssd:
            val = _softplus(v)
            return val, val
        return v, _log_sigmoid(v)

    vc, dc = act(pre_ref[...] + bc_ref[...])
    dc = dc * ac_ref[...]
    valc_ref[...] = vc
    cumc_ref[...] = jnp.where(bwdc_ref[...] > 0.5, _tri_dot_cols(triu, dc), _tri_dot_cols(tril, dc))

    vr, dr = act(pret_ref[...] + br_ref[...])
    dr = dr * ar_ref[...]
    valr_ref[...] = vr
    cumr_ref[...] = jnp.where(bwdr_ref[...] > 0.5, _tri_dot_rows(dr, tril), _tri_dot_rows(dr, triu))


def _gate_call(small, bias, scale, bwd, ssd, name):
    small_t = small.reshape(N_BLOCKS, TB, SMALL)[:, :, :64].transpose(0, 2, 1)
    tri = np.tril(np.ones((TB, TB), np.float32))
    tril = jnp.asarray(tri, BF16)
    triu = jnp.asarray(tri.T, BF16)
    col = lambda v: v.reshape(1, SMALL).astype(F32)
    rowv = lambda v: v[:64].reshape(64, 1).astype(F32)
    cs = lambda shape: pl.BlockSpec(shape, lambda j: (0,) * len(shape))
    return pl.pallas_call(
        functools.partial(_gate_kernel, ssd=ssd),
        grid=(N_BLOCKS,),
        in_specs=[_tok_spec(SMALL), pl.BlockSpec((None, 64, TB), lambda j: (j, 0, 0)),
                  cs((1, SMALL)), cs((64, 1)), cs((1, SMALL)), cs((64, 1)), cs((1, SMALL)), cs((64, 1)),
                  cs((TB, TB)), cs((TB, TB))],
        out_specs=[_tok_spec(SMALL), _tok_spec(SMALL),
                   pl.BlockSpec((None, 64, TB), lambda j: (j, 0, 0)),
                   pl.BlockSpec((None, 64, TB), lambda j: (j, 0, 0))],
        out_shape=[jax.ShapeDtypeStruct((N_TOK, SMALL), F32), jax.ShapeDtypeStruct((N_TOK, SMALL), F32),
                   jax.ShapeDtypeStruct((N_BLOCKS, 64, TB), F32), jax.ShapeDtypeStruct((N_BLOCKS, 64, TB), F32)],
        compiler_params=_cparams(1),
        name=name,
    )(small, small_t, col(bias), rowv(bias), col(scale), rowv(scale), col(bwd), rowv(bwd), tril, triu)


def _scan_block(d):
    if d == 0:
        return lambda b, s: b * SEQ_BLOCKS + s
    return lambda b, s: b * SEQ_BLOCKS + jnp.where(s == 0, 0, SEQ_BLOCKS - s)


def _causal_mask(d):
    t = lax.broadcasted_iota(jnp.int32, (TB, TB), 0)
    s = lax.broadcasted_iota(jnp.int32, (TB, TB), 1)
    return (s <= t) if d == 0 else (s >= t)


def _mlstm_kernel(q_ref, k_ref, v_ref, valc_ref, cumc_ref, valr_ref, cumr_ref, o_ref,
                  c_ref, n_ref, m_ref, *, d):
    @pl.when(pl.program_id(1) == 0)
    def _():
        c_ref[...] = jnp.zeros_like(c_ref)
        n_ref[...] = jnp.zeros_like(n_ref)
        m_ref[...] = jnp.zeros_like(m_ref)

    mask = _causal_mask(d)
    end = TB - 1 if d == 0 else 0
    for h in range(HEADS_M):
        p = d * HEADS_M + h
        cs = slice(h * DH_M, (h + 1) * DH_M)
        q = q_ref[:, cs]
        k = k_ref[:, cs]
        v = v_ref[:, cs]
        i_c = valc_ref[:, p:p + 1]
        cum_c = cumc_ref[:, 8 + p:9 + p]
        i_r = valr_ref[p:p + 1, :]
        cum_r = cumr_ref[8 + p:9 + p, :]
        total = cum_r[:, end:end + 1]
        m_prev = m_ref[h]
        c_st = c_ref[h]
        n_st = n_ref[h]

        dmat = jnp.where(mask, cum_c - cum_r + i_r, -jnp.inf)
        g = cum_c + m_prev
        m_t = jnp.maximum(g, jnp.max(dmat, axis=1, keepdims=True))
        w = jnp.exp(dmat - m_t)
        sg = jnp.exp(g - m_t)
        a = w * lax.dot_general(q, k, (((1,), (1,)), ((), ())), preferred_element_type=F32)
        num = _dot(a.astype(BF16), v) + sg * _dot(q, c_st.astype(BF16))
        den = jnp.sum(a, axis=1, keepdims=True) + sg * jnp.sum(q.astype(F32) * n_st, axis=1, keepdims=True)
        o_ref[:, cs] = (num / jnp.maximum(jnp.abs(den), jnp.exp(-m_t))).astype(o_ref.dtype)

        wl_log = total - cum_c + i_c
        m_new = jnp.maximum(total + m_prev, jnp.max(wl_log, axis=0, keepdims=True))
        wl = jnp.exp(wl_log - m_new)
        decay = jnp.exp(total + m_prev - m_new)
        kf = k.astype(F32)
        wv = (wl * v.astype(F32)).astype(BF16)
        c_ref[h] = decay * c_st + lax.dot_general(k, wv, (((0,), (0,)), ((), ())), preferred_element_type=F32)
        n_ref[h] = decay * n_st + jnp.sum(wl * kf, axis=0, keepdims=True)
        m_ref[h] = m_new


def _mlstm_call(qk, ucm, valc, cumc, valr, cumr, d):
    blk = _scan_block(d)
    tok = lambda col: pl.BlockSpec((TB, D), lambda b, s: (blk(b, s), col))
    small = pl.BlockSpec((TB, SMALL), lambda b, s: (blk(b, s), 0))
    rows = pl.BlockSpec((None, 64, TB), lambda b, s: (blk(b, s), 0, 0))
    return pl.pallas_call(
        functools.partial(_mlstm_kernel, d=d),
        grid=(BATCH, SEQ_BLOCKS),
        in_specs=[tok(0), tok(1), tok(2), small, small, rows, rows],
        out_specs=tok(0),
        out_shape=jax.ShapeDtypeStruct((N_TOK, D), BF16),
        scratch_shapes=[pltpu.VMEM((HEADS_M, DH_M, DH_M), F32), pltpu.VMEM((HEADS_M, 1, DH_M), F32),
                        pltpu.VMEM((HEADS_M, 1, 1), F32)],
        compiler_params=_cparams(2),
        name="mlstm_fwd" if d == 0 else "mlstm_bwd",
    )(qk, qk, ucm, valc, cumc, valr, cumr)


def _expand_heads(cols):
    lane = lax.broadcasted_iota(jnp.int32, (TB, 4 * SSD_P), 1)
    return jnp.where(lane < SSD_P, cols[0],
                     jnp.where(lane < 2 * SSD_P, cols[1], jnp.where(lane < 3 * SSD_P, cols[2], cols[3])))


def _ssd_kernel(x_ref, b_ref, c_ref, dtc_ref, acsc_ref, acsr_ref, o_ref, s_ref, *, d):
    @pl.when(pl.program_id(1) == 0)
    def _():
        s_ref[...] = jnp.zeros_like(s_ref)

    mask = _causal_mask(d)
    end = TB - 1 if d == 0 else 0
    lane = lax.broadcasted_iota(jnp.int32, (TB, 4 * SSD_P), 1)
    lane1 = lax.broadcasted_iota(jnp.int32, (1, 4 * SSD_P), 1)
    for g in range(SSD_GROUPS):
        xg = x_ref[:, g * 256:(g + 1) * 256].astype(F32)
        bg = b_ref[:, g * SSD_N:(g + 1) * SSD_N]
        cg = c_ref[:, g * SSD_N:(g + 1) * SSD_N]
        s_prev = s_ref[g]
        lanes = [SSD_HEADS + d * SSD_HEADS + g * 4 + hh for hh in range(4)]
        dt_cols = [dtc_ref[:, p:p + 1] for p in lanes]
        acs_cols = [acsc_ref[:, p:p + 1] for p in lanes]
        acs_rows = [acsr_ref[p:p + 1, :] for p in lanes]
        tots = [r[:, end:end + 1] for r in acs_rows]

        xdt = xg * _expand_heads(dt_cols)
        xdt_b = xdt.astype(BF16)
        cb = lax.dot_general(cg, bg, (((1,), (1,)), ((), ())), preferred_element_type=F32)
        y = jnp.zeros((TB, 4 * SSD_P), F32)
        for hh in range(4):
            lm = jnp.exp(jnp.where(mask, acs_cols[hh] - acs_rows[hh], -jnp.inf))
            yh = _dot((cb * lm).astype(BF16), xdt_b)
            y = jnp.where((lane >= hh * SSD_P) & (lane < (hh + 1) * SSD_P), yh, y)
        y = y + _dot(cg, s_prev.astype(BF16)) * _expand_heads([jnp.exp(a) for a in acs_cols])
        o_ref[:, g * 256:(g + 1) * 256] = y.astype(o_ref.dtype)

        to_end = _expand_heads([jnp.exp(tots[hh] - acs_cols[hh]) for hh in range(4)])
        st = lax.dot_general(bg, (xdt * to_end).astype(BF16), (((0,), (0,)), ((), ())),
                             preferred_element_type=F32)
        cd = [jnp.exp(t) for t in tots]
        chunk_decay = jnp.where(lane1 < SSD_P, cd[0],
                                jnp.where(lane1 < 2 * SSD_P, cd[1], jnp.where(lane1 < 3 * SSD_P, cd[2], cd[3])))
        s_ref[g] = chunk_decay * s_prev + st


def _ssd_call(xbc, dtc, acsc, acsr, d):
    blk = _scan_block(d)
    small = pl.BlockSpec((TB, SMALL), lambda b, s: (blk(b, s), 0))
    rows = pl.BlockSpec((None, 64, TB), lambda b, s: (blk(b, s), 0, 0))
    return pl.pallas_call(
        functools.partial(_ssd_kernel, d=d),
        grid=(BATCH, SEQ_BLOCKS),
        in_specs=[pl.BlockSpec((TB, D), lambda b, s: (blk(b, s), 0)),
                  pl.BlockSpec((TB, 512), lambda b, s: (blk(b, s), 2)),
                  pl.BlockSpec((TB, 512), lambda b, s: (blk(b, s), 3)),
                  small, small, rows],
        out_specs=pl.BlockSpec((TB, D), lambda b, s: (blk(b, s), 0)),
        out_shape=jax.ShapeDtypeStruct((N_TOK, D), BF16),
        scratch_shapes=[pltpu.VMEM((SSD_GROUPS, SSD_N, 4 * SSD_P), F32)],
        compiler_params=_cparams(2),
        name="ssd_fwd" if d == 0 else "ssd_bwd",
    )(xbc, xbc, xbc, dtc, acsc, acsr)


def _pool_tables():
    masks = np.zeros((2, 4, TB, TB), np.float32)
    inv = np.zeros((2, 4, TB, 1), np.float32)
    for kind, length in ((0, GRID_W), (1, CTX)):
        for gi, w in enumerate(POOL_WINDOWS):
            for r in range(TB):
                base, c = (r // length) * length, r % length
                lo = min(max(c - w // 2, 0), length - 1)
                hi = min(max(c - w // 2 + w - 1, 0), length - 1)
                masks[kind, gi, r, base + lo:base + hi + 1] = 1.0
                inv[kind, gi, r, 0] = 1.0 / (hi - lo + 1)
    return jnp.asarray(masks, BF16), jnp.asarray(inv, F32)


ROUTE_FIELDS = 6


def _route_block(logits, ltri_ref, rt_ref, cnt_ref, cnt_scr):
    @pl.when(pl.program_id(0) == 0)
    def _():
        cnt_scr[...] = jnp.zeros_like(cnt_scr)

    lane = lax.broadcasted_iota(jnp.int32, logits.shape, 1).astype(F32)

    def top1(valid):
        v = jnp.max(jnp.where(valid, logits, -jnp.inf), axis=1, keepdims=True)
        i = jnp.min(jnp.where(valid & (logits == v), lane, float(SMALL)), axis=1, keepdims=True)
        return v, i

    is_g = lane < N_GROUPS_E
    gm, grp = top1(is_g)
    p_grp = 1.0 / jnp.sum(jnp.where(is_g, jnp.exp(logits - gm), 0.0), axis=1, keepdims=True)
    lo_lane = N_GROUPS_E + grp * EPG
    in_grp = (lane >= lo_lane) & (lane < lo_lane + EPG)
    v1, i1 = top1(in_grp)
    v2, i2 = top1(in_grp & (lane != i1))
    t = jnp.exp(v2 - v1)
    w1 = p_grp / (1.0 + t)
    w2 = p_grp * t / (1.0 + t)
    e1 = i1 - N_GROUPS_E
    e2 = i2 - N_GROUPS_E

    oh1 = lane == e1
    oh2 = lane == e2
    oh = jnp.where(oh1 | oh2, 1.0, 0.0)
    before = _dot(ltri_ref[...], oh.astype(BF16)) + cnt_scr[...]
    r1 = jnp.sum(jnp.where(oh1, before, 0.0), axis=1, keepdims=True)
    r2 = jnp.sum(jnp.where(oh2, before, 0.0), axis=1, keepdims=True)
    cnt_scr[...] = cnt_scr[...] + jnp.sum(oh, axis=0, keepdims=True)
    cnt_ref[...] = cnt_scr[...]
    rt = jnp.zeros(logits.shape, F32)
    for k, val in enumerate((e1, e2, w1, w2, r1, r2)):
        rt = jnp.where(lane == k, val, rt)
    rt_ref[...] = rt


def _merge_kernel(x_ref, pa_ref, o_ref, z_ref, mg0_ref, mg1_ref, mg2_ref, hf_ref, hb_ref, yf_ref, yb_ref, xs_ref,
                  pmask_ref, pinv_ref, poolw_ref, pscale_ref, mng_ref, dsk_ref, sng_ref, wbr_ref, wout_ref,
                  gate_ref, sh_ref, sc_ref, gffn_ref, wrh_ref, wrl_ref, brt_ref, ltri_ref,
                  xo_ref, h2_ref, rt_ref, cnt_ref, cnt_scr):
    parts = []
    for g in range(4):
        a_g = pa_ref[:, g * 256:(g + 1) * 256]
        pooled = _dot(pmask_ref[g], a_g) * pinv_ref[g] - a_g.astype(F32)
        parts.append(_dot(pooled.astype(BF16), poolw_ref[g]))
    pool = jnp.concatenate(parts, axis=1) * pscale_ref[...]

    hs = hf_ref[...].astype(F32) + hb_ref[...].astype(F32)
    hn = jnp.concatenate([_rms(hs[:, h * DH_M:(h + 1) * DH_M]) for h in range(HEADS_M)], axis=1)
    ml = _sigmoid(o_ref[...].astype(F32)) * (hn * mng_ref[...])

    y = yf_ref[...].astype(F32) + yb_ref[...].astype(F32) + dsk_ref[...] * xs_ref[...].astype(F32)
    sl = _rms(y * _silu(z_ref[...].astype(F32))) * sng_ref[...]

    acc = _sigmoid(mg0_ref[...].astype(F32)) * _dot(pool.astype(BF16), wbr_ref[0])
    acc = acc + _sigmoid(mg1_ref[...].astype(F32)) * _dot(ml.astype(BF16), wbr_ref[1])
    acc = acc + _sigmoid(mg2_ref[...].astype(F32)) * _dot(sl.astype(BF16), wbr_ref[2])
    xn = x_ref[...] + gate_ref[...] * _dot(acc.astype(BF16), wout_ref[...])
    xo_ref[...] = xn

    h2 = (_rms(xn) * gffn_ref[...]) * (1.0 + sc_ref[...]) + sh_ref[...]
    hi, lo = _split2(h2)
    h2f = hi.astype(F32)
    for cchunk in range(D // 128):
        h2_ref[pl.ds(cchunk, TB, stride=D // 128), :] = h2f[:, cchunk * 128:(cchunk + 1) * 128]
    logits = _dot(hi, wrh_ref[...]) + _dot(lo, wrh_ref[...]) + _dot(hi, wrl_ref[...]) + brt_ref[...]
    _route_block(logits, ltri_ref, rt_ref, cnt_ref, cnt_scr)


def _merge_call(x, urm, hf, hb, yf, yb, xbc, pmask, pinv, poolw, pscale, mng, dsk, sng, wbr, wout,
                mods3, gffn, wr_hi, wr_lo, br):
    ltri = jnp.asarray(np.tril(np.ones((TB, TB), np.float32), -1), BF16)
    kind = lambda j: jnp.where(j % SEQ_BLOCKS == 0, 1, 0)
    vec = _const_spec((1, D))
    return pl.pallas_call(
        _merge_kernel,
        grid=(N_BLOCKS,),
        in_specs=[_tok_spec(),
                  _tok_spec(D, 0), _tok_spec(D, 1), _tok_spec(D, 4),
                  _tok_spec(D, 5), _tok_spec(D, 6), _tok_spec(D, 7),
                  _tok_spec(), _tok_spec(), _tok_spec(), _tok_spec(),
                  _tok_spec(D, 0),
                  pl.BlockSpec((None, 4, TB, TB), lambda j: (kind(j), 0, 0, 0)),
                  pl.BlockSpec((None, 4, TB, 1), lambda j: (kind(j), 0, 0, 0)),
                  _const_spec((4, 256, 256)), vec, vec, vec, vec,
                  _const_spec((3, D, D)), _const_spec((D, D)),
                  _mod_spec(2), _mod_spec(3), _mod_spec(4), vec,
                  _const_spec((D, SMALL)), _const_spec((D, SMALL)), _const_spec((1, SMALL)),
                  _const_spec((TB, TB))],
        out_specs=[_tok_spec(), pl.BlockSpec((TB * ROW_TILES, 128), lambda j: (j, 0)), _tok_spec(SMALL),
                   _const_spec((1, SMALL))],
        out_shape=[jax.ShapeDtypeStruct((N_TOK, D), F32), jax.ShapeDtypeStruct((N_TOK * ROW_TILES, 128), F32),
                   jax.ShapeDtypeStruct((N_TOK, SMALL), F32), jax.ShapeDtypeStruct((1, SMALL), F32)],
        scratch_shapes=[pltpu.VMEM((1, SMALL), F32)],
        compiler_params=_cparams(1),
        name="branch_merge",
    )(x, urm, urm, urm, urm, urm, urm, hf, hb, yf, yb, xbc, pmask, pinv, poolw, pscale, mng, dsk, sng,
      wbr, wout, mods3, mods3, mods3, gffn, wr_hi, wr_lo, br, ltri)


def _moe_kernel(be_ref, nu_ref, tok_ref, h_hbm, wg_ref, wu_ref, wd_ref, o_ref, xbuf, sem):
    i = pl.program_id(0)
    n_used = nu_ref[0]
    slot = i % 2

    def gather_copy(block, s, r):
        row0 = pl.multiple_of(tok_ref[block * MOE_BLOCK + r] * ROW_TILES, ROW_TILES)
        return pltpu.make_async_copy(h_hbm.at[pl.ds(row0, ROW_TILES)],
                                     xbuf.at[s, pl.ds(r * ROW_TILES, ROW_TILES)], sem.at[s])

    def start_gather(block, s):
        for r in range(MOE_BLOCK):
            gather_copy(block, s, r).start()

    @pl.when((i == 0) & (n_used > 0))
    def _():
        start_gather(0, 0)

    @pl.when(i + 1 < n_used)
    def _():
        start_gather(i + 1, 1 - slot)

    @pl.when(i < n_used)
    def _():
        pltpu.make_async_copy(xbuf.at[slot], xbuf.at[slot], sem.at[slot]).wait()
        x = jnp.concatenate([xbuf[slot, pl.ds(c, MOE_BLOCK, stride=ROW_TILES), :] for c in range(ROW_TILES)],
                            axis=1).astype(BF16)
        gt = _dot(x, wg_ref[...].astype(BF16))
        up = _dot(x, wu_ref[...].astype(BF16))
        act = (_silu(gt) * up).astype(BF16)
        o_ref[...] = _dot(act, wd_ref[...].astype(BF16)).astype(o_ref.dtype)

    @pl.when(i >= nu_ref[0])
    def _():
        o_ref[...] = jnp.zeros_like(o_ref)


MOE_CAP = N_TOK * 2 + N_EXPERTS * MOE_BLOCK
MOE_NBLOCKS = MOE_CAP // MOE_BLOCK


def _moe_call(layer, block_expert, n_used, buf_tok, h2, w_gate, w_up, w_down):
    grid_spec = pltpu.PrefetchScalarGridSpec(
        num_scalar_prefetch=3,
        grid=(MOE_NBLOCKS,),
        in_specs=[pl.BlockSpec(memory_space=pl.ANY),
                  pl.BlockSpec((None, None, D, D_EXPERT), lambda i, be, nu, bt: (layer, be[i], 0, 0)),
                  pl.BlockSpec((None, None, D, D_EXPERT), lambda i, be, nu, bt: (layer, be[i], 0, 0)),
                  pl.BlockSpec((None, None, D_EXPERT, D), lambda i, be, nu, bt: (layer, be[i], 0, 0))],
        out_specs=pl.BlockSpec((MOE_BLOCK, D), lambda i, be, nu, bt: (i, 0)),
        scratch_shapes=[pltpu.VMEM((2, MOE_BLOCK * ROW_TILES, 128), F32), pltpu.SemaphoreType.DMA((2,))],
    )
    return pl.pallas_call(
        _moe_kernel,
        grid_spec=grid_spec,
        out_shape=jax.ShapeDtypeStruct((MOE_CAP, D), BF16),
        compiler_params=_cparams(1),
        name="moe_experts",
    )(block_expert, n_used, buf_tok, h2, w_gate, w_up, w_down)


def _slot_tokens_kernel(pos_ref, out_ref):
    def clear(p, carry):
        out_ref[p] = 0
        return carry

    def put(a, carry):
        out_ref[pos_ref[a]] = a // 2
        return carry

    lax.fori_loop(0, MOE_CAP, clear, 0, unroll=8)
    lax.fori_loop(0, 2 * N_TOK, put, 0, unroll=8)


def _slot_tokens_call(pos_flat):
    return pl.pallas_call(
        _slot_tokens_kernel,
        in_specs=[pl.BlockSpec(memory_space=pltpu.SMEM)],
        out_specs=pl.BlockSpec(memory_space=pltpu.SMEM),
        out_shape=jax.ShapeDtypeStruct((MOE_CAP,), jnp.int32),
        name="slot_tokens",
    )(pos_flat)


def _dispatch_plan(route, counts):
    cnt = counts[0, :N_EXPERTS].astype(jnp.int32)
    padded = (cnt + MOE_BLOCK - 1) // MOE_BLOCK * MOE_BLOCK
    pends = jnp.cumsum(padded)
    pstarts = pends - padded
    expert = route[:, 0:2].astype(jnp.int32)
    rank = route[:, 4:6].astype(jnp.int32)
    onehot = expert[:, :, None] == jnp.arange(N_EXPERTS, dtype=jnp.int32)
    pos = jnp.sum(jnp.where(onehot, pstarts, 0), axis=-1) + rank
    buf_tok = _slot_tokens_call(pos.reshape(-1))
    block_start = jnp.arange(MOE_NBLOCKS, dtype=jnp.int32) * MOE_BLOCK
    block_expert = jnp.minimum(jnp.sum((pends[None, :] <= block_start[:, None]).astype(jnp.int32), axis=1),
                               N_EXPERTS - 1)
    n_used = (pends[-1] // MOE_BLOCK).reshape(1)
    return buf_tok, block_expert, n_used, pos


def _to_colmajor(t):
    c = t.shape[-1]
    t = t.reshape(BATCH, T_BATCH, c)
    lat = t[:, CTX:].reshape(BATCH, SEQ // GRID_W, GRID_W, c).swapaxes(1, 2).reshape(BATCH, SEQ, c)
    return jnp.concatenate([t[:, :CTX], lat], axis=1).reshape(N_TOK, c)


def _to_rowmajor(t):
    c = t.shape[-1]
    t = t.reshape(BATCH, T_BATCH, c)
    lat = t[:, CTX:].reshape(BATCH, GRID_W, SEQ // GRID_W, c).swapaxes(1, 2).reshape(BATCH, SEQ, c)
    return jnp.concatenate([t[:, :CTX], lat], axis=1).reshape(N_TOK, c)


def _lanes(*pieces):
    v = jnp.concatenate([jnp.asarray(p, F32).reshape(-1) for p in pieces])
    return jnp.pad(v, (0, SMALL - v.shape[0]))


def kernel(x, c, ctx, c_ctx, w_ada, b_ada, g_norm_mix, g_norm_ffn, w_in, pool_w, pool_scale, mlstm_conv_w,
           mlstm_conv_b, mlstm_gate_b, mlstm_norm_g, ssd_conv_w, ssd_conv_b, ssd_dt_bias, ssd_a_log, ssd_d,
           ssd_norm_g, w_branch, w_out, w_route_group, b_route_group, w_route_expert, b_route_expert,
           w_exp_gate, w_exp_up, w_exp_down, g_final):
    cvec = jnp.concatenate([c, c_ctx[None], jnp.zeros((3, D), F32)], axis=0)
    mods = _ada_table(cvec, w_ada, b_ada)
    xs = jnp.concatenate([ctx, x], axis=1).reshape(N_TOK, D)
    pmask, pinv = _pool_tables()
    row = lambda v: v.reshape(1, -1).astype(F32)
    w_in_bf = w_in.astype(BF16)

    h = small = None
    out = None
    for l in range(DEPTH):
        mods3 = mods[l].reshape(8, 1, 6 * D)
        wi = w_in_bf[l]
        sp = np.cumsum([0, D, D, D, D, D, 16, D, D, 512, 512, 32, 3 * D])
        piece = lambda i: wi[:, sp[i]:sp[i + 1]]
        w_rm = jnp.concatenate([piece(0), piece(4), piece(6), piece(8), piece(9), piece(7), piece(11)],
                               axis=1)
        w_cm = wi[:, sp[1]:sp[4]]
        w_small = jnp.pad(jnp.concatenate([w_in[l][:, sp[5]:sp[6]], w_in[l][:, sp[10]:sp[11]]], axis=1),
                          ((0, 0), (0, SMALL - 48)))
        ws_hi = w_small.astype(BF16)
        ws_lo = (w_small - ws_hi.astype(F32)).astype(BF16)

        if l == 0:
            h, small = _norm_call(xs, row(g_norm_mix[l]), mods3, ws_hi, ws_lo)

        h_cm = _to_colmajor(h)
        small_cm = _to_colmajor(small)
        u_rm = _mm_call(h, w_rm, "in_proj_rowmajor")
        u_cm = _mm_call(h_cm, w_cm, "in_proj_colmajor")

        post_m = jnp.concatenate([jnp.ones((D,), F32), jnp.full((D,), DH_M ** -0.5, F32)]).reshape(1, 2 * D)
        qk = _conv_call(u_cm, 0, 2 * D, mlstm_conv_w[l], row(mlstm_conv_b[l]), post_m, "mlstm_conv")
        gate_bias = _lanes(mlstm_gate_b[l])
        gate_bwd = _lanes(jnp.zeros((12,)), jnp.ones((4,)))
        g_valc, g_cumc, g_valr, g_cumr = _gate_call(small_cm, gate_bias, jnp.ones((SMALL,), F32), gate_bwd,
                                                    False, "mlstm_gates")
        h_f = _mlstm_call(qk, u_cm, g_valc, g_cumc, g_valr, g_cumr, 0)
        h_b = _mlstm_call(qk, u_cm, g_valc, g_cumc, g_valr, g_cumr, 1)
        h_f = _to_rowmajor(h_f)
        h_b = _to_rowmajor(h_b)

        xbc = _conv_call(u_rm, 2 * D, 2 * D, ssd_conv_w[l], row(ssd_conv_b[l]), jnp.ones((1, 2 * D), F32),
                         "ssd_conv")
        a_neg = -jnp.exp(ssd_a_log[l].astype(F32))
        dt_bias = _lanes(jnp.zeros((16,)), ssd_dt_bias[l])
        dt_scale = _lanes(jnp.zeros((16,)), a_neg)
        dt_bwd = _lanes(jnp.zeros((32,)), jnp.ones((16,)))
        s_dtc, s_acsc, _, s_acsr = _gate_call(small, dt_bias, dt_scale, dt_bwd, True, "ssd_gates")
        y_f = _ssd_call(xbc, s_dtc, s_acsc, s_acsr, 0)
        y_b = _ssd_call(xbc, s_dtc, s_acsc, s_acsr, 1)

        w_r = jnp.pad(jnp.concatenate([w_route_group[l], w_route_expert[l]], axis=1),
                      ((0, 0), (0, SMALL - N_GROUPS_E - N_EXPERTS)))
        wr_hi = w_r.astype(BF16)
        wr_lo = (w_r - wr_hi.astype(F32)).astype(BF16)
        b_r = _lanes(b_route_group[l], b_route_expert[l]).reshape(1, SMALL)
        dsk = jnp.repeat(ssd_d[l].astype(F32), SSD_P).reshape(1, D)
        xs, h2, route, counts = _merge_call(
            xs, u_rm, h_f, h_b, y_f, y_b, xbc, pmask, pinv, pool_w[l].astype(BF16), row(pool_scale[l]),
            row(mlstm_norm_g[l]), dsk, row(ssd_norm_g[l]), w_branch[l].astype(BF16), w_out[l].astype(BF16),
            mods3, row(g_norm_ffn[l]), wr_hi, wr_lo, b_r)

        buf_tok, block_expert, n_used, pos = _dispatch_plan(route, counts)
        yb = _moe_call(l, block_expert, n_used, buf_tok, h2, w_exp_gate, w_exp_up, w_exp_down)
        y0 = yb[pos[:, 0]]
        y1 = yb[pos[:, 1]]

        if l + 1 < DEPTH:
            wi_n = w_in[l + 1]
            w_small_n = jnp.pad(jnp.concatenate([wi_n[:, sp[5]:sp[6]], wi_n[:, sp[10]:sp[11]]], axis=1),
                                ((0, 0), (0, SMALL - 48)))
            wsn_hi = w_small_n.astype(BF16)
            wsn_lo = (w_small_n - wsn_hi.astype(F32)).astype(BF16)
            mods3_n = mods[l + 1].reshape(8, 1, 6 * D)
            xs, h, small = _resnorm_call(xs, y0, y1, route, mods3, row(g_norm_mix[l + 1]), mods3_n,
                                         wsn_hi, wsn_lo)
        else:
            out = _final_call(xs, y0, y1, route, mods3, row(g_final))
    return out.reshape(BATCH, SEQ, D)
```

```python
import functools
import math

import jax
import jax.numpy as jnp
import numpy as np
from jax import lax
from jax.experimental import pallas as pl
from jax.experimental.pallas import tpu as pltpu

F32 = jnp.float32
BF16 = jnp.bfloat16

D = 1024
BATCH = 4
SEQ = 4096
CTX = 256
DEPTH = 2
GRID_W = 64
EPS = 1e-6

TB = 256
SEQ_BLOCKS = (SEQ + CTX) // TB
T_BATCH = SEQ + CTX
N_TOK = BATCH * T_BATCH
N_BLOCKS = N_TOK // TB

POOL_WINDOWS = (2, 4, 8, 16)
HEADS_M = 4
DH_M = 256
SSD_HEADS = 16
SSD_P = 64
SSD_GROUPS = 4
SSD_N = 128
N_GROUPS_E = 4
EPG = 8
N_EXPERTS = 32
D_EXPERT = 512
MOE_BLOCK = 256
SMALL = 128
ROW_TILES = D // 128

VMEM_LIMIT = 56 * 1024 * 1024


def _cparams(n_axes):
    return pltpu.CompilerParams(dimension_semantics=("arbitrary",) * n_axes,
                                vmem_limit_bytes=VMEM_LIMIT)


def _dot(a, b):
    return jnp.dot(a, b, preferred_element_type=F32)


def _split2(x):
    hi = x.astype(BF16)
    lo = (x - hi.astype(F32)).astype(BF16)
    return hi, lo


def _split3(x):
    hi = x.astype(BF16)
    r = x - hi.astype(F32)
    mid = r.astype(BF16)
    lo = (r - mid.astype(F32)).astype(BF16)
    return hi, mid, lo


def _sigmoid(x):
    return 1.0 / (1.0 + jnp.exp(-x))


def _silu(x):
    return x * _sigmoid(x)


def _log1pexp_negabs(x):
    return jnp.log(1.0 + jnp.exp(-jnp.abs(x)))


def _log_sigmoid(x):
    return jnp.minimum(x, 0.0) - _log1pexp_negabs(x)


def _softplus(x):
    return jnp.maximum(x, 0.0) + _log1pexp_negabs(x)


def _rms(x):
    return x * lax.rsqrt(jnp.mean(x * x, axis=-1, keepdims=True) + EPS)


def _mod_row(j):
    return jnp.where(j % SEQ_BLOCKS == 0, BATCH, j // SEQ_BLOCKS)


def _ada_kernel(c_ref, w_ref, b_ref, o_ref):
    c = c_ref[...]
    s_hi, s_lo = _split2(_silu(c))
    w_hi, w_lo = _split2(w_ref[...])
    o_ref[...] = _dot(s_hi, w_hi) + _dot(s_lo, w_hi) + _dot(s_hi, w_lo) + b_ref[...]


def _ada_table(cvec, w_ada, b_ada):
    tn = 1024
    return pl.pallas_call(
        _ada_kernel,
        grid=(DEPTH, 6 * D // tn),
        in_specs=[pl.BlockSpec((8, D), lambda l, j: (0, 0)),
                  pl.BlockSpec((None, D, tn), lambda l, j: (l, 0, j)),
                  pl.BlockSpec((None, 1, tn), lambda l, j: (l, 0, j))],
        out_specs=pl.BlockSpec((None, 8, tn), lambda l, j: (l, 0, j)),
        out_shape=jax.ShapeDtypeStruct((DEPTH, 8, 6 * D), F32),
        compiler_params=_cparams(2),
        name="ada_table",
    )(cvec, w_ada, b_ada.reshape(DEPTH, 1, 6 * D))


def _norm_mod_small(x, g_ref, sh_ref, sc_ref, wsh_ref, wsl_ref, h_ref, small_ref):
    h = _rms(x) * g_ref[...]
    h = h * (1.0 + sc_ref[...]) + sh_ref[...]
    h_hi, h_lo = _split2(h)
    h_ref[...] = h_hi
    small_ref[...] = _dot(h_hi, wsh_ref[...]) + _dot(h_lo, wsh_ref[...]) + _dot(h_hi, wsl_ref[...])


def _norm_kernel(x_ref, g_ref, sh_ref, sc_ref, wsh_ref, wsl_ref, h_ref, small_ref):
    _norm_mod_small(x_ref[...], g_ref, sh_ref, sc_ref, wsh_ref, wsl_ref, h_ref, small_ref)


def _moe_residual(x_ref, ya_ref, yb_ref, rt_ref, gate_ref):
    y = rt_ref[:, 2:3] * ya_ref[...].astype(F32) + rt_ref[:, 3:4] * yb_ref[...].astype(F32)
    return x_ref[...] + gate_ref[...] * y


def _resnorm_kernel(x_ref, ya_ref, yb_ref, rt_ref, gate_ref, g_ref, sh_ref, sc_ref, wsh_ref, wsl_ref,
                    xo_ref, h_ref, small_ref):
    x = _moe_residual(x_ref, ya_ref, yb_ref, rt_ref, gate_ref)
    xo_ref[...] = x
    _norm_mod_small(x, g_ref, sh_ref, sc_ref, wsh_ref, wsl_ref, h_ref, small_ref)


def _mod_spec(chunk):
    return pl.BlockSpec((None, 1, D), lambda j: (_mod_row(j), 0, chunk))


def _tok_spec(width=D, col=0):
    return pl.BlockSpec((TB, width), lambda j: (j, col))


def _const_spec(shape):
    nd = len(shape)
    return pl.BlockSpec(shape, lambda j: (0,) * nd)


def _norm_call(x, g, mods3, ws_hi, ws_lo):
    return pl.pallas_call(
        _norm_kernel,
        grid=(N_BLOCKS,),
        in_specs=[_tok_spec(), _const_spec((1, D)), _mod_spec(0), _mod_spec(1),
                  _const_spec((D, SMALL)), _const_spec((D, SMALL))],
        out_specs=[_tok_spec(), _tok_spec(SMALL)],
        out_shape=[jax.ShapeDtypeStruct((N_TOK, D), BF16), jax.ShapeDtypeStruct((N_TOK, SMALL), F32)],
        compiler_params=_cparams(1),
        name="norm_mod",
    )(x, g, mods3, mods3, ws_hi, ws_lo)


def _resnorm_call(x, ya, yb, route, mods3_prev, g, mods3, ws_hi, ws_lo):
    return pl.pallas_call(
        _resnorm_kernel,
        grid=(N_BLOCKS,),
        in_specs=[_tok_spec(), _tok_spec(), _tok_spec(), _tok_spec(SMALL),
                  pl.BlockSpec((None, 1, D), lambda j: (_mod_row(j), 0, 5)),
                  _const_spec((1, D)), _mod_spec(0), _mod_spec(1),
                  _const_spec((D, SMALL)), _const_spec((D, SMALL))],
        out_specs=[_tok_spec(), _tok_spec(), _tok_spec(SMALL)],
        out_shape=[jax.ShapeDtypeStruct((N_TOK, D), F32), jax.ShapeDtypeStruct((N_TOK, D), BF16),
                   jax.ShapeDtypeStruct((N_TOK, SMALL), F32)],
        compiler_params=_cparams(1),
        name="residual_norm_mod",
    )(x, ya, yb, route, mods3_prev, g, mods3, mods3, ws_hi, ws_lo)


def _final_kernel(x_ref, ya_ref, yb_ref, rt_ref, gate_ref, g_ref, o_ref):
    o_ref[...] = _rms(_moe_residual(x_ref, ya_ref, yb_ref, rt_ref, gate_ref)) * g_ref[...]


def _final_call(x, ya, yb, route, mods3, g_final):
    lat = lambda b, s: (b * SEQ_BLOCKS + 1 + s, 0)
    spec = pl.BlockSpec((TB, D), lat)
    return pl.pallas_call(
        _final_kernel,
        grid=(BATCH, SEQ // TB),
        in_specs=[spec, spec, spec, pl.BlockSpec((TB, SMALL), lat),
                  pl.BlockSpec((None, 1, D), lambda b, s: (b, 0, 5)),
                  pl.BlockSpec((1, D), lambda b, s: (0, 0))],
        out_specs=pl.BlockSpec((TB, D), lambda b, s: (b * (SEQ // TB) + s, 0)),
        out_shape=jax.ShapeDtypeStruct((BATCH * SEQ, D), F32),
        compiler_params=_cparams(2),
        name="final_norm",
    )(x, ya, yb, route, mods3, g_final)


def _mm_kernel(h_ref, w_ref, o_ref):
    o_ref[...] = _dot(h_ref[...], w_ref[...]).astype(o_ref.dtype)


def _mm_call(h, w, name):
    n = w.shape[1]
    tm, tn = 1024, 512
    return pl.pallas_call(
        _mm_kernel,
        grid=(n // tn, N_TOK // tm),
        in_specs=[pl.BlockSpec((tm, D), lambda j, i: (i, 0)),
                  pl.BlockSpec((D, tn), lambda j, i: (0, j))],
        out_specs=pl.BlockSpec((tm, tn), lambda j, i: (i, j)),
        out_shape=jax.ShapeDtypeStruct((N_TOK, n), BF16),
        compiler_params=_cparams(2),
        name=name,
    )(h, w)


CONV_HALO = 16


def _conv_kernel(x_ref, w_ref, b_ref, post_ref, o_ref):
    w = w_ref[...]
    bias = b_ref[...]
    post = post_ref[...]
    row = lax.broadcasted_iota(jnp.int32, (TB, 1), 0)
    for k in range(SEQ_BLOCKS):
        r0 = k * TB
        lo = max(r0 - CONV_HALO, 0)
        hi = min(r0 + TB + CONV_HALO, T_BATCH)
        off = r0 - lo
        n = hi - lo
        ext = x_ref[lo:hi, :].astype(F32)
        xm2 = pltpu.roll(ext, 2, 0)[off:off + TB]
        xm1 = pltpu.roll(ext, 1, 0)[off:off + TB]
        x0 = ext[off:off + TB]
        xp1 = pltpu.roll(ext, n - 1, 0)[off:off + TB]
        if k in (0, 1):
            xm2 = jnp.where(row >= 2, xm2, 0.0)
            xm1 = jnp.where(row >= 1, xm1, 0.0)
        if k in (0, SEQ_BLOCKS - 1):
            xp1 = jnp.where(row <= TB - 2, xp1, 0.0)
        y = bias + w[0:1] * xm2 + w[1:2] * xm1 + w[2:3] * x0 + w[3:4] * xp1
        o_ref[r0:r0 + TB, :] = (_silu(y) * post).astype(o_ref.dtype)


def _conv_call(u, col0, width, w, b, post, name):
    tc = 512
    cb0 = col0 // tc
    return pl.pallas_call(
        _conv_kernel,
        grid=(BATCH, width // tc),
        in_specs=[pl.BlockSpec((T_BATCH, tc), lambda bi, c: (bi, cb0 + c)),
                  pl.BlockSpec((4, tc), lambda bi, c: (0, c)),
                  pl.BlockSpec((1, tc), lambda bi, c: (0, c)),
                  pl.BlockSpec((1, tc), lambda bi, c: (0, c))],
        out_specs=pl.BlockSpec((T_BATCH, tc), lambda bi, c: (bi, c)),
        out_shape=jax.ShapeDtypeStruct((N_TOK, width), BF16),
        compiler_params=_cparams(2),
        name=name,
    )(u, w, b, post)


def _tri_dot_cols(tri, x):
    a, b, c = _split3(x)
    return _dot(tri, a) + _dot(tri, b) + _dot(tri, c)


def _tri_dot_rows(x, tri):
    a, b, c = _split3(x)
    return _dot(a, tri) + _dot(b, tri) + _dot(c, tri)


def _gate_kernel(pre_ref, pret_ref, bc_ref, br_ref, ac_ref, ar_ref, bwdc_ref, bwdr_ref, tril_ref, triu_ref,
                 valc_ref, cumc_ref, valr_ref, cumr_ref, *, ssd):
    tril = tril_ref[...]
    triu = triu_ref[...]

    def act(v):
        if ssd:
            val = _softplus(v)
            return val, val
        return v, _log_sigmoid(v)

    vc, dc = act(pre_ref[...] + bc_ref[...])
    dc = dc * ac_ref[...]
    valc_ref[...] = vc
    cumc_ref[...] = jnp.where(bwdc_ref[...] > 0.5, _tri_dot_cols(triu, dc), _tri_dot_cols(tril, dc))

    vr, dr = act(pret_ref[...] + br_ref[...])
    dr = dr * ar_ref[...]
    valr_ref[...] = vr
    cumr_ref[...] = jnp.where(bwdr_ref[...] > 0.5, _tri_dot_rows(dr, tril), _tri_dot_rows(dr, triu))


def _gate_call(small, bias, scale, bwd, ssd, name):
    small_t = small.reshape(N_BLOCKS, TB, SMALL)[:, :, :64].transpose(0, 2, 1)
    tri = np.tril(np.ones((TB, TB), np.float32))
    tril = jnp.asarray(tri, BF16)
    triu = jnp.asarray(tri.T, BF16)
    col = lambda v: v.reshape(1, SMALL).astype(F32)
    rowv = lambda v: v[:64].reshape(64, 1).astype(F32)
    cs = lambda shape: pl.BlockSpec(shape, lambda j: (0,) * len(shape))
    return pl.pallas_call(
        functools.partial(_gate_kernel, ssd=ssd),
        grid=(N_BLOCKS,),
        in_specs=[_tok_spec(SMALL), pl.BlockSpec((None, 64, TB), lambda j: (j, 0, 0)),
                  cs((1, SMALL)), cs((64, 1)), cs((1, SMALL)), cs((64, 1)), cs((1, SMALL)), cs((64, 1)),
                  cs((TB, TB)), cs((TB, TB))],
        out_specs=[_tok_spec(SMALL), _tok_spec(SMALL),
                   pl.BlockSpec((None, 64, TB), lambda j: (j, 0, 0)),
                   pl.BlockSpec((None, 64, TB), lambda j: (j, 0, 0))],
        out_shape=[jax.ShapeDtypeStruct((N_TOK, SMALL), F32), jax.ShapeDtypeStruct((N_TOK, SMALL), F32),
                   jax.ShapeDtypeStruct((N_BLOCKS, 64, TB), F32), jax.ShapeDtypeStruct((N_BLOCKS, 64, TB), F32)],
        compiler_params=_cparams(1),
        name=name,
    )(small, small_t, col(bias), rowv(bias), col(scale), rowv(scale), col(bwd), rowv(bwd), tril, triu)


def _scan_block(d):
    if d == 0:
        return lambda b, s: b * SEQ_BLOCKS + s
    return lambda b, s: b * SEQ_BLOCKS + jnp.where(s == 0, 0, SEQ_BLOCKS - s)


def _causal_mask(d):
    t = lax.broadcasted_iota(jnp.int32, (TB, TB), 0)
    s = lax.broadcasted_iota(jnp.int32, (TB, TB), 1)
    return (s <= t) if d == 0 else (s >= t)


def _mlstm_kernel(q_ref, k_ref, v_ref, valc_ref, cumc_ref, valr_ref, cumr_ref, o_ref,
                  c_ref, n_ref, m_ref, *, d):
    @pl.when(pl.program_id(1) == 0)
    def _():
        c_ref[...] = jnp.zeros_like(c_ref)
        n_ref[...] = jnp.zeros_like(n_ref)
        m_ref[...] = jnp.zeros_like(m_ref)

    mask = _causal_mask(d)
    end = TB - 1 if d == 0 else 0
    for h in range(HEADS_M):
        p = d * HEADS_M + h
        cs = slice(h * DH_M, (h + 1) * DH_M)
        q = q_ref[:, cs]
        k = k_ref[:, cs]
        v = v_ref[:, cs]
        i_c = valc_ref[:, p:p + 1]
        cum_c = cumc_ref[:, 8 + p:9 + p]
        i_r = valr_ref[p:p + 1, :]
        cum_r = cumr_ref[8 + p:9 + p, :]
        total = cum_r[:, end:end + 1]
        m_prev = m_ref[h]
        c_st = c_ref[h]
        n_st = n_ref[h]

        dmat = jnp.where(mask, cum_c - cum_r + i_r, -jnp.inf)
        g = cum_c + m_prev
        m_t = jnp.maximum(g, jnp.max(dmat, axis=1, keepdims=True))
        w = jnp.exp(dmat - m_t)
        sg = jnp.exp(g - m_t)
        a = w * lax.dot_general(q, k, (((1,), (1,)), ((), ())), preferred_element_type=F32)
        num = _dot(a.astype(BF16), v) + sg * _dot(q, c_st.astype(BF16))
        den = jnp.sum(a, axis=1, keepdims=True) + sg * jnp.sum(q.astype(F32) * n_st, axis=1, keepdims=True)
        o_ref[:, cs] = (num / jnp.maximum(jnp.abs(den), jnp.exp(-m_t))).astype(o_ref.dtype)

        wl_log = total - cum_c + i_c
        m_new = jnp.maximum(total + m_prev, jnp.max(wl_log, axis=0, keepdims=True))
        wl = jnp.exp(wl_log - m_new)
        decay = jnp.exp(total + m_prev - m_new)
        kf = k.astype(F32)
        wv = (wl * v.astype(F32)).astype(BF16)
        c_ref[h] = decay * c_st + lax.dot_general(k, wv, (((0,), (0,)), ((), ())), preferred_element_type=F32)
        n_ref[h] = decay * n_st + jnp.sum(wl * kf, axis=0, keepdims=True)
        m_ref[h] = m_new


def _mlstm_call(qk, ucm, valc, cumc, valr, cumr, d):
    blk = _scan_block(d)
    tok = lambda col: pl.BlockSpec((TB, D), lambda b, s: (blk(b, s), col))
    small = pl.BlockSpec((TB, SMALL), lambda b, s: (blk(b, s), 0))
    rows = pl.BlockSpec((None, 64, TB), lambda b, s: (blk(b, s), 0, 0))
    return pl.pallas_call(
        functools.partial(_mlstm_kernel, d=d),
        grid=(BATCH, SEQ_BLOCKS),
        in_specs=[tok(0), tok(1), tok(2), small, small, rows, rows],
        out_specs=tok(0),
        out_shape=jax.ShapeDtypeStruct((N_TOK, D), BF16),
        scratch_shapes=[pltpu.VMEM((HEADS_M, DH_M, DH_M), F32), pltpu.VMEM((HEADS_M, 1, DH_M), F32),
                        pltpu.VMEM((HEADS_M, 1, 1), F32)],
        compiler_params=_cparams(2),
        name="mlstm_fwd" if d == 0 else "mlstm_bwd",
    )(qk, qk, ucm, valc, cumc, valr, cumr)


def _expand_heads(cols):
    lane = lax.broadcasted_iota(jnp.int32, (TB, 4 * SSD_P), 1)
    return jnp.where(lane < SSD_P, cols[0],
                     jnp.where(lane < 2 * SSD_P, cols[1], jnp.where(lane < 3 * SSD_P, cols[2], cols[3])))


def _ssd_kernel(x_ref, b_ref, c_ref, dtc_ref, acsc_ref, acsr_ref, o_ref, s_ref, *, d):
    @pl.when(pl.program_id(1) == 0)
    def _():
        s_ref[...] = jnp.zeros_like(s_ref)

    mask = _causal_mask(d)
    end = TB - 1 if d == 0 else 0
    lane = lax.broadcasted_iota(jnp.int32, (TB, 4 * SSD_P), 1)
    lane1 = lax.broadcasted_iota(jnp.int32, (1, 4 * SSD_P), 1)
    for g in range(SSD_GROUPS):
        xg = x_ref[:, g * 256:(g + 1) * 256].astype(F32)
        bg = b_ref[:, g * SSD_N:(g + 1) * SSD_N]
        cg = c_ref[:, g * SSD_N:(g + 1) * SSD_N]
        s_prev = s_ref[g]
        lanes = [SSD_HEADS + d * SSD_HEADS + g * 4 + hh for hh in range(4)]
        dt_cols = [dtc_ref[:, p:p + 1] for p in lanes]
        acs_cols = [acsc_ref[:, p:p + 1] for p in lanes]
        acs_rows = [acsr_ref[p:p + 1, :] for p in lanes]
        tots = [r[:, end:end + 1] for r in acs_rows]

        xdt = xg * _expand_heads(dt_cols)
        xdt_b = xdt.astype(BF16)
        cb = lax.dot_general(cg, bg, (((1,), (1,)), ((), ())), preferred_element_type=F32)
        y = jnp.zeros((TB, 4 * SSD_P), F32)
        for hh in range(4):
            lm = jnp.exp(jnp.where(mask, acs_cols[hh] - acs_rows[hh], -jnp.inf))
            yh = _dot((cb * lm).astype(BF16), xdt_b)
            y = jnp.where((lane >= hh * SSD_P) & (lane < (hh + 1) * SSD_P), yh, y)
        y = y + _dot(cg, s_prev.astype(BF16)) * _expand_heads([jnp.exp(a) for a in acs_cols])
        o_ref[:, g * 256:(g + 1) * 256] = y.astype(o_ref.dtype)

        to_end = _expand_heads([jnp.exp(tots[hh] - acs_cols[hh]) for hh in range(4)])
        st = lax.dot_general(bg, (xdt * to_end).astype(BF16), (((0,), (0,)), ((), ())),
                             preferred_element_type=F32)
        cd = [jnp.exp(t) for t in tots]
        chunk_decay = jnp.where(lane1 < SSD_P, cd[0],
                                jnp.where(lane1 < 2 * SSD_P, cd[1], jnp.where(lane1 < 3 * SSD_P, cd[2], cd[3])))
        s_ref[g] = chunk_decay * s_prev + st


def _ssd_call(xbc, dtc, acsc, acsr, d):
    blk = _scan_block(d)
    small = pl.BlockSpec((TB, SMALL), lambda b, s: (blk(b, s), 0))
    rows = pl.BlockSpec((None, 64, TB), lambda b, s: (blk(b, s), 0, 0))
    return pl.pallas_call(
        functools.partial(_ssd_kernel, d=d),
        grid=(BATCH, SEQ_BLOCKS),
        in_specs=[pl.BlockSpec((TB, D), lambda b, s: (blk(b, s), 0)),
                  pl.BlockSpec((TB, 512), lambda b, s: (blk(b, s), 2)),
                  pl.BlockSpec((TB, 512), lambda b, s: (blk(b, s), 3)),
                  small, small, rows],
        out_specs=pl.BlockSpec((TB, D), lambda b, s: (blk(b, s), 0)),
        out_shape=jax.ShapeDtypeStruct((N_TOK, D), BF16),
        scratch_shapes=[pltpu.VMEM((SSD_GROUPS, SSD_N, 4 * SSD_P), F32)],
        compiler_params=_cparams(2),
        name="ssd_fwd" if d == 0 else "ssd_bwd",
    )(xbc, xbc, xbc, dtc, acsc, acsr)


def _pool_tables():
    masks = np.zeros((2, 4, TB, TB), np.float32)
    inv = np.zeros((2, 4, TB, 1), np.float32)
    for kind, length in ((0, GRID_W), (1, CTX)):
        for gi, w in enumerate(POOL_WINDOWS):
            for r in range(TB):
                base, c = (r // length) * length, r % length
                lo = min(max(c - w // 2, 0), length - 1)
                hi = min(max(c - w // 2 + w - 1, 0), length - 1)
                masks[kind, gi, r, base + lo:base + hi + 1] = 1.0
                inv[kind, gi, r, 0] = 1.0 / (hi - lo + 1)
    return jnp.asarray(masks, BF16), jnp.asarray(inv, F32)


ROUTE_FIELDS = 6


def _route_block(logits, ltri_ref, rt_ref, cnt_ref, cnt_scr):
    @pl.when(pl.program_id(0) == 0)
    def _():
        cnt_scr[...] = jnp.zeros_like(cnt_scr)

    lane = lax.broadcasted_iota(jnp.int32, logits.shape, 1).astype(F32)

    def top1(valid):
        v = jnp.max(jnp.where(valid, logits, -jnp.inf), axis=1, keepdims=True)
        i = jnp.min(jnp.where(valid & (logits == v), lane, float(SMALL)), axis=1, keepdims=True)
        return v, i

    is_g = lane < N_GROUPS_E
    gm, grp = top1(is_g)
    p_grp = 1.0 / jnp.sum(jnp.where(is_g, jnp.exp(logits - gm), 0.0), axis=1, keepdims=True)
    lo_lane = N_GROUPS_E + grp * EPG
    in_grp = (lane >= lo_lane) & (lane < lo_lane + EPG)
    v1, i1 = top1(in_grp)
    v2, i2 = top1(in_grp & (lane != i1))
    t = jnp.exp(v2 - v1)
    w1 = p_grp / (1.0 + t)
    w2 = p_grp * t / (1.0 + t)
    e1 = i1 - N_GROUPS_E
    e2 = i2 - N_GROUPS_E

    oh1 = lane == e1
    oh2 = lane == e2
    oh = jnp.where(oh1 | oh2, 1.0, 0.0)
    before = _dot(ltri_ref[...], oh.astype(BF16)) + cnt_scr[...]
    r1 = jnp.sum(jnp.where(oh1, before, 0.0), axis=1, keepdims=True)
    r2 = jnp.sum(jnp.where(oh2, before, 0.0), axis=1, keepdims=True)
    cnt_scr[...] = cnt_scr[...] + jnp.sum(oh, axis=0, keepdims=True)
    cnt_ref[...] = cnt_scr[...]
    rt = jnp.zeros(logits.shape, F32)
    for k, val in enumerate((e1, e2, w1, w2, r1, r2)):
        rt = jnp.where(lane == k, val, rt)
    rt_ref[...] = rt


def _merge_kernel(x_ref, pa_ref, o_ref, z_ref, mg0_ref, mg1_ref, mg2_ref, hf_ref, hb_ref, yf_ref, yb_ref, xs_ref,
                  pmask_ref, pinv_ref, poolw_ref, pscale_ref, mng_ref, dsk_ref, sng_ref, wbr_ref, wout_ref,
                  gate_ref, sh_ref, sc_ref, gffn_ref, wrh_ref, wrl_ref, brt_ref, ltri_ref,
                  xo_ref, h2_ref, rt_ref, cnt_ref, cnt_scr):
    parts = []
    for g in range(4):
        a_g = pa_ref[:, g * 256:(g + 1) * 256]
        pooled = _dot(pmask_ref[g], a_g) * pinv_ref[g] - a_g.astype(F32)
        parts.append(_dot(pooled.astype(BF16), poolw_ref[g]))
    pool = jnp.concatenate(parts, axis=1) * pscale_ref[...]

    hs = hf_ref[...].astype(F32) + hb_ref[...].astype(F32)
    hn = jnp.concatenate([_rms(hs[:, h * DH_M:(h + 1) * DH_M]) for h in range(HEADS_M)], axis=1)
    ml = _sigmoid(o_ref[...].astype(F32)) * (hn * mng_ref[...])

    y = yf_ref[...].astype(F32) + yb_ref[...].astype(F32) + dsk_ref[...] * xs_ref[...].astype(F32)
    sl = _rms(y * _silu(z_ref[...].astype(F32))) * sng_ref[...]

    acc = _sigmoid(mg0_ref[...].astype(F32)) * _dot(pool.astype(BF16), wbr_ref[0])
    acc = acc + _sigmoid(mg1_ref[...].astype(F32)) * _dot(ml.astype(BF16), wbr_ref[1])
    acc = acc + _sigmoid(mg2_ref[...].astype(F32)) * _dot(sl.astype(BF16), wbr_ref[2])
    xn = x_ref[...] + gate_ref[...] * _dot(acc.astype(BF16), wout_ref[...])
    xo_ref[...] = xn

    h2 = (_rms(xn) * gffn_ref[...]) * (1.0 + sc_ref[...]) + sh_ref[...]
    hi, lo = _split2(h2)
    h2f = hi.astype(F32)
    for cchunk in range(D // 128):
        h2_ref[pl.ds(cchunk, TB, stride=D // 128), :] = h2f[:, cchunk * 128:(cchunk + 1) * 128]
    logits = _dot(hi, wrh_ref[...]) + _dot(lo, wrh_ref[...]) + _dot(hi, wrl_ref[...]) + brt_ref[...]
    _route_block(logits, ltri_ref, rt_ref, cnt_ref, cnt_scr)


def _merge_call(x, urm, hf, hb, yf, yb, xbc, pmask, pinv, poolw, pscale, mng, dsk, sng, wbr, wout,
                mods3, gffn, wr_hi, wr_lo, br):
    ltri = jnp.asarray(np.tril(np.ones((TB, TB), np.float32), -1), BF16)
    kind = lambda j: jnp.where(j % SEQ_BLOCKS == 0, 1, 0)
    vec = _const_spec((1, D))
    return pl.pallas_call(
        _merge_kernel,
        grid=(N_BLOCKS,),
        in_specs=[_tok_spec(),
                  _tok_spec(D, 0), _tok_spec(D, 1), _tok_spec(D, 4),
                  _tok_spec(D, 5), _tok_spec(D, 6), _tok_spec(D, 7),
                  _tok_spec(), _tok_spec(), _tok_spec(), _tok_spec(),
                  _tok_spec(D, 0),
                  pl.BlockSpec((None, 4, TB, TB), lambda j: (kind(j), 0, 0, 0)),
                  pl.BlockSpec((None, 4, TB, 1), lambda j: (kind(j), 0, 0, 0)),
                  _const_spec((4, 256, 256)), vec, vec, vec, vec,
                  _const_spec((3, D, D)), _const_spec((D, D)),
                  _mod_spec(2), _mod_spec(3), _mod_spec(4), vec,
                  _const_spec((D, SMALL)), _const_spec((D, SMALL)), _const_spec((1, SMALL)),
                  _const_spec((TB, TB))],
        out_specs=[_tok_spec(), pl.BlockSpec((TB * ROW_TILES, 128), lambda j: (j, 0)), _tok_spec(SMALL),
                   _const_spec((1, SMALL))],
        out_shape=[jax.ShapeDtypeStruct((N_TOK, D), F32), jax.ShapeDtypeStruct((N_TOK * ROW_TILES, 128), F32),
                   jax.ShapeDtypeStruct((N_TOK, SMALL), F32), jax.ShapeDtypeStruct((1, SMALL), F32)],
        scratch_shapes=[pltpu.VMEM((1, SMALL), F32)],
        compiler_params=_cparams(1),
        name="branch_merge",
    )(x, urm, urm, urm, urm, urm, urm, hf, hb, yf, yb, xbc, pmask, pinv, poolw, pscale, mng, dsk, sng,
      wbr, wout, mods3, mods3, mods3, gffn, wr_hi, wr_lo, br, ltri)


def _moe_kernel(be_ref, nu_ref, tok0_ref, tokn_ref, h_hbm, wg_ref, wu_ref, wd_ref, o_ref,
                xbuf, wgb, wub, wdb, sem):
    i = pl.program_id(0)
    n_used = nu_ref[0]
    slot = i % 2

    def start_gather(tok_ref, s):
        for r in range(MOE_BLOCK):
            row0 = pl.multiple_of(tok_ref[0, r] * ROW_TILES, ROW_TILES)
            pltpu.make_async_copy(h_hbm.at[pl.ds(row0, ROW_TILES)],
                                  xbuf.at[s, pl.ds(r * ROW_TILES, ROW_TILES)], sem.at[s]).start()

    def wait_gather(s):
        pltpu.make_async_copy(xbuf.at[s], xbuf.at[s], sem.at[s]).wait()

    @pl.when((i == 0) & (n_used > 0))
    def _():
        start_gather(tok0_ref, 0)

    @pl.when((i < n_used) & ((i == 0) | (be_ref[i] != be_ref[jnp.maximum(i - 1, 0)])))
    def _():
        wgb[...] = wg_ref[...].astype(BF16)
        wub[...] = wu_ref[...].astype(BF16)
        wdb[...] = wd_ref[...].astype(BF16)

    @pl.when(i < n_used)
    def _():
        wait_gather(slot)
        x = jnp.concatenate([xbuf[slot, pl.ds(c, MOE_BLOCK, stride=ROW_TILES), :] for c in range(ROW_TILES)],
                            axis=1).astype(BF16)
        start_gather(tokn_ref, 1 - slot)
        gt = _dot(x, wgb[...])
        up = _dot(x, wub[...])
        act = (_silu(gt) * up).astype(BF16)
        o_ref[...] = _dot(act, wdb[...]).astype(o_ref.dtype)

    @pl.when(i == n_used - 1)
    def _():
        wait_gather(1 - slot)

    @pl.when(i >= n_used)
    def _():
        o_ref[...] = jnp.zeros_like(o_ref)


MOE_CAP = N_TOK * 2 + N_EXPERTS * MOE_BLOCK
MOE_NBLOCKS = MOE_CAP // MOE_BLOCK


def _moe_call(layer, block_expert, n_used, buf_tok, h2, w_gate, w_up, w_down):
    grid_spec = pltpu.PrefetchScalarGridSpec(
        num_scalar_prefetch=2,
        grid=(MOE_NBLOCKS,),
        in_specs=[pl.BlockSpec((None, 1, MOE_BLOCK), lambda i, be, nu: (0, 0, 0), memory_space=pltpu.SMEM),
                  pl.BlockSpec((None, 1, MOE_BLOCK), lambda i, be, nu: (jnp.minimum(i + 1, MOE_NBLOCKS - 1), 0, 0),
                               memory_space=pltpu.SMEM),
                  pl.BlockSpec(memory_space=pl.ANY),
                  pl.BlockSpec((None, None, D, D_EXPERT), lambda i, be, nu: (layer, be[i], 0, 0)),
                  pl.BlockSpec((None, None, D, D_EXPERT), lambda i, be, nu: (layer, be[i], 0, 0)),
                  pl.BlockSpec((None, None, D_EXPERT, D), lambda i, be, nu: (layer, be[i], 0, 0))],
        out_specs=pl.BlockSpec((MOE_BLOCK, D), lambda i, be, nu: (i, 0)),
        scratch_shapes=[pltpu.VMEM((2, MOE_BLOCK * ROW_TILES, 128), F32),
                        pltpu.VMEM((D, D_EXPERT), BF16), pltpu.VMEM((D, D_EXPERT), BF16),
                        pltpu.VMEM((D_EXPERT, D), BF16), pltpu.SemaphoreType.DMA((2,))],
    )
    return pl.pallas_call(
        _moe_kernel,
        grid_spec=grid_spec,
        out_shape=jax.ShapeDtypeStruct((MOE_CAP, D), BF16),
        compiler_params=_cparams(1),
        name="moe_experts",
    )(block_expert, n_used, buf_tok, buf_tok, h2, w_gate, w_up, w_down)


def _dispatch_plan(route, counts):
    cnt = counts[0, :N_EXPERTS].astype(jnp.int32)
    padded = (cnt + MOE_BLOCK - 1) // MOE_BLOCK * MOE_BLOCK
    pends = jnp.cumsum(padded)
    pstarts = pends - padded
    expert = route[:, 0:2].astype(jnp.int32)
    rank = route[:, 4:6].astype(jnp.int32)
    onehot = expert[:, :, None] == jnp.arange(N_EXPERTS, dtype=jnp.int32)
    pos = jnp.sum(jnp.where(onehot, pstarts, 0), axis=-1) + rank
    tok = jnp.broadcast_to(jnp.arange(N_TOK, dtype=jnp.int32)[:, None], (N_TOK, 2))
    buf_tok = jnp.zeros((MOE_CAP,), jnp.int32).at[pos.reshape(-1)].set(tok.reshape(-1))
    buf_tok = buf_tok.reshape(MOE_NBLOCKS, 1, MOE_BLOCK)
    block_start = jnp.arange(MOE_NBLOCKS, dtype=jnp.int32) * MOE_BLOCK
    block_expert = jnp.minimum(jnp.sum((pends[None, :] <= block_start[:, None]).astype(jnp.int32), axis=1),
                               N_EXPERTS - 1)
    n_used = (pends[-1] // MOE_BLOCK).reshape(1)
    return buf_tok, block_expert, n_used, pos


def _to_colmajor(t):
    c = t.shape[-1]
    t = t.reshape(BATCH, T_BATCH, c)
    lat = t[:, CTX:].reshape(BATCH, SEQ // GRID_W, GRID_W, c).swapaxes(1, 2).reshape(BATCH, SEQ, c)
    return jnp.concatenate([t[:, :CTX], lat], axis=1).reshape(N_TOK, c)


def _to_rowmajor(t):
    c = t.shape[-1]
    t = t.reshape(BATCH, T_BATCH, c)
    lat = t[:, CTX:].reshape(BATCH, GRID_W, SEQ // GRID_W, c).swapaxes(1, 2).reshape(BATCH, SEQ, c)
    return jnp.concatenate([t[:, :CTX], lat], axis=1).reshape(N_TOK, c)


def _lanes(*pieces):
    v = jnp.concatenate([jnp.asarray(p, F32).reshape(-1) for p in pieces])
    return jnp.pad(v, (0, SMALL - v.shape[0]))


def kernel(x, c, ctx, c_ctx, w_ada, b_ada, g_norm_mix, g_norm_ffn, w_in, pool_w, pool_scale, mlstm_conv_w,
           mlstm_conv_b, mlstm_gate_b, mlstm_norm_g, ssd_conv_w, ssd_conv_b, ssd_dt_bias, ssd_a_log, ssd_d,
           ssd_norm_g, w_branch, w_out, w_route_group, b_route_group, w_route_expert, b_route_expert,
           w_exp_gate, w_exp_up, w_exp_down, g_final):
    cvec = jnp.concatenate([c, c_ctx[None], jnp.zeros((3, D), F32)], axis=0)
    mods = _ada_table(cvec, w_ada, b_ada)
    xs = jnp.concatenate([ctx, x], axis=1).reshape(N_TOK, D)
    pmask, pinv = _pool_tables()
    row = lambda v: v.reshape(1, -1).astype(F32)
    w_in_bf = w_in.astype(BF16)

    h = small = None
    out = None
    for l in range(DEPTH):
        mods3 = mods[l].reshape(8, 1, 6 * D)
        wi = w_in_bf[l]
        sp = np.cumsum([0, D, D, D, D, D, 16, D, D, 512, 512, 32, 3 * D])
        piece = lambda i: wi[:, sp[i]:sp[i + 1]]
        w_rm = jnp.concatenate([piece(0), piece(4), piece(6), piece(8), piece(9), piece(7), piece(11)],
                               axis=1)
        w_cm = wi[:, sp[1]:sp[4]]
        w_small = jnp.pad(jnp.concatenate([w_in[l][:, sp[5]:sp[6]], w_in[l][:, sp[10]:sp[11]]], axis=1),
                          ((0, 0), (0, SMALL - 48)))
        ws_hi = w_small.astype(BF16)
        ws_lo = (w_small - ws_hi.astype(F32)).astype(BF16)

        if l == 0:
            h, small = _norm_call(xs, row(g_norm_mix[l]), mods3, ws_hi, ws_lo)

        h_cm = _to_colmajor(h)
        small_cm = _to_colmajor(small)
        u_rm = _mm_call(h, w_rm, "in_proj_rowmajor")
        u_cm = _mm_call(h_cm, w_cm, "in_proj_colmajor")

        post_m = jnp.concatenate([jnp.ones((D,), F32), jnp.full((D,), DH_M ** -0.5, F32)]).reshape(1, 2 * D)
        qk = _conv_call(u_cm, 0, 2 * D, mlstm_conv_w[l], row(mlstm_conv_b[l]), post_m, "mlstm_conv")
        gate_bias = _lanes(mlstm_gate_b[l])
        gate_bwd = _lanes(jnp.zeros((12,)), jnp.ones((4,)))
        g_valc, g_cumc, g_valr, g_cumr = _gate_call(small_cm, gate_bias, jnp.ones((SMALL,), F32), gate_bwd,
                                                    False, "mlstm_gates")
        h_f = _mlstm_call(qk, u_cm, g_valc, g_cumc, g_valr, g_cumr, 0)
        h_b = _mlstm_call(qk, u_cm, g_valc, g_cumc, g_valr, g_cumr, 1)
        h_f = _to_rowmajor(h_f)
        h_b = _to_rowmajor(h_b)

        xbc = _conv_call(u_rm, 2 * D, 2 * D, ssd_conv_w[l], row(ssd_conv_b[l]), jnp.ones((1, 2 * D), F32),
                         "ssd_conv")
        a_neg = -jnp.exp(ssd_a_log[l].astype(F32))
        dt_bias = _lanes(jnp.zeros((16,)), ssd_dt_bias[l])
        dt_scale = _lanes(jnp.zeros((16,)), a_neg)
        dt_bwd = _lanes(jnp.zeros((32,)), jnp.ones((16,)))
        s_dtc, s_acsc, _, s_acsr = _gate_call(small, dt_bias, dt_scale, dt_bwd, True, "ssd_gates")
        y_f = _ssd_call(xbc, s_dtc, s_acsc, s_acsr, 0)
        y_b = _ssd_call(xbc, s_dtc, s_acsc, s_acsr, 1)

        w_r = jnp.pad(jnp.concatenate([w_route_group[l], w_route_expert[l]], axis=1),
                      ((0, 0), (0, SMALL - N_GROUPS_E - N_EXPERTS)))
        wr_hi = w_r.astype(BF16)
        wr_lo = (w_r - wr_hi.astype(F32)).astype(BF16)
        b_r = _lanes(b_route_group[l], b_route_expert[l]).reshape(1, SMALL)
        dsk = jnp.repeat(ssd_d[l].astype(F32), SSD_P).reshape(1, D)
        xs, h2, route, counts = _merge_call(
            xs, u_rm, h_f, h_b, y_f, y_b, xbc, pmask, pinv, pool_w[l].astype(BF16), row(pool_scale[l]),
            row(mlstm_norm_g[l]), dsk, row(ssd_norm_g[l]), w_branch[l].astype(BF16), w_out[l].astype(BF16),
            mods3, row(g_norm_ffn[l]), wr_hi, wr_lo, b_r)

        buf_tok, block_expert, n_used, pos = _dispatch_plan(route, counts)
        yb = _moe_call(l, block_expert, n_used, buf_tok, h2, w_exp_gate, w_exp_up, w_exp_down)
        y0 = yb[pos[:, 0]]
        y1 = yb[pos[:, 1]]

        if l + 1 < DEPTH:
            wi_n = w_in[l + 1]
            w_small_n = jnp.pad(jnp.concatenate([wi_n[:, sp[5]:sp[6]], wi_n[:, sp[10]:sp[11]]], axis=1),
                                ((0, 0), (0, SMALL - 48)))
            wsn_hi = w_small_n.astype(BF16)
            wsn_lo = (w_small_n - wsn_hi.astype(F32)).astype(BF16)
            mods3_n = mods[l + 1].reshape(8, 1, 6 * D)
            xs, h, small = _resnorm_call(xs, y0, y1, route, mods3, row(g_norm_mix[l + 1]), mods3_n,
                                         wsn_hi, wsn_lo)
        else:
            out = _final_call(xs, y0, y1, route, mods3, row(g_final))
    return out.reshape(BATCH, SEQ, D)
```

```python
import functools
import math

import jax
import jax.numpy as jnp
import numpy as np
from jax import lax
from jax.experimental import pallas as pl
from jax.experimental.pallas import tpu as pltpu

F32 = jnp.float32
BF16 = jnp.bfloat16

D = 1024
BATCH = 4
SEQ = 4096
CTX = 256
DEPTH = 2
GRID_W = 64
EPS = 1e-6

TB = 256
SEQ_BLOCKS = (SEQ + CTX) // TB
T_BATCH = SEQ + CTX
N_TOK = BATCH * T_BATCH
N_BLOCKS = N_TOK // TB

POOL_WINDOWS = (2, 4, 8, 16)
HEADS_M = 4
DH_M = 256
SSD_HEADS = 16
SSD_P = 64
SSD_GROUPS = 4
SSD_N = 128
N_GROUPS_E = 4
EPG = 8
N_EXPERTS = 32
D_EXPERT = 512
MOE_BLOCK = 256
SMALL = 128
ROW_TILES = D // 128

VMEM_LIMIT = 56 * 1024 * 1024


def _cparams(n_axes):
    return pltpu.CompilerParams(dimension_semantics=("arbitrary",) * n_axes,
                                vmem_limit_bytes=VMEM_LIMIT)


def _dot(a, b):
    return jnp.dot(a, b, preferred_element_type=F32)


def _split2(x):
    hi = x.astype(BF16)
    lo = (x - hi.astype(F32)).astype(BF16)
    return hi, lo


def _split3(x):
    hi = x.astype(BF16)
    r = x - hi.astype(F32)
    mid = r.astype(BF16)
    lo = (r - mid.astype(F32)).astype(BF16)
    return hi, mid, lo


def _sigmoid(x):
    return 1.0 / (1.0 + jnp.exp(-x))


def _silu(x):
    return x * _sigmoid(x)


def _log1pexp_negabs(x):
    return jnp.log(1.0 + jnp.exp(-jnp.abs(x)))


def _log_sigmoid(x):
    return jnp.minimum(x, 0.0) - _log1pexp_negabs(x)


def _softplus(x):
    return jnp.maximum(x, 0.0) + _log1pexp_negabs(x)


def _rms(x):
    return x * lax.rsqrt(jnp.mean(x * x, axis=-1, keepdims=True) + EPS)


def _mod_row(j):
    return jnp.where(j % SEQ_BLOCKS == 0, BATCH, j // SEQ_BLOCKS)


def _ada_kernel(c_ref, w_ref, b_ref, o_ref):
    c = c_ref[...]
    s_hi, s_lo = _split2(_silu(c))
    w_hi, w_lo = _split2(w_ref[...])
    o_ref[...] = _dot(s_hi, w_hi) + _dot(s_lo, w_hi) + _dot(s_hi, w_lo) + b_ref[...]


def _ada_table(cvec, w_ada, b_ada):
    tn = 1024
    return pl.pallas_call(
        _ada_kernel,
        grid=(DEPTH, 6 * D // tn),
        in_specs=[pl.BlockSpec((8, D), lambda l, j: (0, 0)),
                  pl.BlockSpec((None, D, tn), lambda l, j: (l, 0, j)),
                  pl.BlockSpec((None, 1, tn), lambda l, j: (l, 0, j))],
        out_specs=pl.BlockSpec((None, 8, tn), lambda l, j: (l, 0, j)),
        out_shape=jax.ShapeDtypeStruct((DEPTH, 8, 6 * D), F32),
        compiler_params=_cparams(2),
        name="ada_table",
    )(cvec, w_ada, b_ada.reshape(DEPTH, 1, 6 * D))


def _norm_mod_small(x, g_ref, sh_ref, sc_ref, wsh_ref, wsl_ref, h_ref, small_ref):
    h = _rms(x) * g_ref[...]
    h = h * (1.0 + sc_ref[...]) + sh_ref[...]
    h_hi, h_lo = _split2(h)
    h_ref[...] = h_hi
    small_ref[...] = _dot(h_hi, wsh_ref[...]) + _dot(h_lo, wsh_ref[...]) + _dot(h_hi, wsl_ref[...])


def _norm_kernel(x_ref, g_ref, sh_ref, sc_ref, wsh_ref, wsl_ref, h_ref, small_ref):
    _norm_mod_small(x_ref[...], g_ref, sh_ref, sc_ref, wsh_ref, wsl_ref, h_ref, small_ref)


def _moe_residual(x_ref, ya_ref, yb_ref, rt_ref, gate_ref):
    y = rt_ref[:, 2:3] * ya_ref[...].astype(F32) + rt_ref[:, 3:4] * yb_ref[...].astype(F32)
    return x_ref[...] + gate_ref[...] * y


def _resnorm_kernel(x_ref, ya_ref, yb_ref, rt_ref, gate_ref, g_ref, sh_ref, sc_ref, wsh_ref, wsl_ref,
                    xo_ref, h_ref, small_ref):
    x = _moe_residual(x_ref, ya_ref, yb_ref, rt_ref, gate_ref)
    xo_ref[...] = x
    _norm_mod_small(x, g_ref, sh_ref, sc_ref, wsh_ref, wsl_ref, h_ref, small_ref)


def _mod_spec(chunk):
    return pl.BlockSpec((None, 1, D), lambda j: (_mod_row(j), 0, chunk))


def _tok_spec(width=D, col=0):
    return pl.BlockSpec((TB, width), lambda j: (j, col))


def _const_spec(shape):
    nd = len(shape)
    return pl.BlockSpec(shape, lambda j: (0,) * nd)


def _norm_call(x, g, mods3, ws_hi, ws_lo):
    return pl.pallas_call(
        _norm_kernel,
        grid=(N_BLOCKS,),
        in_specs=[_tok_spec(), _const_spec((1, D)), _mod_spec(0), _mod_spec(1),
                  _const_spec((D, SMALL)), _const_spec((D, SMALL))],
        out_specs=[_tok_spec(), _tok_spec(SMALL)],
        out_shape=[jax.ShapeDtypeStruct((N_TOK, D), BF16), jax.ShapeDtypeStruct((N_TOK, SMALL), F32)],
        compiler_params=_cparams(1),
        name="norm_mod",
    )(x, g, mods3, mods3, ws_hi, ws_lo)


def _resnorm_call(x, ya, yb, route, mods3_prev, g, mods3, ws_hi, ws_lo):
    return pl.pallas_call(
        _resnorm_kernel,
        grid=(N_BLOCKS,),
        in_specs=[_tok_spec(), _tok_spec(), _tok_spec(), _tok_spec(SMALL),
                  pl.BlockSpec((None, 1, D), lambda j: (_mod_row(j), 0, 5)),
                  _const_spec((1, D)), _mod_spec(0), _mod_spec(1),
                  _const_spec((D, SMALL)), _const_spec((D, SMALL))],
        out_specs=[_tok_spec(), _tok_spec(), _tok_spec(SMALL)],
        out_shape=[jax.ShapeDtypeStruct((N_TOK, D), F32), jax.ShapeDtypeStruct((N_TOK, D), BF16),
                   jax.ShapeDtypeStruct((N_TOK, SMALL), F32)],
        compiler_params=_cparams(1),
        name="residual_norm_mod",
    )(x, ya, yb, route, mods3_prev, g, mods3, mods3, ws_hi, ws_lo)


def _final_kernel(x_ref, ya_ref, yb_ref, rt_ref, gate_ref, g_ref, o_ref):
    o_ref[...] = _rms(_moe_residual(x_ref, ya_ref, yb_ref, rt_ref, gate_ref)) * g_ref[...]


def _final_call(x, ya, yb, route, mods3, g_final):
    lat = lambda b, s: (b * SEQ_BLOCKS + 1 + s, 0)
    spec = pl.BlockSpec((TB, D), lat)
    return pl.pallas_call(
        _final_kernel,
        grid=(BATCH, SEQ // TB),
        in_specs=[spec, spec, spec, pl.BlockSpec((TB, SMALL), lat),
                  pl.BlockSpec((None, 1, D), lambda b, s: (b, 0, 5)),
                  pl.BlockSpec((1, D), lambda b, s: (0, 0))],
        out_specs=pl.BlockSpec((TB, D), lambda b, s: (b * (SEQ // TB) + s, 0)),
        out_shape=jax.ShapeDtypeStruct((BATCH * SEQ, D), F32),
        compiler_params=_cparams(2),
        name="final_norm",
    )(x, ya, yb, route, mods3, g_final)


def _mm_kernel(h_ref, w_ref, o_ref):
    o_ref[...] = _dot(h_ref[...], w_ref[...]).astype(o_ref.dtype)


def _mm_call(h, w, name):
    n = w.shape[1]
    tm, tn = 1024, 512
    return pl.pallas_call(
        _mm_kernel,
        grid=(n // tn, N_TOK // tm),
        in_specs=[pl.BlockSpec((tm, D), lambda j, i: (i, 0)),
                  pl.BlockSpec((D, tn), lambda j, i: (0, j))],
        out_specs=pl.BlockSpec((tm, tn), lambda j, i: (i, j)),
        out_shape=jax.ShapeDtypeStruct((N_TOK, n), BF16),
        compiler_params=_cparams(2),
        name=name,
    )(h, w)


CONV_HALO = 16


def _conv_kernel(x_ref, w_ref, b_ref, post_ref, o_ref):
    w = w_ref[...]
    bias = b_ref[...]
    post = post_ref[...]
    row = lax.broadcasted_iota(jnp.int32, (TB, 1), 0)
    for k in range(SEQ_BLOCKS):
        r0 = k * TB
        lo = max(r0 - CONV_HALO, 0)
        hi = min(r0 + TB + CONV_HALO, T_BATCH)
        off = r0 - lo
        n = hi - lo
        ext = x_ref[lo:hi, :].astype(F32)
        xm2 = pltpu.roll(ext, 2, 0)[off:off + TB]
        xm1 = pltpu.roll(ext, 1, 0)[off:off + TB]
        x0 = ext[off:off + TB]
        xp1 = pltpu.roll(ext, n - 1, 0)[off:off + TB]
        if k in (0, 1):
            xm2 = jnp.where(row >= 2, xm2, 0.0)
            xm1 = jnp.where(row >= 1, xm1, 0.0)
        if k in (0, SEQ_BLOCKS - 1):
            xp1 = jnp.where(row <= TB - 2, xp1, 0.0)
        y = bias + w[0:1] * xm2 + w[1:2] * xm1 + w[2:3] * x0 + w[3:4] * xp1
        o_ref[r0:r0 + TB, :] = (_silu(y) * post).astype(o_ref.dtype)


def _conv_call(u, col0, width, w, b, post, name):
    tc = 512
    cb0 = col0 // tc
    return pl.pallas_call(
        _conv_kernel,
        grid=(BATCH, width // tc),
        in_specs=[pl.BlockSpec((T_BATCH, tc), lambda bi, c: (bi, cb0 + c)),
                  pl.BlockSpec((4, tc), lambda bi, c: (0, c)),
                  pl.BlockSpec((1, tc), lambda bi, c: (0, c)),
                  pl.BlockSpec((1, tc), lambda bi, c: (0, c))],
        out_specs=pl.BlockSpec((T_BATCH, tc), lambda bi, c: (bi, c)),
        out_shape=jax.ShapeDtypeStruct((N_TOK, width), BF16),
        compiler_params=_cparams(2),
        name=name,
    )(u, w, b, post)


def _tri_dot_cols(tri, x):
    a, b, c = _split3(x)
    return _dot(tri, a) + _dot(tri, b) + _dot(tri, c)


def _tri_dot_rows(x, tri):
    a, b, c = _split3(x)
    return _dot(a, tri) + _dot(b, tri) + _dot(c, tri)


def _gate_kernel(pre_ref, pret_ref, bc_ref, br_ref, ac_ref, ar_ref, bwdc_ref, bwdr_ref, tril_ref, triu_ref,
                 valc_ref, cumc_ref, valr_ref, cumr_ref, *, ssd):
    tril = tril_ref[...]
    triu = triu_ref[...]

    def act(v):
        if ssd:
            val = _softplus(v)
            return val, val
        return v, _log_sigmoid(v)

    vc, dc = act(pre_ref[...] + bc_ref[...])
    dc = dc * ac_ref[...]
    valc_ref[...] = vc
    cumc_ref[...] = jnp.where(bwdc_ref[...] > 0.5, _tri_dot_cols(triu, dc), _tri_dot_cols(tril, dc))

    vr, dr = act(pret_ref[...] + br_ref[...])
    dr = dr * ar_ref[...]
    valr_ref[...] = vr
    cumr_ref[...] = jnp.where(bwdr_ref[...] > 0.5, _tri_dot_rows(dr, tril), _tri_dot_rows(dr, triu))


def _gate_call(small, bias, scale, bwd, ssd, name):
    small_t = small.reshape(N_BLOCKS, TB, SMALL)[:, :, :64].transpose(0, 2, 1)
    tri = np.tril(np.ones((TB, TB), np.float32))
    tril = jnp.asarray(tri, BF16)
    triu = jnp.asarray(tri.T, BF16)
    col = lambda v: v.reshape(1, SMALL).astype(F32)
    rowv = lambda v: v[:64].reshape(64, 1).astype(F32)
    cs = lambda shape: pl.BlockSpec(shape, lambda j: (0,) * len(shape))
    return pl.pallas_call(
        functools.partial(_gate_kernel, ssd=ssd),
        grid=(N_BLOCKS,),
        in_specs=[_tok_spec(SMALL), pl.BlockSpec((None, 64, TB), lambda j: (j, 0, 0)),
                  cs((1, SMALL)), cs((64, 1)), cs((1, SMALL)), cs((64, 1)), cs((1, SMALL)), cs((64, 1)),
                  cs((TB, TB)), cs((TB, TB))],
        out_specs=[_tok_spec(SMALL), _tok_spec(SMALL),
                   pl.BlockSpec((None, 64, TB), lambda j: (j, 0, 0)),
                   pl.BlockSpec((None, 64, TB), lambda j: (j, 0, 0))],
        out_shape=[jax.ShapeDtypeStruct((N_TOK, SMALL), F32), jax.ShapeDtypeStruct((N_TOK, SMALL), F32),
                   jax.ShapeDtypeStruct((N_BLOCKS, 64, TB), F32), jax.ShapeDtypeStruct((N_BLOCKS, 64, TB), F32)],
        compiler_params=_cparams(1),
        name=name,
    )(small, small_t, col(bias), rowv(bias), col(scale), rowv(scale), col(bwd), rowv(bwd), tril, triu)


def _scan_block(d):
    if d == 0:
        return lambda b, s: b * SEQ_BLOCKS + s
    return lambda b, s: b * SEQ_BLOCKS + jnp.where(s == 0, 0, SEQ_BLOCKS - s)


def _causal_mask(d):
    t = lax.broadcasted_iota(jnp.int32, (TB, TB), 0)
    s = lax.broadcasted_iota(jnp.int32, (TB, TB), 1)
    return (s <= t) if d == 0 else (s >= t)


def _mlstm_kernel(q_ref, k_ref, v_ref, valc_ref, cumc_ref, valr_ref, cumr_ref, o_ref,
                  c_ref, n_ref, m_ref, *, d):
    @pl.when(pl.program_id(1) == 0)
    def _():
        c_ref[...] = jnp.zeros_like(c_ref)
        n_ref[...] = jnp.zeros_like(n_ref)
        m_ref[...] = jnp.zeros_like(m_ref)

    mask = _causal_mask(d)
    end = TB - 1 if d == 0 else 0
    for h in range(HEADS_M):
        p = d * HEADS_M + h
        cs = slice(h * DH_M, (h + 1) * DH_M)
        q = q_ref[:, cs]
        k = k_ref[:, cs]
        v = v_ref[:, cs]
        i_c = valc_ref[:, p:p + 1]
        cum_c = cumc_ref[:, 8 + p:9 + p]
        i_r = valr_ref[p:p + 1, :]
        cum_r = cumr_ref[8 + p:9 + p, :]
        total = cum_r[:, end:end + 1]
        m_prev = m_ref[h]
        c_st = c_ref[h]
        n_st = n_ref[h]

        dmat = jnp.where(mask, cum_c - cum_r + i_r, -jnp.inf)
        g = cum_c + m_prev
        m_t = jnp.maximum(g, jnp.max(dmat, axis=1, keepdims=True))
        w = jnp.exp(dmat - m_t)
        sg = jnp.exp(g - m_t)
        a = w * lax.dot_general(q, k, (((1,), (1,)), ((), ())), preferred_element_type=F32)
        num = _dot(a.astype(BF16), v) + sg * _dot(q, c_st.astype(BF16))
        den = jnp.sum(a, axis=1, keepdims=True) + sg * jnp.sum(q.astype(F32) * n_st, axis=1, keepdims=True)
        o_ref[:, cs] = (num / jnp.maximum(jnp.abs(den), jnp.exp(-m_t))).astype(o_ref.dtype)

        wl_log = total - cum_c + i_c
        m_new = jnp.maximum(total + m_prev, jnp.max(wl_log, axis=0, keepdims=True))
        wl = jnp.exp(wl_log - m_new)
        decay = jnp.exp(total + m_prev - m_new)
        kf = k.astype(F32)
        wv = (wl * v.astype(F32)).astype(BF16)
        c_ref[h] = decay * c_st + lax.dot_general(k, wv, (((0,), (0,)), ((), ())), preferred_element_type=F32)
        n_ref[h] = decay * n_st + jnp.sum(wl * kf, axis=0, keepdims=True)
        m_ref[h] = m_new


def _mlstm_call(qk, ucm, valc, cumc, valr, cumr, d):
    blk = _scan_block(d)
    tok = lambda col: pl.BlockSpec((TB, D), lambda b, s: (blk(b, s), col))
    small = pl.BlockSpec((TB, SMALL), lambda b, s: (blk(b, s), 0))
    rows = pl.BlockSpec((None, 64, TB), lambda b, s: (blk(b, s), 0, 0))
    return pl.pallas_call(
        functools.partial(_mlstm_kernel, d=d),
        grid=(BATCH, SEQ_BLOCKS),
        in_specs=[tok(0), tok(1), tok(2), small, small, rows, rows],
        out_specs=tok(0),
        out_shape=jax.ShapeDtypeStruct((N_TOK, D), BF16),
        scratch_shapes=[pltpu.VMEM((HEADS_M, DH_M, DH_M), F32), pltpu.VMEM((HEADS_M, 1, DH_M), F32),
                        pltpu.VMEM((HEADS_M, 1, 1), F32)],
        compiler_params=_cparams(2),
        name="mlstm_fwd" if d == 0 else "mlstm_bwd",
    )(qk, qk, ucm, valc, cumc, valr, cumr)


def _expand_heads(cols):
    lane = lax.broadcasted_iota(jnp.int32, (TB, 4 * SSD_P), 1)
    return jnp.where(lane < SSD_P, cols[0],
                     jnp.where(lane < 2 * SSD_P, cols[1], jnp.where(lane < 3 * SSD_P, cols[2], cols[3])))


def _ssd_kernel(x_ref, b_ref, c_ref, dtc_ref, acsc_ref, acsr_ref, o_ref, s_ref, *, d):
    @pl.when(pl.program_id(1) == 0)
    def _():
        s_ref[...] = jnp.zeros_like(s_ref)

    mask = _causal_mask(d)
    end = TB - 1 if d == 0 else 0
    lane = lax.broadcasted_iota(jnp.int32, (TB, 4 * SSD_P), 1)
    lane1 = lax.broadcasted_iota(jnp.int32, (1, 4 * SSD_P), 1)
    for g in range(SSD_GROUPS):
        xg = x_ref[:, g * 256:(g + 1) * 256].astype(F32)
        bg = b_ref[:, g * SSD_N:(g + 1) * SSD_N]
        cg = c_ref[:, g * SSD_N:(g + 1) * SSD_N]
        s_prev = s_ref[g]
        lanes = [SSD_HEADS + d * SSD_HEADS + g * 4 + hh for hh in range(4)]
        dt_cols = [dtc_ref[:, p:p + 1] for p in lanes]
        acs_cols = [acsc_ref[:, p:p + 1] for p in lanes]
        acs_rows = [acsr_ref[p:p + 1, :] for p in lanes]
        tots = [r[:, end:end + 1] for r in acs_rows]

        xdt = xg * _expand_heads(dt_cols)
        xdt_b = xdt.astype(BF16)
        cb = lax.dot_general(cg, bg, (((1,), (1,)), ((), ())), preferred_element_type=F32)
        y = jnp.zeros((TB, 4 * SSD_P), F32)
        for hh in range(4):
            lm = jnp.exp(jnp.where(mask, acs_cols[hh] - acs_rows[hh], -jnp.inf))
            yh = _dot((cb * lm).astype(BF16), xdt_b)
            y = jnp.where((lane >= hh * SSD_P) & (lane < (hh + 1) * SSD_P), yh, y)
        y = y + _dot(cg, s_prev.astype(BF16)) * _expand_heads([jnp.exp(a) for a in acs_cols])
        o_ref[:, g * 256:(g + 1) * 256] = y.astype(o_ref.dtype)

        to_end = _expand_heads([jnp.exp(tots[hh] - acs_cols[hh]) for hh in range(4)])
        st = lax.dot_general(bg, (xdt * to_end).astype(BF16), (((0,), (0,)), ((), ())),
                             preferred_element_type=F32)
        cd = [jnp.exp(t) for t in tots]
        chunk_decay = jnp.where(lane1 < SSD_P, cd[0],
                                jnp.where(lane1 < 2 * SSD_P, cd[1], jnp.where(lane1 < 3 * SSD_P, cd[2], cd[3])))
        s_ref[g] = chunk_decay * s_prev + st


def _ssd_call(xbc, dtc, acsc, acsr, d):
    blk = _scan_block(d)
    small = pl.BlockSpec((TB, SMALL), lambda b, s: (blk(b, s), 0))
    rows = pl.BlockSpec((None, 64, TB), lambda b, s: (blk(b, s), 0, 0))
    return pl.pallas_call(
        functools.partial(_ssd_kernel, d=d),
        grid=(BATCH, SEQ_BLOCKS),
        in_specs=[pl.BlockSpec((TB, D), lambda b, s: (blk(b, s), 0)),
                  pl.BlockSpec((TB, 512), lambda b, s: (blk(b, s), 2)),
                  pl.BlockSpec((TB, 512), lambda b, s: (blk(b, s), 3)),
                  small, small, rows],
        out_specs=pl.BlockSpec((TB, D), lambda b, s: (blk(b, s), 0)),
        out_shape=jax.ShapeDtypeStruct((N_TOK, D), BF16),
        scratch_shapes=[pltpu.VMEM((SSD_GROUPS, SSD_N, 4 * SSD_P), F32)],
        compiler_params=_cparams(2),
        name="ssd_fwd" if d == 0 else "ssd_bwd",
    )(xbc, xbc, xbc, dtc, acsc, acsr)


def _pool_tables():
    masks = np.zeros((2, 4, TB, TB), np.float32)
    inv = np.zeros((2, 4, TB, 1), np.float32)
    for kind, length in ((0, GRID_W), (1, CTX)):
        for gi, w in enumerate(POOL_WINDOWS):
            for r in range(TB):
                base, c = (r // length) * length, r % length
                lo = min(max(c - w // 2, 0), length - 1)
                hi = min(max(c - w // 2 + w - 1, 0), length - 1)
                masks[kind, gi, r, base + lo:base + hi + 1] = 1.0
                inv[kind, gi, r, 0] = 1.0 / (hi - lo + 1)
    return jnp.asarray(masks, BF16), jnp.asarray(inv, F32)


ROUTE_FIELDS = 6


def _route_block(logits, ltri_ref, rt_ref, cnt_ref, cnt_scr):
    @pl.when(pl.program_id(0) == 0)
    def _():
        cnt_scr[...] = jnp.zeros_like(cnt_scr)

    lane = lax.broadcasted_iota(jnp.int32, logits.shape, 1).astype(F32)

    def top1(valid):
        v = jnp.max(jnp.where(valid, logits, -jnp.inf), axis=1, keepdims=True)
        i = jnp.min(jnp.where(valid & (logits == v), lane, float(SMALL)), axis=1, keepdims=True)
        return v, i

    is_g = lane < N_GROUPS_E
    gm, grp = top1(is_g)
    p_grp = 1.0 / jnp.sum(jnp.where(is_g, jnp.exp(logits - gm), 0.0), axis=1, keepdims=True)
    lo_lane = N_GROUPS_E + grp * EPG
    in_grp = (lane >= lo_lane) & (lane < lo_lane + EPG)
    v1, i1 = top1(in_grp)
    v2, i2 = top1(in_grp & (lane != i1))
    t = jnp.exp(v2 - v1)
    w1 = p_grp / (1.0 + t)
    w2 = p_grp * t / (1.0 + t)
    e1 = i1 - N_GROUPS_E
    e2 = i2 - N_GROUPS_E

    oh1 = lane == e1
    oh2 = lane == e2
    oh = jnp.where(oh1 | oh2, 1.0, 0.0)
    before = _dot(ltri_ref[...], oh.astype(BF16)) + cnt_scr[...]
    r1 = jnp.sum(jnp.where(oh1, before, 0.0), axis=1, keepdims=True)
    r2 = jnp.sum(jnp.where(oh2, before, 0.0), axis=1, keepdims=True)
    cnt_scr[...] = cnt_scr[...] + jnp.sum(oh, axis=0, keepdims=True)
    cnt_ref[...] = cnt_scr[...]
    rt = jnp.zeros(logits.shape, F32)
    for k, val in enumerate((e1, e2, w1, w2, r1, r2)):
        rt = jnp.where(lane == k, val, rt)
    rt_ref[...] = rt


def _merge_kernel(x_ref, pa_ref, o_ref, z_ref, mg0_ref, mg1_ref, mg2_ref, hf_ref, hb_ref, yf_ref, yb_ref, xs_ref,
                  pmask_ref, pinv_ref, poolw_ref, pscale_ref, mng_ref, dsk_ref, sng_ref, wbr_ref, wout_ref,
                  gate_ref, sh_ref, sc_ref, gffn_ref, wrh_ref, wrl_ref, brt_ref, ltri_ref,
                  xo_ref, h2_ref, rt_ref, cnt_ref, cnt_scr):
    parts = []
    for g in range(4):
        a_g = pa_ref[:, g * 256:(g + 1) * 256]
        pooled = _dot(pmask_ref[g], a_g) * pinv_ref[g] - a_g.astype(F32)
        parts.append(_dot(pooled.astype(BF16), poolw_ref[g]))
    pool = jnp.concatenate(parts, axis=1) * pscale_ref[...]

    hs = hf_ref[...].astype(F32) + hb_ref[...].astype(F32)
    hn = jnp.concatenate([_rms(hs[:, h * DH_M:(h + 1) * DH_M]) for h in range(HEADS_M)], axis=1)
    ml = _sigmoid(o_ref[...].astype(F32)) * (hn * mng_ref[...])

    y = yf_ref[...].astype(F32) + yb_ref[...].astype(F32) + dsk_ref[...] * xs_ref[...].astype(F32)
    sl = _rms(y * _silu(z_ref[...].astype(F32))) * sng_ref[...]

    acc = _sigmoid(mg0_ref[...].astype(F32)) * _dot(pool.astype(BF16), wbr_ref[0])
    acc = acc + _sigmoid(mg1_ref[...].astype(F32)) * _dot(ml.astype(BF16), wbr_ref[1])
    acc = acc + _sigmoid(mg2_ref[...].astype(F32)) * _dot(sl.astype(BF16), wbr_ref[2])
    xn = x_ref[...] + gate_ref[...] * _dot(acc.astype(BF16), wout_ref[...])
    xo_ref[...] = xn

    h2 = (_rms(xn) * gffn_ref[...]) * (1.0 + sc_ref[...]) + sh_ref[...]
    hi, lo = _split2(h2)
    h2f = hi.astype(F32)
    for cchunk in range(D // 128):
        h2_ref[pl.ds(cchunk, TB, stride=D // 128), :] = h2f[:, cchunk * 128:(cchunk + 1) * 128]
    logits = _dot(hi, wrh_ref[...]) + _dot(lo, wrh_ref[...]) + _dot(hi, wrl_ref[...]) + brt_ref[...]
    _route_block(logits, ltri_ref, rt_ref, cnt_ref, cnt_scr)


def _merge_call(x, urm, hf, hb, yf, yb, xbc, pmask, pinv, poolw, pscale, mng, dsk, sng, wbr, wout,
                mods3, gffn, wr_hi, wr_lo, br):
    ltri = jnp.asarray(np.tril(np.ones((TB, TB), np.float32), -1), BF16)
    kind = lambda j: jnp.where(j % SEQ_BLOCKS == 0, 1, 0)
    vec = _const_spec((1, D))
    return pl.pallas_call(
        _merge_kernel,
        grid=(N_BLOCKS,),
        in_specs=[_tok_spec(),
                  _tok_spec(D, 0), _tok_spec(D, 1), _tok_spec(D, 4),
                  _tok_spec(D, 5), _tok_spec(D, 6), _tok_spec(D, 7),
                  _tok_spec(), _tok_spec(), _tok_spec(), _tok_spec(),
                  _tok_spec(D, 0),
                  pl.BlockSpec((None, 4, TB, TB), lambda j: (kind(j), 0, 0, 0)),
                  pl.BlockSpec((None, 4, TB, 1), lambda j: (kind(j), 0, 0, 0)),
                  _const_spec((4, 256, 256)), vec, vec, vec, vec,
                  _const_spec((3, D, D)), _const_spec((D, D)),
                  _mod_spec(2), _mod_spec(3), _mod_spec(4), vec,
                  _const_spec((D, SMALL)), _const_spec((D, SMALL)), _const_spec((1, SMALL)),
                  _const_spec((TB, TB))],
        out_specs=[_tok_spec(), pl.BlockSpec((TB * ROW_TILES, 128), lambda j: (j, 0)), _tok_spec(SMALL),
                   _const_spec((1, SMALL))],
        out_shape=[jax.ShapeDtypeStruct((N_TOK, D), F32), jax.ShapeDtypeStruct((N_TOK * ROW_TILES, 128), F32),
                   jax.ShapeDtypeStruct((N_TOK, SMALL), F32), jax.ShapeDtypeStruct((1, SMALL), F32)],
        scratch_shapes=[pltpu.VMEM((1, SMALL), F32)],
        compiler_params=_cparams(1),
        name="branch_merge",
    )(x, urm, urm, urm, urm, urm, urm, hf, hb, yf, yb, xbc, pmask, pinv, poolw, pscale, mng, dsk, sng,
      wbr, wout, mods3, mods3, mods3, gffn, wr_hi, wr_lo, br, ltri)


GATHER_AHEAD = 2
GATHER_SLOTS = GATHER_AHEAD + 1
GATHER_DMA_PRIORITY = 1


def _moe_kernel(be_ref, nu_ref, tok0_ref, tok1_ref, tokn_ref, h_hbm, wg_ref, wu_ref, wd_ref, o_ref,
                xbuf, wgb, wub, wdb, sem):
    i = pl.program_id(0)
    n_used = nu_ref[0]
    slot = i % GATHER_SLOTS

    def start_gather(tok_ref, s):
        for r in range(MOE_BLOCK):
            row0 = pl.multiple_of(tok_ref[0, r] * ROW_TILES, ROW_TILES)
            pltpu.make_async_copy(h_hbm.at[pl.ds(row0, ROW_TILES)],
                                  xbuf.at[s, pl.ds(r * ROW_TILES, ROW_TILES)],
                                  sem.at[s]).start(priority=GATHER_DMA_PRIORITY)

    def wait_gather(s):
        pltpu.make_async_copy(xbuf.at[s], xbuf.at[s], sem.at[s]).wait()

    @pl.when((i == 0) & (n_used > 0))
    def _():
        start_gather(tok0_ref, 0)
        start_gather(tok1_ref, 1)

    @pl.when((i < n_used) & ((i == 0) | (be_ref[i] != be_ref[jnp.maximum(i - 1, 0)])))
    def _():
        wgb[...] = wg_ref[...].astype(BF16)
        wub[...] = wu_ref[...].astype(BF16)
        wdb[...] = wd_ref[...].astype(BF16)

    @pl.when(i < n_used)
    def _():
        wait_gather(slot)
        x = jnp.concatenate([xbuf[slot, pl.ds(c, MOE_BLOCK, stride=ROW_TILES), :] for c in range(ROW_TILES)],
                            axis=1).astype(BF16)
        start_gather(tokn_ref, (i + GATHER_AHEAD) % GATHER_SLOTS)
        gt = _dot(x, wgb[...])
        up = _dot(x, wub[...])
        act = (_silu(gt) * up).astype(BF16)
        o_ref[...] = _dot(act, wdb[...]).astype(o_ref.dtype)

    @pl.when(i == n_used - 1)
    def _():
        for ahead in range(1, GATHER_AHEAD + 1):
            wait_gather((i + ahead) % GATHER_SLOTS)

    @pl.when(i >= n_used)
    def _():
        o_ref[...] = jnp.zeros_like(o_ref)


MOE_CAP = N_TOK * 2 + N_EXPERTS * MOE_BLOCK
MOE_NBLOCKS = MOE_CAP // MOE_BLOCK


def _moe_call(layer, block_expert, n_used, buf_tok, h2, w_gate, w_up, w_down):
    grid_spec = pltpu.PrefetchScalarGridSpec(
        num_scalar_prefetch=2,
        grid=(MOE_NBLOCKS,),
        in_specs=[pl.BlockSpec((None, 1, MOE_BLOCK), lambda i, be, nu: (0, 0, 0), memory_space=pltpu.SMEM),
                  pl.BlockSpec((None, 1, MOE_BLOCK), lambda i, be, nu: (1, 0, 0), memory_space=pltpu.SMEM),
                  pl.BlockSpec((None, 1, MOE_BLOCK),
                               lambda i, be, nu: (jnp.minimum(i + GATHER_AHEAD, MOE_NBLOCKS - 1), 0, 0),
                               memory_space=pltpu.SMEM),
                  pl.BlockSpec(memory_space=pl.ANY),
                  pl.BlockSpec((None, None, D, D_EXPERT), lambda i, be, nu: (layer, be[i], 0, 0)),
                  pl.BlockSpec((None, None, D, D_EXPERT), lambda i, be, nu: (layer, be[i], 0, 0)),
                  pl.BlockSpec((None, None, D_EXPERT, D), lambda i, be, nu: (layer, be[i], 0, 0))],
        out_specs=pl.BlockSpec((MOE_BLOCK, D), lambda i, be, nu: (i, 0)),
        scratch_shapes=[pltpu.VMEM((GATHER_SLOTS, MOE_BLOCK * ROW_TILES, 128), F32),
                        pltpu.VMEM((D, D_EXPERT), BF16), pltpu.VMEM((D, D_EXPERT), BF16),
                        pltpu.VMEM((D_EXPERT, D), BF16), pltpu.SemaphoreType.DMA((GATHER_SLOTS,))],
    )
    return pl.pallas_call(
        _moe_kernel,
        grid_spec=grid_spec,
        out_shape=jax.ShapeDtypeStruct((MOE_CAP, D), BF16),
        compiler_params=_cparams(1),
        name="moe_experts",
    )(block_expert, n_used, buf_tok, buf_tok, buf_tok, h2, w_gate, w_up, w_down)


def _dispatch_plan(route, counts):
    cnt = counts[0, :N_EXPERTS].astype(jnp.int32)
    padded = (cnt + MOE_BLOCK - 1) // MOE_BLOCK * MOE_BLOCK
    pends = jnp.cumsum(padded)
    pstarts = pends - padded
    expert = route[:, 0:2].astype(jnp.int32)
    rank = route[:, 4:6].astype(jnp.int32)
    onehot = expert[:, :, None] == jnp.arange(N_EXPERTS, dtype=jnp.int32)
    pos = jnp.sum(jnp.where(onehot, pstarts, 0), axis=-1) + rank
    tok = jnp.broadcast_to(jnp.arange(N_TOK, dtype=jnp.int32)[:, None], (N_TOK, 2))
    buf_tok = jnp.zeros((MOE_CAP,), jnp.int32).at[pos.reshape(-1)].set(tok.reshape(-1))
    buf_tok = buf_tok.reshape(MOE_NBLOCKS, 1, MOE_BLOCK)
    block_start = jnp.arange(MOE_NBLOCKS, dtype=jnp.int32) * MOE_BLOCK
    block_expert = jnp.minimum(jnp.sum((pends[None, :] <= block_start[:, None]).astype(jnp.int32), axis=1),
                               N_EXPERTS - 1)
    n_used = (pends[-1] // MOE_BLOCK).reshape(1)
    return buf_tok, block_expert, n_used, pos


def _to_colmajor(t):
    c = t.shape[-1]
    t = t.reshape(BATCH, T_BATCH, c)
    lat = t[:, CTX:].reshape(BATCH, SEQ // GRID_W, GRID_W, c).swapaxes(1, 2).reshape(BATCH, SEQ, c)
    return jnp.concatenate([t[:, :CTX], lat], axis=1).reshape(N_TOK, c)


def _to_rowmajor(t):
    c = t.shape[-1]
    t = t.reshape(BATCH, T_BATCH, c)
    lat = t[:, CTX:].reshape(BATCH, GRID_W, SEQ // GRID_W, c).swapaxes(1, 2).reshape(BATCH, SEQ, c)
    return jnp.concatenate([t[:, :CTX], lat], axis=1).reshape(N_TOK, c)


def _lanes(*pieces):
    v = jnp.concatenate([jnp.asarray(p, F32).reshape(-1) for p in pieces])
    return jnp.pad(v, (0, SMALL - v.shape[0]))


def kernel(x, c, ctx, c_ctx, w_ada, b_ada, g_norm_mix, g_norm_ffn, w_in, pool_w, pool_scale, mlstm_conv_w,
           mlstm_conv_b, mlstm_gate_b, mlstm_norm_g, ssd_conv_w, ssd_conv_b, ssd_dt_bias, ssd_a_log, ssd_d,
           ssd_norm_g, w_branch, w_out, w_route_group, b_route_group, w_route_expert, b_route_expert,
           w_exp_gate, w_exp_up, w_exp_down, g_final):
    cvec = jnp.concatenate([c, c_ctx[None], jnp.zeros((3, D), F32)], axis=0)
    mods = _ada_table(cvec, w_ada, b_ada)
    xs = jnp.concatenate([ctx, x], axis=1).reshape(N_TOK, D)
    pmask, pinv = _pool_tables()
    row = lambda v: v.reshape(1, -1).astype(F32)
    w_in_bf = w_in.astype(BF16)

    h = small = None
    out = None
    for l in range(DEPTH):
        mods3 = mods[l].reshape(8, 1, 6 * D)
        wi = w_in_bf[l]
        sp = np.cumsum([0, D, D, D, D, D, 16, D, D, 512, 512, 32, 3 * D])
        piece = lambda i: wi[:, sp[i]:sp[i + 1]]
        w_rm = jnp.concatenate([piece(0), piece(4), piece(6), piece(8), piece(9), piece(7), piece(11)],
                               axis=1)
        w_cm = wi[:, sp[1]:sp[4]]
        w_small = jnp.pad(jnp.concatenate([w_in[l][:, sp[5]:sp[6]], w_in[l][:, sp[10]:sp[11]]], axis=1),
                          ((0, 0), (0, SMALL - 48)))
        ws_hi = w_small.astype(BF16)
        ws_lo = (w_small - ws_hi.astype(F32)).astype(BF16)

        if l == 0:
            h, small = _norm_call(xs, row(g_norm_mix[l]), mods3, ws_hi, ws_lo)

        h_cm = _to_colmajor(h)
        small_cm = _to_colmajor(small)
        u_rm = _mm_call(h, w_rm, "in_proj_rowmajor")
        u_cm = _mm_call(h_cm, w_cm, "in_proj_colmajor")

        post_m = jnp.concatenate([jnp.ones((D,), F32), jnp.full((D,), DH_M ** -0.5, F32)]).reshape(1, 2 * D)
        qk = _conv_call(u_cm, 0, 2 * D, mlstm_conv_w[l], row(mlstm_conv_b[l]), post_m, "mlstm_conv")
        gate_bias = _lanes(mlstm_gate_b[l])
        gate_bwd = _lanes(jnp.zeros((12,)), jnp.ones((4,)))
        g_valc, g_cumc, g_valr, g_cumr = _gate_call(small_cm, gate_bias, jnp.ones((SMALL,), F32), gate_bwd,
                                                    False, "mlstm_gates")
        h_f = _mlstm_call(qk, u_cm, g_valc, g_cumc, g_valr, g_cumr, 0)
        h_b = _mlstm_call(qk, u_cm, g_valc, g_cumc, g_valr, g_cumr, 1)
        h_f = _to_rowmajor(h_f)
        h_b = _to_rowmajor(h_b)

        xbc = _conv_call(u_rm, 2 * D, 2 * D, ssd_conv_w[l], row(ssd_conv_b[l]), jnp.ones((1, 2 * D), F32),
                         "ssd_conv")
        a_neg = -jnp.exp(ssd_a_log[l].astype(F32))
        dt_bias = _lanes(jnp.zeros((16,)), ssd_dt_bias[l])
        dt_scale = _lanes(jnp.zeros((16,)), a_neg)
        dt_bwd = _lanes(jnp.zeros((32,)), jnp.ones((16,)))
        s_dtc, s_acsc, _, s_acsr = _gate_call(small, dt_bias, dt_scale, dt_bwd, True, "ssd_gates")
        y_f = _ssd_call(xbc, s_dtc, s_acsc, s_acsr, 0)
        y_b = _ssd_call(xbc, s_dtc, s_acsc, s_acsr, 1)

        w_r = jnp.pad(jnp.concatenate([w_route_group[l], w_route_expert[l]], axis=1),
                      ((0, 0), (0, SMALL - N_GROUPS_E - N_EXPERTS)))
        wr_hi = w_r.astype(BF16)
        wr_lo = (w_r - wr_hi.astype(F32)).astype(BF16)
        b_r = _lanes(b_route_group[l], b_route_expert[l]).reshape(1, SMALL)
        dsk = jnp.repeat(ssd_d[l].astype(F32), SSD_P).reshape(1, D)
        xs, h2, route, counts = _merge_call(
            xs, u_rm, h_f, h_b, y_f, y_b, xbc, pmask, pinv, pool_w[l].astype(BF16), row(pool_scale[l]),
            row(mlstm_norm_g[l]), dsk, row(ssd_norm_g[l]), w_branch[l].astype(BF16), w_out[l].astype(BF16),
            mods3, row(g_norm_ffn[l]), wr_hi, wr_lo, b_r)

        buf_tok, block_expert, n_used, pos = _dispatch_plan(route, counts)
        yb = _moe_call(l, block_expert, n_used, buf_tok, h2, w_exp_gate, w_exp_up, w_exp_down)
        y0 = yb[pos[:, 0]]
        y1 = yb[pos[:, 1]]

        if l + 1 < DEPTH:
            wi_n = w_in[l + 1]
            w_small_n = jnp.pad(jnp.concatenate([wi_n[:, sp[5]:sp[6]], wi_n[:, sp[10]:sp[11]]], axis=1),
                                ((0, 0), (0, SMALL - 48)))
            wsn_hi = w_small_n.astype(BF16)
            wsn_lo = (w_small_n - wsn_hi.astype(F32)).astype(BF16)
            mods3_n = mods[l + 1].reshape(8, 1, 6 * D)
            xs, h, small = _resnorm_call(xs, y0, y1, route, mods3, row(g_norm_mix[l + 1]), mods3_n,
                                         wsn_hi, wsn_lo)
        else:
            out = _final_call(xs, y0, y1, route, mods3, row(g_final))
    return out.reshape(BATCH, SEQ, D)
```

```python
import functools
import math

import jax
import jax.numpy as jnp
import numpy as np
from jax import lax
from jax.experimental import pallas as pl
from jax.experimental.pallas import tpu as pltpu

F32 = jnp.float32
BF16 = jnp.bfloat16

D = 1024
BATCH = 4
SEQ = 4096
CTX = 256
DEPTH = 2
GRID_W = 64
EPS = 1e-6

TB = 256
SEQ_BLOCKS = (SEQ + CTX) // TB
T_BATCH = SEQ + CTX
N_TOK = BATCH * T_BATCH
N_BLOCKS = N_TOK // TB

POOL_WINDOWS = (2, 4, 8, 16)
HEADS_M = 4
DH_M = 256
SSD_HEADS = 16
SSD_P = 64
SSD_GROUPS = 4
SSD_N = 128
N_GROUPS_E = 4
EPG = 8
N_EXPERTS = 32
D_EXPERT = 512
MOE_BLOCK = 256
SMALL = 128
ROW_TILES = D // 128

VMEM_LIMIT = 56 * 1024 * 1024


def _cparams(n_axes):
    return pltpu.CompilerParams(dimension_semantics=("arbitrary",) * n_axes,
                                vmem_limit_bytes=VMEM_LIMIT)


def _dot(a, b):
    return jnp.dot(a, b, preferred_element_type=F32)


def _split2(x):
    hi = x.astype(BF16)
    lo = (x - hi.astype(F32)).astype(BF16)
    return hi, lo


def _split3(x):
    hi = x.astype(BF16)
    r = x - hi.astype(F32)
    mid = r.astype(BF16)
    lo = (r - mid.astype(F32)).astype(BF16)
    return hi, mid, lo


def _sigmoid(x):
    return 1.0 / (1.0 + jnp.exp(-x))


def _silu(x):
    return x * _sigmoid(x)


def _log1pexp_negabs(x):
    return jnp.log(1.0 + jnp.exp(-jnp.abs(x)))


def _log_sigmoid(x):
    return jnp.minimum(x, 0.0) - _log1pexp_negabs(x)


def _softplus(x):
    return jnp.maximum(x, 0.0) + _log1pexp_negabs(x)


def _rms(x):
    return x * lax.rsqrt(jnp.mean(x * x, axis=-1, keepdims=True) + EPS)


def _mod_row(j):
    return jnp.where(j % SEQ_BLOCKS == 0, BATCH, j // SEQ_BLOCKS)


def _ada_kernel(c_ref, w_ref, b_ref, o_ref):
    c = c_ref[...]
    s_hi, s_lo = _split2(_silu(c))
    w_hi, w_lo = _split2(w_ref[...])
    o_ref[...] = _dot(s_hi, w_hi) + _dot(s_lo, w_hi) + _dot(s_hi, w_lo) + b_ref[...]


def _ada_table(cvec, w_ada, b_ada):
    tn = 1024
    return pl.pallas_call(
        _ada_kernel,
        grid=(DEPTH, 6 * D // tn),
        in_specs=[pl.BlockSpec((8, D), lambda l, j: (0, 0)),
                  pl.BlockSpec((None, D, tn), lambda l, j: (l, 0, j)),
                  pl.BlockSpec((None, 1, tn), lambda l, j: (l, 0, j))],
        out_specs=pl.BlockSpec((None, 8, tn), lambda l, j: (l, 0, j)),
        out_shape=jax.ShapeDtypeStruct((DEPTH, 8, 6 * D), F32),
        compiler_params=_cparams(2),
        name="ada_table",
    )(cvec, w_ada, b_ada.reshape(DEPTH, 1, 6 * D))


def _norm_mod_small(x, g_ref, sh_ref, sc_ref, wsh_ref, wsl_ref, h_ref, small_ref):
    h = _rms(x) * g_ref[...]
    h = h * (1.0 + sc_ref[...]) + sh_ref[...]
    h_hi, h_lo = _split2(h)
    h_ref[...] = h_hi
    small_ref[...] = _dot(h_hi, wsh_ref[...]) + _dot(h_lo, wsh_ref[...]) + _dot(h_hi, wsl_ref[...])


def _norm_kernel(x_ref, g_ref, sh_ref, sc_ref, wsh_ref, wsl_ref, h_ref, small_ref):
    _norm_mod_small(x_ref[...], g_ref, sh_ref, sc_ref, wsh_ref, wsl_ref, h_ref, small_ref)


def _moe_residual(x_ref, ya_ref, yb_ref, rt_ref, gate_ref):
    y = rt_ref[:, 2:3] * ya_ref[...].astype(F32) + rt_ref[:, 3:4] * yb_ref[...].astype(F32)
    return x_ref[...] + gate_ref[...] * y


def _resnorm_kernel(x_ref, ya_ref, yb_ref, rt_ref, gate_ref, g_ref, sh_ref, sc_ref, wsh_ref, wsl_ref,
                    xo_ref, h_ref, small_ref):
    x = _moe_residual(x_ref, ya_ref, yb_ref, rt_ref, gate_ref)
    xo_ref[...] = x
    _norm_mod_small(x, g_ref, sh_ref, sc_ref, wsh_ref, wsl_ref, h_ref, small_ref)


def _mod_spec(chunk):
    return pl.BlockSpec((None, 1, D), lambda j: (_mod_row(j), 0, chunk))


def _tok_spec(width=D, col=0):
    return pl.BlockSpec((TB, width), lambda j: (j, col))


def _const_spec(shape):
    nd = len(shape)
    return pl.BlockSpec(shape, lambda j: (0,) * nd)


def _norm_call(x, g, mods3, ws_hi, ws_lo):
    return pl.pallas_call(
        _norm_kernel,
        grid=(N_BLOCKS,),
        in_specs=[_tok_spec(), _const_spec((1, D)), _mod_spec(0), _mod_spec(1),
                  _const_spec((D, SMALL)), _const_spec((D, SMALL))],
        out_specs=[_tok_spec(), _tok_spec(SMALL)],
        out_shape=[jax.ShapeDtypeStruct((N_TOK, D), BF16), jax.ShapeDtypeStruct((N_TOK, SMALL), F32)],
        compiler_params=_cparams(1),
        name="norm_mod",
    )(x, g, mods3, mods3, ws_hi, ws_lo)


def _resnorm_call(x, ya, yb, route, mods3_prev, g, mods3, ws_hi, ws_lo):
    return pl.pallas_call(
        _resnorm_kernel,
        grid=(N_BLOCKS,),
        in_specs=[_tok_spec(), _tok_spec(), _tok_spec(), _tok_spec(SMALL),
                  pl.BlockSpec((None, 1, D), lambda j: (_mod_row(j), 0, 5)),
                  _const_spec((1, D)), _mod_spec(0), _mod_spec(1),
                  _const_spec((D, SMALL)), _const_spec((D, SMALL))],
        out_specs=[_tok_spec(), _tok_spec(), _tok_spec(SMALL)],
        out_shape=[jax.ShapeDtypeStruct((N_TOK, D), F32), jax.ShapeDtypeStruct((N_TOK, D), BF16),
                   jax.ShapeDtypeStruct((N_TOK, SMALL), F32)],
        compiler_params=_cparams(1),
        name="residual_norm_mod",
    )(x, ya, yb, route, mods3_prev, g, mods3, mods3, ws_hi, ws_lo)


def _final_kernel(x_ref, ya_ref, yb_ref, rt_ref, gate_ref, g_ref, o_ref):
    o_ref[...] = _rms(_moe_residual(x_ref, ya_ref, yb_ref, rt_ref, gate_ref)) * g_ref[...]


def _final_call(x, ya, yb, route, mods3, g_final):
    lat = lambda b, s: (b * SEQ_BLOCKS + 1 + s, 0)
    spec = pl.BlockSpec((TB, D), lat)
    return pl.pallas_call(
        _final_kernel,
        grid=(BATCH, SEQ // TB),
        in_specs=[spec, spec, spec, pl.BlockSpec((TB, SMALL), lat),
                  pl.BlockSpec((None, 1, D), lambda b, s: (b, 0, 5)),
                  pl.BlockSpec((1, D), lambda b, s: (0, 0))],
        out_specs=pl.BlockSpec((TB, D), lambda b, s: (b * (SEQ // TB) + s, 0)),
        out_shape=jax.ShapeDtypeStruct((BATCH * SEQ, D), F32),
        compiler_params=_cparams(2),
        name="final_norm",
    )(x, ya, yb, route, mods3, g_final)


def _mm_kernel(h_ref, w_ref, o_ref):
    o_ref[...] = _dot(h_ref[...], w_ref[...]).astype(o_ref.dtype)


def _mm_call(h, w, name):
    n = w.shape[1]
    tm, tn = 1024, 512
    return pl.pallas_call(
        _mm_kernel,
        grid=(n // tn, N_TOK // tm),
        in_specs=[pl.BlockSpec((tm, D), lambda j, i: (i, 0)),
                  pl.BlockSpec((D, tn), lambda j, i: (0, j))],
        out_specs=pl.BlockSpec((tm, tn), lambda j, i: (i, j)),
        out_shape=jax.ShapeDtypeStruct((N_TOK, n), BF16),
        compiler_params=_cparams(2),
        name=name,
    )(h, w)


CONV_HALO = 16


def _conv_kernel(x_ref, w_ref, b_ref, post_ref, o_ref):
    w = w_ref[...]
    bias = b_ref[...]
    post = post_ref[...]
    row = lax.broadcasted_iota(jnp.int32, (TB, 1), 0)
    for k in range(SEQ_BLOCKS):
        r0 = k * TB
        lo = max(r0 - CONV_HALO, 0)
        hi = min(r0 + TB + CONV_HALO, T_BATCH)
        off = r0 - lo
        n = hi - lo
        ext = x_ref[lo:hi, :].astype(F32)
        xm2 = pltpu.roll(ext, 2, 0)[off:off + TB]
        xm1 = pltpu.roll(ext, 1, 0)[off:off + TB]
        x0 = ext[off:off + TB]
        xp1 = pltpu.roll(ext, n - 1, 0)[off:off + TB]
        if k in (0, 1):
            xm2 = jnp.where(row >= 2, xm2, 0.0)
            xm1 = jnp.where(row >= 1, xm1, 0.0)
        if k in (0, SEQ_BLOCKS - 1):
            xp1 = jnp.where(row <= TB - 2, xp1, 0.0)
        y = bias + w[0:1] * xm2 + w[1:2] * xm1 + w[2:3] * x0 + w[3:4] * xp1
        o_ref[r0:r0 + TB, :] = (_silu(y) * post).astype(o_ref.dtype)


def _conv_call(u, col0, width, w, b, post, name):
    tc = 512
    cb0 = col0 // tc
    return pl.pallas_call(
        _conv_kernel,
        grid=(BATCH, width // tc),
        in_specs=[pl.BlockSpec((T_BATCH, tc), lambda bi, c: (bi, cb0 + c)),
                  pl.BlockSpec((4, tc), lambda bi, c: (0, c)),
                  pl.BlockSpec((1, tc), lambda bi, c: (0, c)),
                  pl.BlockSpec((1, tc), lambda bi, c: (0, c))],
        out_specs=pl.BlockSpec((T_BATCH, tc), lambda bi, c: (bi, c)),
        out_shape=jax.ShapeDtypeStruct((N_TOK, width), BF16),
        compiler_params=_cparams(2),
        name=name,
    )(u, w, b, post)


def _tri_dot_cols(tri, x):
    a, b, c = _split3(x)
    return _dot(tri, a) + _dot(tri, b) + _dot(tri, c)


def _tri_dot_rows(x, tri):
    a, b, c = _split3(x)
    return _dot(a, tri) + _dot(b, tri) + _dot(c, tri)


def _gate_kernel(pre_ref, pret_ref, bc_ref, br_ref, ac_ref, ar_ref, bwdc_ref, bwdr_ref, tril_ref, triu_ref,
                 valc_ref, cumc_ref, valr_ref, cumr_ref, *, ssd):
    tril = tril_ref[...]
    triu = triu_ref[...]

    def act(v):
        if ssd:
            val = _softplus(v)
            return val, val
        return v, _log_sigmoid(v)

    vc, dc = act(pre_ref[...] + bc_ref[...])
    dc = dc * ac_ref[...]
    valc_ref[...] = vc
    cumc_ref[...] = jnp.where(bwdc_ref[...] > 0.5, _tri_dot_cols(triu, dc), _tri_dot_cols(tril, dc))

    vr, dr = act(pret_ref[...] + br_ref[...])
    dr = dr * ar_ref[...]
    valr_ref[...] = vr
    cumr_ref[...] = jnp.where(bwdr_ref[...] > 0.5, _tri_dot_rows(dr, tril), _tri_dot_rows(dr, triu))


def _gate_call(small, bias, scale, bwd, ssd, name):
    small_t = small.reshape(N_BLOCKS, TB, SMALL)[:, :, :64].transpose(0, 2, 1)
    tri = np.tril(np.ones((TB, TB), np.float32))
    tril = jnp.asarray(tri, BF16)
    triu = jnp.asarray(tri.T, BF16)
    col = lambda v: v.reshape(1, SMALL).astype(F32)
    rowv = lambda v: v[:64].reshape(64, 1).astype(F32)
    cs = lambda shape: pl.BlockSpec(shape, lambda j: (0,) * len(shape))
    return pl.pallas_call(
        functools.partial(_gate_kernel, ssd=ssd),
        grid=(N_BLOCKS,),
        in_specs=[_tok_spec(SMALL), pl.BlockSpec((None, 64, TB), lambda j: (j, 0, 0)),
                  cs((1, SMALL)), cs((64, 1)), cs((1, SMALL)), cs((64, 1)), cs((1, SMALL)), cs((64, 1)),
                  cs((TB, TB)), cs((TB, TB))],
        out_specs=[_tok_spec(SMALL), _tok_spec(SMALL),
                   pl.BlockSpec((None, 64, TB), lambda j: (j, 0, 0)),
                   pl.BlockSpec((None, 64, TB), lambda j: (j, 0, 0))],
        out_shape=[jax.ShapeDtypeStruct((N_TOK, SMALL), F32), jax.ShapeDtypeStruct((N_TOK, SMALL), F32),
                   jax.ShapeDtypeStruct((N_BLOCKS, 64, TB), F32), jax.ShapeDtypeStruct((N_BLOCKS, 64, TB), F32)],
        compiler_params=_cparams(1),
        name=name,
    )(small, small_t, col(bias), rowv(bias), col(scale), rowv(scale), col(bwd), rowv(bwd), tril, triu)


def _scan_block(d):
    if d == 0:
        return lambda b, s: b * SEQ_BLOCKS + s
    return lambda b, s: b * SEQ_BLOCKS + jnp.where(s == 0, 0, SEQ_BLOCKS - s)


def _causal_mask(d):
    t = lax.broadcasted_iota(jnp.int32, (TB, TB), 0)
    s = lax.broadcasted_iota(jnp.int32, (TB, TB), 1)
    return (s <= t) if d == 0 else (s >= t)


def _mlstm_kernel(q_ref, k_ref, v_ref, valc_ref, cumc_ref, valr_ref, cumr_ref, o_ref,
                  c_ref, n_ref, m_ref, *, d):
    @pl.when(pl.program_id(1) == 0)
    def _():
        c_ref[...] = jnp.zeros_like(c_ref)
        n_ref[...] = jnp.zeros_like(n_ref)
        m_ref[...] = jnp.zeros_like(m_ref)

    mask = _causal_mask(d)
    end = TB - 1 if d == 0 else 0
    for h in range(HEADS_M):
        p = d * HEADS_M + h
        cs = slice(h * DH_M, (h + 1) * DH_M)
        q = q_ref[:, cs]
        k = k_ref[:, cs]
        v = v_ref[:, cs]
        i_c = valc_ref[:, p:p + 1]
        cum_c = cumc_ref[:, 8 + p:9 + p]
        i_r = valr_ref[p:p + 1, :]
        cum_r = cumr_ref[8 + p:9 + p, :]
        total = cum_r[:, end:end + 1]
        m_prev = m_ref[h]
        c_st = c_ref[h]
        n_st = n_ref[h]

        dmat = jnp.where(mask, cum_c - cum_r + i_r, -jnp.inf)
        g = cum_c + m_prev
        m_t = jnp.maximum(g, jnp.max(dmat, axis=1, keepdims=True))
        w = jnp.exp(dmat - m_t)
        sg = jnp.exp(g - m_t)
        a = w * lax.dot_general(q, k, (((1,), (1,)), ((), ())), preferred_element_type=F32)
        num = _dot(a.astype(BF16), v) + sg * _dot(q, c_st.astype(BF16))
        den = jnp.sum(a, axis=1, keepdims=True) + sg * jnp.sum(q.astype(F32) * n_st, axis=1, keepdims=True)
        o_ref[:, cs] = (num / jnp.maximum(jnp.abs(den), jnp.exp(-m_t))).astype(o_ref.dtype)

        wl_log = total - cum_c + i_c
        m_new = jnp.maximum(total + m_prev, jnp.max(wl_log, axis=0, keepdims=True))
        wl = jnp.exp(wl_log - m_new)
        decay = jnp.exp(total + m_prev - m_new)
        kf = k.astype(F32)
        wv = (wl * v.astype(F32)).astype(BF16)
        c_ref[h] = decay * c_st + lax.dot_general(k, wv, (((0,), (0,)), ((), ())), preferred_element_type=F32)
        n_ref[h] = decay * n_st + jnp.sum(wl * kf, axis=0, keepdims=True)
        m_ref[h] = m_new


def _mlstm_call(qk, ucm, valc, cumc, valr, cumr, d):
    blk = _scan_block(d)
    tok = lambda col: pl.BlockSpec((TB, D), lambda b, s: (blk(b, s), col))
    small = pl.BlockSpec((TB, SMALL), lambda b, s: (blk(b, s), 0))
    rows = pl.BlockSpec((None, 64, TB), lambda b, s: (blk(b, s), 0, 0))
    return pl.pallas_call(
        functools.partial(_mlstm_kernel, d=d),
        grid=(BATCH, SEQ_BLOCKS),
        in_specs=[tok(0), tok(1), tok(2), small, small, rows, rows],
        out_specs=tok(0),
        out_shape=jax.ShapeDtypeStruct((N_TOK, D), BF16),
        scratch_shapes=[pltpu.VMEM((HEADS_M, DH_M, DH_M), F32), pltpu.VMEM((HEADS_M, 1, DH_M), F32),
                        pltpu.VMEM((HEADS_M, 1, 1), F32)],
        compiler_params=_cparams(2),
        name="mlstm_fwd" if d == 0 else "mlstm_bwd",
    )(qk, qk, ucm, valc, cumc, valr, cumr)


SUB = 128


def _ssd_kernel(x_ref, b_ref, c_ref, acsc_ref, dtr_ref, acsr_ref, o_ref, s_ref, *, d):
    @pl.when(pl.program_id(1) == 0)
    def _():
        s_ref[...] = jnp.zeros_like(s_ref)

    t_i = lax.broadcasted_iota(jnp.int32, (SUB, SUB), 0)
    s_i = lax.broadcasted_iota(jnp.int32, (SUB, SUB), 1)
    mask = (s_i <= t_i) if d == 0 else (s_i >= t_i)
    lo = lax.broadcasted_iota(jnp.int32, (SUB, 2 * SSD_P), 1) < SSD_P
    lo1 = lax.broadcasted_iota(jnp.int32, (1, 2 * SSD_P), 1) < SSD_P
    zero_b = jnp.zeros((SUB, 2 * SSD_P), BF16)
    bcast = lambda col: jnp.broadcast_to(col, (SUB, SUB))

    base_row = jnp.zeros((1, SMALL), F32)
    for ci, c in enumerate((0, 1) if d == 0 else (1, 0)):
        r0 = c * SUB
        rows = slice(r0, r0 + SUB)
        if ci == 1:
            edge = SUB - 1 if d == 0 else SUB
            base_row = acsc_ref[edge:edge + 1, :]
        last = r0 + (SUB - 1 if d == 0 else 0)
        acs_sub = acsc_ref[rows, :]
        tot_row = acsc_ref[last:last + 1, :] - base_row
        cd_row = jnp.exp(tot_row)
        from_start = jnp.exp(acs_sub - base_row)
        for g in range(SSD_GROUPS):
            bg = b_ref[rows, g * SSD_N:(g + 1) * SSD_N]
            cg = c_ref[rows, g * SSD_N:(g + 1) * SSD_N]
            cb = lax.dot_general(cg, bg, (((1,), (1,)), ((), ())), preferred_element_type=F32)
            bg_t = jnp.transpose(bg.astype(F32)).astype(BF16)
            for j in range(2):
                tile = slice(g * 4 * SSD_P + j * 2 * SSD_P, g * 4 * SSD_P + (j + 1) * 2 * SSD_P)
                stile = slice(j * 2 * SSD_P, (j + 1) * 2 * SSD_P)
                ps = [SSD_HEADS + d * SSD_HEADS + g * 4 + j * 2 + k for k in (0, 1)]
                xt = x_ref[rows, tile]
                st = s_ref[g, :, stile]
                st_b = st.astype(BF16)
                intra, inter, to_state = [], [], []
                for p in ps:
                    cum_col = bcast(acs_sub[:, p:p + 1])
                    cum_row = acsr_ref[p:p + 1, rows]
                    dt_row = dtr_ref[p:p + 1, rows]
                    seg = jnp.where(mask, cum_col - (cum_row - jnp.log(dt_row)), -jnp.inf)
                    intra.append((cb * jnp.exp(seg)).astype(BF16))
                    inter.append(cg * bcast(from_start[:, p:p + 1]).astype(BF16))
                    w_row = dt_row * jnp.exp(tot_row[:, p:p + 1] - (cum_row - base_row[:, p:p + 1]))
                    to_state.append(bg_t * w_row.astype(BF16))
                x_pair = [jnp.where(lo, xt, zero_b), jnp.where(lo, zero_b, xt)]
                s_pair = [jnp.where(lo, st_b, zero_b), jnp.where(lo, zero_b, st_b)]
                lhs = jnp.concatenate(intra + inter, axis=1)
                o_ref[rows, tile] = _dot(lhs, jnp.concatenate(x_pair + s_pair, axis=0)).astype(o_ref.dtype)
                cd_sel = jnp.where(lo1, cd_row[:, ps[0]:ps[0] + 1], cd_row[:, ps[1]:ps[1] + 1])
                s_ref[g, :, stile] = cd_sel * st + _dot(jnp.concatenate(to_state, axis=1),
                                                        jnp.concatenate(x_pair, axis=0))


def _ssd_call(xbc, acsc, dtr, acsr, d):
    blk = _scan_block(d)
    small = pl.BlockSpec((TB, SMALL), lambda b, s: (blk(b, s), 0))
    rows = pl.BlockSpec((None, 64, TB), lambda b, s: (blk(b, s), 0, 0))
    return pl.pallas_call(
        functools.partial(_ssd_kernel, d=d),
        grid=(BATCH, SEQ_BLOCKS),
        in_specs=[pl.BlockSpec((TB, D), lambda b, s: (blk(b, s), 0)),
                  pl.BlockSpec((TB, 512), lambda b, s: (blk(b, s), 2)),
                  pl.BlockSpec((TB, 512), lambda b, s: (blk(b, s), 3)),
                  small, rows, rows],
        out_specs=pl.BlockSpec((TB, D), lambda b, s: (blk(b, s), 0)),
        out_shape=jax.ShapeDtypeStruct((N_TOK, D), BF16),
        scratch_shapes=[pltpu.VMEM((SSD_GROUPS, SSD_N, 4 * SSD_P), F32)],
        compiler_params=_cparams(2),
        name="ssd_fwd" if d == 0 else "ssd_bwd",
    )(xbc, xbc, xbc, acsc, dtr, acsr)


def _pool_tables():
    masks = np.zeros((2, 4, TB, TB), np.float32)
    inv = np.zeros((2, 4, TB, 1), np.float32)
    for kind, length in ((0, GRID_W), (1, CTX)):
        for gi, w in enumerate(POOL_WINDOWS):
            for r in range(TB):
                base, c = (r // length) * length, r % length
                lo = min(max(c - w // 2, 0), length - 1)
                hi = min(max(c - w // 2 + w - 1, 0), length - 1)
                masks[kind, gi, r, base + lo:base + hi + 1] = 1.0
                inv[kind, gi, r, 0] = 1.0 / (hi - lo + 1)
    return jnp.asarray(masks, BF16), jnp.asarray(inv, F32)


ROUTE_FIELDS = 6


def _route_block(logits, ltri_ref, rt_ref, cnt_ref, cnt_scr):
    @pl.when(pl.program_id(0) == 0)
    def _():
        cnt_scr[...] = jnp.zeros_like(cnt_scr)

    lane = lax.broadcasted_iota(jnp.int32, logits.shape, 1).astype(F32)

    def top1(valid):
        v = jnp.max(jnp.where(valid, logits, -jnp.inf), axis=1, keepdims=True)
        i = jnp.min(jnp.where(valid & (logits == v), lane, float(SMALL)), axis=1, keepdims=True)
        return v, i

    is_g = lane < N_GROUPS_E
    gm, grp = top1(is_g)
    p_grp = 1.0 / jnp.sum(jnp.where(is_g, jnp.exp(logits - gm), 0.0), axis=1, keepdims=True)
    lo_lane = N_GROUPS_E + grp * EPG
    in_grp = (lane >= lo_lane) & (lane < lo_lane + EPG)
    v1, i1 = top1(in_grp)
    v2, i2 = top1(in_grp & (lane != i1))
    t = jnp.exp(v2 - v1)
    w1 = p_grp / (1.0 + t)
    w2 = p_grp * t / (1.0 + t)
    e1 = i1 - N_GROUPS_E
    e2 = i2 - N_GROUPS_E

    oh1 = lane == e1
    oh2 = lane == e2
    oh = jnp.where(oh1 | oh2, 1.0, 0.0)
    before = _dot(ltri_ref[...], oh.astype(BF16)) + cnt_scr[...]
    r1 = jnp.sum(jnp.where(oh1, before, 0.0), axis=1, keepdims=True)
    r2 = jnp.sum(jnp.where(oh2, before, 0.0), axis=1, keepdims=True)
    cnt_scr[...] = cnt_scr[...] + jnp.sum(oh, axis=0, keepdims=True)
    cnt_ref[...] = cnt_scr[...]
    rt = jnp.zeros(logits.shape, F32)
    for k, val in enumerate((e1, e2, w1, w2, r1, r2)):
        rt = jnp.where(lane == k, val, rt)
    rt_ref[...] = rt


def _merge_kernel(x_ref, pa_ref, o_ref, z_ref, mg0_ref, mg1_ref, mg2_ref, hf_ref, hb_ref, yf_ref, yb_ref, xs_ref,
                  pmask_ref, pinv_ref, poolw_ref, pscale_ref, mng_ref, dsk_ref, sng_ref, wbr_ref, wout_ref,
                  gate_ref, sh_ref, sc_ref, gffn_ref, wrh_ref, wrl_ref, brt_ref, ltri_ref,
                  xo_ref, h2_ref, rt_ref, cnt_ref, cnt_scr):
    parts = []
    for g in range(4):
        a_g = pa_ref[:, g * 256:(g + 1) * 256]
        pooled = _dot(pmask_ref[g], a_g) * pinv_ref[g] - a_g.astype(F32)
        parts.append(_dot(pooled.astype(BF16), poolw_ref[g]))
    pool = jnp.concatenate(parts, axis=1) * pscale_ref[...]

    hs = hf_ref[...].astype(F32) + hb_ref[...].astype(F32)
    hn = jnp.concatenate([_rms(hs[:, h * DH_M:(h + 1) * DH_M]) for h in range(HEADS_M)], axis=1)
    ml = _sigmoid(o_ref[...].astype(F32)) * (hn * mng_ref[...])

    y = yf_ref[...].astype(F32) + yb_ref[...].astype(F32) + dsk_ref[...] * xs_ref[...].astype(F32)
    sl = _rms(y * _silu(z_ref[...].astype(F32))) * sng_ref[...]

    acc = _sigmoid(mg0_ref[...].astype(F32)) * _dot(pool.astype(BF16), wbr_ref[0])
    acc = acc + _sigmoid(mg1_ref[...].astype(F32)) * _dot(ml.astype(BF16), wbr_ref[1])
    acc = acc + _sigmoid(mg2_ref[...].astype(F32)) * _dot(sl.astype(BF16), wbr_ref[2])
    xn = x_ref[...] + gate_ref[...] * _dot(acc.astype(BF16), wout_ref[...])
    xo_ref[...] = xn

    h2 = (_rms(xn) * gffn_ref[...]) * (1.0 + sc_ref[...]) + sh_ref[...]
    hi, lo = _split2(h2)
    h2f = hi.astype(F32)
    for cchunk in range(D // 128):
        h2_ref[pl.ds(cchunk, TB, stride=D // 128), :] = h2f[:, cchunk * 128:(cchunk + 1) * 128]
    logits = _dot(hi, wrh_ref[...]) + _dot(lo, wrh_ref[...]) + _dot(hi, wrl_ref[...]) + brt_ref[...]
    _route_block(logits, ltri_ref, rt_ref, cnt_ref, cnt_scr)


def _merge_call(x, urm, hf, hb, yf, yb, xbc, pmask, pinv, poolw, pscale, mng, dsk, sng, wbr, wout,
                mods3, gffn, wr_hi, wr_lo, br):
    ltri = jnp.asarray(np.tril(np.ones((TB, TB), np.float32), -1), BF16)
    kind = lambda j: jnp.where(j % SEQ_BLOCKS == 0, 1, 0)
    vec = _const_spec((1, D))
    return pl.pallas_call(
        _merge_kernel,
        grid=(N_BLOCKS,),
        in_specs=[_tok_spec(),
                  _tok_spec(D, 0), _tok_spec(D, 1), _tok_spec(D, 4),
                  _tok_spec(D, 5), _tok_spec(D, 6), _tok_spec(D, 7),
                  _tok_spec(), _tok_spec(), _tok_spec(), _tok_spec(),
                  _tok_spec(D, 0),
                  pl.BlockSpec((None, 4, TB, TB), lambda j: (kind(j), 0, 0, 0)),
                  pl.BlockSpec((None, 4, TB, 1), lambda j: (kind(j), 0, 0, 0)),
                  _const_spec((4, 256, 256)), vec, vec, vec, vec,
                  _const_spec((3, D, D)), _const_spec((D, D)),
                  _mod_spec(2), _mod_spec(3), _mod_spec(4), vec,
                  _const_spec((D, SMALL)), _const_spec((D, SMALL)), _const_spec((1, SMALL)),
                  _const_spec((TB, TB))],
        out_specs=[_tok_spec(), pl.BlockSpec((TB * ROW_TILES, 128), lambda j: (j, 0)), _tok_spec(SMALL),
                   _const_spec((1, SMALL))],
        out_shape=[jax.ShapeDtypeStruct((N_TOK, D), F32), jax.ShapeDtypeStruct((N_TOK * ROW_TILES, 128), F32),
                   jax.ShapeDtypeStruct((N_TOK, SMALL), F32), jax.ShapeDtypeStruct((1, SMALL), F32)],
        scratch_shapes=[pltpu.VMEM((1, SMALL), F32)],
        compiler_params=_cparams(1),
        name="branch_merge",
    )(x, urm, urm, urm, urm, urm, urm, hf, hb, yf, yb, xbc, pmask, pinv, poolw, pscale, mng, dsk, sng,
      wbr, wout, mods3, mods3, mods3, gffn, wr_hi, wr_lo, br, ltri)


GATHER_AHEAD = 2
GATHER_SLOTS = GATHER_AHEAD + 1
GATHER_DMA_PRIORITY = 1


def _moe_kernel(be_ref, nu_ref, tok0_ref, tok1_ref, tokn_ref, h_hbm, wg_ref, wu_ref, wd_ref, o_ref,
                xbuf, wgb, wub, wdb, sem):
    i = pl.program_id(0)
    n_used = nu_ref[0]
    slot = i % GATHER_SLOTS

    def start_gather(tok_ref, s):
        for r in range(MOE_BLOCK):
            row0 = pl.multiple_of(tok_ref[0, r] * ROW_TILES, ROW_TILES)
            pltpu.make_async_copy(h_hbm.at[pl.ds(row0, ROW_TILES)],
                                  xbuf.at[s, pl.ds(r * ROW_TILES, ROW_TILES)],
                                  sem.at[s]).start(priority=GATHER_DMA_PRIORITY)

    def wait_gather(s):
        pltpu.make_async_copy(xbuf.at[s], xbuf.at[s], sem.at[s]).wait()

    @pl.when((i == 0) & (n_used > 0))
    def _():
        start_gather(tok0_ref, 0)
        start_gather(tok1_ref, 1)

    @pl.when((i < n_used) & ((i == 0) | (be_ref[i] != be_ref[jnp.maximum(i - 1, 0)])))
    def _():
        wgb[...] = wg_ref[...].astype(BF16)
        wub[...] = wu_ref[...].astype(BF16)
        wdb[...] = wd_ref[...].astype(BF16)

    @pl.when(i < n_used)
    def _():
        wait_gather(slot)
        x = jnp.concatenate([xbuf[slot, pl.ds(c, MOE_BLOCK, stride=ROW_TILES), :] for c in range(ROW_TILES)],
                            axis=1).astype(BF16)
        start_gather(tokn_ref, (i + GATHER_AHEAD) % GATHER_SLOTS)
        gt = _dot(x, wgb[...])
        up = _dot(x, wub[...])
        act = (_silu(gt) * up).astype(BF16)
        o_ref[...] = _dot(act, wdb[...]).astype(o_ref.dtype)

    @pl.when(i == n_used - 1)
    def _():
        for ahead in range(1, GATHER_AHEAD + 1):
            wait_gather((i + ahead) % GATHER_SLOTS)

    @pl.when(i >= n_used)
    def _():
        o_ref[...] = jnp.zeros_like(o_ref)


MOE_CAP = N_TOK * 2 + N_EXPERTS * MOE_BLOCK
MOE_NBLOCKS = MOE_CAP // MOE_BLOCK


def _moe_call(layer, block_expert, n_used, buf_tok, h2, w_gate, w_up, w_down):
    grid_spec = pltpu.PrefetchScalarGridSpec(
        num_scalar_prefetch=2,
        grid=(MOE_NBLOCKS,),
        in_specs=[pl.BlockSpec((None, 1, MOE_BLOCK), lambda i, be, nu: (0, 0, 0), memory_space=pltpu.SMEM),
                  pl.BlockSpec((None, 1, MOE_BLOCK), lambda i, be, nu: (1, 0, 0), memory_space=pltpu.SMEM),
                  pl.BlockSpec((None, 1, MOE_BLOCK),
                               lambda i, be, nu: (jnp.minimum(i + GATHER_AHEAD, MOE_NBLOCKS - 1), 0, 0),
                               memory_space=pltpu.SMEM),
                  pl.BlockSpec(memory_space=pl.ANY),
                  pl.BlockSpec((None, None, D, D_EXPERT), lambda i, be, nu: (layer, be[i], 0, 0)),
                  pl.BlockSpec((None, None, D, D_EXPERT), lambda i, be, nu: (layer, be[i], 0, 0)),
                  pl.BlockSpec((None, None, D_EXPERT, D), lambda i, be, nu: (layer, be[i], 0, 0))],
        out_specs=pl.BlockSpec((MOE_BLOCK, D), lambda i, be, nu: (i, 0)),
        scratch_shapes=[pltpu.VMEM((GATHER_SLOTS, MOE_BLOCK * ROW_TILES, 128), F32),
                        pltpu.VMEM((D, D_EXPERT), BF16), pltpu.VMEM((D, D_EXPERT), BF16),
                        pltpu.VMEM((D_EXPERT, D), BF16), pltpu.SemaphoreType.DMA((GATHER_SLOTS,))],
    )
    return pl.pallas_call(
        _moe_kernel,
        grid_spec=grid_spec,
        out_shape=jax.ShapeDtypeStruct((MOE_CAP, D), BF16),
        compiler_params=_cparams(1),
        name="moe_experts",
    )(block_expert, n_used, buf_tok, buf_tok, buf_tok, h2, w_gate, w_up, w_down)


def _dispatch_plan(route, counts):
    cnt = counts[0, :N_EXPERTS].astype(jnp.int32)
    padded = (cnt + MOE_BLOCK - 1) // MOE_BLOCK * MOE_BLOCK
    pends = jnp.cumsum(padded)
    pstarts = pends - padded
    expert = route[:, 0:2].astype(jnp.int32)
    rank = route[:, 4:6].astype(jnp.int32)
    onehot = expert[:, :, None] == jnp.arange(N_EXPERTS, dtype=jnp.int32)
    pos = jnp.sum(jnp.where(onehot, pstarts, 0), axis=-1) + rank
    tok = jnp.broadcast_to(jnp.arange(N_TOK, dtype=jnp.int32)[:, None], (N_TOK, 2))
    buf_tok = jnp.zeros((MOE_CAP,), jnp.int32).at[pos.reshape(-1)].set(tok.reshape(-1))
    buf_tok = buf_tok.reshape(MOE_NBLOCKS, 1, MOE_BLOCK)
    block_start = jnp.arange(MOE_NBLOCKS, dtype=jnp.int32) * MOE_BLOCK
    block_expert = jnp.minimum(jnp.sum((pends[None, :] <= block_start[:, None]).astype(jnp.int32), axis=1),
                               N_EXPERTS - 1)
    n_used = (pends[-1] // MOE_BLOCK).reshape(1)
    return buf_tok, block_expert, n_used, pos


def _to_colmajor(t):
    c = t.shape[-1]
    t = t.reshape(BATCH, T_BATCH, c)
    lat = t[:, CTX:].reshape(BATCH, SEQ // GRID_W, GRID_W, c).swapaxes(1, 2).reshape(BATCH, SEQ, c)
    return jnp.concatenate([t[:, :CTX], lat], axis=1).reshape(N_TOK, c)


def _to_rowmajor(t):
    c = t.shape[-1]
    t = t.reshape(BATCH, T_BATCH, c)
    lat = t[:, CTX:].reshape(BATCH, GRID_W, SEQ // GRID_W, c).swapaxes(1, 2).reshape(BATCH, SEQ, c)
    return jnp.concatenate([t[:, :CTX], lat], axis=1).reshape(N_TOK, c)


def _lanes(*pieces):
    v = jnp.concatenate([jnp.asarray(p, F32).reshape(-1) for p in pieces])
    return jnp.pad(v, (0, SMALL - v.shape[0]))


def kernel(x, c, ctx, c_ctx, w_ada, b_ada, g_norm_mix, g_norm_ffn, w_in, pool_w, pool_scale, mlstm_conv_w,
           mlstm_conv_b, mlstm_gate_b, mlstm_norm_g, ssd_conv_w, ssd_conv_b, ssd_dt_bias, ssd_a_log, ssd_d,
           ssd_norm_g, w_branch, w_out, w_route_group, b_route_group, w_route_expert, b_route_expert,
           w_exp_gate, w_exp_up, w_exp_down, g_final):
    cvec = jnp.concatenate([c, c_ctx[None], jnp.zeros((3, D), F32)], axis=0)
    mods = _ada_table(cvec, w_ada, b_ada)
    xs = jnp.concatenate([ctx, x], axis=1).reshape(N_TOK, D)
    pmask, pinv = _pool_tables()
    row = lambda v: v.reshape(1, -1).astype(F32)
    w_in_bf = w_in.astype(BF16)

    h = small = None
    out = None
    for l in range(DEPTH):
        mods3 = mods[l].reshape(8, 1, 6 * D)
        wi = w_in_bf[l]
        sp = np.cumsum([0, D, D, D, D, D, 16, D, D, 512, 512, 32, 3 * D])
        piece = lambda i: wi[:, sp[i]:sp[i + 1]]
        w_rm = jnp.concatenate([piece(0), piece(4), piece(6), piece(8), piece(9), piece(7), piece(11)],
                               axis=1)
        w_cm = wi[:, sp[1]:sp[4]]
        w_small = jnp.pad(jnp.concatenate([w_in[l][:, sp[5]:sp[6]], w_in[l][:, sp[10]:sp[11]]], axis=1),
                          ((0, 0), (0, SMALL - 48)))
        ws_hi = w_small.astype(BF16)
        ws_lo = (w_small - ws_hi.astype(F32)).astype(BF16)

        if l == 0:
            h, small = _norm_call(xs, row(g_norm_mix[l]), mods3, ws_hi, ws_lo)

        h_cm = _to_colmajor(h)
        small_cm = _to_colmajor(small)
        u_rm = _mm_call(h, w_rm, "in_proj_rowmajor")
        u_cm = _mm_call(h_cm, w_cm, "in_proj_colmajor")

        post_m = jnp.concatenate([jnp.ones((D,), F32), jnp.full((D,), DH_M ** -0.5, F32)]).reshape(1, 2 * D)
        qk = _conv_call(u_cm, 0, 2 * D, mlstm_conv_w[l], row(mlstm_conv_b[l]), post_m, "mlstm_conv")
        gate_bias = _lanes(mlstm_gate_b[l])
        gate_bwd = _lanes(jnp.zeros((12,)), jnp.ones((4,)))
        g_valc, g_cumc, g_valr, g_cumr = _gate_call(small_cm, gate_bias, jnp.ones((SMALL,), F32), gate_bwd,
                                                    False, "mlstm_gates")
        h_f = _mlstm_call(qk, u_cm, g_valc, g_cumc, g_valr, g_cumr, 0)
        h_b = _mlstm_call(qk, u_cm, g_valc, g_cumc, g_valr, g_cumr, 1)
        h_f = _to_rowmajor(h_f)
        h_b = _to_rowmajor(h_b)

        xbc = _conv_call(u_rm, 2 * D, 2 * D, ssd_conv_w[l], row(ssd_conv_b[l]), jnp.ones((1, 2 * D), F32),
                         "ssd_conv")
        a_neg = -jnp.exp(ssd_a_log[l].astype(F32))
        dt_bias = _lanes(jnp.zeros((16,)), ssd_dt_bias[l])
        dt_scale = _lanes(jnp.zeros((16,)), a_neg)
        dt_bwd = _lanes(jnp.zeros((32,)), jnp.ones((16,)))
        _, s_acsc, s_dtr, s_acsr = _gate_call(small, dt_bias, dt_scale, dt_bwd, True, "ssd_gates")
        y_f = _ssd_call(xbc, s_acsc, s_dtr, s_acsr, 0)
        y_b = _ssd_call(xbc, s_acsc, s_dtr, s_acsr, 1)

        w_r = jnp.pad(jnp.concatenate([w_route_group[l], w_route_expert[l]], axis=1),
                      ((0, 0), (0, SMALL - N_GROUPS_E - N_EXPERTS)))
        wr_hi = w_r.astype(BF16)
        wr_lo = (w_r - wr_hi.astype(F32)).astype(BF16)
        b_r = _lanes(b_route_group[l], b_route_expert[l]).reshape(1, SMALL)
        dsk = jnp.repeat(ssd_d[l].astype(F32), SSD_P).reshape(1, D)
        xs, h2, route, counts = _merge_call(
            xs, u_rm, h_f, h_b, y_f, y_b, xbc, pmask, pinv, pool_w[l].astype(BF16), row(pool_scale[l]),
            row(mlstm_norm_g[l]), dsk, row(ssd_norm_g[l]), w_branch[l].astype(BF16), w_out[l].astype(BF16),
            mods3, row(g_norm_ffn[l]), wr_hi, wr_lo, b_r)

        buf_tok, block_expert, n_used, pos = _dispatch_plan(route, counts)
        yb = _moe_call(l, block_expert, n_used, buf_tok, h2, w_exp_gate, w_exp_up, w_exp_down)
        y0 = yb[pos[:, 0]]
        y1 = yb[pos[:, 1]]

        if l + 1 < DEPTH:
            wi_n = w_in[l + 1]
            w_small_n = jnp.pad(jnp.concatenate([wi_n[:, sp[5]:sp[6]], wi_n[:, sp[10]:sp[11]]], axis=1),
                                ((0, 0), (0, SMALL - 48)))
            wsn_hi = w_small_n.astype(BF16)
            wsn_lo = (w_small_n - wsn_hi.astype(F32)).astype(BF16)
            mods3_n = mods[l + 1].reshape(8, 1, 6 * D)
            xs, h, small = _resnorm_call(xs, y0, y1, route, mods3, row(g_norm_mix[l + 1]), mods3_n,
                                         wsn_hi, wsn_lo)
        else:
            out = _final_call(xs, y0, y1, route, mods3, row(g_final))
    return out.reshape(BATCH, SEQ, D)
```

```python
import functools
import math

import jax
import jax.numpy as jnp
import numpy as np
from jax import lax
from jax.experimental import pallas as pl
from jax.experimental.pallas import tpu as pltpu

F32 = jnp.float32
BF16 = jnp.bfloat16

D = 1024
BATCH = 4
SEQ = 4096
CTX = 256
DEPTH = 2
GRID_W = 64
EPS = 1e-6

TB = 256
SEQ_BLOCKS = (SEQ + CTX) // TB
T_BATCH = SEQ + CTX
N_TOK = BATCH * T_BATCH
N_BLOCKS = N_TOK // TB

POOL_WINDOWS = (2, 4, 8, 16)
HEADS_M = 4
DH_M = 256
SSD_HEADS = 16
SSD_P = 64
SSD_GROUPS = 4
SSD_N = 128
N_GROUPS_E = 4
EPG = 8
N_EXPERTS = 32
D_EXPERT = 512
MOE_BLOCK = 256
SMALL = 128
ROW_TILES = D // 128

VMEM_LIMIT = 56 * 1024 * 1024


def _cparams(n_axes):
    return pltpu.CompilerParams(dimension_semantics=("arbitrary",) * n_axes,
                                vmem_limit_bytes=VMEM_LIMIT)


def _dot(a, b):
    return jnp.dot(a, b, preferred_element_type=F32)


def _split2(x):
    hi = x.astype(BF16)
    lo = (x - hi.astype(F32)).astype(BF16)
    return hi, lo


def _split3(x):
    hi = x.astype(BF16)
    r = x - hi.astype(F32)
    mid = r.astype(BF16)
    lo = (r - mid.astype(F32)).astype(BF16)
    return hi, mid, lo


def _sigmoid(x):
    return 1.0 / (1.0 + jnp.exp(-x))


def _silu(x):
    return x * _sigmoid(x)


def _log1pexp_negabs(x):
    return jnp.log(1.0 + jnp.exp(-jnp.abs(x)))


def _log_sigmoid(x):
    return jnp.minimum(x, 0.0) - _log1pexp_negabs(x)


def _softplus(x):
    return jnp.maximum(x, 0.0) + _log1pexp_negabs(x)


def _rms(x):
    return x * lax.rsqrt(jnp.mean(x * x, axis=-1, keepdims=True) + EPS)


def _mod_row(j):
    return jnp.where(j % SEQ_BLOCKS == 0, BATCH, j // SEQ_BLOCKS)


def _ada_kernel(c_ref, w_ref, b_ref, o_ref):
    c = c_ref[...]
    s_hi, s_lo = _split2(_silu(c))
    w_hi, w_lo = _split2(w_ref[...])
    o_ref[...] = _dot(s_hi, w_hi) + _dot(s_lo, w_hi) + _dot(s_hi, w_lo) + b_ref[...]


def _ada_table(cvec, w_ada, b_ada):
    tn = 1024
    return pl.pallas_call(
        _ada_kernel,
        grid=(DEPTH, 6 * D // tn),
        in_specs=[pl.BlockSpec((8, D), lambda l, j: (0, 0)),
                  pl.BlockSpec((None, D, tn), lambda l, j: (l, 0, j)),
                  pl.BlockSpec((None, 1, tn), lambda l, j: (l, 0, j))],
        out_specs=pl.BlockSpec((None, 8, tn), lambda l, j: (l, 0, j)),
        out_shape=jax.ShapeDtypeStruct((DEPTH, 8, 6 * D), F32),
        compiler_params=_cparams(2),
        name="ada_table",
    )(cvec, w_ada, b_ada.reshape(DEPTH, 1, 6 * D))


def _norm_mod_small(x, g_ref, sh_ref, sc_ref, wsh_ref, wsl_ref, h_ref, small_ref):
    h = _rms(x) * g_ref[...]
    h = h * (1.0 + sc_ref[...]) + sh_ref[...]
    h_hi, h_lo = _split2(h)
    h_ref[...] = h_hi
    small_ref[...] = _dot(h_hi, wsh_ref[...]) + _dot(h_lo, wsh_ref[...]) + _dot(h_hi, wsl_ref[...])


def _norm_kernel(x_ref, g_ref, sh_ref, sc_ref, wsh_ref, wsl_ref, h_ref, small_ref):
    _norm_mod_small(x_ref[...], g_ref, sh_ref, sc_ref, wsh_ref, wsl_ref, h_ref, small_ref)


def _moe_residual(x_ref, ya_ref, yb_ref, rt_ref, gate_ref):
    y = rt_ref[:, 2:3] * ya_ref[...].astype(F32) + rt_ref[:, 3:4] * yb_ref[...].astype(F32)
    return x_ref[...] + gate_ref[...] * y


def _resnorm_kernel(x_ref, ya_ref, yb_ref, rt_ref, gate_ref, g_ref, sh_ref, sc_ref, wsh_ref, wsl_ref,
                    xo_ref, h_ref, small_ref):
    x = _moe_residual(x_ref, ya_ref, yb_ref, rt_ref, gate_ref)
    xo_ref[...] = x
    _norm_mod_small(x, g_ref, sh_ref, sc_ref, wsh_ref, wsl_ref, h_ref, small_ref)


def _mod_spec(chunk):
    return pl.BlockSpec((None, 1, D), lambda j: (_mod_row(j), 0, chunk))


def _tok_spec(width=D, col=0):
    return pl.BlockSpec((TB, width), lambda j: (j, col))


def _const_spec(shape):
    nd = len(shape)
    return pl.BlockSpec(shape, lambda j: (0,) * nd)


def _norm_call(x, g, mods3, ws_hi, ws_lo):
    return pl.pallas_call(
        _norm_kernel,
        grid=(N_BLOCKS,),
        in_specs=[_tok_spec(), _const_spec((1, D)), _mod_spec(0), _mod_spec(1),
                  _const_spec((D, SMALL)), _const_spec((D, SMALL))],
        out_specs=[_tok_spec(), _tok_spec(SMALL)],
        out_shape=[jax.ShapeDtypeStruct((N_TOK, D), BF16), jax.ShapeDtypeStruct((N_TOK, SMALL), F32)],
        compiler_params=_cparams(1),
        name="norm_mod",
    )(x, g, mods3, mods3, ws_hi, ws_lo)


def _resnorm_call(x, ya, yb, route, mods3_prev, g, mods3, ws_hi, ws_lo):
    return pl.pallas_call(
        _resnorm_kernel,
        grid=(N_BLOCKS,),
        in_specs=[_tok_spec(), _tok_spec(), _tok_spec(), _tok_spec(SMALL),
                  pl.BlockSpec((None, 1, D), lambda j: (_mod_row(j), 0, 5)),
                  _const_spec((1, D)), _mod_spec(0), _mod_spec(1),
                  _const_spec((D, SMALL)), _const_spec((D, SMALL))],
        out_specs=[_tok_spec(), _tok_spec(), _tok_spec(SMALL)],
        out_shape=[jax.ShapeDtypeStruct((N_TOK, D), F32), jax.ShapeDtypeStruct((N_TOK, D), BF16),
                   jax.ShapeDtypeStruct((N_TOK, SMALL), F32)],
        compiler_params=_cparams(1),
        name="residual_norm_mod",
    )(x, ya, yb, route, mods3_prev, g, mods3, mods3, ws_hi, ws_lo)


def _final_kernel(x_ref, ya_ref, yb_ref, rt_ref, gate_ref, g_ref, o_ref):
    o_ref[...] = _rms(_moe_residual(x_ref, ya_ref, yb_ref, rt_ref, gate_ref)) * g_ref[...]


def _final_call(x, ya, yb, route, mods3, g_final):
    lat = lambda b, s: (b * SEQ_BLOCKS + 1 + s, 0)
    spec = pl.BlockSpec((TB, D), lat)
    return pl.pallas_call(
        _final_kernel,
        grid=(BATCH, SEQ // TB),
        in_specs=[spec, spec, spec, pl.BlockSpec((TB, SMALL), lat),
                  pl.BlockSpec((None, 1, D), lambda b, s: (b, 0, 5)),
                  pl.BlockSpec((1, D), lambda b, s: (0, 0))],
        out_specs=pl.BlockSpec((TB, D), lambda b, s: (b * (SEQ // TB) + s, 0)),
        out_shape=jax.ShapeDtypeStruct((BATCH * SEQ, D), F32),
        compiler_params=_cparams(2),
        name="final_norm",
    )(x, ya, yb, route, mods3, g_final)


def _mm_kernel(h_ref, w_ref, o_ref):
    o_ref[...] = _dot(h_ref[...], w_ref[...]).astype(o_ref.dtype)


def _mm_call(h, w, name):
    n = w.shape[1]
    tm, tn = 1024, 512
    return pl.pallas_call(
        _mm_kernel,
        grid=(n // tn, N_TOK // tm),
        in_specs=[pl.BlockSpec((tm, D), lambda j, i: (i, 0)),
                  pl.BlockSpec((D, tn), lambda j, i: (0, j))],
        out_specs=pl.BlockSpec((tm, tn), lambda j, i: (i, j)),
        out_shape=jax.ShapeDtypeStruct((N_TOK, n), BF16),
        compiler_params=_cparams(2),
        name=name,
    )(h, w)


CONV_HALO = 16


def _conv_kernel(x_ref, w_ref, b_ref, post_ref, o_ref):
    w = w_ref[...]
    bias = b_ref[...]
    post = post_ref[...]
    row = lax.broadcasted_iota(jnp.int32, (TB, 1), 0)
    for k in range(SEQ_BLOCKS):
        r0 = k * TB
        lo = max(r0 - CONV_HALO, 0)
        hi = min(r0 + TB + CONV_HALO, T_BATCH)
        off = r0 - lo
        n = hi - lo
        ext = x_ref[lo:hi, :].astype(F32)
        xm2 = pltpu.roll(ext, 2, 0)[off:off + TB]
        xm1 = pltpu.roll(ext, 1, 0)[off:off + TB]
        x0 = ext[off:off + TB]
        xp1 = pltpu.roll(ext, n - 1, 0)[off:off + TB]
        if k in (0, 1):
            xm2 = jnp.where(row >= 2, xm2, 0.0)
            xm1 = jnp.where(row >= 1, xm1, 0.0)
        if k in (0, SEQ_BLOCKS - 1):
            xp1 = jnp.where(row <= TB - 2, xp1, 0.0)
        y = bias + w[0:1] * xm2 + w[1:2] * xm1 + w[2:3] * x0 + w[3:4] * xp1
        o_ref[r0:r0 + TB, :] = (_silu(y) * post).astype(o_ref.dtype)


def _conv_call(u, col0, width, w, b, post, name):
    tc = 512
    cb0 = col0 // tc
    return pl.pallas_call(
        _conv_kernel,
        grid=(BATCH, width // tc),
        in_specs=[pl.BlockSpec((T_BATCH, tc), lambda bi, c: (bi, cb0 + c)),
                  pl.BlockSpec((4, tc), lambda bi, c: (0, c)),
                  pl.BlockSpec((1, tc), lambda bi, c: (0, c)),
                  pl.BlockSpec((1, tc), lambda bi, c: (0, c))],
        out_specs=pl.BlockSpec((T_BATCH, tc), lambda bi, c: (bi, c)),
        out_shape=jax.ShapeDtypeStruct((N_TOK, width), BF16),
        compiler_params=_cparams(2),
        name=name,
    )(u, w, b, post)


def _tri_dot_cols(tri, x):
    a, b, c = _split3(x)
    return _dot(tri, a) + _dot(tri, b) + _dot(tri, c)


def _tri_dot_rows(x, tri):
    a, b, c = _split3(x)
    return _dot(a, tri) + _dot(b, tri) + _dot(c, tri)


def _gate_kernel(pre_ref, pret_ref, bc_ref, br_ref, ac_ref, ar_ref, bwdc_ref, bwdr_ref, tril_ref, triu_ref,
                 valc_ref, cumc_ref, valr_ref, cumr_ref, *, ssd):
    tril = tril_ref[...]
    triu = triu_ref[...]

    def act(v):
        if ssd:
            val = _softplus(v)
            return val, val
        return v, _log_sigmoid(v)

    vc, dc = act(pre_ref[...] + bc_ref[...])
    dc = dc * ac_ref[...]
    valc_ref[...] = vc
    cumc_ref[...] = jnp.where(bwdc_ref[...] > 0.5, _tri_dot_cols(triu, dc), _tri_dot_cols(tril, dc))

    vr, dr = act(pret_ref[...] + br_ref[...])
    dr = dr * ar_ref[...]
    valr_ref[...] = vr
    cumr_ref[...] = jnp.where(bwdr_ref[...] > 0.5, _tri_dot_rows(dr, tril), _tri_dot_rows(dr, triu))


def _gate_call(small, bias, scale, bwd, ssd, name):
    small_t = small.reshape(N_BLOCKS, TB, SMALL)[:, :, :64].transpose(0, 2, 1)
    tri = np.tril(np.ones((TB, TB), np.float32))
    tril = jnp.asarray(tri, BF16)
    triu = jnp.asarray(tri.T, BF16)
    col = lambda v: v.reshape(1, SMALL).astype(F32)
    rowv = lambda v: v[:64].reshape(64, 1).astype(F32)
    cs = lambda shape: pl.BlockSpec(shape, lambda j: (0,) * len(shape))
    return pl.pallas_call(
        functools.partial(_gate_kernel, ssd=ssd),
        grid=(N_BLOCKS,),
        in_specs=[_tok_spec(SMALL), pl.BlockSpec((None, 64, TB), lambda j: (j, 0, 0)),
                  cs((1, SMALL)), cs((64, 1)), cs((1, SMALL)), cs((64, 1)), cs((1, SMALL)), cs((64, 1)),
                  cs((TB, TB)), cs((TB, TB))],
        out_specs=[_tok_spec(SMALL), _tok_spec(SMALL),
                   pl.BlockSpec((None, 64, TB), lambda j: (j, 0, 0)),
                   pl.BlockSpec((None, 64, TB), lambda j: (j, 0, 0))],
        out_shape=[jax.ShapeDtypeStruct((N_TOK, SMALL), F32), jax.ShapeDtypeStruct((N_TOK, SMALL), F32),
                   jax.ShapeDtypeStruct((N_BLOCKS, 64, TB), F32), jax.ShapeDtypeStruct((N_BLOCKS, 64, TB), F32)],
        compiler_params=_cparams(1),
        name=name,
    )(small, small_t, col(bias), rowv(bias), col(scale), rowv(scale), col(bwd), rowv(bwd), tril, triu)


def _scan_block(d):
    if d == 0:
        return lambda b, s: b * SEQ_BLOCKS + s
    return lambda b, s: b * SEQ_BLOCKS + jnp.where(s == 0, 0, SEQ_BLOCKS - s)


def _causal_mask(d):
    t = lax.broadcasted_iota(jnp.int32, (TB, TB), 0)
    s = lax.broadcasted_iota(jnp.int32, (TB, TB), 1)
    return (s <= t) if d == 0 else (s >= t)


SUB = 128


def _mlstm_kernel(q_ref, k_ref, v_ref, valc_ref, cumc_ref, valr_ref, cumr_ref, o_ref,
                  c_ref, n_ref, m_ref, *, d):
    @pl.when(pl.program_id(1) == 0)
    def _():
        c_ref[...] = jnp.zeros_like(c_ref)
        n_ref[...] = jnp.zeros_like(n_ref)
        m_ref[...] = jnp.zeros_like(m_ref)

    t_i = lax.broadcasted_iota(jnp.int32, (SUB, SUB), 0)
    s_i = lax.broadcasted_iota(jnp.int32, (SUB, SUB), 1)
    mask = (s_i <= t_i) if d == 0 else (s_i >= t_i)
    bcast = lambda col: jnp.broadcast_to(col, (SUB, SUB))
    rep = lambda f: jnp.concatenate([f] * (DH_M // SUB), axis=1)
    order = (0, 1) if d == 0 else (1, 0)

    for h in range(HEADS_M):
        p = d * HEADS_M + h
        cs = slice(h * DH_M, (h + 1) * DH_M)
        base = jnp.zeros((1, 1), F32)
        c_st = c_ref[h]
        n_st = n_ref[h]
        m_st = m_ref[h]
        for ci, c in enumerate(order):
            r0 = c * SUB
            rows = slice(r0, r0 + SUB)
            if ci == 1:
                edge = SUB - 1 if d == 0 else SUB
                base = cumr_ref[8 + p:9 + p, edge:edge + 1]
            last = r0 + (SUB - 1 if d == 0 else 0)
            q = q_ref[rows, cs]
            k = k_ref[rows, cs]
            v = v_ref[rows, cs]
            cum_c = bcast(cumc_ref[rows, 8 + p:9 + p]) - base
            i_c = bcast(valc_ref[rows, p:p + 1])
            cum_r = cumr_ref[8 + p:9 + p, rows] - base
            i_r = valr_ref[p:p + 1, rows]
            total = cumr_ref[8 + p:9 + p, last:last + 1] - base

            dmat = jnp.where(mask, cum_c - cum_r + i_r, -jnp.inf)
            m_loc = jnp.max(dmat, axis=1, keepdims=True)
            a = jnp.exp(dmat - m_loc) * lax.dot_general(q, k, (((1,), (1,)), ((), ())),
                                                        preferred_element_type=F32)
            num_loc = _dot(a.astype(BF16), v)
            den_loc = jnp.sum(a, axis=1, keepdims=True)
            wl_log = total - cum_c + i_c
            m_kv = jnp.max(wl_log, axis=0, keepdims=True)[:, 0:1]
            wl = rep(jnp.exp(wl_log - m_kv))
            kv_loc = lax.dot_general(k, v * wl.astype(BF16), (((0,), (0,)), ((), ())),
                                     preferred_element_type=F32)
            kn_loc = jnp.sum(wl * k.astype(F32), axis=0, keepdims=True)

            g = cum_c + m_st
            m_t = jnp.maximum(g, m_loc)
            f_loc = jnp.exp(m_loc - m_t)
            f_st = jnp.exp(g - m_t)
            num = rep(f_loc) * num_loc + rep(f_st) * _dot(q, c_st.astype(BF16))
            den = f_loc * den_loc + f_st * jnp.sum(q.astype(F32) * n_st, axis=1, keepdims=True)
            inv = 1.0 / jnp.maximum(jnp.abs(den), jnp.exp(-m_t))
            o_ref[rows, cs] = (num * rep(inv)).astype(o_ref.dtype)

            m_new = jnp.maximum(total + m_st, m_kv)
            keep = jnp.exp(total + m_st - m_new)
            take = jnp.exp(m_kv - m_new)
            c_st = keep * c_st + take * kv_loc
            n_st = keep * n_st + take * kn_loc
            m_st = m_new
        c_ref[h] = c_st
        n_ref[h] = n_st
        m_ref[h] = m_st


def _mlstm_call(qk, ucm, valc, cumc, valr, cumr, d):
    blk = _scan_block(d)
    tok = lambda col: pl.BlockSpec((TB, D), lambda b, s: (blk(b, s), col))
    small = pl.BlockSpec((TB, SMALL), lambda b, s: (blk(b, s), 0))
    rows = pl.BlockSpec((None, 64, TB), lambda b, s: (blk(b, s), 0, 0))
    return pl.pallas_call(
        functools.partial(_mlstm_kernel, d=d),
        grid=(BATCH, SEQ_BLOCKS),
        in_specs=[tok(0), tok(1), tok(2), small, small, rows, rows],
        out_specs=tok(0),
        out_shape=jax.ShapeDtypeStruct((N_TOK, D), BF16),
        scratch_shapes=[pltpu.VMEM((HEADS_M, DH_M, DH_M), F32), pltpu.VMEM((HEADS_M, 1, DH_M), F32),
                        pltpu.VMEM((HEADS_M, 1, 1), F32)],
        compiler_params=_cparams(2),
        name="mlstm_fwd" if d == 0 else "mlstm_bwd",
    )(qk, qk, ucm, valc, cumc, valr, cumr)


def _ssd_kernel(x_ref, b_ref, c_ref, acsc_ref, dtr_ref, acsr_ref, o_ref, s_ref, *, d):
    @pl.when(pl.program_id(1) == 0)
    def _():
        s_ref[...] = jnp.zeros_like(s_ref)

    t_i = lax.broadcasted_iota(jnp.int32, (SUB, SUB), 0)
    s_i = lax.broadcasted_iota(jnp.int32, (SUB, SUB), 1)
    mask = (s_i <= t_i) if d == 0 else (s_i >= t_i)
    lo = lax.broadcasted_iota(jnp.int32, (SUB, 2 * SSD_P), 1) < SSD_P
    lo1 = lax.broadcasted_iota(jnp.int32, (1, 2 * SSD_P), 1) < SSD_P
    zero_b = jnp.zeros((SUB, 2 * SSD_P), BF16)
    bcast = lambda col: jnp.broadcast_to(col, (SUB, SUB))

    base_row = jnp.zeros((1, SMALL), F32)
    for ci, c in enumerate((0, 1) if d == 0 else (1, 0)):
        r0 = c * SUB
        rows = slice(r0, r0 + SUB)
        if ci == 1:
            edge = SUB - 1 if d == 0 else SUB
            base_row = acsc_ref[edge:edge + 1, :]
        last = r0 + (SUB - 1 if d == 0 else 0)
        acs_sub = acsc_ref[rows, :]
        tot_row = acsc_ref[last:last + 1, :] - base_row
        cd_row = jnp.exp(tot_row)
        from_start = jnp.exp(acs_sub - base_row)
        for g in range(SSD_GROUPS):
            bg = b_ref[rows, g * SSD_N:(g + 1) * SSD_N]
            cg = c_ref[rows, g * SSD_N:(g + 1) * SSD_N]
            cb = lax.dot_general(cg, bg, (((1,), (1,)), ((), ())), preferred_element_type=F32)
            bg_t = jnp.transpose(bg.astype(F32)).astype(BF16)
            for j in range(2):
                tile = slice(g * 4 * SSD_P + j * 2 * SSD_P, g * 4 * SSD_P + (j + 1) * 2 * SSD_P)
                stile = slice(j * 2 * SSD_P, (j + 1) * 2 * SSD_P)
                ps = [SSD_HEADS + d * SSD_HEADS + g * 4 + j * 2 + k for k in (0, 1)]
                xt = x_ref[rows, tile]
                st = s_ref[g, :, stile]
                st_b = st.astype(BF16)
                intra, inter, to_state = [], [], []
                for p in ps:
                    cum_col = bcast(acs_sub[:, p:p + 1])
                    cum_row = acsr_ref[p:p + 1, rows]
                    dt_row = dtr_ref[p:p + 1, rows]
                    seg = jnp.where(mask, cum_col - (cum_row - jnp.log(dt_row)), -jnp.inf)
                    intra.append((cb * jnp.exp(seg)).astype(BF16))
                    inter.append(cg * bcast(from_start[:, p:p + 1]).astype(BF16))
                    w_row = dt_row * jnp.exp(tot_row[:, p:p + 1] - (cum_row - base_row[:, p:p + 1]))
                    to_state.append(bg_t * w_row.astype(BF16))
                x_pair = [jnp.where(lo, xt, zero_b), jnp.where(lo, zero_b, xt)]
                s_pair = [jnp.where(lo, st_b, zero_b), jnp.where(lo, zero_b, st_b)]
                lhs = jnp.concatenate(intra + inter, axis=1)
                o_ref[rows, tile] = _dot(lhs, jnp.concatenate(x_pair + s_pair, axis=0)).astype(o_ref.dtype)
                cd_sel = jnp.where(lo1, cd_row[:, ps[0]:ps[0] + 1], cd_row[:, ps[1]:ps[1] + 1])
                s_ref[g, :, stile] = cd_sel * st + _dot(jnp.concatenate(to_state, axis=1),
                                                        jnp.concatenate(x_pair, axis=0))


def _ssd_call(xbc, acsc, dtr, acsr, d):
    blk = _scan_block(d)
    small = pl.BlockSpec((TB, SMALL), lambda b, s: (blk(b, s), 0))
    rows = pl.BlockSpec((None, 64, TB), lambda b, s: (blk(b, s), 0, 0))
    return pl.pallas_call(
        functools.partial(_ssd_kernel, d=d),
        grid=(BATCH, SEQ_BLOCKS),
        in_specs=[pl.BlockSpec((TB, D), lambda b, s: (blk(b, s), 0)),
                  pl.BlockSpec((TB, 512), lambda b, s: (blk(b, s), 2)),
                  pl.BlockSpec((TB, 512), lambda b, s: (blk(b, s), 3)),
                  small, rows, rows],
        out_specs=pl.BlockSpec((TB, D), lambda b, s: (blk(b, s), 0)),
        out_shape=jax.ShapeDtypeStruct((N_TOK, D), BF16),
        scratch_shapes=[pltpu.VMEM((SSD_GROUPS, SSD_N, 4 * SSD_P), F32)],
        compiler_params=_cparams(2),
        name="ssd_fwd" if d == 0 else "ssd_bwd",
    )(xbc, xbc, xbc, acsc, dtr, acsr)


def _pool_tables():
    masks = np.zeros((2, 4, TB, TB), np.float32)
    inv = np.zeros((2, 4, TB, 1), np.float32)
    for kind, length in ((0, GRID_W), (1, CTX)):
        for gi, w in enumerate(POOL_WINDOWS):
            for r in range(TB):
                base, c = (r // length) * length, r % length
                lo = min(max(c - w // 2, 0), length - 1)
                hi = min(max(c - w // 2 + w - 1, 0), length - 1)
                masks[kind, gi, r, base + lo:base + hi + 1] = 1.0
                inv[kind, gi, r, 0] = 1.0 / (hi - lo + 1)
    return jnp.asarray(masks, BF16), jnp.asarray(inv, F32)


ROUTE_FIELDS = 6


def _route_block(logits, ltri_ref, rt_ref, cnt_ref, cnt_scr):
    @pl.when(pl.program_id(0) == 0)
    def _():
        cnt_scr[...] = jnp.zeros_like(cnt_scr)

    lane = lax.broadcasted_iota(jnp.int32, logits.shape, 1).astype(F32)

    def top1(valid):
        v = jnp.max(jnp.where(valid, logits, -jnp.inf), axis=1, keepdims=True)
        i = jnp.min(jnp.where(valid & (logits == v), lane, float(SMALL)), axis=1, keepdims=True)
        return v, i

    is_g = lane < N_GROUPS_E
    gm, grp = top1(is_g)
    p_grp = 1.0 / jnp.sum(jnp.where(is_g, jnp.exp(logits - gm), 0.0), axis=1, keepdims=True)
    lo_lane = N_GROUPS_E + grp * EPG
    in_grp = (lane >= lo_lane) & (lane < lo_lane + EPG)
    v1, i1 = top1(in_grp)
    v2, i2 = top1(in_grp & (lane != i1))
    t = jnp.exp(v2 - v1)
    w1 = p_grp / (1.0 + t)
    w2 = p_grp * t / (1.0 + t)
    e1 = i1 - N_GROUPS_E
    e2 = i2 - N_GROUPS_E

    oh1 = lane == e1
    oh2 = lane == e2
    oh = jnp.where(oh1 | oh2, 1.0, 0.0)
    before = _dot(ltri_ref[...], oh.astype(BF16)) + cnt_scr[...]
    r1 = jnp.sum(jnp.where(oh1, before, 0.0), axis=1, keepdims=True)
    r2 = jnp.sum(jnp.where(oh2, before, 0.0), axis=1, keepdims=True)
    cnt_scr[...] = cnt_scr[...] + jnp.sum(oh, axis=0, keepdims=True)
    cnt_ref[...] = cnt_scr[...]
    rt = jnp.zeros(logits.shape, F32)
    for k, val in enumerate((e1, e2, w1, w2, r1, r2)):
        rt = jnp.where(lane == k, val, rt)
    rt_ref[...] = rt


def _merge_kernel(x_ref, pa_ref, o_ref, z_ref, mg0_ref, mg1_ref, mg2_ref, hf_ref, hb_ref, yf_ref, yb_ref, xs_ref,
                  pmask_ref, pinv_ref, poolw_ref, pscale_ref, mng_ref, dsk_ref, sng_ref, wbr_ref, wout_ref,
                  gate_ref, sh_ref, sc_ref, gffn_ref, wrh_ref, wrl_ref, brt_ref, ltri_ref,
                  xo_ref, h2_ref, rt_ref, cnt_ref, cnt_scr):
    parts = []
    for g in range(4):
        a_g = pa_ref[:, g * 256:(g + 1) * 256]
        pooled = _dot(pmask_ref[g], a_g) * pinv_ref[g] - a_g.astype(F32)
        parts.append(_dot(pooled.astype(BF16), poolw_ref[g]))
    pool = jnp.concatenate(parts, axis=1) * pscale_ref[...]

    hs = hf_ref[...].astype(F32) + hb_ref[...].astype(F32)
    hn = jnp.concatenate([_rms(hs[:, h * DH_M:(h + 1) * DH_M]) for h in range(HEADS_M)], axis=1)
    ml = _sigmoid(o_ref[...].astype(F32)) * (hn * mng_ref[...])

    y = yf_ref[...].astype(F32) + yb_ref[...].astype(F32) + dsk_ref[...] * xs_ref[...].astype(F32)
    sl = _rms(y * _silu(z_ref[...].astype(F32))) * sng_ref[...]

    acc = _sigmoid(mg0_ref[...].astype(F32)) * _dot(pool.astype(BF16), wbr_ref[0])
    acc = acc + _sigmoid(mg1_ref[...].astype(F32)) * _dot(ml.astype(BF16), wbr_ref[1])
    acc = acc + _sigmoid(mg2_ref[...].astype(F32)) * _dot(sl.astype(BF16), wbr_ref[2])
    xn = x_ref[...] + gate_ref[...] * _dot(acc.astype(BF16), wout_ref[...])
    xo_ref[...] = xn

    h2 = (_rms(xn) * gffn_ref[...]) * (1.0 + sc_ref[...]) + sh_ref[...]
    hi, lo = _split2(h2)
    h2f = hi.astype(F32)
    for cchunk in range(D // 128):
        h2_ref[pl.ds(cchunk, TB, stride=D // 128), :] = h2f[:, cchunk * 128:(cchunk + 1) * 128]
    logits = _dot(hi, wrh_ref[...]) + _dot(lo, wrh_ref[...]) + _dot(hi, wrl_ref[...]) + brt_ref[...]
    _route_block(logits, ltri_ref, rt_ref, cnt_ref, cnt_scr)


def _merge_call(x, urm, hf, hb, yf, yb, xbc, pmask, pinv, poolw, pscale, mng, dsk, sng, wbr, wout,
                mods3, gffn, wr_hi, wr_lo, br):
    ltri = jnp.asarray(np.tril(np.ones((TB, TB), np.float32), -1), BF16)
    kind = lambda j: jnp.where(j % SEQ_BLOCKS == 0, 1, 0)
    vec = _const_spec((1, D))
    return pl.pallas_call(
        _merge_kernel,
        grid=(N_BLOCKS,),
        in_specs=[_tok_spec(),
                  _tok_spec(D, 0), _tok_spec(D, 1), _tok_spec(D, 4),
                  _tok_spec(D, 5), _tok_spec(D, 6), _tok_spec(D, 7),
                  _tok_spec(), _tok_spec(), _tok_spec(), _tok_spec(),
                  _tok_spec(D, 0),
                  pl.BlockSpec((None, 4, TB, TB), lambda j: (kind(j), 0, 0, 0)),
                  pl.BlockSpec((None, 4, TB, 1), lambda j: (kind(j), 0, 0, 0)),
                  _const_spec((4, 256, 256)), vec, vec, vec, vec,
                  _const_spec((3, D, D)), _const_spec((D, D)),
                  _mod_spec(2), _mod_spec(3), _mod_spec(4), vec,
                  _const_spec((D, SMALL)), _const_spec((D, SMALL)), _const_spec((1, SMALL)),
                  _const_spec((TB, TB))],
        out_specs=[_tok_spec(), pl.BlockSpec((TB * ROW_TILES, 128), lambda j: (j, 0)), _tok_spec(SMALL),
                   _const_spec((1, SMALL))],
        out_shape=[jax.ShapeDtypeStruct((N_TOK, D), F32), jax.ShapeDtypeStruct((N_TOK * ROW_TILES, 128), F32),
                   jax.ShapeDtypeStruct((N_TOK, SMALL), F32), jax.ShapeDtypeStruct((1, SMALL), F32)],
        scratch_shapes=[pltpu.VMEM((1, SMALL), F32)],
        compiler_params=_cparams(1),
        name="branch_merge",
    )(x, urm, urm, urm, urm, urm, urm, hf, hb, yf, yb, xbc, pmask, pinv, poolw, pscale, mng, dsk, sng,
      wbr, wout, mods3, mods3, mods3, gffn, wr_hi, wr_lo, br, ltri)


GATHER_AHEAD = 2
GATHER_SLOTS = GATHER_AHEAD + 1
GATHER_DMA_PRIORITY = 1


def _moe_kernel(be_ref, nu_ref, tok0_ref, tok1_ref, tokn_ref, h_hbm, wg_ref, wu_ref, wd_ref, o_ref,
                xbuf, wgb, wub, wdb, sem):
    i = pl.program_id(0)
    n_used = nu_ref[0]
    slot = i % GATHER_SLOTS

    def start_gather(tok_ref, s):
        for r in range(MOE_BLOCK):
            row0 = pl.multiple_of(tok_ref[0, r] * ROW_TILES, ROW_TILES)
            pltpu.make_async_copy(h_hbm.at[pl.ds(row0, ROW_TILES)],
                                  xbuf.at[s, pl.ds(r * ROW_TILES, ROW_TILES)],
                                  sem.at[s]).start(priority=GATHER_DMA_PRIORITY)

    def wait_gather(s):
        pltpu.make_async_copy(xbuf.at[s], xbuf.at[s], sem.at[s]).wait()

    @pl.when((i == 0) & (n_used > 0))
    def _():
        start_gather(tok0_ref, 0)
        start_gather(tok1_ref, 1)

    @pl.when((i < n_used) & ((i == 0) | (be_ref[i] != be_ref[jnp.maximum(i - 1, 0)])))
    def _():
        wgb[...] = wg_ref[...].astype(BF16)
        wub[...] = wu_ref[...].astype(BF16)
        wdb[...] = wd_ref[...].astype(BF16)

    @pl.when(i < n_used)
    def _():
        wait_gather(slot)
        x = jnp.concatenate([xbuf[slot, pl.ds(c, MOE_BLOCK, stride=ROW_TILES), :] for c in range(ROW_TILES)],
                            axis=1).astype(BF16)
        start_gather(tokn_ref, (i + GATHER_AHEAD) % GATHER_SLOTS)
        gt = _dot(x, wgb[...])
        up = _dot(x, wub[...])
        act = (_silu(gt) * up).astype(BF16)
        o_ref[...] = _dot(act, wdb[...]).astype(o_ref.dtype)

    @pl.when(i == n_used - 1)
    def _():
        for ahead in range(1, GATHER_AHEAD + 1):
            wait_gather((i + ahead) % GATHER_SLOTS)

    @pl.when(i >= n_used)
    def _():
        o_ref[...] = jnp.zeros_like(o_ref)


MOE_CAP = N_TOK * 2 + N_EXPERTS * MOE_BLOCK
MOE_NBLOCKS = MOE_CAP // MOE_BLOCK


def _moe_call(layer, block_expert, n_used, buf_tok, h2, w_gate, w_up, w_down):
    grid_spec = pltpu.PrefetchScalarGridSpec(
        num_scalar_prefetch=2,
        grid=(MOE_NBLOCKS,),
        in_specs=[pl.BlockSpec((None, 1, MOE_BLOCK), lambda i, be, nu: (0, 0, 0), memory_space=pltpu.SMEM),
                  pl.BlockSpec((None, 1, MOE_BLOCK), lambda i, be, nu: (1, 0, 0), memory_space=pltpu.SMEM),
                  pl.BlockSpec((None, 1, MOE_BLOCK),
                               lambda i, be, nu: (jnp.minimum(i + GATHER_AHEAD, MOE_NBLOCKS - 1), 0, 0),
                               memory_space=pltpu.SMEM),
                  pl.BlockSpec(memory_space=pl.ANY),
                  pl.BlockSpec((None, None, D, D_EXPERT), lambda i, be, nu: (layer, be[i], 0, 0)),
                  pl.BlockSpec((None, None, D, D_EXPERT), lambda i, be, nu: (layer, be[i], 0, 0)),
                  pl.BlockSpec((None, None, D_EXPERT, D), lambda i, be, nu: (layer, be[i], 0, 0))],
        out_specs=pl.BlockSpec((MOE_BLOCK, D), lambda i, be, nu: (i, 0)),
        scratch_shapes=[pltpu.VMEM((GATHER_SLOTS, MOE_BLOCK * ROW_TILES, 128), F32),
                        pltpu.VMEM((D, D_EXPERT), BF16), pltpu.VMEM((D, D_EXPERT), BF16),
                        pltpu.VMEM((D_EXPERT, D), BF16), pltpu.SemaphoreType.DMA((GATHER_SLOTS,))],
    )
    return pl.pallas_call(
        _moe_kernel,
        grid_spec=grid_spec,
        out_shape=jax.ShapeDtypeStruct((MOE_CAP, D), BF16),
        compiler_params=_cparams(1),
        name="moe_experts",
    )(block_expert, n_used, buf_tok, buf_tok, buf_tok, h2, w_gate, w_up, w_down)


def _dispatch_plan(route, counts):
    cnt = counts[0, :N_EXPERTS].astype(jnp.int32)
    padded = (cnt + MOE_BLOCK - 1) // MOE_BLOCK * MOE_BLOCK
    pends = jnp.cumsum(padded)
    pstarts = pends - padded
    expert = route[:, 0:2].astype(jnp.int32)
    rank = route[:, 4:6].astype(jnp.int32)
    onehot = expert[:, :, None] == jnp.arange(N_EXPERTS, dtype=jnp.int32)
    pos = jnp.sum(jnp.where(onehot, pstarts, 0), axis=-1) + rank
    tok = jnp.broadcast_to(jnp.arange(N_TOK, dtype=jnp.int32)[:, None], (N_TOK, 2))
    buf_tok = jnp.zeros((MOE_CAP,), jnp.int32).at[pos.reshape(-1)].set(tok.reshape(-1))
    buf_tok = buf_tok.reshape(MOE_NBLOCKS, 1, MOE_BLOCK)
    block_start = jnp.arange(MOE_NBLOCKS, dtype=jnp.int32) * MOE_BLOCK
    block_expert = jnp.minimum(jnp.sum((pends[None, :] <= block_start[:, None]).astype(jnp.int32), axis=1),
                               N_EXPERTS - 1)
    n_used = (pends[-1] // MOE_BLOCK).reshape(1)
    return buf_tok, block_expert, n_used, pos


def _to_colmajor(t):
    c = t.shape[-1]
    t = t.reshape(BATCH, T_BATCH, c)
    lat = t[:, CTX:].reshape(BATCH, SEQ // GRID_W, GRID_W, c).swapaxes(1, 2).reshape(BATCH, SEQ, c)
    return jnp.concatenate([t[:, :CTX], lat], axis=1).reshape(N_TOK, c)


def _to_rowmajor(t):
    c = t.shape[-1]
    t = t.reshape(BATCH, T_BATCH, c)
    lat = t[:, CTX:].reshape(BATCH, GRID_W, SEQ // GRID_W, c).swapaxes(1, 2).reshape(BATCH, SEQ, c)
    return jnp.concatenate([t[:, :CTX], lat], axis=1).reshape(N_TOK, c)


def _lanes(*pieces):
    v = jnp.concatenate([jnp.asarray(p, F32).reshape(-1) for p in pieces])
    return jnp.pad(v, (0, SMALL - v.shape[0]))


def kernel(x, c, ctx, c_ctx, w_ada, b_ada, g_norm_mix, g_norm_ffn, w_in, pool_w, pool_scale, mlstm_conv_w,
           mlstm_conv_b, mlstm_gate_b, mlstm_norm_g, ssd_conv_w, ssd_conv_b, ssd_dt_bias, ssd_a_log, ssd_d,
           ssd_norm_g, w_branch, w_out, w_route_group, b_route_group, w_route_expert, b_route_expert,
           w_exp_gate, w_exp_up, w_exp_down, g_final):
    cvec = jnp.concatenate([c, c_ctx[None], jnp.zeros((3, D), F32)], axis=0)
    mods = _ada_table(cvec, w_ada, b_ada)
    xs = jnp.concatenate([ctx, x], axis=1).reshape(N_TOK, D)
    pmask, pinv = _pool_tables()
    row = lambda v: v.reshape(1, -1).astype(F32)
    w_in_bf = w_in.astype(BF16)

    h = small = None
    out = None
    for l in range(DEPTH):
        mods3 = mods[l].reshape(8, 1, 6 * D)
        wi = w_in_bf[l]
        sp = np.cumsum([0, D, D, D, D, D, 16, D, D, 512, 512, 32, 3 * D])
        piece = lambda i: wi[:, sp[i]:sp[i + 1]]
        w_rm = jnp.concatenate([piece(0), piece(4), piece(6), piece(8), piece(9), piece(7), piece(11)],
                               axis=1)
        w_cm = wi[:, sp[1]:sp[4]]
        w_small = jnp.pad(jnp.concatenate([w_in[l][:, sp[5]:sp[6]], w_in[l][:, sp[10]:sp[11]]], axis=1),
                          ((0, 0), (0, SMALL - 48)))
        ws_hi = w_small.astype(BF16)
        ws_lo = (w_small - ws_hi.astype(F32)).astype(BF16)

        if l == 0:
            h, small = _norm_call(xs, row(g_norm_mix[l]), mods3, ws_hi, ws_lo)

        h_cm = _to_colmajor(h)
        small_cm = _to_colmajor(small)
        u_rm = _mm_call(h, w_rm, "in_proj_rowmajor")
        u_cm = _mm_call(h_cm, w_cm, "in_proj_colmajor")

        post_m = jnp.concatenate([jnp.ones((D,), F32), jnp.full((D,), DH_M ** -0.5, F32)]).reshape(1, 2 * D)
        qk = _conv_call(u_cm, 0, 2 * D, mlstm_conv_w[l], row(mlstm_conv_b[l]), post_m, "mlstm_conv")
        gate_bias = _lanes(mlstm_gate_b[l])
        gate_bwd = _lanes(jnp.zeros((12,)), jnp.ones((4,)))
        g_valc, g_cumc, g_valr, g_cumr = _gate_call(small_cm, gate_bias, jnp.ones((SMALL,), F32), gate_bwd,
                                                    False, "mlstm_gates")
        h_f = _mlstm_call(qk, u_cm, g_valc, g_cumc, g_valr, g_cumr, 0)
        h_b = _mlstm_call(qk, u_cm, g_valc, g_cumc, g_valr, g_cumr, 1)
        h_f = _to_rowmajor(h_f)
        h_b = _to_rowmajor(h_b)

        xbc = _conv_call(u_rm, 2 * D, 2 * D, ssd_conv_w[l], row(ssd_conv_b[l]), jnp.ones((1, 2 * D), F32),
                         "ssd_conv")
        a_neg = -jnp.exp(ssd_a_log[l].astype(F32))
        dt_bias = _lanes(jnp.zeros((16,)), ssd_dt_bias[l])
        dt_scale = _lanes(jnp.zeros((16,)), a_neg)
        dt_bwd = _lanes(jnp.zeros((32,)), jnp.ones((16,)))
        _, s_acsc, s_dtr, s_acsr = _gate_call(small, dt_bias, dt_scale, dt_bwd, True, "ssd_gates")
        y_f = _ssd_call(xbc, s_acsc, s_dtr, s_acsr, 0)
        y_b = _ssd_call(xbc, s_acsc, s_dtr, s_acsr, 1)

        w_r = jnp.pad(jnp.concatenate([w_route_group[l], w_route_expert[l]], axis=1),
                      ((0, 0), (0, SMALL - N_GROUPS_E - N_EXPERTS)))
        wr_hi = w_r.astype(BF16)
        wr_lo = (w_r - wr_hi.astype(F32)).astype(BF16)
        b_r = _lanes(b_route_group[l], b_route_expert[l]).reshape(1, SMALL)
        dsk = jnp.repeat(ssd_d[l].astype(F32), SSD_P).reshape(1, D)
        xs, h2, route, counts = _merge_call(
            xs, u_rm, h_f, h_b, y_f, y_b, xbc, pmask, pinv, pool_w[l].astype(BF16), row(pool_scale[l]),
            row(mlstm_norm_g[l]), dsk, row(ssd_norm_g[l]), w_branch[l].astype(BF16), w_out[l].astype(BF16),
            mods3, row(g_norm_ffn[l]), wr_hi, wr_lo, b_r)

        buf_tok, block_expert, n_used, pos = _dispatch_plan(route, counts)
        yb = _moe_call(l, block_expert, n_used, buf_tok, h2, w_exp_gate, w_exp_up, w_exp_down)
        y0 = yb[pos[:, 0]]
        y1 = yb[pos[:, 1]]

        if l + 1 < DEPTH:
            wi_n = w_in[l + 1]
            w_small_n = jnp.pad(jnp.concatenate([wi_n[:, sp[5]:sp[6]], wi_n[:, sp[10]:sp[11]]], axis=1),
                                ((0, 0), (0, SMALL - 48)))
            wsn_hi = w_small_n.astype(BF16)
            wsn_lo = (w_small_n - wsn_hi.astype(F32)).astype(BF16)
            mods3_n = mods[l + 1].reshape(8, 1, 6 * D)
            xs, h, small = _resnorm_call(xs, y0, y1, route, mods3, row(g_norm_mix[l + 1]), mods3_n,
                                         wsn_hi, wsn_lo)
        else:
            out = _final_call(xs, y0, y1, route, mods3, row(g_final))
    return out.reshape(BATCH, SEQ, D)
```

```python
import functools
import math

import jax
import jax.numpy as jnp
import numpy as np
from jax import lax
from jax.experimental import pallas as pl
from jax.experimental.pallas import tpu as pltpu

F32 = jnp.float32
BF16 = jnp.bfloat16

D = 1024
BATCH = 4
SEQ = 4096
CTX = 256
DEPTH = 2
GRID_W = 64
EPS = 1e-6

TB = 256
SEQ_BLOCKS = (SEQ + CTX) // TB
T_BATCH = SEQ + CTX
N_TOK = BATCH * T_BATCH
N_BLOCKS = N_TOK // TB

POOL_WINDOWS = (2, 4, 8, 16)
HEADS_M = 4
DH_M = 256
SSD_HEADS = 16
SSD_P = 64
SSD_GROUPS = 4
SSD_N = 128
N_GROUPS_E = 4
EPG = 8
N_EXPERTS = 32
D_EXPERT = 512
MOE_BLOCK = 256
SMALL = 128
ROW_TILES = D // 128

VMEM_LIMIT = 56 * 1024 * 1024


def _cparams(n_axes):
    return pltpu.CompilerParams(dimension_semantics=("arbitrary",) * n_axes,
                                vmem_limit_bytes=VMEM_LIMIT)


def _dot(a, b):
    return jnp.dot(a, b, preferred_element_type=F32)


def _split2(x):
    hi = x.astype(BF16)
    lo = (x - hi.astype(F32)).astype(BF16)
    return hi, lo


def _split3(x):
    hi = x.astype(BF16)
    r = x - hi.astype(F32)
    mid = r.astype(BF16)
    lo = (r - mid.astype(F32)).astype(BF16)
    return hi, mid, lo


def _sigmoid(x):
    return 1.0 / (1.0 + jnp.exp(-x))


def _silu(x):
    return x * _sigmoid(x)


def _log1pexp_negabs(x):
    return jnp.log(1.0 + jnp.exp(-jnp.abs(x)))


def _log_sigmoid(x):
    return jnp.minimum(x, 0.0) - _log1pexp_negabs(x)


def _softplus(x):
    return jnp.maximum(x, 0.0) + _log1pexp_negabs(x)


def _rms(x):
    return x * lax.rsqrt(jnp.mean(x * x, axis=-1, keepdims=True) + EPS)


def _mod_row(j):
    return jnp.where(j % SEQ_BLOCKS == 0, BATCH, j // SEQ_BLOCKS)


def _ada_kernel(c_ref, w_ref, b_ref, o_ref):
    c = c_ref[...]
    s_hi, s_lo = _split2(_silu(c))
    w_hi, w_lo = _split2(w_ref[...])
    o_ref[...] = _dot(s_hi, w_hi) + _dot(s_lo, w_hi) + _dot(s_hi, w_lo) + b_ref[...]


def _ada_table(cvec, w_ada, b_ada):
    tn = 1024
    return pl.pallas_call(
        _ada_kernel,
        grid=(DEPTH, 6 * D // tn),
        in_specs=[pl.BlockSpec((8, D), lambda l, j: (0, 0)),
                  pl.BlockSpec((None, D, tn), lambda l, j: (l, 0, j)),
                  pl.BlockSpec((None, 1, tn), lambda l, j: (l, 0, j))],
        out_specs=pl.BlockSpec((None, 8, tn), lambda l, j: (l, 0, j)),
        out_shape=jax.ShapeDtypeStruct((DEPTH, 8, 6 * D), F32),
        compiler_params=_cparams(2),
        name="ada_table",
    )(cvec, w_ada, b_ada.reshape(DEPTH, 1, 6 * D))


def _norm_mod_small(x, g_ref, sh_ref, sc_ref, wsh_ref, wsl_ref, h_ref, small_ref):
    h = _rms(x) * g_ref[...]
    h = h * (1.0 + sc_ref[...]) + sh_ref[...]
    h_hi, h_lo = _split2(h)
    h_ref[...] = h_hi
    small_ref[...] = _dot(h_hi, wsh_ref[...]) + _dot(h_lo, wsh_ref[...]) + _dot(h_hi, wsl_ref[...])


def _norm_kernel(x_ref, g_ref, sh_ref, sc_ref, wsh_ref, wsl_ref, h_ref, small_ref):
    _norm_mod_small(x_ref[...], g_ref, sh_ref, sc_ref, wsh_ref, wsl_ref, h_ref, small_ref)


def _moe_residual(x_ref, ya_ref, yb_ref, rt_ref, gate_ref):
    y = rt_ref[:, 2:3] * ya_ref[...].astype(F32) + rt_ref[:, 3:4] * yb_ref[...].astype(F32)
    return x_ref[...] + gate_ref[...] * y


def _resnorm_kernel(x_ref, ya_ref, yb_ref, rt_ref, gate_ref, g_ref, sh_ref, sc_ref, wsh_ref, wsl_ref,
                    xo_ref, h_ref, small_ref):
    x = _moe_residual(x_ref, ya_ref, yb_ref, rt_ref, gate_ref)
    xo_ref[...] = x
    _norm_mod_small(x, g_ref, sh_ref, sc_ref, wsh_ref, wsl_ref, h_ref, small_ref)


def _mod_spec(chunk):
    return pl.BlockSpec((None, 1, D), lambda j: (_mod_row(j), 0, chunk))


def _tok_spec(width=D, col=0):
    return pl.BlockSpec((TB, width), lambda j: (j, col))


def _const_spec(shape):
    nd = len(shape)
    return pl.BlockSpec(shape, lambda j: (0,) * nd)


def _norm_call(x, g, mods3, ws_hi, ws_lo):
    return pl.pallas_call(
        _norm_kernel,
        grid=(N_BLOCKS,),
        in_specs=[_tok_spec(), _const_spec((1, D)), _mod_spec(0), _mod_spec(1),
                  _const_spec((D, SMALL)), _const_spec((D, SMALL))],
        out_specs=[_tok_spec(), _tok_spec(SMALL)],
        out_shape=[jax.ShapeDtypeStruct((N_TOK, D), BF16), jax.ShapeDtypeStruct((N_TOK, SMALL), F32)],
        compiler_params=_cparams(1),
        name="norm_mod",
    )(x, g, mods3, mods3, ws_hi, ws_lo)


def _resnorm_call(x, ya, yb, route, mods3_prev, g, mods3, ws_hi, ws_lo):
    return pl.pallas_call(
        _resnorm_kernel,
        grid=(N_BLOCKS,),
        in_specs=[_tok_spec(), _tok_spec(), _tok_spec(), _tok_spec(SMALL),
                  pl.BlockSpec((None, 1, D), lambda j: (_mod_row(j), 0, 5)),
                  _const_spec((1, D)), _mod_spec(0), _mod_spec(1),
                  _const_spec((D, SMALL)), _const_spec((D, SMALL))],
        out_specs=[_tok_spec(), _tok_spec(), _tok_spec(SMALL)],
        out_shape=[jax.ShapeDtypeStruct((N_TOK, D), F32), jax.ShapeDtypeStruct((N_TOK, D), BF16),
                   jax.ShapeDtypeStruct((N_TOK, SMALL), F32)],
        compiler_params=_cparams(1),
        name="residual_norm_mod",
    )(x, ya, yb, route, mods3_prev, g, mods3, mods3, ws_hi, ws_lo)


def _final_kernel(x_ref, ya_ref, yb_ref, rt_ref, gate_ref, g_ref, o_ref):
    o_ref[...] = _rms(_moe_residual(x_ref, ya_ref, yb_ref, rt_ref, gate_ref)) * g_ref[...]


def _final_call(x, ya, yb, route, mods3, g_final):
    lat = lambda b, s: (b * SEQ_BLOCKS + 1 + s, 0)
    spec = pl.BlockSpec((TB, D), lat)
    return pl.pallas_call(
        _final_kernel,
        grid=(BATCH, SEQ // TB),
        in_specs=[spec, spec, spec, pl.BlockSpec((TB, SMALL), lat),
                  pl.BlockSpec((None, 1, D), lambda b, s: (b, 0, 5)),
                  pl.BlockSpec((1, D), lambda b, s: (0, 0))],
        out_specs=pl.BlockSpec((TB, D), lambda b, s: (b * (SEQ // TB) + s, 0)),
        out_shape=jax.ShapeDtypeStruct((BATCH * SEQ, D), F32),
        compiler_params=_cparams(2),
        name="final_norm",
    )(x, ya, yb, route, mods3, g_final)


def _mm_kernel(h_ref, w_ref, o_ref):
    o_ref[...] = _dot(h_ref[...], w_ref[...]).astype(o_ref.dtype)


def _mm_call(h, w, name):
    n = w.shape[1]
    tm, tn = 1024, 1024
    return pl.pallas_call(
        _mm_kernel,
        grid=(n // tn, N_TOK // tm),
        in_specs=[pl.BlockSpec((tm, D), lambda j, i: (i, 0)),
                  pl.BlockSpec((D, tn), lambda j, i: (0, j))],
        out_specs=pl.BlockSpec((tm, tn), lambda j, i: (i, j)),
        out_shape=jax.ShapeDtypeStruct((N_TOK, n), BF16),
        compiler_params=_cparams(2),
        name=name,
    )(h, w)


CONV_HALO = 16


def _conv_kernel(x_ref, w_ref, b_ref, post_ref, o_ref):
    w = w_ref[...]
    bias = b_ref[...]
    post = post_ref[...]
    row = lax.broadcasted_iota(jnp.int32, (TB, 1), 0)
    for k in range(SEQ_BLOCKS):
        r0 = k * TB
        lo = max(r0 - CONV_HALO, 0)
        hi = min(r0 + TB + CONV_HALO, T_BATCH)
        off = r0 - lo
        n = hi - lo
        ext = x_ref[lo:hi, :].astype(F32)
        xm2 = pltpu.roll(ext, 2, 0)[off:off + TB]
        xm1 = pltpu.roll(ext, 1, 0)[off:off + TB]
        x0 = ext[off:off + TB]
        xp1 = pltpu.roll(ext, n - 1, 0)[off:off + TB]
        if k in (0, 1):
            xm2 = jnp.where(row >= 2, xm2, 0.0)
            xm1 = jnp.where(row >= 1, xm1, 0.0)
        if k in (0, SEQ_BLOCKS - 1):
            xp1 = jnp.where(row <= TB - 2, xp1, 0.0)
        y = bias + w[0:1] * xm2 + w[1:2] * xm1 + w[2:3] * x0 + w[3:4] * xp1
        o_ref[r0:r0 + TB, :] = (_silu(y) * post).astype(o_ref.dtype)


def _conv_call(u, col0, width, w, b, post, name):
    tc = 512
    cb0 = col0 // tc
    return pl.pallas_call(
        _conv_kernel,
        grid=(BATCH, width // tc),
        in_specs=[pl.BlockSpec((T_BATCH, tc), lambda bi, c: (bi, cb0 + c)),
                  pl.BlockSpec((4, tc), lambda bi, c: (0, c)),
                  pl.BlockSpec((1, tc), lambda bi, c: (0, c)),
                  pl.BlockSpec((1, tc), lambda bi, c: (0, c))],
        out_specs=pl.BlockSpec((T_BATCH, tc), lambda bi, c: (bi, c)),
        out_shape=jax.ShapeDtypeStruct((N_TOK, width), BF16),
        compiler_params=_cparams(2),
        name=name,
    )(u, w, b, post)


def _tri_dot_cols(tri, x):
    a, b, c = _split3(x)
    return _dot(tri, a) + _dot(tri, b) + _dot(tri, c)


def _tri_dot_rows(x, tri):
    a, b, c = _split3(x)
    return _dot(a, tri) + _dot(b, tri) + _dot(c, tri)


def _gate_kernel(pre_ref, pret_ref, bc_ref, br_ref, ac_ref, ar_ref, bwdc_ref, bwdr_ref, tril_ref, triu_ref,
                 valc_ref, cumc_ref, valr_ref, cumr_ref, *, ssd):
    tril = tril_ref[...]
    triu = triu_ref[...]

    def act(v):
        if ssd:
            val = _softplus(v)
            return val, val
        return v, _log_sigmoid(v)

    vc, dc = act(pre_ref[...] + bc_ref[...])
    dc = dc * ac_ref[...]
    valc_ref[...] = vc
    cumc_ref[...] = jnp.where(bwdc_ref[...] > 0.5, _tri_dot_cols(triu, dc), _tri_dot_cols(tril, dc))

    vr, dr = act(pret_ref[...] + br_ref[...])
    dr = dr * ar_ref[...]
    valr_ref[...] = vr
    cumr_ref[...] = jnp.where(bwdr_ref[...] > 0.5, _tri_dot_rows(dr, tril), _tri_dot_rows(dr, triu))


def _gate_call(small, bias, scale, bwd, ssd, name):
    small_t = small.reshape(N_BLOCKS, TB, SMALL)[:, :, :64].transpose(0, 2, 1)
    tri = np.tril(np.ones((TB, TB), np.float32))
    tril = jnp.asarray(tri, BF16)
    triu = jnp.asarray(tri.T, BF16)
    col = lambda v: v.reshape(1, SMALL).astype(F32)
    rowv = lambda v: v[:64].reshape(64, 1).astype(F32)
    cs = lambda shape: pl.BlockSpec(shape, lambda j: (0,) * len(shape))
    return pl.pallas_call(
        functools.partial(_gate_kernel, ssd=ssd),
        grid=(N_BLOCKS,),
        in_specs=[_tok_spec(SMALL), pl.BlockSpec((None, 64, TB), lambda j: (j, 0, 0)),
                  cs((1, SMALL)), cs((64, 1)), cs((1, SMALL)), cs((64, 1)), cs((1, SMALL)), cs((64, 1)),
                  cs((TB, TB)), cs((TB, TB))],
        out_specs=[_tok_spec(SMALL), _tok_spec(SMALL),
                   pl.BlockSpec((None, 64, TB), lambda j: (j, 0, 0)),
                   pl.BlockSpec((None, 64, TB), lambda j: (j, 0, 0))],
        out_shape=[jax.ShapeDtypeStruct((N_TOK, SMALL), F32), jax.ShapeDtypeStruct((N_TOK, SMALL), F32),
                   jax.ShapeDtypeStruct((N_BLOCKS, 64, TB), F32), jax.ShapeDtypeStruct((N_BLOCKS, 64, TB), F32)],
        compiler_params=_cparams(1),
        name=name,
    )(small, small_t, col(bias), rowv(bias), col(scale), rowv(scale), col(bwd), rowv(bwd), tril, triu)


def _scan_block(d):
    if d == 0:
        return lambda b, s: b * SEQ_BLOCKS + s
    return lambda b, s: b * SEQ_BLOCKS + jnp.where(s == 0, 0, SEQ_BLOCKS - s)


def _causal_mask(d):
    t = lax.broadcasted_iota(jnp.int32, (TB, TB), 0)
    s = lax.broadcasted_iota(jnp.int32, (TB, TB), 1)
    return (s <= t) if d == 0 else (s >= t)


SUB = 128


def _mlstm_kernel(q_ref, k_ref, v_ref, valc_ref, cumc_ref, valr_ref, cumr_ref, o_ref,
                  c_ref, n_ref, m_ref, *, d):
    @pl.when(pl.program_id(1) == 0)
    def _():
        c_ref[...] = jnp.zeros_like(c_ref)
        n_ref[...] = jnp.zeros_like(n_ref)
        m_ref[...] = jnp.zeros_like(m_ref)

    t_i = lax.broadcasted_iota(jnp.int32, (SUB, SUB), 0)
    s_i = lax.broadcasted_iota(jnp.int32, (SUB, SUB), 1)
    mask = (s_i <= t_i) if d == 0 else (s_i >= t_i)
    bcast = lambda col: jnp.broadcast_to(col, (SUB, SUB))
    rep = lambda f: jnp.concatenate([f] * (DH_M // SUB), axis=1)
    order = (0, 1) if d == 0 else (1, 0)

    for h in range(HEADS_M):
        p = d * HEADS_M + h
        cs = slice(h * DH_M, (h + 1) * DH_M)
        base = jnp.zeros((1, 1), F32)
        c_st = c_ref[h]
        n_st = n_ref[h]
        m_st = m_ref[h]
        for ci, c in enumerate(order):
            r0 = c * SUB
            rows = slice(r0, r0 + SUB)
            if ci == 1:
                edge = SUB - 1 if d == 0 else SUB
                base = cumr_ref[8 + p:9 + p, edge:edge + 1]
            last = r0 + (SUB - 1 if d == 0 else 0)
            q = q_ref[rows, cs]
            k = k_ref[rows, cs]
            v = v_ref[rows, cs]
            cum_c = bcast(cumc_ref[rows, 8 + p:9 + p]) - base
            i_c = bcast(valc_ref[rows, p:p + 1])
            cum_r = cumr_ref[8 + p:9 + p, rows] - base
            i_r = valr_ref[p:p + 1, rows]
            total = cumr_ref[8 + p:9 + p, last:last + 1] - base

            dmat = jnp.where(mask, cum_c - cum_r + i_r, -jnp.inf)
            m_loc = jnp.max(dmat, axis=1, keepdims=True)
            a = jnp.exp(dmat - m_loc) * lax.dot_general(q, k, (((1,), (1,)), ((), ())),
                                                        preferred_element_type=F32)
            num_loc = _dot(a.astype(BF16), v)
            den_loc = jnp.sum(a, axis=1, keepdims=True)
            wl_log = total - cum_c + i_c
            m_kv = jnp.max(wl_log, axis=0, keepdims=True)[:, 0:1]
            wl = rep(jnp.exp(wl_log - m_kv))
            kv_loc = lax.dot_general(k, v * wl.astype(BF16), (((0,), (0,)), ((), ())),
                                     preferred_element_type=F32)
            kn_loc = jnp.sum(wl * k.astype(F32), axis=0, keepdims=True)

            g = cum_c + m_st
            m_t = jnp.maximum(g, m_loc)
            f_loc = jnp.exp(m_loc - m_t)
            f_st = jnp.exp(g - m_t)
            num = rep(f_loc) * num_loc + rep(f_st) * _dot(q, c_st.astype(BF16))
            den = f_loc * den_loc + f_st * jnp.sum(q.astype(F32) * n_st, axis=1, keepdims=True)
            inv = 1.0 / jnp.maximum(jnp.abs(den), jnp.exp(-m_t))
            o_ref[rows, cs] = (num * rep(inv)).astype(o_ref.dtype)

            m_new = jnp.maximum(total + m_st, m_kv)
            keep = jnp.exp(total + m_st - m_new)
            take = jnp.exp(m_kv - m_new)
            c_st = keep * c_st + take * kv_loc
            n_st = keep * n_st + take * kn_loc
            m_st = m_new
        c_ref[h] = c_st
        n_ref[h] = n_st
        m_ref[h] = m_st


def _mlstm_call(qk, ucm, valc, cumc, valr, cumr, d):
    blk = _scan_block(d)
    tok = lambda col: pl.BlockSpec((TB, D), lambda b, s: (blk(b, s), col))
    small = pl.BlockSpec((TB, SMALL), lambda b, s: (blk(b, s), 0))
    rows = pl.BlockSpec((None, 64, TB), lambda b, s: (blk(b, s), 0, 0))
    return pl.pallas_call(
        functools.partial(_mlstm_kernel, d=d),
        grid=(BATCH, SEQ_BLOCKS),
        in_specs=[tok(0), tok(1), tok(2), small, small, rows, rows],
        out_specs=tok(0),
        out_shape=jax.ShapeDtypeStruct((N_TOK, D), BF16),
        scratch_shapes=[pltpu.VMEM((HEADS_M, DH_M, DH_M), F32), pltpu.VMEM((HEADS_M, 1, DH_M), F32),
                        pltpu.VMEM((HEADS_M, 1, 1), F32)],
        compiler_params=_cparams(2),
        name="mlstm_fwd" if d == 0 else "mlstm_bwd",
    )(qk, qk, ucm, valc, cumc, valr, cumr)


def _ssd_kernel(x_ref, b_ref, c_ref, acsc_ref, dtr_ref, acsr_ref, o_ref, s_ref, *, d):
    @pl.when(pl.program_id(1) == 0)
    def _():
        s_ref[...] = jnp.zeros_like(s_ref)

    t_i = lax.broadcasted_iota(jnp.int32, (SUB, SUB), 0)
    s_i = lax.broadcasted_iota(jnp.int32, (SUB, SUB), 1)
    mask = (s_i <= t_i) if d == 0 else (s_i >= t_i)
    lo = lax.broadcasted_iota(jnp.int32, (SUB, 2 * SSD_P), 1) < SSD_P
    lo1 = lax.broadcasted_iota(jnp.int32, (1, 2 * SSD_P), 1) < SSD_P
    zero_b = jnp.zeros((SUB, 2 * SSD_P), BF16)
    bcast = lambda col: jnp.broadcast_to(col, (SUB, SUB))

    base_row = jnp.zeros((1, SMALL), F32)
    for ci, c in enumerate((0, 1) if d == 0 else (1, 0)):
        r0 = c * SUB
        rows = slice(r0, r0 + SUB)
        if ci == 1:
            edge = SUB - 1 if d == 0 else SUB
            base_row = acsc_ref[edge:edge + 1, :]
        last = r0 + (SUB - 1 if d == 0 else 0)
        acs_sub = acsc_ref[rows, :]
        tot_row = acsc_ref[last:last + 1, :] - base_row
        cd_row = jnp.exp(tot_row)
        from_start = jnp.exp(acs_sub - base_row)
        for g in range(SSD_GROUPS):
            bg = b_ref[rows, g * SSD_N:(g + 1) * SSD_N]
            cg = c_ref[rows, g * SSD_N:(g + 1) * SSD_N]
            cb = lax.dot_general(cg, bg, (((1,), (1,)), ((), ())), preferred_element_type=F32)
            bg_t = jnp.transpose(bg.astype(F32)).astype(BF16)
            for j in range(2):
                tile = slice(g * 4 * SSD_P + j * 2 * SSD_P, g * 4 * SSD_P + (j + 1) * 2 * SSD_P)
                stile = slice(j * 2 * SSD_P, (j + 1) * 2 * SSD_P)
                ps = [SSD_HEADS + d * SSD_HEADS + g * 4 + j * 2 + k for k in (0, 1)]
                xt = x_ref[rows, tile]
                st = s_ref[g, :, stile]
                st_b = st.astype(BF16)
                intra, inter, to_state = [], [], []
                for p in ps:
                    cum_col = bcast(acs_sub[:, p:p + 1])
                    cum_row = acsr_ref[p:p + 1, rows]
                    dt_row = dtr_ref[p:p + 1, rows]
                    seg = jnp.where(mask, cum_col - (cum_row - jnp.log(dt_row)), -jnp.inf)
                    intra.append((cb * jnp.exp(seg)).astype(BF16))
                    inter.append(cg * bcast(from_start[:, p:p + 1]).astype(BF16))
                    w_row = dt_row * jnp.exp(tot_row[:, p:p + 1] - (cum_row - base_row[:, p:p + 1]))
                    to_state.append(bg_t * w_row.astype(BF16))
                x_pair = [jnp.where(lo, xt, zero_b), jnp.where(lo, zero_b, xt)]
                s_pair = [jnp.where(lo, st_b, zero_b), jnp.where(lo, zero_b, st_b)]
                lhs = jnp.concatenate(intra + inter, axis=1)
                o_ref[rows, tile] = _dot(lhs, jnp.concatenate(x_pair + s_pair, axis=0)).astype(o_ref.dtype)
                cd_sel = jnp.where(lo1, cd_row[:, ps[0]:ps[0] + 1], cd_row[:, ps[1]:ps[1] + 1])
                s_ref[g, :, stile] = cd_sel * st + _dot(jnp.concatenate(to_state, axis=1),
                                                        jnp.concatenate(x_pair, axis=0))


def _ssd_call(xbc, acsc, dtr, acsr, d):
    blk = _scan_block(d)
    small = pl.BlockSpec((TB, SMALL), lambda b, s: (blk(b, s), 0))
    rows = pl.BlockSpec((None, 64, TB), lambda b, s: (blk(b, s), 0, 0))
    return pl.pallas_call(
        functools.partial(_ssd_kernel, d=d),
        grid=(BATCH, SEQ_BLOCKS),
        in_specs=[pl.BlockSpec((TB, D), lambda b, s: (blk(b, s), 0)),
                  pl.BlockSpec((TB, 512), lambda b, s: (blk(b, s), 2)),
                  pl.BlockSpec((TB, 512), lambda b, s: (blk(b, s), 3)),
                  small, rows, rows],
        out_specs=pl.BlockSpec((TB, D), lambda b, s: (blk(b, s), 0)),
        out_shape=jax.ShapeDtypeStruct((N_TOK, D), BF16),
        scratch_shapes=[pltpu.VMEM((SSD_GROUPS, SSD_N, 4 * SSD_P), F32)],
        compiler_params=_cparams(2),
        name="ssd_fwd" if d == 0 else "ssd_bwd",
    )(xbc, xbc, xbc, acsc, dtr, acsr)


def _pool_tables():
    masks = np.zeros((2, 4, TB, TB), np.float32)
    inv = np.zeros((2, 4, TB, 1), np.float32)
    for kind, length in ((0, GRID_W), (1, CTX)):
        for gi, w in enumerate(POOL_WINDOWS):
            for r in range(TB):
                base, c = (r // length) * length, r % length
                lo = min(max(c - w // 2, 0), length - 1)
                hi = min(max(c - w // 2 + w - 1, 0), length - 1)
                masks[kind, gi, r, base + lo:base + hi + 1] = 1.0
                inv[kind, gi, r, 0] = 1.0 / (hi - lo + 1)
    return jnp.asarray(masks, BF16), jnp.asarray(inv, F32)


ROUTE_FIELDS = 6


def _route_block(logits, ltri_ref, rt_ref, cnt_ref, cnt_scr):
    @pl.when(pl.program_id(0) == 0)
    def _():
        cnt_scr[...] = jnp.zeros_like(cnt_scr)

    lane = lax.broadcasted_iota(jnp.int32, logits.shape, 1).astype(F32)

    def top1(valid):
        v = jnp.max(jnp.where(valid, logits, -jnp.inf), axis=1, keepdims=True)
        i = jnp.min(jnp.where(valid & (logits == v), lane, float(SMALL)), axis=1, keepdims=True)
        return v, i

    is_g = lane < N_GROUPS_E
    gm, grp = top1(is_g)
    p_grp = 1.0 / jnp.sum(jnp.where(is_g, jnp.exp(logits - gm), 0.0), axis=1, keepdims=True)
    lo_lane = N_GROUPS_E + grp * EPG
    in_grp = (lane >= lo_lane) & (lane < lo_lane + EPG)
    v1, i1 = top1(in_grp)
    v2, i2 = top1(in_grp & (lane != i1))
    t = jnp.exp(v2 - v1)
    w1 = p_grp / (1.0 + t)
    w2 = p_grp * t / (1.0 + t)
    e1 = i1 - N_GROUPS_E
    e2 = i2 - N_GROUPS_E

    oh1 = lane == e1
    oh2 = lane == e2
    oh = jnp.where(oh1 | oh2, 1.0, 0.0)
    before = _dot(ltri_ref[...], oh.astype(BF16)) + cnt_scr[...]
    r1 = jnp.sum(jnp.where(oh1, before, 0.0), axis=1, keepdims=True)
    r2 = jnp.sum(jnp.where(oh2, before, 0.0), axis=1, keepdims=True)
    cnt_scr[...] = cnt_scr[...] + jnp.sum(oh, axis=0, keepdims=True)
    cnt_ref[...] = cnt_scr[...]
    rt = jnp.zeros(logits.shape, F32)
    for k, val in enumerate((e1, e2, w1, w2, r1, r2)):
        rt = jnp.where(lane == k, val, rt)
    rt_ref[...] = rt


def _merge_kernel(x_ref, pa_ref, o_ref, z_ref, mg0_ref, mg1_ref, mg2_ref, hf_ref, hb_ref, yf_ref, yb_ref, xs_ref,
                  pmask_ref, pinv_ref, poolw_ref, pscale_ref, mng_ref, dsk_ref, sng_ref, wbr_ref, wout_ref,
                  gate_ref, sh_ref, sc_ref, gffn_ref, wrh_ref, wrl_ref, brt_ref, ltri_ref,
                  xo_ref, h2_ref, rt_ref, cnt_ref, cnt_scr):
    parts = []
    for g in range(4):
        a_g = pa_ref[:, g * 256:(g + 1) * 256]
        pooled = _dot(pmask_ref[g], a_g) * pinv_ref[g] - a_g.astype(F32)
        parts.append(_dot(pooled.astype(BF16), poolw_ref[g]))
    pool = jnp.concatenate(parts, axis=1) * pscale_ref[...]

    hs = hf_ref[...].astype(F32) + hb_ref[...].astype(F32)
    hn = jnp.concatenate([_rms(hs[:, h * DH_M:(h + 1) * DH_M]) for h in range(HEADS_M)], axis=1)
    ml = _sigmoid(o_ref[...].astype(F32)) * (hn * mng_ref[...])

    y = yf_ref[...].astype(F32) + yb_ref[...].astype(F32) + dsk_ref[...] * xs_ref[...].astype(F32)
    sl = _rms(y * _silu(z_ref[...].astype(F32))) * sng_ref[...]

    acc = _sigmoid(mg0_ref[...].astype(F32)) * _dot(pool.astype(BF16), wbr_ref[0])
    acc = acc + _sigmoid(mg1_ref[...].astype(F32)) * _dot(ml.astype(BF16), wbr_ref[1])
    acc = acc + _sigmoid(mg2_ref[...].astype(F32)) * _dot(sl.astype(BF16), wbr_ref[2])
    xn = x_ref[...] + gate_ref[...] * _dot(acc.astype(BF16), wout_ref[...])
    xo_ref[...] = xn

    h2 = (_rms(xn) * gffn_ref[...]) * (1.0 + sc_ref[...]) + sh_ref[...]
    hi, lo = _split2(h2)
    h2f = hi.astype(F32)
    for cchunk in range(D // 128):
        h2_ref[pl.ds(cchunk, TB, stride=D // 128), :] = h2f[:, cchunk * 128:(cchunk + 1) * 128]
    logits = _dot(hi, wrh_ref[...]) + _dot(lo, wrh_ref[...]) + _dot(hi, wrl_ref[...]) + brt_ref[...]
    _route_block(logits, ltri_ref, rt_ref, cnt_ref, cnt_scr)


def _merge_call(x, urm, hf, hb, yf, yb, xbc, pmask, pinv, poolw, pscale, mng, dsk, sng, wbr, wout,
                mods3, gffn, wr_hi, wr_lo, br):
    ltri = jnp.asarray(np.tril(np.ones((TB, TB), np.float32), -1), BF16)
    kind = lambda j: jnp.where(j % SEQ_BLOCKS == 0, 1, 0)
    vec = _const_spec((1, D))
    return pl.pallas_call(
        _merge_kernel,
        grid=(N_BLOCKS,),
        in_specs=[_tok_spec(),
                  _tok_spec(D, 0), _tok_spec(D, 1), _tok_spec(D, 4),
                  _tok_spec(D, 5), _tok_spec(D, 6), _tok_spec(D, 7),
                  _tok_spec(), _tok_spec(), _tok_spec(), _tok_spec(),
                  _tok_spec(D, 0),
                  pl.BlockSpec((None, 4, TB, TB), lambda j: (kind(j), 0, 0, 0)),
                  pl.BlockSpec((None, 4, TB, 1), lambda j: (kind(j), 0, 0, 0)),
                  _const_spec((4, 256, 256)), vec, vec, vec, vec,
                  _const_spec((3, D, D)), _const_spec((D, D)),
                  _mod_spec(2), _mod_spec(3), _mod_spec(4), vec,
                  _const_spec((D, SMALL)), _const_spec((D, SMALL)), _const_spec((1, SMALL)),
                  _const_spec((TB, TB))],
        out_specs=[_tok_spec(), pl.BlockSpec((TB * ROW_TILES, 128), lambda j: (j, 0)), _tok_spec(SMALL),
                   _const_spec((1, SMALL))],
        out_shape=[jax.ShapeDtypeStruct((N_TOK, D), F32), jax.ShapeDtypeStruct((N_TOK * ROW_TILES, 128), F32),
                   jax.ShapeDtypeStruct((N_TOK, SMALL), F32), jax.ShapeDtypeStruct((1, SMALL), F32)],
        scratch_shapes=[pltpu.VMEM((1, SMALL), F32)],
        compiler_params=_cparams(1),
        name="branch_merge",
    )(x, urm, urm, urm, urm, urm, urm, hf, hb, yf, yb, xbc, pmask, pinv, poolw, pscale, mng, dsk, sng,
      wbr, wout, mods3, mods3, mods3, gffn, wr_hi, wr_lo, br, ltri)


GATHER_AHEAD = 2
GATHER_SLOTS = GATHER_AHEAD + 1
GATHER_DMA_QUEUES = 2


def _moe_kernel(be_ref, nu_ref, tok0_ref, tok1_ref, tokn_ref, h_hbm, wg_ref, wu_ref, wd_ref, o_ref,
                xbuf, wgb, wub, wdb, sem):
    i = pl.program_id(0)
    n_used = nu_ref[0]
    slot = i % GATHER_SLOTS

    def start_gather(tok_ref, s):
        for r in range(MOE_BLOCK):
            row0 = pl.multiple_of(tok_ref[0, r] * ROW_TILES, ROW_TILES)
            pltpu.make_async_copy(h_hbm.at[pl.ds(row0, ROW_TILES)],
                                  xbuf.at[s, pl.ds(r * ROW_TILES, ROW_TILES)],
                                  sem.at[s]).start(priority=r % GATHER_DMA_QUEUES)

    def wait_gather(s):
        pltpu.make_async_copy(xbuf.at[s], xbuf.at[s], sem.at[s]).wait()

    @pl.when((i == 0) & (n_used > 0))
    def _():
        start_gather(tok0_ref, 0)
        start_gather(tok1_ref, 1)

    @pl.when((i < n_used) & ((i == 0) | (be_ref[i] != be_ref[jnp.maximum(i - 1, 0)])))
    def _():
        wgb[...] = wg_ref[...].astype(BF16)
        wub[...] = wu_ref[...].astype(BF16)
        wdb[...] = wd_ref[...].astype(BF16)

    @pl.when(i < n_used)
    def _():
        wait_gather(slot)
        x = jnp.concatenate([xbuf[slot, pl.ds(c, MOE_BLOCK, stride=ROW_TILES), :] for c in range(ROW_TILES)],
                            axis=1).astype(BF16)
        start_gather(tokn_ref, (i + GATHER_AHEAD) % GATHER_SLOTS)
        gt = _dot(x, wgb[...])
        up = _dot(x, wub[...])
        act = (_silu(gt) * up).astype(BF16)
        o_ref[...] = _dot(act, wdb[...]).astype(o_ref.dtype)

    @pl.when(i == n_used - 1)
    def _():
        for ahead in range(1, GATHER_AHEAD + 1):
            wait_gather((i + ahead) % GATHER_SLOTS)

    @pl.when(i >= n_used)
    def _():
        o_ref[...] = jnp.zeros_like(o_ref)


MOE_CAP = N_TOK * 2 + N_EXPERTS * MOE_BLOCK
MOE_NBLOCKS = MOE_CAP // MOE_BLOCK


def _moe_call(layer, block_expert, n_used, buf_tok, h2, w_gate, w_up, w_down):
    grid_spec = pltpu.PrefetchScalarGridSpec(
        num_scalar_prefetch=2,
        grid=(MOE_NBLOCKS,),
        in_specs=[pl.BlockSpec((None, 1, MOE_BLOCK), lambda i, be, nu: (0, 0, 0), memory_space=pltpu.SMEM),
                  pl.BlockSpec((None, 1, MOE_BLOCK), lambda i, be, nu: (1, 0, 0), memory_space=pltpu.SMEM),
                  pl.BlockSpec((None, 1, MOE_BLOCK),
                               lambda i, be, nu: (jnp.minimum(i + GATHER_AHEAD, MOE_NBLOCKS - 1), 0, 0),
                               memory_space=pltpu.SMEM),
                  pl.BlockSpec(memory_space=pl.ANY),
                  pl.BlockSpec((None, None, D, D_EXPERT), lambda i, be, nu: (layer, be[i], 0, 0)),
                  pl.BlockSpec((None, None, D, D_EXPERT), lambda i, be, nu: (layer, be[i], 0, 0)),
                  pl.BlockSpec((None, None, D_EXPERT, D), lambda i, be, nu: (layer, be[i], 0, 0))],
        out_specs=pl.BlockSpec((MOE_BLOCK, D), lambda i, be, nu: (i, 0)),
        scratch_shapes=[pltpu.VMEM((GATHER_SLOTS, MOE_BLOCK * ROW_TILES, 128), F32),
                        pltpu.VMEM((D, D_EXPERT), BF16), pltpu.VMEM((D, D_EXPERT), BF16),
                        pltpu.VMEM((D_EXPERT, D), BF16), pltpu.SemaphoreType.DMA((GATHER_SLOTS,))],
    )
    return pl.pallas_call(
        _moe_kernel,
        grid_spec=grid_spec,
        out_shape=jax.ShapeDtypeStruct((MOE_CAP, D), BF16),
        compiler_params=_cparams(1),
        name="moe_experts",
    )(block_expert, n_used, buf_tok, buf_tok, buf_tok, h2, w_gate, w_up, w_down)


def _dispatch_plan(route, counts):
    cnt = counts[0, :N_EXPERTS].astype(jnp.int32)
    padded = (cnt + MOE_BLOCK - 1) // MOE_BLOCK * MOE_BLOCK
    pends = jnp.cumsum(padded)
    pstarts = pends - padded
    expert = route[:, 0:2].astype(jnp.int32)
    rank = route[:, 4:6].astype(jnp.int32)
    onehot = expert[:, :, None] == jnp.arange(N_EXPERTS, dtype=jnp.int32)
    pos = jnp.sum(jnp.where(onehot, pstarts, 0), axis=-1) + rank
    tok = jnp.broadcast_to(jnp.arange(N_TOK, dtype=jnp.int32)[:, None], (N_TOK, 2))
    buf_tok = jnp.zeros((MOE_CAP,), jnp.int32).at[pos.reshape(-1)].set(tok.reshape(-1))
    buf_tok = buf_tok.reshape(MOE_NBLOCKS, 1, MOE_BLOCK)
    block_start = jnp.arange(MOE_NBLOCKS, dtype=jnp.int32) * MOE_BLOCK
    block_expert = jnp.minimum(jnp.sum((pends[None, :] <= block_start[:, None]).astype(jnp.int32), axis=1),
                               N_EXPERTS - 1)
    n_used = (pends[-1] // MOE_BLOCK).reshape(1)
    return buf_tok, block_expert, n_used, pos


def _to_colmajor(t):
    c = t.shape[-1]
    t = t.reshape(BATCH, T_BATCH, c)
    lat = t[:, CTX:].reshape(BATCH, SEQ // GRID_W, GRID_W, c).swapaxes(1, 2).reshape(BATCH, SEQ, c)
    return jnp.concatenate([t[:, :CTX], lat], axis=1).reshape(N_TOK, c)


def _to_rowmajor(t):
    c = t.shape[-1]
    t = t.reshape(BATCH, T_BATCH, c)
    lat = t[:, CTX:].reshape(BATCH, GRID_W, SEQ // GRID_W, c).swapaxes(1, 2).reshape(BATCH, SEQ, c)
    return jnp.concatenate([t[:, :CTX], lat], axis=1).reshape(N_TOK, c)


def _lanes(*pieces):
    v = jnp.concatenate([jnp.asarray(p, F32).reshape(-1) for p in pieces])
    return jnp.pad(v, (0, SMALL - v.shape[0]))


def kernel(x, c, ctx, c_ctx, w_ada, b_ada, g_norm_mix, g_norm_ffn, w_in, pool_w, pool_scale, mlstm_conv_w,
           mlstm_conv_b, mlstm_gate_b, mlstm_norm_g, ssd_conv_w, ssd_conv_b, ssd_dt_bias, ssd_a_log, ssd_d,
           ssd_norm_g, w_branch, w_out, w_route_group, b_route_group, w_route_expert, b_route_expert,
           w_exp_gate, w_exp_up, w_exp_down, g_final):
    cvec = jnp.concatenate([c, c_ctx[None], jnp.zeros((3, D), F32)], axis=0)
    mods = _ada_table(cvec, w_ada, b_ada)
    xs = jnp.concatenate([ctx, x], axis=1).reshape(N_TOK, D)
    pmask, pinv = _pool_tables()
    row = lambda v: v.reshape(1, -1).astype(F32)
    w_in_bf = w_in.astype(BF16)

    h = small = None
    out = None
    for l in range(DEPTH):
        mods3 = mods[l].reshape(8, 1, 6 * D)
        wi = w_in_bf[l]
        sp = np.cumsum([0, D, D, D, D, D, 16, D, D, 512, 512, 32, 3 * D])
        piece = lambda i: wi[:, sp[i]:sp[i + 1]]
        w_rm = jnp.concatenate([piece(0), piece(4), piece(6), piece(8), piece(9), piece(7), piece(11)],
                               axis=1)
        w_cm = wi[:, sp[1]:sp[4]]
        w_small = jnp.pad(jnp.concatenate([w_in[l][:, sp[5]:sp[6]], w_in[l][:, sp[10]:sp[11]]], axis=1),
                          ((0, 0), (0, SMALL - 48)))
        ws_hi = w_small.astype(BF16)
        ws_lo = (w_small - ws_hi.astype(F32)).astype(BF16)

        if l == 0:
            h, small = _norm_call(xs, row(g_norm_mix[l]), mods3, ws_hi, ws_lo)

        h_cm = _to_colmajor(h)
        small_cm = _to_colmajor(small)
        u_rm = _mm_call(h, w_rm, "in_proj_rowmajor")
        u_cm = _mm_call(h_cm, w_cm, "in_proj_colmajor")

        post_m = jnp.concatenate([jnp.ones((D,), F32), jnp.full((D,), DH_M ** -0.5, F32)]).reshape(1, 2 * D)
        qk = _conv_call(u_cm, 0, 2 * D, mlstm_conv_w[l], row(mlstm_conv_b[l]), post_m, "mlstm_conv")
        gate_bias = _lanes(mlstm_gate_b[l])
        gate_bwd = _lanes(jnp.zeros((12,)), jnp.ones((4,)))
        g_valc, g_cumc, g_valr, g_cumr = _gate_call(small_cm, gate_bias, jnp.ones((SMALL,), F32), gate_bwd,
                                                    False, "mlstm_gates")
        h_f = _mlstm_call(qk, u_cm, g_valc, g_cumc, g_valr, g_cumr, 0)
        h_b = _mlstm_call(qk, u_cm, g_valc, g_cumc, g_valr, g_cumr, 1)
        h_f = _to_rowmajor(h_f)
        h_b = _to_rowmajor(h_b)

        xbc = _conv_call(u_rm, 2 * D, 2 * D, ssd_conv_w[l], row(ssd_conv_b[l]), jnp.ones((1, 2 * D), F32),
                         "ssd_conv")
        a_neg = -jnp.exp(ssd_a_log[l].astype(F32))
        dt_bias = _lanes(jnp.zeros((16,)), ssd_dt_bias[l])
        dt_scale = _lanes(jnp.zeros((16,)), a_neg)
        dt_bwd = _lanes(jnp.zeros((32,)), jnp.ones((16,)))
        _, s_acsc, s_dtr, s_acsr = _gate_call(small, dt_bias, dt_scale, dt_bwd, True, "ssd_gates")
        y_f = _ssd_call(xbc, s_acsc, s_dtr, s_acsr, 0)
        y_b = _ssd_call(xbc, s_acsc, s_dtr, s_acsr, 1)

        w_r = jnp.pad(jnp.concatenate([w_route_group[l], w_route_expert[l]], axis=1),
                      ((0, 0), (0, SMALL - N_GROUPS_E - N_EXPERTS)))
        wr_hi = w_r.astype(BF16)
        wr_lo = (w_r - wr_hi.astype(F32)).astype(BF16)
        b_r = _lanes(b_route_group[l], b_route_expert[l]).reshape(1, SMALL)
        dsk = jnp.repeat(ssd_d[l].astype(F32), SSD_P).reshape(1, D)
        xs, h2, route, counts = _merge_call(
            xs, u_rm, h_f, h_b, y_f, y_b, xbc, pmask, pinv, pool_w[l].astype(BF16), row(pool_scale[l]),
            row(mlstm_norm_g[l]), dsk, row(ssd_norm_g[l]), w_branch[l].astype(BF16), w_out[l].astype(BF16),
            mods3, row(g_norm_ffn[l]), wr_hi, wr_lo, b_r)

        buf_tok, block_expert, n_used, pos = _dispatch_plan(route, counts)
        yb = _moe_call(l, block_expert, n_used, buf_tok, h2, w_exp_gate, w_exp_up, w_exp_down)
        y0 = yb[pos[:, 0]]
        y1 = yb[pos[:, 1]]

        if l + 1 < DEPTH:
            wi_n = w_in[l + 1]
            w_small_n = jnp.pad(jnp.concatenate([wi_n[:, sp[5]:sp[6]], wi_n[:, sp[10]:sp[11]]], axis=1),
                                ((0, 0), (0, SMALL - 48)))
            wsn_hi = w_small_n.astype(BF16)
            wsn_lo = (w_small_n - wsn_hi.astype(F32)).astype(BF16)
            mods3_n = mods[l + 1].reshape(8, 1, 6 * D)
            xs, h, small = _resnorm_call(xs, y0, y1, route, mods3, row(g_norm_mix[l + 1]), mods3_n,
                                         wsn_hi, wsn_lo)
        else:
            out = _final_call(xs, y0, y1, route, mods3, row(g_final))
    return out.reshape(BATCH, SEQ, D)
```

```python
import functools
import math

import jax
import jax.numpy as jnp
import numpy as np
from jax import lax
from jax.experimental import pallas as pl
from jax.experimental.pallas import tpu as pltpu

F32 = jnp.float32
BF16 = jnp.bfloat16

D = 1024
BATCH = 4
SEQ = 4096
CTX = 256
DEPTH = 2
GRID_W = 64
EPS = 1e-6

TB = 256
SEQ_BLOCKS = (SEQ + CTX) // TB
T_BATCH = SEQ + CTX
N_TOK = BATCH * T_BATCH
N_BLOCKS = N_TOK // TB

POOL_WINDOWS = (2, 4, 8, 16)
HEADS_M = 4
DH_M = 256
SSD_HEADS = 16
SSD_P = 64
SSD_GROUPS = 4
SSD_N = 128
N_GROUPS_E = 4
EPG = 8
N_EXPERTS = 32
D_EXPERT = 512
MOE_BLOCK = 256
SMALL = 128
ROW_TILES = D // 128

VMEM_LIMIT = 56 * 1024 * 1024


def _cparams(n_axes):
    return pltpu.CompilerParams(dimension_semantics=("arbitrary",) * n_axes,
                                vmem_limit_bytes=VMEM_LIMIT)


def _dot(a, b):
    return jnp.dot(a, b, preferred_element_type=F32)


def _split2(x):
    hi = x.astype(BF16)
    lo = (x - hi.astype(F32)).astype(BF16)
    return hi, lo


def _split3(x):
    hi = x.astype(BF16)
    r = x - hi.astype(F32)
    mid = r.astype(BF16)
    lo = (r - mid.astype(F32)).astype(BF16)
    return hi, mid, lo


def _sigmoid(x):
    return 1.0 / (1.0 + jnp.exp(-x))


def _silu(x):
    return x * _sigmoid(x)


def _log1pexp_negabs(x):
    return jnp.log(1.0 + jnp.exp(-jnp.abs(x)))


def _log_sigmoid(x):
    return jnp.minimum(x, 0.0) - _log1pexp_negabs(x)


def _softplus(x):
    return jnp.maximum(x, 0.0) + _log1pexp_negabs(x)


def _rms(x):
    return x * lax.rsqrt(jnp.mean(x * x, axis=-1, keepdims=True) + EPS)


def _mod_row(j):
    return jnp.where(j % SEQ_BLOCKS == 0, BATCH, j // SEQ_BLOCKS)


def _ada_kernel(c_ref, w_ref, b_ref, o_ref):
    c = c_ref[...]
    s_hi, s_lo = _split2(_silu(c))
    w_hi, w_lo = _split2(w_ref[...])
    o_ref[...] = _dot(s_hi, w_hi) + _dot(s_lo, w_hi) + _dot(s_hi, w_lo) + b_ref[...]


def _ada_table(cvec, w_ada, b_ada):
    tn = 1024
    return pl.pallas_call(
        _ada_kernel,
        grid=(DEPTH, 6 * D // tn),
        in_specs=[pl.BlockSpec((8, D), lambda l, j: (0, 0)),
                  pl.BlockSpec((None, D, tn), lambda l, j: (l, 0, j)),
                  pl.BlockSpec((None, 1, tn), lambda l, j: (l, 0, j))],
        out_specs=pl.BlockSpec((None, 8, tn), lambda l, j: (l, 0, j)),
        out_shape=jax.ShapeDtypeStruct((DEPTH, 8, 6 * D), F32),
        compiler_params=_cparams(2),
        name="ada_table",
    )(cvec, w_ada, b_ada.reshape(DEPTH, 1, 6 * D))


def _norm_mod_small(x, g_ref, sh_ref, sc_ref, wsh_ref, wsl_ref, h_ref, small_ref):
    h = _rms(x) * g_ref[...]
    h = h * (1.0 + sc_ref[...]) + sh_ref[...]
    h_hi, h_lo = _split2(h)
    h_ref[...] = h_hi
    small_ref[...] = _dot(h_hi, wsh_ref[...]) + _dot(h_lo, wsh_ref[...]) + _dot(h_hi, wsl_ref[...])


def _norm_kernel(x_ref, g_ref, sh_ref, sc_ref, wsh_ref, wsl_ref, h_ref, small_ref):
    _norm_mod_small(x_ref[...], g_ref, sh_ref, sc_ref, wsh_ref, wsl_ref, h_ref, small_ref)


def _moe_residual(x_ref, ya_ref, yb_ref, rt_ref, gate_ref):
    y = rt_ref[:, 2:3] * ya_ref[...].astype(F32) + rt_ref[:, 3:4] * yb_ref[...].astype(F32)
    return x_ref[...] + gate_ref[...] * y


def _resnorm_kernel(x_ref, ya_ref, yb_ref, rt_ref, gate_ref, g_ref, sh_ref, sc_ref, wsh_ref, wsl_ref,
                    xo_ref, h_ref, small_ref):
    x = _moe_residual(x_ref, ya_ref, yb_ref, rt_ref, gate_ref)
    xo_ref[...] = x
    _norm_mod_small(x, g_ref, sh_ref, sc_ref, wsh_ref, wsl_ref, h_ref, small_ref)


def _mod_spec(chunk):
    return pl.BlockSpec((None, 1, D), lambda j: (_mod_row(j), 0, chunk))


def _tok_spec(width=D, col=0):
    return pl.BlockSpec((TB, width), lambda j: (j, col))


def _const_spec(shape):
    nd = len(shape)
    return pl.BlockSpec(shape, lambda j: (0,) * nd)


def _norm_call(x, g, mods3, ws_hi, ws_lo):
    return pl.pallas_call(
        _norm_kernel,
        grid=(N_BLOCKS,),
        in_specs=[_tok_spec(), _const_spec((1, D)), _mod_spec(0), _mod_spec(1),
                  _const_spec((D, SMALL)), _const_spec((D, SMALL))],
        out_specs=[_tok_spec(), _tok_spec(SMALL)],
        out_shape=[jax.ShapeDtypeStruct((N_TOK, D), BF16), jax.ShapeDtypeStruct((N_TOK, SMALL), F32)],
        compiler_params=_cparams(1),
        name="norm_mod",
    )(x, g, mods3, mods3, ws_hi, ws_lo)


def _resnorm_call(x, ya, yb, route, mods3_prev, g, mods3, ws_hi, ws_lo):
    return pl.pallas_call(
        _resnorm_kernel,
        grid=(N_BLOCKS,),
        in_specs=[_tok_spec(), _tok_spec(), _tok_spec(), _tok_spec(SMALL),
                  pl.BlockSpec((None, 1, D), lambda j: (_mod_row(j), 0, 5)),
                  _const_spec((1, D)), _mod_spec(0), _mod_spec(1),
                  _const_spec((D, SMALL)), _const_spec((D, SMALL))],
        out_specs=[_tok_spec(), _tok_spec(), _tok_spec(SMALL)],
        out_shape=[jax.ShapeDtypeStruct((N_TOK, D), F32), jax.ShapeDtypeStruct((N_TOK, D), BF16),
                   jax.ShapeDtypeStruct((N_TOK, SMALL), F32)],
        compiler_params=_cparams(1),
        name="residual_norm_mod",
    )(x, ya, yb, route, mods3_prev, g, mods3, mods3, ws_hi, ws_lo)


def _final_kernel(x_ref, ya_ref, yb_ref, rt_ref, gate_ref, g_ref, o_ref):
    o_ref[...] = _rms(_moe_residual(x_ref, ya_ref, yb_ref, rt_ref, gate_ref)) * g_ref[...]


def _final_call(x, ya, yb, route, mods3, g_final):
    lat = lambda b, s: (b * SEQ_BLOCKS + 1 + s, 0)
    spec = pl.BlockSpec((TB, D), lat)
    return pl.pallas_call(
        _final_kernel,
        grid=(BATCH, SEQ // TB),
        in_specs=[spec, spec, spec, pl.BlockSpec((TB, SMALL), lat),
                  pl.BlockSpec((None, 1, D), lambda b, s: (b, 0, 5)),
                  pl.BlockSpec((1, D), lambda b, s: (0, 0))],
        out_specs=pl.BlockSpec((TB, D), lambda b, s: (b * (SEQ // TB) + s, 0)),
        out_shape=jax.ShapeDtypeStruct((BATCH * SEQ, D), F32),
        compiler_params=_cparams(2),
        name="final_norm",
    )(x, ya, yb, route, mods3, g_final)


def _mm_kernel(h_ref, w_ref, o_ref):
    o_ref[...] = _dot(h_ref[...], w_ref[...]).astype(o_ref.dtype)


def _mm_call(h, w, name):
    n = w.shape[1]
    tm, tn = N_TOK // 8, 1024
    return pl.pallas_call(
        _mm_kernel,
        grid=(n // tn, N_TOK // tm),
        in_specs=[pl.BlockSpec((tm, D), lambda j, i: (i, 0)),
                  pl.BlockSpec((D, tn), lambda j, i: (0, j))],
        out_specs=pl.BlockSpec((tm, tn), lambda j, i: (i, j)),
        out_shape=jax.ShapeDtypeStruct((N_TOK, n), BF16),
        compiler_params=_cparams(2),
        name=name,
    )(h, w)


CONV_HALO = 16


def _conv_kernel(x_ref, w_ref, b_ref, post_ref, o_ref):
    w = w_ref[...]
    bias = b_ref[...]
    post = post_ref[...]
    row = lax.broadcasted_iota(jnp.int32, (TB, 1), 0)
    for k in range(SEQ_BLOCKS):
        r0 = k * TB
        lo = max(r0 - CONV_HALO, 0)
        hi = min(r0 + TB + CONV_HALO, T_BATCH)
        off = r0 - lo
        n = hi - lo
        ext = x_ref[lo:hi, :].astype(F32)
        xm2 = pltpu.roll(ext, 2, 0)[off:off + TB]
        xm1 = pltpu.roll(ext, 1, 0)[off:off + TB]
        x0 = ext[off:off + TB]
        xp1 = pltpu.roll(ext, n - 1, 0)[off:off + TB]
        if k in (0, 1):
            xm2 = jnp.where(row >= 2, xm2, 0.0)
            xm1 = jnp.where(row >= 1, xm1, 0.0)
        if k in (0, SEQ_BLOCKS - 1):
            xp1 = jnp.where(row <= TB - 2, xp1, 0.0)
        y = bias + w[0:1] * xm2 + w[1:2] * xm1 + w[2:3] * x0 + w[3:4] * xp1
        o_ref[r0:r0 + TB, :] = (_silu(y) * post).astype(o_ref.dtype)


def _conv_call(u, col0, width, w, b, post, name):
    tc = 512
    cb0 = col0 // tc
    return pl.pallas_call(
        _conv_kernel,
        grid=(BATCH, width // tc),
        in_specs=[pl.BlockSpec((T_BATCH, tc), lambda bi, c: (bi, cb0 + c)),
                  pl.BlockSpec((4, tc), lambda bi, c: (0, c)),
                  pl.BlockSpec((1, tc), lambda bi, c: (0, c)),
                  pl.BlockSpec((1, tc), lambda bi, c: (0, c))],
        out_specs=pl.BlockSpec((T_BATCH, tc), lambda bi, c: (bi, c)),
        out_shape=jax.ShapeDtypeStruct((N_TOK, width), BF16),
        compiler_params=_cparams(2),
        name=name,
    )(u, w, b, post)


def _tri_dot_cols(tri, x):
    a, b, c = _split3(x)
    return _dot(tri, a) + _dot(tri, b) + _dot(tri, c)


def _tri_dot_rows(x, tri):
    a, b, c = _split3(x)
    return _dot(a, tri) + _dot(b, tri) + _dot(c, tri)


def _gate_kernel(pre_ref, pret_ref, bc_ref, br_ref, ac_ref, ar_ref, bwdc_ref, bwdr_ref, tril_ref, triu_ref,
                 valc_ref, cumc_ref, valr_ref, cumr_ref, *, ssd):
    tril = tril_ref[...]
    triu = triu_ref[...]

    def act(v):
        if ssd:
            val = _softplus(v)
            return val, val
        return v, _log_sigmoid(v)

    vc, dc = act(pre_ref[...] + bc_ref[...])
    dc = dc * ac_ref[...]
    valc_ref[...] = vc
    cumc_ref[...] = jnp.where(bwdc_ref[...] > 0.5, _tri_dot_cols(triu, dc), _tri_dot_cols(tril, dc))

    vr, dr = act(pret_ref[...] + br_ref[...])
    dr = dr * ar_ref[...]
    valr_ref[...] = vr
    cumr_ref[...] = jnp.where(bwdr_ref[...] > 0.5, _tri_dot_rows(dr, tril), _tri_dot_rows(dr, triu))


def _gate_call(small, bias, scale, bwd, ssd, name):
    small_t = small.reshape(N_BLOCKS, TB, SMALL)[:, :, :64].transpose(0, 2, 1)
    tri = np.tril(np.ones((TB, TB), np.float32))
    tril = jnp.asarray(tri, BF16)
    triu = jnp.asarray(tri.T, BF16)
    col = lambda v: v.reshape(1, SMALL).astype(F32)
    rowv = lambda v: v[:64].reshape(64, 1).astype(F32)
    cs = lambda shape: pl.BlockSpec(shape, lambda j: (0,) * len(shape))
    return pl.pallas_call(
        functools.partial(_gate_kernel, ssd=ssd),
        grid=(N_BLOCKS,),
        in_specs=[_tok_spec(SMALL), pl.BlockSpec((None, 64, TB), lambda j: (j, 0, 0)),
                  cs((1, SMALL)), cs((64, 1)), cs((1, SMALL)), cs((64, 1)), cs((1, SMALL)), cs((64, 1)),
                  cs((TB, TB)), cs((TB, TB))],
        out_specs=[_tok_spec(SMALL), _tok_spec(SMALL),
                   pl.BlockSpec((None, 64, TB), lambda j: (j, 0, 0)),
                   pl.BlockSpec((None, 64, TB), lambda j: (j, 0, 0))],
        out_shape=[jax.ShapeDtypeStruct((N_TOK, SMALL), F32), jax.ShapeDtypeStruct((N_TOK, SMALL), F32),
                   jax.ShapeDtypeStruct((N_BLOCKS, 64, TB), F32), jax.ShapeDtypeStruct((N_BLOCKS, 64, TB), F32)],
        compiler_params=_cparams(1),
        name=name,
    )(small, small_t, col(bias), rowv(bias), col(scale), rowv(scale), col(bwd), rowv(bwd), tril, triu)


def _scan_block(d):
    if d == 0:
        return lambda b, s: b * SEQ_BLOCKS + s
    return lambda b, s: b * SEQ_BLOCKS + jnp.where(s == 0, 0, SEQ_BLOCKS - s)


def _causal_mask(d):
    t = lax.broadcasted_iota(jnp.int32, (TB, TB), 0)
    s = lax.broadcasted_iota(jnp.int32, (TB, TB), 1)
    return (s <= t) if d == 0 else (s >= t)


SUB = 128


def _mlstm_kernel(q_ref, k_ref, v_ref, valc_ref, cumc_ref, valr_ref, cumr_ref, *rest, d):
    prev_ref = rest[0] if d == 1 else None
    o_ref, c_ref, n_ref, m_ref = rest[-4:]
    @pl.when(pl.program_id(1) == 0)
    def _():
        c_ref[...] = jnp.zeros_like(c_ref)
        n_ref[...] = jnp.zeros_like(n_ref)
        m_ref[...] = jnp.zeros_like(m_ref)

    t_i = lax.broadcasted_iota(jnp.int32, (SUB, SUB), 0)
    s_i = lax.broadcasted_iota(jnp.int32, (SUB, SUB), 1)
    mask = (s_i <= t_i) if d == 0 else (s_i >= t_i)
    bcast = lambda col: jnp.broadcast_to(col, (SUB, SUB))
    rep = lambda f: jnp.concatenate([f] * (DH_M // SUB), axis=1)
    order = (0, 1) if d == 0 else (1, 0)

    for h in range(HEADS_M):
        p = d * HEADS_M + h
        cs = slice(h * DH_M, (h + 1) * DH_M)
        base = jnp.zeros((1, 1), F32)
        c_st = c_ref[h]
        n_st = n_ref[h]
        m_st = m_ref[h]
        for ci, c in enumerate(order):
            r0 = c * SUB
            rows = slice(r0, r0 + SUB)
            if ci == 1:
                edge = SUB - 1 if d == 0 else SUB
                base = cumr_ref[8 + p:9 + p, edge:edge + 1]
            last = r0 + (SUB - 1 if d == 0 else 0)
            q = q_ref[rows, cs]
            k = k_ref[rows, cs]
            v = v_ref[rows, cs]
            cum_c = bcast(cumc_ref[rows, 8 + p:9 + p]) - base
            i_c = bcast(valc_ref[rows, p:p + 1])
            cum_r = cumr_ref[8 + p:9 + p, rows] - base
            i_r = valr_ref[p:p + 1, rows]
            total = cumr_ref[8 + p:9 + p, last:last + 1] - base

            dmat = jnp.where(mask, cum_c - cum_r + i_r, -jnp.inf)
            m_loc = jnp.max(dmat, axis=1, keepdims=True)
            a = jnp.exp(dmat - m_loc) * lax.dot_general(q, k, (((1,), (1,)), ((), ())),
                                                        preferred_element_type=F32)
            num_loc = _dot(a.astype(BF16), v)
            den_loc = jnp.sum(a, axis=1, keepdims=True)
            wl_log = total - cum_c + i_c
            m_kv = jnp.max(wl_log, axis=0, keepdims=True)[:, 0:1]
            wl = rep(jnp.exp(wl_log - m_kv))
            kv_loc = lax.dot_general(k, v * wl.astype(BF16), (((0,), (0,)), ((), ())),
                                     preferred_element_type=F32)
            kn_loc = jnp.sum(wl * k.astype(F32), axis=0, keepdims=True)

            g = cum_c + m_st
            m_t = jnp.maximum(g, m_loc)
            f_loc = jnp.exp(m_loc - m_t)
            f_st = jnp.exp(g - m_t)
            num = rep(f_loc) * num_loc + rep(f_st) * _dot(q, c_st.astype(BF16))
            den = f_loc * den_loc + f_st * jnp.sum(q.astype(F32) * n_st, axis=1, keepdims=True)
            inv = 1.0 / jnp.maximum(jnp.abs(den), jnp.exp(-m_t))
            h_out = num * rep(inv)
            if prev_ref is not None:
                h_out = h_out + prev_ref[rows, cs].astype(F32)
            o_ref[rows, cs] = h_out.astype(o_ref.dtype)

            m_new = jnp.maximum(total + m_st, m_kv)
            keep = jnp.exp(total + m_st - m_new)
            take = jnp.exp(m_kv - m_new)
            c_st = keep * c_st + take * kv_loc
            n_st = keep * n_st + take * kn_loc
            m_st = m_new
        c_ref[h] = c_st
        n_ref[h] = n_st
        m_ref[h] = m_st


def _mlstm_call(qk, ucm, valc, cumc, valr, cumr, d, prev=None):
    blk = _scan_block(d)
    tok = lambda col: pl.BlockSpec((TB, D), lambda b, s: (blk(b, s), col))
    small = pl.BlockSpec((TB, SMALL), lambda b, s: (blk(b, s), 0))
    rows = pl.BlockSpec((None, 64, TB), lambda b, s: (blk(b, s), 0, 0))
    extra = () if d == 0 else (prev,)
    return pl.pallas_call(
        functools.partial(_mlstm_kernel, d=d),
        grid=(BATCH, SEQ_BLOCKS),
        in_specs=[tok(0), tok(1), tok(2), small, small, rows, rows] + [tok(0)] * len(extra),
        out_specs=tok(0),
        out_shape=jax.ShapeDtypeStruct((N_TOK, D), BF16),
        scratch_shapes=[pltpu.VMEM((HEADS_M, DH_M, DH_M), F32), pltpu.VMEM((HEADS_M, 1, DH_M), F32),
                        pltpu.VMEM((HEADS_M, 1, 1), F32)],
        compiler_params=_cparams(2),
        name="mlstm_fwd" if d == 0 else "mlstm_bwd",
    )(qk, qk, ucm, valc, cumc, valr, cumr, *extra)


def _ssd_kernel(x_ref, b_ref, c_ref, acsc_ref, dtr_ref, acsr_ref, *rest, d):
    prev_ref = rest[0] if d == 1 else None
    o_ref, s_ref = rest[-2:]
    @pl.when(pl.program_id(1) == 0)
    def _():
        s_ref[...] = jnp.zeros_like(s_ref)

    t_i = lax.broadcasted_iota(jnp.int32, (SUB, SUB), 0)
    s_i = lax.broadcasted_iota(jnp.int32, (SUB, SUB), 1)
    mask = (s_i <= t_i) if d == 0 else (s_i >= t_i)
    lo = lax.broadcasted_iota(jnp.int32, (SUB, 2 * SSD_P), 1) < SSD_P
    lo1 = lax.broadcasted_iota(jnp.int32, (1, 2 * SSD_P), 1) < SSD_P
    zero_b = jnp.zeros((SUB, 2 * SSD_P), BF16)
    bcast = lambda col: jnp.broadcast_to(col, (SUB, SUB))

    base_row = jnp.zeros((1, SMALL), F32)
    for ci, c in enumerate((0, 1) if d == 0 else (1, 0)):
        r0 = c * SUB
        rows = slice(r0, r0 + SUB)
        if ci == 1:
            edge = SUB - 1 if d == 0 else SUB
            base_row = acsc_ref[edge:edge + 1, :]
        last = r0 + (SUB - 1 if d == 0 else 0)
        acs_sub = acsc_ref[rows, :]
        tot_row = acsc_ref[last:last + 1, :] - base_row
        cd_row = jnp.exp(tot_row)
        from_start = jnp.exp(acs_sub - base_row)
        for g in range(SSD_GROUPS):
            bg = b_ref[rows, g * SSD_N:(g + 1) * SSD_N]
            cg = c_ref[rows, g * SSD_N:(g + 1) * SSD_N]
            cb = lax.dot_general(cg, bg, (((1,), (1,)), ((), ())), preferred_element_type=F32)
            bg_t = jnp.transpose(bg.astype(F32)).astype(BF16)
            for j in range(2):
                tile = slice(g * 4 * SSD_P + j * 2 * SSD_P, g * 4 * SSD_P + (j + 1) * 2 * SSD_P)
                stile = slice(j * 2 * SSD_P, (j + 1) * 2 * SSD_P)
                ps = [SSD_HEADS + d * SSD_HEADS + g * 4 + j * 2 + k for k in (0, 1)]
                xt = x_ref[rows, tile]
                st = s_ref[g, :, stile]
                st_b = st.astype(BF16)
                intra, inter, to_state = [], [], []
                for p in ps:
                    cum_col = bcast(acs_sub[:, p:p + 1])
                    cum_row = acsr_ref[p:p + 1, rows]
                    dt_row = dtr_ref[p:p + 1, rows]
                    seg = jnp.where(mask, cum_col - (cum_row - jnp.log(dt_row)), -jnp.inf)
                    intra.append((cb * jnp.exp(seg)).astype(BF16))
                    inter.append(cg * bcast(from_start[:, p:p + 1]).astype(BF16))
                    w_row = dt_row * jnp.exp(tot_row[:, p:p + 1] - (cum_row - base_row[:, p:p + 1]))
                    to_state.append(bg_t * w_row.astype(BF16))
                x_pair = [jnp.where(lo, xt, zero_b), jnp.where(lo, zero_b, xt)]
                s_pair = [jnp.where(lo, st_b, zero_b), jnp.where(lo, zero_b, st_b)]
                lhs = jnp.concatenate(intra + inter, axis=1)
                y_out = _dot(lhs, jnp.concatenate(x_pair + s_pair, axis=0))
                if prev_ref is not None:
                    y_out = y_out + prev_ref[rows, tile].astype(F32)
                o_ref[rows, tile] = y_out.astype(o_ref.dtype)
                cd_sel = jnp.where(lo1, cd_row[:, ps[0]:ps[0] + 1], cd_row[:, ps[1]:ps[1] + 1])
                s_ref[g, :, stile] = cd_sel * st + _dot(jnp.concatenate(to_state, axis=1),
                                                        jnp.concatenate(x_pair, axis=0))


def _ssd_call(xbc, acsc, dtr, acsr, d, prev=None):
    blk = _scan_block(d)
    small = pl.BlockSpec((TB, SMALL), lambda b, s: (blk(b, s), 0))
    rows = pl.BlockSpec((None, 64, TB), lambda b, s: (blk(b, s), 0, 0))
    tok = pl.BlockSpec((TB, D), lambda b, s: (blk(b, s), 0))
    extra = () if d == 0 else (prev,)
    return pl.pallas_call(
        functools.partial(_ssd_kernel, d=d),
        grid=(BATCH, SEQ_BLOCKS),
        in_specs=[tok,
                  pl.BlockSpec((TB, 512), lambda b, s: (blk(b, s), 2)),
                  pl.BlockSpec((TB, 512), lambda b, s: (blk(b, s), 3)),
                  small, rows, rows] + [tok] * len(extra),
        out_specs=pl.BlockSpec((TB, D), lambda b, s: (blk(b, s), 0)),
        out_shape=jax.ShapeDtypeStruct((N_TOK, D), BF16),
        scratch_shapes=[pltpu.VMEM((SSD_GROUPS, SSD_N, 4 * SSD_P), F32)],
        compiler_params=_cparams(2),
        name="ssd_fwd" if d == 0 else "ssd_bwd",
    )(xbc, xbc, xbc, acsc, dtr, acsr, *extra)


def _pool_tables():
    masks = np.zeros((2, 4, TB, TB), np.float32)
    inv = np.zeros((2, 4, TB, 1), np.float32)
    for kind, length in ((0, GRID_W), (1, CTX)):
        for gi, w in enumerate(POOL_WINDOWS):
            for r in range(TB):
                base, c = (r // length) * length, r % length
                lo = min(max(c - w // 2, 0), length - 1)
                hi = min(max(c - w // 2 + w - 1, 0), length - 1)
                masks[kind, gi, r, base + lo:base + hi + 1] = 1.0
                inv[kind, gi, r, 0] = 1.0 / (hi - lo + 1)
    return jnp.asarray(masks, BF16), jnp.asarray(inv, F32)


ROUTE_FIELDS = 6


def _route_block(logits, ltri_ref, rt_ref, cnt_ref, cnt_scr):
    @pl.when(pl.program_id(0) == 0)
    def _():
        cnt_scr[...] = jnp.zeros_like(cnt_scr)

    lane = lax.broadcasted_iota(jnp.int32, logits.shape, 1).astype(F32)

    def top1(valid):
        v = jnp.max(jnp.where(valid, logits, -jnp.inf), axis=1, keepdims=True)
        i = jnp.min(jnp.where(valid & (logits == v), lane, float(SMALL)), axis=1, keepdims=True)
        return v, i

    is_g = lane < N_GROUPS_E
    gm, grp = top1(is_g)
    p_grp = 1.0 / jnp.sum(jnp.where(is_g, jnp.exp(logits - gm), 0.0), axis=1, keepdims=True)
    lo_lane = N_GROUPS_E + grp * EPG
    in_grp = (lane >= lo_lane) & (lane < lo_lane + EPG)
    v1, i1 = top1(in_grp)
    v2, i2 = top1(in_grp & (lane != i1))
    t = jnp.exp(v2 - v1)
    w1 = p_grp / (1.0 + t)
    w2 = p_grp * t / (1.0 + t)
    e1 = i1 - N_GROUPS_E
    e2 = i2 - N_GROUPS_E

    oh1 = lane == e1
    oh2 = lane == e2
    oh = jnp.where(oh1 | oh2, 1.0, 0.0)
    before = _dot(ltri_ref[...], oh.astype(BF16)) + cnt_scr[...]
    r1 = jnp.sum(jnp.where(oh1, before, 0.0), axis=1, keepdims=True)
    r2 = jnp.sum(jnp.where(oh2, before, 0.0), axis=1, keepdims=True)
    cnt_scr[...] = cnt_scr[...] + jnp.sum(oh, axis=0, keepdims=True)
    cnt_ref[...] = cnt_scr[...]
    rt = jnp.zeros(logits.shape, F32)
    for k, val in enumerate((e1, e2, w1, w2, r1, r2)):
        rt = jnp.where(lane == k, val, rt)
    rt_ref[...] = rt


def _merge_kernel(x_ref, pa_ref, o_ref, z_ref, mg0_ref, mg1_ref, mg2_ref, hs_ref, ys_ref, xs_ref,
                  pmask_ref, pinv_ref, poolw_ref, pscale_ref, mng_ref, dsk_ref, sng_ref, wbr_ref, wout_ref,
                  gate_ref, sh_ref, sc_ref, gffn_ref, wrh_ref, wrl_ref, brt_ref, ltri_ref,
                  xo_ref, h2_ref, rt_ref, cnt_ref, cnt_scr):
    parts = []
    for g in range(4):
        a_g = pa_ref[:, g * 256:(g + 1) * 256]
        pooled = _dot(pmask_ref[g], a_g) * pinv_ref[g] - a_g.astype(F32)
        parts.append(_dot(pooled.astype(BF16), poolw_ref[g]))
    pool = jnp.concatenate(parts, axis=1) * pscale_ref[...]

    hs = hs_ref[...].astype(F32)
    hn = jnp.concatenate([_rms(hs[:, h * DH_M:(h + 1) * DH_M]) for h in range(HEADS_M)], axis=1)
    ml = _sigmoid(o_ref[...].astype(F32)) * (hn * mng_ref[...])

    y = ys_ref[...].astype(F32) + dsk_ref[...] * xs_ref[...].astype(F32)
    sl = _rms(y * _silu(z_ref[...].astype(F32))) * sng_ref[...]

    acc = _sigmoid(mg0_ref[...].astype(F32)) * _dot(pool.astype(BF16), wbr_ref[0])
    acc = acc + _sigmoid(mg1_ref[...].astype(F32)) * _dot(ml.astype(BF16), wbr_ref[1])
    acc = acc + _sigmoid(mg2_ref[...].astype(F32)) * _dot(sl.astype(BF16), wbr_ref[2])
    xn = x_ref[...] + gate_ref[...] * _dot(acc.astype(BF16), wout_ref[...])
    xo_ref[...] = xn

    h2 = (_rms(xn) * gffn_ref[...]) * (1.0 + sc_ref[...]) + sh_ref[...]
    hi, lo = _split2(h2)
    h2f = hi.astype(F32)
    for cchunk in range(D // 128):
        h2_ref[pl.ds(cchunk, TB, stride=D // 128), :] = h2f[:, cchunk * 128:(cchunk + 1) * 128]
    logits = _dot(hi, wrh_ref[...]) + _dot(lo, wrh_ref[...]) + _dot(hi, wrl_ref[...]) + brt_ref[...]
    _route_block(logits, ltri_ref, rt_ref, cnt_ref, cnt_scr)


def _merge_call(x, urm, hs, ys, xbc, pmask, pinv, poolw, pscale, mng, dsk, sng, wbr, wout,
                mods3, gffn, wr_hi, wr_lo, br):
    ltri = jnp.asarray(np.tril(np.ones((TB, TB), np.float32), -1), BF16)
    kind = lambda j: jnp.where(j % SEQ_BLOCKS == 0, 1, 0)
    vec = _const_spec((1, D))
    return pl.pallas_call(
        _merge_kernel,
        grid=(N_BLOCKS,),
        in_specs=[_tok_spec(),
                  _tok_spec(D, 0), _tok_spec(D, 1), _tok_spec(D, 4),
                  _tok_spec(D, 5), _tok_spec(D, 6), _tok_spec(D, 7),
                  _tok_spec(), _tok_spec(),
                  _tok_spec(D, 0),
                  pl.BlockSpec((None, 4, TB, TB), lambda j: (kind(j), 0, 0, 0)),
                  pl.BlockSpec((None, 4, TB, 1), lambda j: (kind(j), 0, 0, 0)),
                  _const_spec((4, 256, 256)), vec, vec, vec, vec,
                  _const_spec((3, D, D)), _const_spec((D, D)),
                  _mod_spec(2), _mod_spec(3), _mod_spec(4), vec,
                  _const_spec((D, SMALL)), _const_spec((D, SMALL)), _const_spec((1, SMALL)),
                  _const_spec((TB, TB))],
        out_specs=[_tok_spec(), pl.BlockSpec((TB * ROW_TILES, 128), lambda j: (j, 0)), _tok_spec(SMALL),
                   _const_spec((1, SMALL))],
        out_shape=[jax.ShapeDtypeStruct((N_TOK, D), F32), jax.ShapeDtypeStruct((N_TOK * ROW_TILES, 128), F32),
                   jax.ShapeDtypeStruct((N_TOK, SMALL), F32), jax.ShapeDtypeStruct((1, SMALL), F32)],
        scratch_shapes=[pltpu.VMEM((1, SMALL), F32)],
        compiler_params=_cparams(1),
        name="branch_merge",
    )(x, urm, urm, urm, urm, urm, urm, hs, ys, xbc, pmask, pinv, poolw, pscale, mng, dsk, sng,
      wbr, wout, mods3, mods3, mods3, gffn, wr_hi, wr_lo, br, ltri)


GATHER_AHEAD = 2
GATHER_SLOTS = GATHER_AHEAD + 1
GATHER_DMA_QUEUES = 2


def _moe_kernel(be_ref, nu_ref, tok0_ref, tok1_ref, tokn_ref, h_hbm, wg_ref, wu_ref, wd_ref, o_ref,
                xbuf, wgb, wub, wdb, sem):
    i = pl.program_id(0)
    n_used = nu_ref[0]
    slot = i % GATHER_SLOTS

    def start_gather(tok_ref, s):
        for r in range(MOE_BLOCK):
            row0 = pl.multiple_of(tok_ref[0, r] * ROW_TILES, ROW_TILES)
            pltpu.make_async_copy(h_hbm.at[pl.ds(row0, ROW_TILES)],
                                  xbuf.at[s, pl.ds(r * ROW_TILES, ROW_TILES)],
                                  sem.at[s]).start(priority=r % GATHER_DMA_QUEUES)

    def wait_gather(s):
        pltpu.make_async_copy(xbuf.at[s], xbuf.at[s], sem.at[s]).wait()

    @pl.when((i == 0) & (n_used > 0))
    def _():
        start_gather(tok0_ref, 0)
        start_gather(tok1_ref, 1)

    @pl.when((i < n_used) & ((i == 0) | (be_ref[i] != be_ref[jnp.maximum(i - 1, 0)])))
    def _():
        wgb[...] = wg_ref[...].astype(BF16)
        wub[...] = wu_ref[...].astype(BF16)
        wdb[...] = wd_ref[...].astype(BF16)

    @pl.when(i < n_used)
    def _():
        wait_gather(slot)
        x = jnp.concatenate([xbuf[slot, pl.ds(c, MOE_BLOCK, stride=ROW_TILES), :] for c in range(ROW_TILES)],
                            axis=1).astype(BF16)
        start_gather(tokn_ref, (i + GATHER_AHEAD) % GATHER_SLOTS)
        gt = _dot(x, wgb[...])
        up = _dot(x, wub[...])
        act = (_silu(gt) * up).astype(BF16)
        o_ref[...] = _dot(act, wdb[...]).astype(o_ref.dtype)

    @pl.when(i == n_used - 1)
    def _():
        for ahead in range(1, GATHER_AHEAD + 1):
            wait_gather((i + ahead) % GATHER_SLOTS)

    @pl.when(i >= n_used)
    def _():
        o_ref[...] = jnp.zeros_like(o_ref)


MOE_CAP = N_TOK * 2 + N_EXPERTS * MOE_BLOCK
MOE_NBLOCKS = MOE_CAP // MOE_BLOCK


def _moe_call(layer, block_expert, n_used, buf_tok, h2, w_gate, w_up, w_down):
    grid_spec = pltpu.PrefetchScalarGridSpec(
        num_scalar_prefetch=2,
        grid=(MOE_NBLOCKS,),
        in_specs=[pl.BlockSpec((None, 1, MOE_BLOCK), lambda i, be, nu: (0, 0, 0), memory_space=pltpu.SMEM),
                  pl.BlockSpec((None, 1, MOE_BLOCK), lambda i, be, nu: (1, 0, 0), memory_space=pltpu.SMEM),
                  pl.BlockSpec((None, 1, MOE_BLOCK),
                               lambda i, be, nu: (jnp.minimum(i + GATHER_AHEAD, MOE_NBLOCKS - 1), 0, 0),
                               memory_space=pltpu.SMEM),
                  pl.BlockSpec(memory_space=pl.ANY),
                  pl.BlockSpec((None, None, D, D_EXPERT), lambda i, be, nu: (layer, be[i], 0, 0)),
                  pl.BlockSpec((None, None, D, D_EXPERT), lambda i, be, nu: (layer, be[i], 0, 0)),
                  pl.BlockSpec((None, None, D_EXPERT, D), lambda i, be, nu: (layer, be[i], 0, 0))],
        out_specs=pl.BlockSpec((MOE_BLOCK, D), lambda i, be, nu: (i, 0)),
        scratch_shapes=[pltpu.VMEM((GATHER_SLOTS, MOE_BLOCK * ROW_TILES, 128), F32),
                        pltpu.VMEM((D, D_EXPERT), BF16), pltpu.VMEM((D, D_EXPERT), BF16),
                        pltpu.VMEM((D_EXPERT, D), BF16), pltpu.SemaphoreType.DMA((GATHER_SLOTS,))],
    )
    return pl.pallas_call(
        _moe_kernel,
        grid_spec=grid_spec,
        out_shape=jax.ShapeDtypeStruct((MOE_CAP, D), BF16),
        compiler_params=_cparams(1),
        name="moe_experts",
    )(block_expert, n_used, buf_tok, buf_tok, buf_tok, h2, w_gate, w_up, w_down)


def _dispatch_plan(route, counts):
    cnt = counts[0, :N_EXPERTS].astype(jnp.int32)
    padded = (cnt + MOE_BLOCK - 1) // MOE_BLOCK * MOE_BLOCK
    pends = jnp.cumsum(padded)
    pstarts = pends - padded
    expert = route[:, 0:2].astype(jnp.int32)
    rank = route[:, 4:6].astype(jnp.int32)
    onehot = expert[:, :, None] == jnp.arange(N_EXPERTS, dtype=jnp.int32)
    pos = jnp.sum(jnp.where(onehot, pstarts, 0), axis=-1) + rank
    tok = jnp.broadcast_to(jnp.arange(N_TOK, dtype=jnp.int32)[:, None], (N_TOK, 2))
    buf_tok = jnp.zeros((MOE_CAP,), jnp.int32).at[pos.reshape(-1)].set(tok.reshape(-1))
    buf_tok = buf_tok.reshape(MOE_NBLOCKS, 1, MOE_BLOCK)
    block_start = jnp.arange(MOE_NBLOCKS, dtype=jnp.int32) * MOE_BLOCK
    block_expert = jnp.minimum(jnp.sum((pends[None, :] <= block_start[:, None]).astype(jnp.int32), axis=1),
                               N_EXPERTS - 1)
    n_used = (pends[-1] // MOE_BLOCK).reshape(1)
    return buf_tok, block_expert, n_used, pos


def _to_colmajor(t):
    c = t.shape[-1]
    t = t.reshape(BATCH, T_BATCH, c)
    lat = t[:, CTX:].reshape(BATCH, SEQ // GRID_W, GRID_W, c).swapaxes(1, 2).reshape(BATCH, SEQ, c)
    return jnp.concatenate([t[:, :CTX], lat], axis=1).reshape(N_TOK, c)


def _to_rowmajor(t):
    c = t.shape[-1]
    t = t.reshape(BATCH, T_BATCH, c)
    lat = t[:, CTX:].reshape(BATCH, GRID_W, SEQ // GRID_W, c).swapaxes(1, 2).reshape(BATCH, SEQ, c)
    return jnp.concatenate([t[:, :CTX], lat], axis=1).reshape(N_TOK, c)


def _lanes(*pieces):
    v = jnp.concatenate([jnp.asarray(p, F32).reshape(-1) for p in pieces])
    return jnp.pad(v, (0, SMALL - v.shape[0]))


def kernel(x, c, ctx, c_ctx, w_ada, b_ada, g_norm_mix, g_norm_ffn, w_in, pool_w, pool_scale, mlstm_conv_w,
           mlstm_conv_b, mlstm_gate_b, mlstm_norm_g, ssd_conv_w, ssd_conv_b, ssd_dt_bias, ssd_a_log, ssd_d,
           ssd_norm_g, w_branch, w_out, w_route_group, b_route_group, w_route_expert, b_route_expert,
           w_exp_gate, w_exp_up, w_exp_down, g_final):
    cvec = jnp.concatenate([c, c_ctx[None], jnp.zeros((3, D), F32)], axis=0)
    mods = _ada_table(cvec, w_ada, b_ada)
    xs = jnp.concatenate([ctx, x], axis=1).reshape(N_TOK, D)
    pmask, pinv = _pool_tables()
    row = lambda v: v.reshape(1, -1).astype(F32)
    w_in_bf = w_in.astype(BF16)

    h = small = None
    out = None
    for l in range(DEPTH):
        mods3 = mods[l].reshape(8, 1, 6 * D)
        wi = w_in_bf[l]
        sp = np.cumsum([0, D, D, D, D, D, 16, D, D, 512, 512, 32, 3 * D])
        piece = lambda i: wi[:, sp[i]:sp[i + 1]]
        w_rm = jnp.concatenate([piece(0), piece(4), piece(6), piece(8), piece(9), piece(7), piece(11)],
                               axis=1)
        w_cm = wi[:, sp[1]:sp[4]]
        w_small = jnp.pad(jnp.concatenate([w_in[l][:, sp[5]:sp[6]], w_in[l][:, sp[10]:sp[11]]], axis=1),
                          ((0, 0), (0, SMALL - 48)))
        ws_hi = w_small.astype(BF16)
        ws_lo = (w_small - ws_hi.astype(F32)).astype(BF16)

        if l == 0:
            h, small = _norm_call(xs, row(g_norm_mix[l]), mods3, ws_hi, ws_lo)

        h_cm = _to_colmajor(h)
        small_cm = _to_colmajor(small)
        u_rm = _mm_call(h, w_rm, "in_proj_rowmajor")
        u_cm = _mm_call(h_cm, w_cm, "in_proj_colmajor")

        post_m = jnp.concatenate([jnp.ones((D,), F32), jnp.full((D,), DH_M ** -0.5, F32)]).reshape(1, 2 * D)
        qk = _conv_call(u_cm, 0, 2 * D, mlstm_conv_w[l], row(mlstm_conv_b[l]), post_m, "mlstm_conv")
        gate_bias = _lanes(mlstm_gate_b[l])
        gate_bwd = _lanes(jnp.zeros((12,)), jnp.ones((4,)))
        g_valc, g_cumc, g_valr, g_cumr = _gate_call(small_cm, gate_bias, jnp.ones((SMALL,), F32), gate_bwd,
                                                    False, "mlstm_gates")
        h_f = _mlstm_call(qk, u_cm, g_valc, g_cumc, g_valr, g_cumr, 0)
        h_fb = _mlstm_call(qk, u_cm, g_valc, g_cumc, g_valr, g_cumr, 1, prev=h_f)
        h_fb = _to_rowmajor(h_fb)

        xbc = _conv_call(u_rm, 2 * D, 2 * D, ssd_conv_w[l], row(ssd_conv_b[l]), jnp.ones((1, 2 * D), F32),
                         "ssd_conv")
        a_neg = -jnp.exp(ssd_a_log[l].astype(F32))
        dt_bias = _lanes(jnp.zeros((16,)), ssd_dt_bias[l])
        dt_scale = _lanes(jnp.zeros((16,)), a_neg)
        dt_bwd = _lanes(jnp.zeros((32,)), jnp.ones((16,)))
        _, s_acsc, s_dtr, s_acsr = _gate_call(small, dt_bias, dt_scale, dt_bwd, True, "ssd_gates")
        y_f = _ssd_call(xbc, s_acsc, s_dtr, s_acsr, 0)
        y_fb = _ssd_call(xbc, s_acsc, s_dtr, s_acsr, 1, prev=y_f)

        w_r = jnp.pad(jnp.concatenate([w_route_group[l], w_route_expert[l]], axis=1),
                      ((0, 0), (0, SMALL - N_GROUPS_E - N_EXPERTS)))
        wr_hi = w_r.astype(BF16)
        wr_lo = (w_r - wr_hi.astype(F32)).astype(BF16)
        b_r = _lanes(b_route_group[l], b_route_expert[l]).reshape(1, SMALL)
        dsk = jnp.repeat(ssd_d[l].astype(F32), SSD_P).reshape(1, D)
        xs, h2, route, counts = _merge_call(
            xs, u_rm, h_fb, y_fb, xbc, pmask, pinv, pool_w[l].astype(BF16), row(pool_scale[l]),
            row(mlstm_norm_g[l]), dsk, row(ssd_norm_g[l]), w_branch[l].astype(BF16), w_out[l].astype(BF16),
            mods3, row(g_norm_ffn[l]), wr_hi, wr_lo, b_r)

        buf_tok, block_expert, n_used, pos = _dispatch_plan(route, counts)
        yb = _moe_call(l, block_expert, n_used, buf_tok, h2, w_exp_gate, w_exp_up, w_exp_down)
        y0 = yb[pos[:, 0]]
        y1 = yb[pos[:, 1]]

        if l + 1 < DEPTH:
            wi_n = w_in[l + 1]
            w_small_n = jnp.pad(jnp.concatenate([wi_n[:, sp[5]:sp[6]], wi_n[:, sp[10]:sp[11]]], axis=1),
                                ((0, 0), (0, SMALL - 48)))
            wsn_hi = w_small_n.astype(BF16)
            wsn_lo = (w_small_n - wsn_hi.astype(F32)).astype(BF16)
            mods3_n = mods[l + 1].reshape(8, 1, 6 * D)
            xs, h, small = _resnorm_call(xs, y0, y1, route, mods3, row(g_norm_mix[l + 1]), mods3_n,
                                         wsn_hi, wsn_lo)
        else:
            out = _final_call(xs, y0, y1, route, mods3, row(g_final))
    return out.reshape(BATCH, SEQ, D)
```

```python
import functools
import math

import jax
import jax.numpy as jnp
import numpy as np
from jax import lax
from jax.experimental import pallas as pl
from jax.experimental.pallas import tpu as pltpu

F32 = jnp.float32
BF16 = jnp.bfloat16

D = 1024
BATCH = 4
SEQ = 4096
CTX = 256
DEPTH = 2
GRID_W = 64
EPS = 1e-6

TB = 256
SEQ_BLOCKS = (SEQ + CTX) // TB
T_BATCH = SEQ + CTX
N_TOK = BATCH * T_BATCH
N_BLOCKS = N_TOK // TB

POOL_WINDOWS = (2, 4, 8, 16)
HEADS_M = 4
DH_M = 256
SSD_HEADS = 16
SSD_P = 64
SSD_GROUPS = 4
SSD_N = 128
N_GROUPS_E = 4
EPG = 8
N_EXPERTS = 32
D_EXPERT = 512
MOE_BLOCK = 256
SMALL = 128
ROW_TILES = D // 128

VMEM_LIMIT = 56 * 1024 * 1024


def _cparams(n_axes):
    return pltpu.CompilerParams(dimension_semantics=("arbitrary",) * n_axes,
                                vmem_limit_bytes=VMEM_LIMIT)


def _dot(a, b):
    return jnp.dot(a, b, preferred_element_type=F32)


def _split2(x):
    hi = x.astype(BF16)
    lo = (x - hi.astype(F32)).astype(BF16)
    return hi, lo


def _split3(x):
    hi = x.astype(BF16)
    r = x - hi.astype(F32)
    mid = r.astype(BF16)
    lo = (r - mid.astype(F32)).astype(BF16)
    return hi, mid, lo


def _sigmoid(x):
    return 1.0 / (1.0 + jnp.exp(-x))


def _silu(x):
    return x * _sigmoid(x)


def _log1pexp_negabs(x):
    return jnp.log(1.0 + jnp.exp(-jnp.abs(x)))


def _log_sigmoid(x):
    return jnp.minimum(x, 0.0) - _log1pexp_negabs(x)


def _softplus(x):
    return jnp.maximum(x, 0.0) + _log1pexp_negabs(x)


def _rms(x):
    return x * lax.rsqrt(jnp.mean(x * x, axis=-1, keepdims=True) + EPS)


def _mod_row(j):
    return jnp.where(j % SEQ_BLOCKS == 0, BATCH, j // SEQ_BLOCKS)


def _ada_kernel(c_ref, w_ref, b_ref, o_ref):
    c = c_ref[...]
    s_hi, s_lo = _split2(_silu(c))
    w_hi, w_lo = _split2(w_ref[...])
    o_ref[...] = _dot(s_hi, w_hi) + _dot(s_lo, w_hi) + _dot(s_hi, w_lo) + b_ref[...]


def _ada_table(cvec, w_ada, b_ada):
    tn = 1024
    return pl.pallas_call(
        _ada_kernel,
        grid=(DEPTH, 6 * D // tn),
        in_specs=[pl.BlockSpec((8, D), lambda l, j: (0, 0)),
                  pl.BlockSpec((None, D, tn), lambda l, j: (l, 0, j)),
                  pl.BlockSpec((None, 1, tn), lambda l, j: (l, 0, j))],
        out_specs=pl.BlockSpec((None, 8, tn), lambda l, j: (l, 0, j)),
        out_shape=jax.ShapeDtypeStruct((DEPTH, 8, 6 * D), F32),
        compiler_params=_cparams(2),
        name="ada_table",
    )(cvec, w_ada, b_ada.reshape(DEPTH, 1, 6 * D))


def _norm_mod_small(x, g_ref, sh_ref, sc_ref, wsh_ref, wsl_ref, h_ref, small_ref):
    h = _rms(x) * g_ref[...]
    h = h * (1.0 + sc_ref[...]) + sh_ref[...]
    h_hi, h_lo = _split2(h)
    h_ref[...] = h_hi
    small_ref[...] = _dot(h_hi, wsh_ref[...]) + _dot(h_lo, wsh_ref[...]) + _dot(h_hi, wsl_ref[...])


def _norm_kernel(x_ref, g_ref, sh_ref, sc_ref, wsh_ref, wsl_ref, h_ref, small_ref):
    _norm_mod_small(x_ref[...], g_ref, sh_ref, sc_ref, wsh_ref, wsl_ref, h_ref, small_ref)


def _moe_residual(x_ref, ya_ref, yb_ref, rt_ref, gate_ref):
    y = rt_ref[:, 2:3] * ya_ref[...].astype(F32) + rt_ref[:, 3:4] * yb_ref[...].astype(F32)
    return x_ref[...] + gate_ref[...] * y


def _resnorm_kernel(x_ref, ya_ref, yb_ref, rt_ref, gate_ref, g_ref, sh_ref, sc_ref, wsh_ref, wsl_ref,
                    xo_ref, h_ref, small_ref):
    x = _moe_residual(x_ref, ya_ref, yb_ref, rt_ref, gate_ref)
    xo_ref[...] = x
    _norm_mod_small(x, g_ref, sh_ref, sc_ref, wsh_ref, wsl_ref, h_ref, small_ref)


def _mod_spec(chunk):
    return pl.BlockSpec((None, 1, D), lambda j: (_mod_row(j), 0, chunk))


def _tok_spec(width=D, col=0):
    return pl.BlockSpec((TB, width), lambda j: (j, col))


def _const_spec(shape):
    nd = len(shape)
    return pl.BlockSpec(shape, lambda j: (0,) * nd)


def _norm_call(x, g, mods3, ws_hi, ws_lo):
    return pl.pallas_call(
        _norm_kernel,
        grid=(N_BLOCKS,),
        in_specs=[_tok_spec(), _const_spec((1, D)), _mod_spec(0), _mod_spec(1),
                  _const_spec((D, SMALL)), _const_spec((D, SMALL))],
        out_specs=[_tok_spec(), _tok_spec(SMALL)],
        out_shape=[jax.ShapeDtypeStruct((N_TOK, D), BF16), jax.ShapeDtypeStruct((N_TOK, SMALL), F32)],
        compiler_params=_cparams(1),
        name="norm_mod",
    )(x, g, mods3, mods3, ws_hi, ws_lo)


def _resnorm_call(x, ya, yb, route, mods3_prev, g, mods3, ws_hi, ws_lo):
    return pl.pallas_call(
        _resnorm_kernel,
        grid=(N_BLOCKS,),
        in_specs=[_tok_spec(), _tok_spec(), _tok_spec(), _tok_spec(SMALL),
                  pl.BlockSpec((None, 1, D), lambda j: (_mod_row(j), 0, 5)),
                  _const_spec((1, D)), _mod_spec(0), _mod_spec(1),
                  _const_spec((D, SMALL)), _const_spec((D, SMALL))],
        out_specs=[_tok_spec(), _tok_spec(), _tok_spec(SMALL)],
        out_shape=[jax.ShapeDtypeStruct((N_TOK, D), F32), jax.ShapeDtypeStruct((N_TOK, D), BF16),
                   jax.ShapeDtypeStruct((N_TOK, SMALL), F32)],
        compiler_params=_cparams(1),
        name="residual_norm_mod",
    )(x, ya, yb, route, mods3_prev, g, mods3, mods3, ws_hi, ws_lo)


def _final_kernel(x_ref, ya_ref, yb_ref, rt_ref, gate_ref, g_ref, o_ref):
    o_ref[...] = _rms(_moe_residual(x_ref, ya_ref, yb_ref, rt_ref, gate_ref)) * g_ref[...]


def _final_call(x, ya, yb, route, mods3, g_final):
    lat = lambda b, s: (b * SEQ_BLOCKS + 1 + s, 0)
    spec = pl.BlockSpec((TB, D), lat)
    return pl.pallas_call(
        _final_kernel,
        grid=(BATCH, SEQ // TB),
        in_specs=[spec, spec, spec, pl.BlockSpec((TB, SMALL), lat),
                  pl.BlockSpec((None, 1, D), lambda b, s: (b, 0, 5)),
                  pl.BlockSpec((1, D), lambda b, s: (0, 0))],
        out_specs=pl.BlockSpec((TB, D), lambda b, s: (b * (SEQ // TB) + s, 0)),
        out_shape=jax.ShapeDtypeStruct((BATCH * SEQ, D), F32),
        compiler_params=_cparams(2),
        name="final_norm",
    )(x, ya, yb, route, mods3, g_final)


def _mm_kernel(h_ref, w_ref, o_ref):
    o_ref[...] = _dot(h_ref[...], w_ref[...]).astype(o_ref.dtype)


MM_TM, MM_TN = N_TOK // 8, 1024


def _mm_call(h, w, name):
    n = w.shape[1]
    tm, tn = MM_TM, MM_TN
    return pl.pallas_call(
        _mm_kernel,
        grid=(n // tn, N_TOK // tm),
        in_specs=[pl.BlockSpec((tm, D), lambda j, i: (i, 0)),
                  pl.BlockSpec((D, tn), lambda j, i: (0, j))],
        out_specs=pl.BlockSpec((tm, tn), lambda j, i: (i, j)),
        out_shape=jax.ShapeDtypeStruct((N_TOK, n), BF16),
        compiler_params=_cparams(2),
        name=name,
    )(h, w)


def _mm_f32w_kernel(h_ref, w_ref, o_ref, wb_ref):
    @pl.when(pl.program_id(1) == 0)
    def _():
        wb_ref[...] = w_ref[...].astype(BF16)

    o_ref[...] = _dot(h_ref[...], wb_ref[...]).astype(o_ref.dtype)


def _mm_f32w_call(h, w_all, layer, col_block, n_tiles, name):
    tm, tn = MM_TM, MM_TN
    return pl.pallas_call(
        _mm_f32w_kernel,
        grid=(n_tiles, N_TOK // tm),
        in_specs=[pl.BlockSpec((tm, D), lambda j, i: (i, 0)),
                  pl.BlockSpec((None, D, tn), lambda j, i: (layer, 0, col_block(j)))],
        out_specs=pl.BlockSpec((tm, tn), lambda j, i: (i, j)),
        out_shape=jax.ShapeDtypeStruct((N_TOK, n_tiles * tn), BF16),
        scratch_shapes=[pltpu.VMEM((D, tn), BF16)],
        compiler_params=_cparams(2),
        name=name,
    )(h, w_all)


CONV_HALO = 16


def _conv_kernel(x_ref, w_ref, b_ref, post_ref, o_ref):
    w = w_ref[...]
    bias = b_ref[...]
    post = post_ref[...]
    row = lax.broadcasted_iota(jnp.int32, (TB, 1), 0)
    for k in range(SEQ_BLOCKS):
        r0 = k * TB
        lo = max(r0 - CONV_HALO, 0)
        hi = min(r0 + TB + CONV_HALO, T_BATCH)
        off = r0 - lo
        n = hi - lo
        ext = x_ref[lo:hi, :].astype(F32)
        xm2 = pltpu.roll(ext, 2, 0)[off:off + TB]
        xm1 = pltpu.roll(ext, 1, 0)[off:off + TB]
        x0 = ext[off:off + TB]
        xp1 = pltpu.roll(ext, n - 1, 0)[off:off + TB]
        if k in (0, 1):
            xm2 = jnp.where(row >= 2, xm2, 0.0)
            xm1 = jnp.where(row >= 1, xm1, 0.0)
        if k in (0, SEQ_BLOCKS - 1):
            xp1 = jnp.where(row <= TB - 2, xp1, 0.0)
        y = bias + w[0:1] * xm2 + w[1:2] * xm1 + w[2:3] * x0 + w[3:4] * xp1
        o_ref[r0:r0 + TB, :] = (_silu(y) * post).astype(o_ref.dtype)


def _conv_call(u, in_tile, width, w, b, post, name):
    tc = 512
    return pl.pallas_call(
        _conv_kernel,
        grid=(BATCH, width // tc),
        in_specs=[pl.BlockSpec((T_BATCH, tc), lambda bi, c: (bi, in_tile(c))),
                  pl.BlockSpec((4, tc), lambda bi, c: (0, c)),
                  pl.BlockSpec((1, tc), lambda bi, c: (0, c)),
                  pl.BlockSpec((1, tc), lambda bi, c: (0, c))],
        out_specs=pl.BlockSpec((T_BATCH, tc), lambda bi, c: (bi, c)),
        out_shape=jax.ShapeDtypeStruct((N_TOK, width), BF16),
        compiler_params=_cparams(2),
        name=name,
    )(u, w, b, post)


def _tri_dot_cols(tri, x):
    a, b, c = _split3(x)
    return _dot(tri, a) + _dot(tri, b) + _dot(tri, c)


def _tri_dot_rows(x, tri):
    a, b, c = _split3(x)
    return _dot(a, tri) + _dot(b, tri) + _dot(c, tri)


def _gate_kernel(pre_ref, pret_ref, bc_ref, br_ref, ac_ref, ar_ref, bwdc_ref, bwdr_ref, tril_ref, triu_ref,
                 valc_ref, cumc_ref, valr_ref, cumr_ref, *, ssd):
    tril = tril_ref[...]
    triu = triu_ref[...]

    def act(v):
        if ssd:
            val = _softplus(v)
            return val, val
        return v, _log_sigmoid(v)

    vc, dc = act(pre_ref[...] + bc_ref[...])
    dc = dc * ac_ref[...]
    valc_ref[...] = vc
    cumc_ref[...] = jnp.where(bwdc_ref[...] > 0.5, _tri_dot_cols(triu, dc), _tri_dot_cols(tril, dc))

    vr, dr = act(pret_ref[...] + br_ref[...])
    dr = dr * ar_ref[...]
    valr_ref[...] = vr
    cumr_ref[...] = jnp.where(bwdr_ref[...] > 0.5, _tri_dot_rows(dr, tril), _tri_dot_rows(dr, triu))


def _gate_call(small, bias, scale, bwd, ssd, name):
    small_t = small.reshape(N_BLOCKS, TB, SMALL)[:, :, :64].transpose(0, 2, 1)
    tri = np.tril(np.ones((TB, TB), np.float32))
    tril = jnp.asarray(tri, BF16)
    triu = jnp.asarray(tri.T, BF16)
    col = lambda v: v.reshape(1, SMALL).astype(F32)
    rowv = lambda v: v[:64].reshape(64, 1).astype(F32)
    cs = lambda shape: pl.BlockSpec(shape, lambda j: (0,) * len(shape))
    return pl.pallas_call(
        functools.partial(_gate_kernel, ssd=ssd),
        grid=(N_BLOCKS,),
        in_specs=[_tok_spec(SMALL), pl.BlockSpec((None, 64, TB), lambda j: (j, 0, 0)),
                  cs((1, SMALL)), cs((64, 1)), cs((1, SMALL)), cs((64, 1)), cs((1, SMALL)), cs((64, 1)),
                  cs((TB, TB)), cs((TB, TB))],
        out_specs=[_tok_spec(SMALL), _tok_spec(SMALL),
                   pl.BlockSpec((None, 64, TB), lambda j: (j, 0, 0)),
                   pl.BlockSpec((None, 64, TB), lambda j: (j, 0, 0))],
        out_shape=[jax.ShapeDtypeStruct((N_TOK, SMALL), F32), jax.ShapeDtypeStruct((N_TOK, SMALL), F32),
                   jax.ShapeDtypeStruct((N_BLOCKS, 64, TB), F32), jax.ShapeDtypeStruct((N_BLOCKS, 64, TB), F32)],
        compiler_params=_cparams(1),
        name=name,
    )(small, small_t, col(bias), rowv(bias), col(scale), rowv(scale), col(bwd), rowv(bwd), tril, triu)


def _scan_block(d):
    if d == 0:
        return lambda b, s: b * SEQ_BLOCKS + s
    return lambda b, s: b * SEQ_BLOCKS + jnp.where(s == 0, 0, SEQ_BLOCKS - s)


def _causal_mask(d):
    t = lax.broadcasted_iota(jnp.int32, (TB, TB), 0)
    s = lax.broadcasted_iota(jnp.int32, (TB, TB), 1)
    return (s <= t) if d == 0 else (s >= t)


SUB = 128


def _mlstm_kernel(q_ref, k_ref, v_ref, valc_ref, cumc_ref, valr_ref, cumr_ref, *rest, d):
    prev_ref = rest[0] if d == 1 else None
    o_ref, c_ref, n_ref, m_ref = rest[-4:]
    @pl.when(pl.program_id(1) == 0)
    def _():
        c_ref[...] = jnp.zeros_like(c_ref)
        n_ref[...] = jnp.zeros_like(n_ref)
        m_ref[...] = jnp.zeros_like(m_ref)

    t_i = lax.broadcasted_iota(jnp.int32, (SUB, SUB), 0)
    s_i = lax.broadcasted_iota(jnp.int32, (SUB, SUB), 1)
    mask = (s_i <= t_i) if d == 0 else (s_i >= t_i)
    bcast = lambda col: jnp.broadcast_to(col, (SUB, SUB))
    rep = lambda f: jnp.concatenate([f] * (DH_M // SUB), axis=1)
    order = (0, 1) if d == 0 else (1, 0)

    for h in range(HEADS_M):
        p = d * HEADS_M + h
        cs = slice(h * DH_M, (h + 1) * DH_M)
        base = jnp.zeros((1, 1), F32)
        c_st = c_ref[h]
        n_st = n_ref[h]
        m_st = m_ref[h]
        for ci, c in enumerate(order):
            r0 = c * SUB
            rows = slice(r0, r0 + SUB)
            if ci == 1:
                edge = SUB - 1 if d == 0 else SUB
                base = cumr_ref[8 + p:9 + p, edge:edge + 1]
            last = r0 + (SUB - 1 if d == 0 else 0)
            q = q_ref[rows, cs]
            k = k_ref[rows, cs]
            v = v_ref[rows, cs]
            cum_c = bcast(cumc_ref[rows, 8 + p:9 + p]) - base
            i_c = bcast(valc_ref[rows, p:p + 1])
            cum_r = cumr_ref[8 + p:9 + p, rows] - base
            i_r = valr_ref[p:p + 1, rows]
            total = cumr_ref[8 + p:9 + p, last:last + 1] - base

            dmat = jnp.where(mask, cum_c - cum_r + i_r, -jnp.inf)
            m_loc = jnp.max(dmat, axis=1, keepdims=True)
            a = jnp.exp(dmat - m_loc) * lax.dot_general(q, k, (((1,), (1,)), ((), ())),
                                                        preferred_element_type=F32)
            num_loc = _dot(a.astype(BF16), v)
            den_loc = jnp.sum(a, axis=1, keepdims=True)
            wl_log = total - cum_c + i_c
            m_kv = jnp.max(wl_log, axis=0, keepdims=True)[:, 0:1]
            wl = rep(jnp.exp(wl_log - m_kv))
            kv_loc = lax.dot_general(k, v * wl.astype(BF16), (((0,), (0,)), ((), ())),
                                     preferred_element_type=F32)
            kn_loc = jnp.sum(wl * k.astype(F32), axis=0, keepdims=True)

            g = cum_c + m_st
            m_t = jnp.maximum(g, m_loc)
            f_loc = jnp.exp(m_loc - m_t)
            f_st = jnp.exp(g - m_t)
            num = rep(f_loc) * num_loc + rep(f_st) * _dot(q, c_st.astype(BF16))
            den = f_loc * den_loc + f_st * jnp.sum(q.astype(F32) * n_st, axis=1, keepdims=True)
            inv = 1.0 / jnp.maximum(jnp.abs(den), jnp.exp(-m_t))
            h_out = num * rep(inv)
            if prev_ref is not None:
                h_out = h_out + prev_ref[rows, cs].astype(F32)
            o_ref[rows, cs] = h_out.astype(o_ref.dtype)

            m_new = jnp.maximum(total + m_st, m_kv)
            keep = jnp.exp(total + m_st - m_new)
            take = jnp.exp(m_kv - m_new)
            c_st = keep * c_st + take * kv_loc
            n_st = keep * n_st + take * kn_loc
            m_st = m_new
        c_ref[h] = c_st
        n_ref[h] = n_st
        m_ref[h] = m_st


def _mlstm_call(qk, ucm, valc, cumc, valr, cumr, d, prev=None):
    blk = _scan_block(d)
    tok = lambda col: pl.BlockSpec((TB, D), lambda b, s: (blk(b, s), col))
    small = pl.BlockSpec((TB, SMALL), lambda b, s: (blk(b, s), 0))
    rows = pl.BlockSpec((None, 64, TB), lambda b, s: (blk(b, s), 0, 0))
    extra = () if d == 0 else (prev,)
    return pl.pallas_call(
        functools.partial(_mlstm_kernel, d=d),
        grid=(BATCH, SEQ_BLOCKS),
        in_specs=[tok(0), tok(1), tok(2), small, small, rows, rows] + [tok(0)] * len(extra),
        out_specs=tok(0),
        out_shape=jax.ShapeDtypeStruct((N_TOK, D), BF16),
        scratch_shapes=[pltpu.VMEM((HEADS_M, DH_M, DH_M), F32), pltpu.VMEM((HEADS_M, 1, DH_M), F32),
                        pltpu.VMEM((HEADS_M, 1, 1), F32)],
        compiler_params=_cparams(2),
        name="mlstm_fwd" if d == 0 else "mlstm_bwd",
    )(qk, qk, ucm, valc, cumc, valr, cumr, *extra)


def _ssd_kernel(x_ref, b_ref, c_ref, acsc_ref, dtr_ref, acsr_ref, *rest, d):
    prev_ref = rest[0] if d == 1 else None
    o_ref, s_ref = rest[-2:]
    @pl.when(pl.program_id(1) == 0)
    def _():
        s_ref[...] = jnp.zeros_like(s_ref)

    t_i = lax.broadcasted_iota(jnp.int32, (SUB, SUB), 0)
    s_i = lax.broadcasted_iota(jnp.int32, (SUB, SUB), 1)
    mask = (s_i <= t_i) if d == 0 else (s_i >= t_i)
    lo = lax.broadcasted_iota(jnp.int32, (SUB, 2 * SSD_P), 1) < SSD_P
    lo1 = lax.broadcasted_iota(jnp.int32, (1, 2 * SSD_P), 1) < SSD_P
    zero_b = jnp.zeros((SUB, 2 * SSD_P), BF16)
    bcast = lambda col: jnp.broadcast_to(col, (SUB, SUB))

    base_row = jnp.zeros((1, SMALL), F32)
    for ci, c in enumerate((0, 1) if d == 0 else (1, 0)):
        r0 = c * SUB
        rows = slice(r0, r0 + SUB)
        if ci == 1:
            edge = SUB - 1 if d == 0 else SUB
            base_row = acsc_ref[edge:edge + 1, :]
        last = r0 + (SUB - 1 if d == 0 else 0)
        acs_sub = acsc_ref[rows, :]
        tot_row = acsc_ref[last:last + 1, :] - base_row
        cd_row = jnp.exp(tot_row)
        from_start = jnp.exp(acs_sub - base_row)
        for g in range(SSD_GROUPS):
            bg = b_ref[rows, g * SSD_N:(g + 1) * SSD_N]
            cg = c_ref[rows, g * SSD_N:(g + 1) * SSD_N]
            cb = lax.dot_general(cg, bg, (((1,), (1,)), ((), ())), preferred_element_type=F32)
            bg_t = jnp.transpose(bg.astype(F32)).astype(BF16)
            for j in range(2):
                tile = slice(g * 4 * SSD_P + j * 2 * SSD_P, g * 4 * SSD_P + (j + 1) * 2 * SSD_P)
                stile = slice(j * 2 * SSD_P, (j + 1) * 2 * SSD_P)
                ps = [SSD_HEADS + d * SSD_HEADS + g * 4 + j * 2 + k for k in (0, 1)]
                xt = x_ref[rows, tile]
                st = s_ref[g, :, stile]
                st_b = st.astype(BF16)
                intra, inter, to_state = [], [], []
                for p in ps:
                    cum_col = bcast(acs_sub[:, p:p + 1])
                    cum_row = acsr_ref[p:p + 1, rows]
                    dt_row = dtr_ref[p:p + 1, rows]
                    seg = jnp.where(mask, cum_col - (cum_row - jnp.log(dt_row)), -jnp.inf)
                    intra.append((cb * jnp.exp(seg)).astype(BF16))
                    inter.append(cg * bcast(from_start[:, p:p + 1]).astype(BF16))
                    w_row = dt_row * jnp.exp(tot_row[:, p:p + 1] - (cum_row - base_row[:, p:p + 1]))
                    to_state.append(bg_t * w_row.astype(BF16))
                x_pair = [jnp.where(lo, xt, zero_b), jnp.where(lo, zero_b, xt)]
                s_pair = [jnp.where(lo, st_b, zero_b), jnp.where(lo, zero_b, st_b)]
                lhs = jnp.concatenate(intra + inter, axis=1)
                y_out = _dot(lhs, jnp.concatenate(x_pair + s_pair, axis=0))
                if prev_ref is not None:
                    y_out = y_out + prev_ref[rows, tile].astype(F32)
                o_ref[rows, tile] = y_out.astype(o_ref.dtype)
                cd_sel = jnp.where(lo1, cd_row[:, ps[0]:ps[0] + 1], cd_row[:, ps[1]:ps[1] + 1])
                s_ref[g, :, stile] = cd_sel * st + _dot(jnp.concatenate(to_state, axis=1),
                                                        jnp.concatenate(x_pair, axis=0))


def _ssd_call(xbc, acsc, dtr, acsr, d, prev=None):
    blk = _scan_block(d)
    small = pl.BlockSpec((TB, SMALL), lambda b, s: (blk(b, s), 0))
    rows = pl.BlockSpec((None, 64, TB), lambda b, s: (blk(b, s), 0, 0))
    tok = pl.BlockSpec((TB, D), lambda b, s: (blk(b, s), 0))
    extra = () if d == 0 else (prev,)
    return pl.pallas_call(
        functools.partial(_ssd_kernel, d=d),
        grid=(BATCH, SEQ_BLOCKS),
        in_specs=[tok,
                  pl.BlockSpec((TB, 512), lambda b, s: (blk(b, s), 2)),
                  pl.BlockSpec((TB, 512), lambda b, s: (blk(b, s), 3)),
                  small, rows, rows] + [tok] * len(extra),
        out_specs=pl.BlockSpec((TB, D), lambda b, s: (blk(b, s), 0)),
        out_shape=jax.ShapeDtypeStruct((N_TOK, D), BF16),
        scratch_shapes=[pltpu.VMEM((SSD_GROUPS, SSD_N, 4 * SSD_P), F32)],
        compiler_params=_cparams(2),
        name="ssd_fwd" if d == 0 else "ssd_bwd",
    )(xbc, xbc, xbc, acsc, dtr, acsr, *extra)


def _pool_tables():
    masks = np.zeros((2, 4, TB, TB), np.float32)
    inv = np.zeros((2, 4, TB, 1), np.float32)
    for kind, length in ((0, GRID_W), (1, CTX)):
        for gi, w in enumerate(POOL_WINDOWS):
            for r in range(TB):
                base, c = (r // length) * length, r % length
                lo = min(max(c - w // 2, 0), length - 1)
                hi = min(max(c - w // 2 + w - 1, 0), length - 1)
                masks[kind, gi, r, base + lo:base + hi + 1] = 1.0
                inv[kind, gi, r, 0] = 1.0 / (hi - lo + 1)
    return jnp.asarray(masks, BF16), jnp.asarray(inv, F32)


ROUTE_FIELDS = 6


def _route_block(logits, ltri_ref, rt_ref, cnt_ref, cnt_scr):
    @pl.when(pl.program_id(0) == 0)
    def _():
        cnt_scr[...] = jnp.zeros_like(cnt_scr)

    lane = lax.broadcasted_iota(jnp.int32, logits.shape, 1).astype(F32)

    def top1(valid):
        v = jnp.max(jnp.where(valid, logits, -jnp.inf), axis=1, keepdims=True)
        i = jnp.min(jnp.where(valid & (logits == v), lane, float(SMALL)), axis=1, keepdims=True)
        return v, i

    is_g = lane < N_GROUPS_E
    gm, grp = top1(is_g)
    p_grp = 1.0 / jnp.sum(jnp.where(is_g, jnp.exp(logits - gm), 0.0), axis=1, keepdims=True)
    lo_lane = N_GROUPS_E + grp * EPG
    in_grp = (lane >= lo_lane) & (lane < lo_lane + EPG)
    v1, i1 = top1(in_grp)
    v2, i2 = top1(in_grp & (lane != i1))
    t = jnp.exp(v2 - v1)
    w1 = p_grp / (1.0 + t)
    w2 = p_grp * t / (1.0 + t)
    e1 = i1 - N_GROUPS_E
    e2 = i2 - N_GROUPS_E

    oh1 = lane == e1
    oh2 = lane == e2
    oh = jnp.where(oh1 | oh2, 1.0, 0.0)
    before = _dot(ltri_ref[...], oh.astype(BF16)) + cnt_scr[...]
    r1 = jnp.sum(jnp.where(oh1, before, 0.0), axis=1, keepdims=True)
    r2 = jnp.sum(jnp.where(oh2, before, 0.0), axis=1, keepdims=True)
    cnt_scr[...] = cnt_scr[...] + jnp.sum(oh, axis=0, keepdims=True)
    cnt_ref[...] = cnt_scr[...]
    rt = jnp.zeros(logits.shape, F32)
    for k, val in enumerate((e1, e2, w1, w2, r1, r2)):
        rt = jnp.where(lane == k, val, rt)
    rt_ref[...] = rt


def _merge_kernel(x_ref, pa_ref, o_ref, z_ref, mg0_ref, mg1_ref, mg2_ref, hs_ref, ys_ref, xs_ref,
                  pmask_ref, pinv_ref, poolw_ref, pscale_ref, mng_ref, dsk_ref, sng_ref, wbr_ref, wout_ref,
                  gate_ref, sh_ref, sc_ref, gffn_ref, wrh_ref, wrl_ref, brt_ref, ltri_ref,
                  xo_ref, h2_ref, rt_ref, cnt_ref, cnt_scr):
    parts = []
    for g in range(4):
        a_g = pa_ref[:, g * 256:(g + 1) * 256]
        pooled = _dot(pmask_ref[g], a_g) * pinv_ref[g] - a_g.astype(F32)
        parts.append(_dot(pooled.astype(BF16), poolw_ref[g]))
    pool = jnp.concatenate(parts, axis=1) * pscale_ref[...]

    hs = hs_ref[...].astype(F32)
    hn = jnp.concatenate([_rms(hs[:, h * DH_M:(h + 1) * DH_M]) for h in range(HEADS_M)], axis=1)
    ml = _sigmoid(o_ref[...].astype(F32)) * (hn * mng_ref[...])

    y = ys_ref[...].astype(F32) + dsk_ref[...] * xs_ref[...].astype(F32)
    sl = _rms(y * _silu(z_ref[...].astype(F32))) * sng_ref[...]

    acc = _sigmoid(mg0_ref[...].astype(F32)) * _dot(pool.astype(BF16), wbr_ref[0])
    acc = acc + _sigmoid(mg1_ref[...].astype(F32)) * _dot(ml.astype(BF16), wbr_ref[1])
    acc = acc + _sigmoid(mg2_ref[...].astype(F32)) * _dot(sl.astype(BF16), wbr_ref[2])
    xn = x_ref[...] + gate_ref[...] * _dot(acc.astype(BF16), wout_ref[...])
    xo_ref[...] = xn

    h2 = (_rms(xn) * gffn_ref[...]) * (1.0 + sc_ref[...]) + sh_ref[...]
    hi, lo = _split2(h2)
    h2f = hi.astype(F32)
    for cchunk in range(D // 128):
        h2_ref[pl.ds(cchunk, TB, stride=D // 128), :] = h2f[:, cchunk * 128:(cchunk + 1) * 128]
    logits = _dot(hi, wrh_ref[...]) + _dot(lo, wrh_ref[...]) + _dot(hi, wrl_ref[...]) + brt_ref[...]
    _route_block(logits, ltri_ref, rt_ref, cnt_ref, cnt_scr)


def _merge_call(x, u_po, u_tail, hs, ys, xbc, pmask, pinv, poolw, pscale, mng, dsk, sng, wbr, wout,
                mods3, gffn, wr_hi, wr_lo, br):
    ltri = jnp.asarray(np.tril(np.ones((TB, TB), np.float32), -1), BF16)
    kind = lambda j: jnp.where(j % SEQ_BLOCKS == 0, 1, 0)
    vec = _const_spec((1, D))
    return pl.pallas_call(
        _merge_kernel,
        grid=(N_BLOCKS,),
        in_specs=[_tok_spec(),
                  _tok_spec(D, 0), _tok_spec(D, 1), _tok_spec(D, 1),
                  _tok_spec(D, 3), _tok_spec(D, 4), _tok_spec(D, 5),
                  _tok_spec(), _tok_spec(),
                  _tok_spec(D, 0),
                  pl.BlockSpec((None, 4, TB, TB), lambda j: (kind(j), 0, 0, 0)),
                  pl.BlockSpec((None, 4, TB, 1), lambda j: (kind(j), 0, 0, 0)),
                  _const_spec((4, 256, 256)), vec, vec, vec, vec,
                  _const_spec((3, D, D)), _const_spec((D, D)),
                  _mod_spec(2), _mod_spec(3), _mod_spec(4), vec,
                  _const_spec((D, SMALL)), _const_spec((D, SMALL)), _const_spec((1, SMALL)),
                  _const_spec((TB, TB))],
        out_specs=[_tok_spec(), pl.BlockSpec((TB * ROW_TILES, 128), lambda j: (j, 0)), _tok_spec(SMALL),
                   _const_spec((1, SMALL))],
        out_shape=[jax.ShapeDtypeStruct((N_TOK, D), F32), jax.ShapeDtypeStruct((N_TOK * ROW_TILES, 128), F32),
                   jax.ShapeDtypeStruct((N_TOK, SMALL), F32), jax.ShapeDtypeStruct((1, SMALL), F32)],
        scratch_shapes=[pltpu.VMEM((1, SMALL), F32)],
        compiler_params=_cparams(1),
        name="branch_merge",
    )(x, u_po, u_po, u_tail, u_tail, u_tail, u_tail, hs, ys, xbc, pmask, pinv, poolw, pscale, mng, dsk, sng,
      wbr, wout, mods3, mods3, mods3, gffn, wr_hi, wr_lo, br, ltri)


GATHER_AHEAD = 2
GATHER_SLOTS = GATHER_AHEAD + 1
GATHER_DMA_QUEUES = 2


def _moe_kernel(be_ref, nu_ref, tok0_ref, tok1_ref, tokn_ref, h_hbm, wg_ref, wu_ref, wd_ref, o_ref,
                xbuf, wgb, wub, wdb, sem):
    i = pl.program_id(0)
    n_used = nu_ref[0]
    slot = i % GATHER_SLOTS

    def start_gather(tok_ref, s):
        for r in range(MOE_BLOCK):
            row0 = pl.multiple_of(tok_ref[0, r] * ROW_TILES, ROW_TILES)
            pltpu.make_async_copy(h_hbm.at[pl.ds(row0, ROW_TILES)],
                                  xbuf.at[s, pl.ds(r * ROW_TILES, ROW_TILES)],
                                  sem.at[s]).start(priority=r % GATHER_DMA_QUEUES)

    def wait_gather(s):
        pltpu.make_async_copy(xbuf.at[s], xbuf.at[s], sem.at[s]).wait()

    @pl.when((i == 0) & (n_used > 0))
    def _():
        start_gather(tok0_ref, 0)
        start_gather(tok1_ref, 1)

    @pl.when((i < n_used) & ((i == 0) | (be_ref[i] != be_ref[jnp.maximum(i - 1, 0)])))
    def _():
        wgb[...] = wg_ref[...].astype(BF16)
        wub[...] = wu_ref[...].astype(BF16)
        wdb[...] = wd_ref[...].astype(BF16)

    @pl.when(i < n_used)
    def _():
        wait_gather(slot)
        x = jnp.concatenate([xbuf[slot, pl.ds(c, MOE_BLOCK, stride=ROW_TILES), :] for c in range(ROW_TILES)],
                            axis=1).astype(BF16)
        start_gather(tokn_ref, (i + GATHER_AHEAD) % GATHER_SLOTS)
        gt = _dot(x, wgb[...])
        up = _dot(x, wub[...])
        act = (_silu(gt) * up).astype(BF16)
        o_ref[...] = _dot(act, wdb[...]).astype(o_ref.dtype)

    @pl.when(i == n_used - 1)
    def _():
        for ahead in range(1, GATHER_AHEAD + 1):
            wait_gather((i + ahead) % GATHER_SLOTS)

    @pl.when(i >= n_used)
    def _():
        o_ref[...] = jnp.zeros_like(o_ref)


MOE_CAP = N_TOK * 2 + N_EXPERTS * MOE_BLOCK
MOE_NBLOCKS = MOE_CAP // MOE_BLOCK


def _moe_call(layer, block_expert, n_used, buf_tok, h2, w_gate, w_up, w_down):
    grid_spec = pltpu.PrefetchScalarGridSpec(
        num_scalar_prefetch=2,
        grid=(MOE_NBLOCKS,),
        in_specs=[pl.BlockSpec((None, 1, MOE_BLOCK), lambda i, be, nu: (0, 0, 0), memory_space=pltpu.SMEM),
                  pl.BlockSpec((None, 1, MOE_BLOCK), lambda i, be, nu: (1, 0, 0), memory_space=pltpu.SMEM),
                  pl.BlockSpec((None, 1, MOE_BLOCK),
                               lambda i, be, nu: (jnp.minimum(i + GATHER_AHEAD, MOE_NBLOCKS - 1), 0, 0),
                               memory_space=pltpu.SMEM),
                  pl.BlockSpec(memory_space=pl.ANY),
                  pl.BlockSpec((None, None, D, D_EXPERT), lambda i, be, nu: (layer, be[i], 0, 0)),
                  pl.BlockSpec((None, None, D, D_EXPERT), lambda i, be, nu: (layer, be[i], 0, 0)),
                  pl.BlockSpec((None, None, D_EXPERT, D), lambda i, be, nu: (layer, be[i], 0, 0))],
        out_specs=pl.BlockSpec((MOE_BLOCK, D), lambda i, be, nu: (i, 0)),
        scratch_shapes=[pltpu.VMEM((GATHER_SLOTS, MOE_BLOCK * ROW_TILES, 128), F32),
                        pltpu.VMEM((D, D_EXPERT), BF16), pltpu.VMEM((D, D_EXPERT), BF16),
                        pltpu.VMEM((D_EXPERT, D), BF16), pltpu.SemaphoreType.DMA((GATHER_SLOTS,))],
    )
    return pl.pallas_call(
        _moe_kernel,
        grid_spec=grid_spec,
        out_shape=jax.ShapeDtypeStruct((MOE_CAP, D), BF16),
        compiler_params=_cparams(1),
        name="moe_experts",
    )(block_expert, n_used, buf_tok, buf_tok, buf_tok, h2, w_gate, w_up, w_down)


def _dispatch_plan(route, counts):
    cnt = counts[0, :N_EXPERTS].astype(jnp.int32)
    padded = (cnt + MOE_BLOCK - 1) // MOE_BLOCK * MOE_BLOCK
    pends = jnp.cumsum(padded)
    pstarts = pends - padded
    expert = route[:, 0:2].astype(jnp.int32)
    rank = route[:, 4:6].astype(jnp.int32)
    onehot = expert[:, :, None] == jnp.arange(N_EXPERTS, dtype=jnp.int32)
    pos = jnp.sum(jnp.where(onehot, pstarts, 0), axis=-1) + rank
    tok = jnp.broadcast_to(jnp.arange(N_TOK, dtype=jnp.int32)[:, None], (N_TOK, 2))
    buf_tok = jnp.zeros((MOE_CAP,), jnp.int32).at[pos.reshape(-1)].set(tok.reshape(-1))
    buf_tok = buf_tok.reshape(MOE_NBLOCKS, 1, MOE_BLOCK)
    block_start = jnp.arange(MOE_NBLOCKS, dtype=jnp.int32) * MOE_BLOCK
    block_expert = jnp.minimum(jnp.sum((pends[None, :] <= block_start[:, None]).astype(jnp.int32), axis=1),
                               N_EXPERTS - 1)
    n_used = (pends[-1] // MOE_BLOCK).reshape(1)
    return buf_tok, block_expert, n_used, pos


def _to_colmajor(t):
    c = t.shape[-1]
    t = t.reshape(BATCH, T_BATCH, c)
    lat = t[:, CTX:].reshape(BATCH, SEQ // GRID_W, GRID_W, c).swapaxes(1, 2).reshape(BATCH, SEQ, c)
    return jnp.concatenate([t[:, :CTX], lat], axis=1).reshape(N_TOK, c)


def _to_rowmajor(t):
    c = t.shape[-1]
    t = t.reshape(BATCH, T_BATCH, c)
    lat = t[:, CTX:].reshape(BATCH, GRID_W, SEQ // GRID_W, c).swapaxes(1, 2).reshape(BATCH, SEQ, c)
    return jnp.concatenate([t[:, :CTX], lat], axis=1).reshape(N_TOK, c)


def _lanes(*pieces):
    v = jnp.concatenate([jnp.asarray(p, F32).reshape(-1) for p in pieces])
    return jnp.pad(v, (0, SMALL - v.shape[0]))


def kernel(x, c, ctx, c_ctx, w_ada, b_ada, g_norm_mix, g_norm_ffn, w_in, pool_w, pool_scale, mlstm_conv_w,
           mlstm_conv_b, mlstm_gate_b, mlstm_norm_g, ssd_conv_w, ssd_conv_b, ssd_dt_bias, ssd_a_log, ssd_d,
           ssd_norm_g, w_branch, w_out, w_route_group, b_route_group, w_route_expert, b_route_expert,
           w_exp_gate, w_exp_up, w_exp_down, g_final):
    cvec = jnp.concatenate([c, c_ctx[None], jnp.zeros((3, D), F32)], axis=0)
    mods = _ada_table(cvec, w_ada, b_ada)
    xs = jnp.concatenate([ctx, x], axis=1).reshape(N_TOK, D)
    pmask, pinv = _pool_tables()
    row = lambda v: v.reshape(1, -1).astype(F32)

    h = small = None
    out = None
    for l in range(DEPTH):
        mods3 = mods[l].reshape(8, 1, 6 * D)
        sp = np.cumsum([0, D, D, D, D, D, 16, D, D, 512, 512, 32, 3 * D])
        w_tail = jnp.concatenate([w_in[l][:, sp[6]:sp[10]], w_in[l][:, sp[11]:sp[12]]], axis=1).astype(BF16)
        w_small = jnp.pad(jnp.concatenate([w_in[l][:, sp[5]:sp[6]], w_in[l][:, sp[10]:sp[11]]], axis=1),
                          ((0, 0), (0, SMALL - 48)))
        ws_hi = w_small.astype(BF16)
        ws_lo = (w_small - ws_hi.astype(F32)).astype(BF16)

        if l == 0:
            h, small = _norm_call(xs, row(g_norm_mix[l]), mods3, ws_hi, ws_lo)

        h_cm = _to_colmajor(h)
        small_cm = _to_colmajor(small)
        u_po = _mm_f32w_call(h, w_in, l, lambda j: 4 * j, 2, "in_proj_pool_o")
        u_tail = _mm_call(h, w_tail, "in_proj_rowmajor")
        u_cm = _mm_f32w_call(h_cm, w_in, l, lambda j: 1 + j, 3, "in_proj_colmajor")

        post_m = jnp.concatenate([jnp.ones((D,), F32), jnp.full((D,), DH_M ** -0.5, F32)]).reshape(1, 2 * D)
        qk = _conv_call(u_cm, lambda c: c, 2 * D, mlstm_conv_w[l], row(mlstm_conv_b[l]), post_m, "mlstm_conv")
        gate_bias = _lanes(mlstm_gate_b[l])
        gate_bwd = _lanes(jnp.zeros((12,)), jnp.ones((4,)))
        g_valc, g_cumc, g_valr, g_cumr = _gate_call(small_cm, gate_bias, jnp.ones((SMALL,), F32), gate_bwd,
                                                    False, "mlstm_gates")
        h_f = _mlstm_call(qk, u_cm, g_valc, g_cumc, g_valr, g_cumr, 0)
        h_fb = _mlstm_call(qk, u_cm, g_valc, g_cumc, g_valr, g_cumr, 1, prev=h_f)
        h_fb = _to_rowmajor(h_fb)

        xbc = _conv_call(u_tail, lambda c: c + jnp.where(c >= 2, 2, 0), 2 * D, ssd_conv_w[l],
                         row(ssd_conv_b[l]), jnp.ones((1, 2 * D), F32), "ssd_conv")
        a_neg = -jnp.exp(ssd_a_log[l].astype(F32))
        dt_bias = _lanes(jnp.zeros((16,)), ssd_dt_bias[l])
        dt_scale = _lanes(jnp.zeros((16,)), a_neg)
        dt_bwd = _lanes(jnp.zeros((32,)), jnp.ones((16,)))
        _, s_acsc, s_dtr, s_acsr = _gate_call(small, dt_bias, dt_scale, dt_bwd, True, "ssd_gates")
        y_f = _ssd_call(xbc, s_acsc, s_dtr, s_acsr, 0)
        y_fb = _ssd_call(xbc, s_acsc, s_dtr, s_acsr, 1, prev=y_f)

        w_r = jnp.pad(jnp.concatenate([w_route_group[l], w_route_expert[l]], axis=1),
                      ((0, 0), (0, SMALL - N_GROUPS_E - N_EXPERTS)))
        wr_hi = w_r.astype(BF16)
        wr_lo = (w_r - wr_hi.astype(F32)).astype(BF16)
        b_r = _lanes(b_route_group[l], b_route_expert[l]).reshape(1, SMALL)
        dsk = jnp.repeat(ssd_d[l].astype(F32), SSD_P).reshape(1, D)
        xs, h2, route, counts = _merge_call(
            xs, u_po, u_tail, h_fb, y_fb, xbc, pmask, pinv, pool_w[l].astype(BF16), row(pool_scale[l]),
            row(mlstm_norm_g[l]), dsk, row(ssd_norm_g[l]), w_branch[l].astype(BF16), w_out[l].astype(BF16),
            mods3, row(g_norm_ffn[l]), wr_hi, wr_lo, b_r)

        buf_tok, block_expert, n_used, pos = _dispatch_plan(route, counts)
        yb = _moe_call(l, block_expert, n_used, buf_tok, h2, w_exp_gate, w_exp_up, w_exp_down)
        y0 = yb[pos[:, 0]]
        y1 = yb[pos[:, 1]]

        if l + 1 < DEPTH:
            wi_n = w_in[l + 1]
            w_small_n = jnp.pad(jnp.concatenate([wi_n[:, sp[5]:sp[6]], wi_n[:, sp[10]:sp[11]]], axis=1),
                                ((0, 0), (0, SMALL - 48)))
            wsn_hi = w_small_n.astype(BF16)
            wsn_lo = (w_small_n - wsn_hi.astype(F32)).astype(BF16)
            mods3_n = mods[l + 1].reshape(8, 1, 6 * D)
            xs, h, small = _resnorm_call(xs, y0, y1, route, mods3, row(g_norm_mix[l + 1]), mods3_n,
                                         wsn_hi, wsn_lo)
        else:
            out = _final_call(xs, y0, y1, route, mods3, row(g_final))
    return out.reshape(BATCH, SEQ, D)
```

```python
import functools
import math

import jax
import jax.numpy as jnp
import numpy as np
from jax import lax
from jax.experimental import pallas as pl
from jax.experimental.pallas import tpu as pltpu

F32 = jnp.float32
BF16 = jnp.bfloat16

D = 1024
BATCH = 4
SEQ = 4096
CTX = 256
DEPTH = 2
GRID_W = 64
EPS = 1e-6

TB = 256
SEQ_BLOCKS = (SEQ + CTX) // TB
T_BATCH = SEQ + CTX
N_TOK = BATCH * T_BATCH
N_BLOCKS = N_TOK // TB

POOL_WINDOWS = (2, 4, 8, 16)
HEADS_M = 4
DH_M = 256
SSD_HEADS = 16
SSD_P = 64
SSD_GROUPS = 4
SSD_N = 128
N_GROUPS_E = 4
EPG = 8
N_EXPERTS = 32
D_EXPERT = 512
MOE_BLOCK = 256
SMALL = 128
ROW_TILES = D // 128

VMEM_LIMIT = 56 * 1024 * 1024


def _cparams(n_axes):
    return pltpu.CompilerParams(dimension_semantics=("arbitrary",) * n_axes,
                                vmem_limit_bytes=VMEM_LIMIT)


def _dot(a, b):
    return jnp.dot(a, b, preferred_element_type=F32)


def _split2(x):
    hi = x.astype(BF16)
    lo = (x - hi.astype(F32)).astype(BF16)
    return hi, lo


def _split3(x):
    hi = x.astype(BF16)
    r = x - hi.astype(F32)
    mid = r.astype(BF16)
    lo = (r - mid.astype(F32)).astype(BF16)
    return hi, mid, lo


def _sigmoid(x):
    return 1.0 / (1.0 + jnp.exp(-x))


def _silu(x):
    return x * _sigmoid(x)


def _log1pexp_negabs(x):
    return jnp.log(1.0 + jnp.exp(-jnp.abs(x)))


def _log_sigmoid(x):
    return jnp.minimum(x, 0.0) - _log1pexp_negabs(x)


def _softplus(x):
    return jnp.maximum(x, 0.0) + _log1pexp_negabs(x)


def _rms(x):
    return x * lax.rsqrt(jnp.mean(x * x, axis=-1, keepdims=True) + EPS)


def _mod_row(j):
    return jnp.where(j % SEQ_BLOCKS == 0, BATCH, j // SEQ_BLOCKS)


def _ada_kernel(c_ref, w_ref, b_ref, o_ref):
    c = c_ref[...]
    s_hi, s_lo = _split2(_silu(c))
    w_hi, w_lo = _split2(w_ref[...])
    o_ref[...] = _dot(s_hi, w_hi) + _dot(s_lo, w_hi) + _dot(s_hi, w_lo) + b_ref[...]


def _ada_table(cvec, w_ada, b_ada):
    tn = 1024
    return pl.pallas_call(
        _ada_kernel,
        grid=(DEPTH, 6 * D // tn),
        in_specs=[pl.BlockSpec((8, D), lambda l, j: (0, 0)),
                  pl.BlockSpec((None, D, tn), lambda l, j: (l, 0, j)),
                  pl.BlockSpec((None, 1, tn), lambda l, j: (l, 0, j))],
        out_specs=pl.BlockSpec((None, 8, tn), lambda l, j: (l, 0, j)),
        out_shape=jax.ShapeDtypeStruct((DEPTH, 8, 6 * D), F32),
        compiler_params=_cparams(2),
        name="ada_table",
    )(cvec, w_ada, b_ada.reshape(DEPTH, 1, 6 * D))


def _norm_mod_small(x, g_ref, sh_ref, sc_ref, wsh_ref, wsl_ref, h_ref, small_ref):
    h = _rms(x) * g_ref[...]
    h = h * (1.0 + sc_ref[...]) + sh_ref[...]
    h_hi, h_lo = _split2(h)
    h_ref[...] = h_hi
    small_ref[...] = _dot(h_hi, wsh_ref[...]) + _dot(h_lo, wsh_ref[...]) + _dot(h_hi, wsl_ref[...])


def _norm_kernel(x_ref, g_ref, sh_ref, sc_ref, wsh_ref, wsl_ref, h_ref, small_ref):
    _norm_mod_small(x_ref[...], g_ref, sh_ref, sc_ref, wsh_ref, wsl_ref, h_ref, small_ref)


def _moe_residual(x_ref, ya_ref, yb_ref, rt_ref, gate_ref):
    y = rt_ref[:, 2:3] * ya_ref[...].astype(F32) + rt_ref[:, 3:4] * yb_ref[...].astype(F32)
    return x_ref[...] + gate_ref[...] * y


def _resnorm_kernel(x_ref, ya_ref, yb_ref, rt_ref, gate_ref, g_ref, sh_ref, sc_ref, wsh_ref, wsl_ref,
                    xo_ref, h_ref, small_ref):
    x = _moe_residual(x_ref, ya_ref, yb_ref, rt_ref, gate_ref)
    xo_ref[...] = x
    _norm_mod_small(x, g_ref, sh_ref, sc_ref, wsh_ref, wsl_ref, h_ref, small_ref)


def _mod_spec(chunk):
    return pl.BlockSpec((None, 1, D), lambda j: (_mod_row(j), 0, chunk))


def _tok_spec(width=D, col=0):
    return pl.BlockSpec((TB, width), lambda j: (j, col))


def _const_spec(shape):
    nd = len(shape)
    return pl.BlockSpec(shape, lambda j: (0,) * nd)


def _norm_call(x, g, mods3, ws_hi, ws_lo):
    return pl.pallas_call(
        _norm_kernel,
        grid=(N_BLOCKS,),
        in_specs=[_tok_spec(), _const_spec((1, D)), _mod_spec(0), _mod_spec(1),
                  _const_spec((D, SMALL)), _const_spec((D, SMALL))],
        out_specs=[_tok_spec(), _tok_spec(SMALL)],
        out_shape=[jax.ShapeDtypeStruct((N_TOK, D), BF16), jax.ShapeDtypeStruct((N_TOK, SMALL), F32)],
        compiler_params=_cparams(1),
        name="norm_mod",
    )(x, g, mods3, mods3, ws_hi, ws_lo)


def _resnorm_call(x, ya, yb, route, mods3_prev, g, mods3, ws_hi, ws_lo):
    return pl.pallas_call(
        _resnorm_kernel,
        grid=(N_BLOCKS,),
        in_specs=[_tok_spec(), _tok_spec(), _tok_spec(), _tok_spec(SMALL),
                  pl.BlockSpec((None, 1, D), lambda j: (_mod_row(j), 0, 5)),
                  _const_spec((1, D)), _mod_spec(0), _mod_spec(1),
                  _const_spec((D, SMALL)), _const_spec((D, SMALL))],
        out_specs=[_tok_spec(), _tok_spec(), _tok_spec(SMALL)],
        out_shape=[jax.ShapeDtypeStruct((N_TOK, D), F32), jax.ShapeDtypeStruct((N_TOK, D), BF16),
                   jax.ShapeDtypeStruct((N_TOK, SMALL), F32)],
        compiler_params=_cparams(1),
        name="residual_norm_mod",
    )(x, ya, yb, route, mods3_prev, g, mods3, mods3, ws_hi, ws_lo)


def _final_kernel(x_ref, ya_ref, yb_ref, rt_ref, gate_ref, g_ref, o_ref):
    o_ref[...] = _rms(_moe_residual(x_ref, ya_ref, yb_ref, rt_ref, gate_ref)) * g_ref[...]


def _final_call(x, ya, yb, route, mods3, g_final):
    lat = lambda b, s: (b * SEQ_BLOCKS + 1 + s, 0)
    spec = pl.BlockSpec((TB, D), lat)
    return pl.pallas_call(
        _final_kernel,
        grid=(BATCH, SEQ // TB),
        in_specs=[spec, spec, spec, pl.BlockSpec((TB, SMALL), lat),
                  pl.BlockSpec((None, 1, D), lambda b, s: (b, 0, 5)),
                  pl.BlockSpec((1, D), lambda b, s: (0, 0))],
        out_specs=pl.BlockSpec((TB, D), lambda b, s: (b * (SEQ // TB) + s, 0)),
        out_shape=jax.ShapeDtypeStruct((BATCH * SEQ, D), F32),
        compiler_params=_cparams(2),
        name="final_norm",
    )(x, ya, yb, route, mods3, g_final)


def _dot_nt(a, bt):
    return lax.dot_general(a, bt, (((1,), (1,)), ((), ())), preferred_element_type=F32)


def _mm_kernel(h_ref, wt_ref, o_ref):
    o_ref[...] = _dot_nt(h_ref[...], wt_ref[...]).astype(o_ref.dtype)


MM_TM, MM_TN = N_TOK // 8, 1024


def _mm_call(h, wt, name):
    n = wt.shape[0]
    tm, tn = MM_TM, MM_TN
    return pl.pallas_call(
        _mm_kernel,
        grid=(n // tn, N_TOK // tm),
        in_specs=[pl.BlockSpec((tm, D), lambda j, i: (i, 0)),
                  pl.BlockSpec((tn, D), lambda j, i: (j, 0))],
        out_specs=pl.BlockSpec((tm, tn), lambda j, i: (i, j)),
        out_shape=jax.ShapeDtypeStruct((N_TOK, n), BF16),
        compiler_params=_cparams(2),
        name=name,
    )(h, wt)


def _mm_f32w_kernel(h_ref, wt_ref, o_ref, wb_ref):
    @pl.when(pl.program_id(1) == 0)
    def _():
        wb_ref[...] = wt_ref[...].astype(BF16)

    o_ref[...] = _dot_nt(h_ref[...], wb_ref[...]).astype(o_ref.dtype)


def _mm_f32w_call(h, wt_all, layer, row_block, n_tiles, name):
    tm, tn = MM_TM, MM_TN
    return pl.pallas_call(
        _mm_f32w_kernel,
        grid=(n_tiles, N_TOK // tm),
        in_specs=[pl.BlockSpec((tm, D), lambda j, i: (i, 0)),
                  pl.BlockSpec((None, tn, D), lambda j, i: (layer, row_block(j), 0))],
        out_specs=pl.BlockSpec((tm, tn), lambda j, i: (i, j)),
        out_shape=jax.ShapeDtypeStruct((N_TOK, n_tiles * tn), BF16),
        scratch_shapes=[pltpu.VMEM((tn, D), BF16)],
        compiler_params=_cparams(2),
        name=name,
    )(h, wt_all)


CONV_HALO = 16


def _conv_kernel(x_ref, w_ref, b_ref, post_ref, o_ref):
    w = w_ref[...]
    bias = b_ref[...]
    post = post_ref[...]
    row = lax.broadcasted_iota(jnp.int32, (TB, 1), 0)
    for k in range(SEQ_BLOCKS):
        r0 = k * TB
        lo = max(r0 - CONV_HALO, 0)
        hi = min(r0 + TB + CONV_HALO, T_BATCH)
        off = r0 - lo
        n = hi - lo
        ext = x_ref[lo:hi, :].astype(F32)
        xm2 = pltpu.roll(ext, 2, 0)[off:off + TB]
        xm1 = pltpu.roll(ext, 1, 0)[off:off + TB]
        x0 = ext[off:off + TB]
        xp1 = pltpu.roll(ext, n - 1, 0)[off:off + TB]
        if k in (0, 1):
            xm2 = jnp.where(row >= 2, xm2, 0.0)
            xm1 = jnp.where(row >= 1, xm1, 0.0)
        if k in (0, SEQ_BLOCKS - 1):
            xp1 = jnp.where(row <= TB - 2, xp1, 0.0)
        y = bias + w[0:1] * xm2 + w[1:2] * xm1 + w[2:3] * x0 + w[3:4] * xp1
        o_ref[r0:r0 + TB, :] = (_silu(y) * post).astype(o_ref.dtype)


def _conv_call(u, in_tile, width, w, b, post, name):
    tc = 512
    return pl.pallas_call(
        _conv_kernel,
        grid=(BATCH, width // tc),
        in_specs=[pl.BlockSpec((T_BATCH, tc), lambda bi, c: (bi, in_tile(c))),
                  pl.BlockSpec((4, tc), lambda bi, c: (0, c)),
                  pl.BlockSpec((1, tc), lambda bi, c: (0, c)),
                  pl.BlockSpec((1, tc), lambda bi, c: (0, c))],
        out_specs=pl.BlockSpec((T_BATCH, tc), lambda bi, c: (bi, c)),
        out_shape=jax.ShapeDtypeStruct((N_TOK, width), BF16),
        compiler_params=_cparams(2),
        name=name,
    )(u, w, b, post)


def _tri_dot_cols(tri, x):
    a, b, c = _split3(x)
    return _dot(tri, a) + _dot(tri, b) + _dot(tri, c)


def _tri_dot_rows(x, tri):
    a, b, c = _split3(x)
    return _dot(a, tri) + _dot(b, tri) + _dot(c, tri)


def _gate_kernel(pre_ref, pret_ref, bc_ref, br_ref, ac_ref, ar_ref, bwdc_ref, bwdr_ref, tril_ref, triu_ref,
                 valc_ref, cumc_ref, valr_ref, cumr_ref, *, ssd):
    tril = tril_ref[...]
    triu = triu_ref[...]

    def act(v):
        if ssd:
            val = _softplus(v)
            return val, val
        return v, _log_sigmoid(v)

    vc, dc = act(pre_ref[...] + bc_ref[...])
    dc = dc * ac_ref[...]
    valc_ref[...] = vc
    cumc_ref[...] = jnp.where(bwdc_ref[...] > 0.5, _tri_dot_cols(triu, dc), _tri_dot_cols(tril, dc))

    vr, dr = act(pret_ref[...] + br_ref[...])
    dr = dr * ar_ref[...]
    valr_ref[...] = vr
    cumr_ref[...] = jnp.where(bwdr_ref[...] > 0.5, _tri_dot_rows(dr, tril), _tri_dot_rows(dr, triu))


def _gate_call(small, bias, scale, bwd, ssd, name):
    small_t = small.reshape(N_BLOCKS, TB, SMALL)[:, :, :64].transpose(0, 2, 1)
    tri = np.tril(np.ones((TB, TB), np.float32))
    tril = jnp.asarray(tri, BF16)
    triu = jnp.asarray(tri.T, BF16)
    col = lambda v: v.reshape(1, SMALL).astype(F32)
    rowv = lambda v: v[:64].reshape(64, 1).astype(F32)
    cs = lambda shape: pl.BlockSpec(shape, lambda j: (0,) * len(shape))
    return pl.pallas_call(
        functools.partial(_gate_kernel, ssd=ssd),
        grid=(N_BLOCKS,),
        in_specs=[_tok_spec(SMALL), pl.BlockSpec((None, 64, TB), lambda j: (j, 0, 0)),
                  cs((1, SMALL)), cs((64, 1)), cs((1, SMALL)), cs((64, 1)), cs((1, SMALL)), cs((64, 1)),
                  cs((TB, TB)), cs((TB, TB))],
        out_specs=[_tok_spec(SMALL), _tok_spec(SMALL),
                   pl.BlockSpec((None, 64, TB), lambda j: (j, 0, 0)),
                   pl.BlockSpec((None, 64, TB), lambda j: (j, 0, 0))],
        out_shape=[jax.ShapeDtypeStruct((N_TOK, SMALL), F32), jax.ShapeDtypeStruct((N_TOK, SMALL), F32),
                   jax.ShapeDtypeStruct((N_BLOCKS, 64, TB), F32), jax.ShapeDtypeStruct((N_BLOCKS, 64, TB), F32)],
        compiler_params=_cparams(1),
        name=name,
    )(small, small_t, col(bias), rowv(bias), col(scale), rowv(scale), col(bwd), rowv(bwd), tril, triu)


def _scan_block(d):
    if d == 0:
        return lambda b, s: b * SEQ_BLOCKS + s
    return lambda b, s: b * SEQ_BLOCKS + jnp.where(s == 0, 0, SEQ_BLOCKS - s)


def _causal_mask(d):
    t = lax.broadcasted_iota(jnp.int32, (TB, TB), 0)
    s = lax.broadcasted_iota(jnp.int32, (TB, TB), 1)
    return (s <= t) if d == 0 else (s >= t)


SUB = 128


def _mlstm_kernel(q_ref, k_ref, v_ref, valc_ref, cumc_ref, valr_ref, cumr_ref, *rest, d):
    prev_ref = rest[0] if d == 1 else None
    o_ref, c_ref, n_ref, m_ref = rest[-4:]
    @pl.when(pl.program_id(1) == 0)
    def _():
        c_ref[...] = jnp.zeros_like(c_ref)
        n_ref[...] = jnp.zeros_like(n_ref)
        m_ref[...] = jnp.zeros_like(m_ref)

    t_i = lax.broadcasted_iota(jnp.int32, (SUB, SUB), 0)
    s_i = lax.broadcasted_iota(jnp.int32, (SUB, SUB), 1)
    mask = (s_i <= t_i) if d == 0 else (s_i >= t_i)
    bcast = lambda col: jnp.broadcast_to(col, (SUB, SUB))
    rep = lambda f: jnp.concatenate([f] * (DH_M // SUB), axis=1)
    order = (0, 1) if d == 0 else (1, 0)

    for h in range(HEADS_M):
        p = d * HEADS_M + h
        cs = slice(h * DH_M, (h + 1) * DH_M)
        base = jnp.zeros((1, 1), F32)
        c_st = c_ref[h]
        n_st = n_ref[h]
        m_st = m_ref[h]
        for ci, c in enumerate(order):
            r0 = c * SUB
            rows = slice(r0, r0 + SUB)
            if ci == 1:
                edge = SUB - 1 if d == 0 else SUB
                base = cumr_ref[8 + p:9 + p, edge:edge + 1]
            last = r0 + (SUB - 1 if d == 0 else 0)
            q = q_ref[rows, cs]
            k = k_ref[rows, cs]
            v = v_ref[rows, cs]
            cum_c = bcast(cumc_ref[rows, 8 + p:9 + p]) - base
            i_c = bcast(valc_ref[rows, p:p + 1])
            cum_r = cumr_ref[8 + p:9 + p, rows] - base
            i_r = valr_ref[p:p + 1, rows]
            total = cumr_ref[8 + p:9 + p, last:last + 1] - base

            dmat = jnp.where(mask, cum_c - cum_r + i_r, -jnp.inf)
            m_loc = jnp.max(dmat, axis=1, keepdims=True)
            a = jnp.exp(dmat - m_loc) * lax.dot_general(q, k, (((1,), (1,)), ((), ())),
                                                        preferred_element_type=F32)
            num_loc = _dot(a.astype(BF16), v)
            den_loc = jnp.sum(a, axis=1, keepdims=True)
            wl_log = total - cum_c + i_c
            m_kv = jnp.max(wl_log, axis=0, keepdims=True)[:, 0:1]
            wl = rep(jnp.exp(wl_log - m_kv))
            kv_loc = lax.dot_general(k, v * wl.astype(BF16), (((0,), (0,)), ((), ())),
                                     preferred_element_type=F32)
            kn_loc = jnp.sum(wl * k.astype(F32), axis=0, keepdims=True)

            g = cum_c + m_st
            m_t = jnp.maximum(g, m_loc)
            f_loc = jnp.exp(m_loc - m_t)
            f_st = jnp.exp(g - m_t)
            num = rep(f_loc) * num_loc + rep(f_st) * _dot(q, c_st.astype(BF16))
            den = f_loc * den_loc + f_st * jnp.sum(q.astype(F32) * n_st, axis=1, keepdims=True)
            inv = 1.0 / jnp.maximum(jnp.abs(den), jnp.exp(-m_t))
            h_out = num * rep(inv)
            if prev_ref is not None:
                h_out = h_out + prev_ref[rows, cs].astype(F32)
            o_ref[rows, cs] = h_out.astype(o_ref.dtype)

            m_new = jnp.maximum(total + m_st, m_kv)
            keep = jnp.exp(total + m_st - m_new)
            take = jnp.exp(m_kv - m_new)
            c_st = keep * c_st + take * kv_loc
            n_st = keep * n_st + take * kn_loc
            m_st = m_new
        c_ref[h] = c_st
        n_ref[h] = n_st
        m_ref[h] = m_st


def _mlstm_call(qk, ucm, valc, cumc, valr, cumr, d, prev=None):
    blk = _scan_block(d)
    tok = lambda col: pl.BlockSpec((TB, D), lambda b, s: (blk(b, s), col))
    small = pl.BlockSpec((TB, SMALL), lambda b, s: (blk(b, s), 0))
    rows = pl.BlockSpec((None, 64, TB), lambda b, s: (blk(b, s), 0, 0))
    extra = () if d == 0 else (prev,)
    return pl.pallas_call(
        functools.partial(_mlstm_kernel, d=d),
        grid=(BATCH, SEQ_BLOCKS),
        in_specs=[tok(0), tok(1), tok(2), small, small, rows, rows] + [tok(0)] * len(extra),
        out_specs=tok(0),
        out_shape=jax.ShapeDtypeStruct((N_TOK, D), BF16),
        scratch_shapes=[pltpu.VMEM((HEADS_M, DH_M, DH_M), F32), pltpu.VMEM((HEADS_M, 1, DH_M), F32),
                        pltpu.VMEM((HEADS_M, 1, 1), F32)],
        compiler_params=_cparams(2),
        name="mlstm_fwd" if d == 0 else "mlstm_bwd",
    )(qk, qk, ucm, valc, cumc, valr, cumr, *extra)


def _ssd_kernel(x_ref, b_ref, c_ref, acsc_ref, dtr_ref, acsr_ref, *rest, d):
    prev_ref = rest[0] if d == 1 else None
    o_ref, s_ref = rest[-2:]
    @pl.when(pl.program_id(1) == 0)
    def _():
        s_ref[...] = jnp.zeros_like(s_ref)

    t_i = lax.broadcasted_iota(jnp.int32, (SUB, SUB), 0)
    s_i = lax.broadcasted_iota(jnp.int32, (SUB, SUB), 1)
    mask = (s_i <= t_i) if d == 0 else (s_i >= t_i)
    lo = lax.broadcasted_iota(jnp.int32, (SUB, 2 * SSD_P), 1) < SSD_P
    lo1 = lax.broadcasted_iota(jnp.int32, (1, 2 * SSD_P), 1) < SSD_P
    zero_b = jnp.zeros((SUB, 2 * SSD_P), BF16)
    bcast = lambda col: jnp.broadcast_to(col, (SUB, SUB))

    base_row = jnp.zeros((1, SMALL), F32)
    for ci, c in enumerate((0, 1) if d == 0 else (1, 0)):
        r0 = c * SUB
        rows = slice(r0, r0 + SUB)
        if ci == 1:
            edge = SUB - 1 if d == 0 else SUB
            base_row = acsc_ref[edge:edge + 1, :]
        last = r0 + (SUB - 1 if d == 0 else 0)
        acs_sub = acsc_ref[rows, :]
        tot_row = acsc_ref[last:last + 1, :] - base_row
        cd_row = jnp.exp(tot_row)
        from_start = jnp.exp(acs_sub - base_row)
        for g in range(SSD_GROUPS):
            bg = b_ref[rows, g * SSD_N:(g + 1) * SSD_N]
            cg = c_ref[rows, g * SSD_N:(g + 1) * SSD_N]
            cb = lax.dot_general(cg, bg, (((1,), (1,)), ((), ())), preferred_element_type=F32)
            bg_t = jnp.transpose(bg.astype(F32)).astype(BF16)
            for j in range(2):
                tile = slice(g * 4 * SSD_P + j * 2 * SSD_P, g * 4 * SSD_P + (j + 1) * 2 * SSD_P)
                stile = slice(j * 2 * SSD_P, (j + 1) * 2 * SSD_P)
                ps = [SSD_HEADS + d * SSD_HEADS + g * 4 + j * 2 + k for k in (0, 1)]
                xt = x_ref[rows, tile]
                st = s_ref[g, :, stile]
                st_b = st.astype(BF16)
                intra, inter, to_state = [], [], []
                for p in ps:
                    cum_col = bcast(acs_sub[:, p:p + 1])
                    cum_row = acsr_ref[p:p + 1, rows]
                    dt_row = dtr_ref[p:p + 1, rows]
                    seg = jnp.where(mask, cum_col - (cum_row - jnp.log(dt_row)), -jnp.inf)
                    intra.append((cb * jnp.exp(seg)).astype(BF16))
                    inter.append(cg * bcast(from_start[:, p:p + 1]).astype(BF16))
                    w_row = dt_row * jnp.exp(tot_row[:, p:p + 1] - (cum_row - base_row[:, p:p + 1]))
                    to_state.append(bg_t * w_row.astype(BF16))
                x_pair = [jnp.where(lo, xt, zero_b), jnp.where(lo, zero_b, xt)]
                s_pair = [jnp.where(lo, st_b, zero_b), jnp.where(lo, zero_b, st_b)]
                lhs = jnp.concatenate(intra + inter, axis=1)
                y_out = _dot(lhs, jnp.concatenate(x_pair + s_pair, axis=0))
                if prev_ref is not None:
                    y_out = y_out + prev_ref[rows, tile].astype(F32)
                o_ref[rows, tile] = y_out.astype(o_ref.dtype)
                cd_sel = jnp.where(lo1, cd_row[:, ps[0]:ps[0] + 1], cd_row[:, ps[1]:ps[1] + 1])
                s_ref[g, :, stile] = cd_sel * st + _dot(jnp.concatenate(to_state, axis=1),
                                                        jnp.concatenate(x_pair, axis=0))


def _ssd_call(xbc, acsc, dtr, acsr, d, prev=None):
    blk = _scan_block(d)
    small = pl.BlockSpec((TB, SMALL), lambda b, s: (blk(b, s), 0))
    rows = pl.BlockSpec((None, 64, TB), lambda b, s: (blk(b, s), 0, 0))
    tok = pl.BlockSpec((TB, D), lambda b, s: (blk(b, s), 0))
    extra = () if d == 0 else (prev,)
    return pl.pallas_call(
        functools.partial(_ssd_kernel, d=d),
        grid=(BATCH, SEQ_BLOCKS),
        in_specs=[tok,
                  pl.BlockSpec((TB, 512), lambda b, s: (blk(b, s), 2)),
                  pl.BlockSpec((TB, 512), lambda b, s: (blk(b, s), 3)),
                  small, rows, rows] + [tok] * len(extra),
        out_specs=pl.BlockSpec((TB, D), lambda b, s: (blk(b, s), 0)),
        out_shape=jax.ShapeDtypeStruct((N_TOK, D), BF16),
        scratch_shapes=[pltpu.VMEM((SSD_GROUPS, SSD_N, 4 * SSD_P), F32)],
        compiler_params=_cparams(2),
        name="ssd_fwd" if d == 0 else "ssd_bwd",
    )(xbc, xbc, xbc, acsc, dtr, acsr, *extra)


def _pool_tables():
    masks = np.zeros((2, 4, TB, TB), np.float32)
    inv = np.zeros((2, 4, TB, 1), np.float32)
    for kind, length in ((0, GRID_W), (1, CTX)):
        for gi, w in enumerate(POOL_WINDOWS):
            for r in range(TB):
                base, c = (r // length) * length, r % length
                lo = min(max(c - w // 2, 0), length - 1)
                hi = min(max(c - w // 2 + w - 1, 0), length - 1)
                masks[kind, gi, r, base + lo:base + hi + 1] = 1.0
                inv[kind, gi, r, 0] = 1.0 / (hi - lo + 1)
    return jnp.asarray(masks, BF16), jnp.asarray(inv, F32)


ROUTE_FIELDS = 6


def _route_block(logits, ltri_ref, rt_ref, cnt_ref, cnt_scr):
    @pl.when(pl.program_id(0) == 0)
    def _():
        cnt_scr[...] = jnp.zeros_like(cnt_scr)

    lane = lax.broadcasted_iota(jnp.int32, logits.shape, 1).astype(F32)

    def top1(valid):
        v = jnp.max(jnp.where(valid, logits, -jnp.inf), axis=1, keepdims=True)
        i = jnp.min(jnp.where(valid & (logits == v), lane, float(SMALL)), axis=1, keepdims=True)
        return v, i

    is_g = lane < N_GROUPS_E
    gm, grp = top1(is_g)
    p_grp = 1.0 / jnp.sum(jnp.where(is_g, jnp.exp(logits - gm), 0.0), axis=1, keepdims=True)
    lo_lane = N_GROUPS_E + grp * EPG
    in_grp = (lane >= lo_lane) & (lane < lo_lane + EPG)
    v1, i1 = top1(in_grp)
    v2, i2 = top1(in_grp & (lane != i1))
    t = jnp.exp(v2 - v1)
    w1 = p_grp / (1.0 + t)
    w2 = p_grp * t / (1.0 + t)
    e1 = i1 - N_GROUPS_E
    e2 = i2 - N_GROUPS_E

    oh1 = lane == e1
    oh2 = lane == e2
    oh = jnp.where(oh1 | oh2, 1.0, 0.0)
    before = _dot(ltri_ref[...], oh.astype(BF16)) + cnt_scr[...]
    r1 = jnp.sum(jnp.where(oh1, before, 0.0), axis=1, keepdims=True)
    r2 = jnp.sum(jnp.where(oh2, before, 0.0), axis=1, keepdims=True)
    cnt_scr[...] = cnt_scr[...] + jnp.sum(oh, axis=0, keepdims=True)
    cnt_ref[...] = cnt_scr[...]
    rt = jnp.zeros(logits.shape, F32)
    for k, val in enumerate((e1, e2, w1, w2, r1, r2)):
        rt = jnp.where(lane == k, val, rt)
    rt_ref[...] = rt


def _merge_kernel(x_ref, pa_ref, o_ref, z_ref, mg0_ref, mg1_ref, mg2_ref, hs_ref, ys_ref, xs_ref,
                  pmask_ref, pinv_ref, poolw_ref, pscale_ref, mng_ref, dsk_ref, sng_ref, wbr_ref, wout_ref,
                  gate_ref, sh_ref, sc_ref, gffn_ref, wrh_ref, wrl_ref, brt_ref, ltri_ref,
                  xo_ref, h2_ref, rt_ref, cnt_ref, cnt_scr):
    parts = []
    for g in range(4):
        a_g = pa_ref[:, g * 256:(g + 1) * 256]
        pooled = _dot(pmask_ref[g], a_g) * pinv_ref[g] - a_g.astype(F32)
        parts.append(_dot(pooled.astype(BF16), poolw_ref[g]))
    pool = jnp.concatenate(parts, axis=1) * pscale_ref[...]

    hs = hs_ref[...].astype(F32)
    hn = jnp.concatenate([_rms(hs[:, h * DH_M:(h + 1) * DH_M]) for h in range(HEADS_M)], axis=1)
    ml = _sigmoid(o_ref[...].astype(F32)) * (hn * mng_ref[...])

    y = ys_ref[...].astype(F32) + dsk_ref[...] * xs_ref[...].astype(F32)
    sl = _rms(y * _silu(z_ref[...].astype(F32))) * sng_ref[...]

    acc = _sigmoid(mg0_ref[...].astype(F32)) * _dot(pool.astype(BF16), wbr_ref[0])
    acc = acc + _sigmoid(mg1_ref[...].astype(F32)) * _dot(ml.astype(BF16), wbr_ref[1])
    acc = acc + _sigmoid(mg2_ref[...].astype(F32)) * _dot(sl.astype(BF16), wbr_ref[2])
    xn = x_ref[...] + gate_ref[...] * _dot(acc.astype(BF16), wout_ref[...])
    xo_ref[...] = xn

    h2 = (_rms(xn) * gffn_ref[...]) * (1.0 + sc_ref[...]) + sh_ref[...]
    hi, lo = _split2(h2)
    h2f = hi.astype(F32)
    for cchunk in range(D // 128):
        h2_ref[pl.ds(cchunk, TB, stride=D // 128), :] = h2f[:, cchunk * 128:(cchunk + 1) * 128]
    logits = _dot(hi, wrh_ref[...]) + _dot(lo, wrh_ref[...]) + _dot(hi, wrl_ref[...]) + brt_ref[...]
    _route_block(logits, ltri_ref, rt_ref, cnt_ref, cnt_scr)


def _merge_call(x, u_po, u_tail, hs, ys, xbc, pmask, pinv, poolw, pscale, mng, dsk, sng, wbr, wout,
                mods3, gffn, wr_hi, wr_lo, br):
    ltri = jnp.asarray(np.tril(np.ones((TB, TB), np.float32), -1), BF16)
    kind = lambda j: jnp.where(j % SEQ_BLOCKS == 0, 1, 0)
    vec = _const_spec((1, D))
    return pl.pallas_call(
        _merge_kernel,
        grid=(N_BLOCKS,),
        in_specs=[_tok_spec(),
                  _tok_spec(D, 0), _tok_spec(D, 1), _tok_spec(D, 1),
                  _tok_spec(D, 3), _tok_spec(D, 4), _tok_spec(D, 5),
                  _tok_spec(), _tok_spec(),
                  _tok_spec(D, 0),
                  pl.BlockSpec((None, 4, TB, TB), lambda j: (kind(j), 0, 0, 0)),
                  pl.BlockSpec((None, 4, TB, 1), lambda j: (kind(j), 0, 0, 0)),
                  _const_spec((4, 256, 256)), vec, vec, vec, vec,
                  _const_spec((3, D, D)), _const_spec((D, D)),
                  _mod_spec(2), _mod_spec(3), _mod_spec(4), vec,
                  _const_spec((D, SMALL)), _const_spec((D, SMALL)), _const_spec((1, SMALL)),
                  _const_spec((TB, TB))],
        out_specs=[_tok_spec(), pl.BlockSpec((TB * ROW_TILES, 128), lambda j: (j, 0)), _tok_spec(SMALL),
                   _const_spec((1, SMALL))],
        out_shape=[jax.ShapeDtypeStruct((N_TOK, D), F32), jax.ShapeDtypeStruct((N_TOK * ROW_TILES, 128), F32),
                   jax.ShapeDtypeStruct((N_TOK, SMALL), F32), jax.ShapeDtypeStruct((1, SMALL), F32)],
        scratch_shapes=[pltpu.VMEM((1, SMALL), F32)],
        compiler_params=_cparams(1),
        name="branch_merge",
    )(x, u_po, u_po, u_tail, u_tail, u_tail, u_tail, hs, ys, xbc, pmask, pinv, poolw, pscale, mng, dsk, sng,
      wbr, wout, mods3, mods3, mods3, gffn, wr_hi, wr_lo, br, ltri)


GATHER_AHEAD = 2
GATHER_SLOTS = GATHER_AHEAD + 1
GATHER_DMA_QUEUES = 2


def _moe_kernel(be_ref, nu_ref, tok0_ref, tok1_ref, tokn_ref, h_hbm, wg_ref, wu_ref, wd_ref, o_ref,
                xbuf, wgb, wub, wdb, sem):
    i = pl.program_id(0)
    n_used = nu_ref[0]
    slot = i % GATHER_SLOTS

    def start_gather(tok_ref, s):
        for r in range(MOE_BLOCK):
            row0 = pl.multiple_of(tok_ref[0, r] * ROW_TILES, ROW_TILES)
            pltpu.make_async_copy(h_hbm.at[pl.ds(row0, ROW_TILES)],
                                  xbuf.at[s, pl.ds(r * ROW_TILES, ROW_TILES)],
                                  sem.at[s]).start(priority=r % GATHER_DMA_QUEUES)

    def wait_gather(s):
        pltpu.make_async_copy(xbuf.at[s], xbuf.at[s], sem.at[s]).wait()

    @pl.when((i == 0) & (n_used > 0))
    def _():
        start_gather(tok0_ref, 0)
        start_gather(tok1_ref, 1)

    @pl.when((i < n_used) & ((i == 0) | (be_ref[i] != be_ref[jnp.maximum(i - 1, 0)])))
    def _():
        wgb[...] = wg_ref[...].astype(BF16)
        wub[...] = wu_ref[...].astype(BF16)
        wdb[...] = wd_ref[...].astype(BF16)

    @pl.when(i < n_used)
    def _():
        wait_gather(slot)
        x = jnp.concatenate([xbuf[slot, pl.ds(c, MOE_BLOCK, stride=ROW_TILES), :] for c in range(ROW_TILES)],
                            axis=1).astype(BF16)
        start_gather(tokn_ref, (i + GATHER_AHEAD) % GATHER_SLOTS)
        gt = _dot(x, wgb[...])
        up = _dot(x, wub[...])
        act = (_silu(gt) * up).astype(BF16)
        o_ref[...] = _dot(act, wdb[...]).astype(o_ref.dtype)

    @pl.when(i == n_used - 1)
    def _():
        for ahead in range(1, GATHER_AHEAD + 1):
            wait_gather((i + ahead) % GATHER_SLOTS)

    @pl.when(i >= n_used)
    def _():
        o_ref[...] = jnp.zeros_like(o_ref)


MOE_CAP = N_TOK * 2 + N_EXPERTS * MOE_BLOCK
MOE_NBLOCKS = MOE_CAP // MOE_BLOCK


def _moe_call(layer, block_expert, n_used, buf_tok, h2, w_gate, w_up, w_down):
    grid_spec = pltpu.PrefetchScalarGridSpec(
        num_scalar_prefetch=2,
        grid=(MOE_NBLOCKS,),
        in_specs=[pl.BlockSpec((None, 1, MOE_BLOCK), lambda i, be, nu: (0, 0, 0), memory_space=pltpu.SMEM),
                  pl.BlockSpec((None, 1, MOE_BLOCK), lambda i, be, nu: (1, 0, 0), memory_space=pltpu.SMEM),
                  pl.BlockSpec((None, 1, MOE_BLOCK),
                               lambda i, be, nu: (jnp.minimum(i + GATHER_AHEAD, MOE_NBLOCKS - 1), 0, 0),
                               memory_space=pltpu.SMEM),
                  pl.BlockSpec(memory_space=pl.ANY),
                  pl.BlockSpec((None, None, D, D_EXPERT), lambda i, be, nu: (layer, be[i], 0, 0)),
                  pl.BlockSpec((None, None, D, D_EXPERT), lambda i, be, nu: (layer, be[i], 0, 0)),
                  pl.BlockSpec((None, None, D_EXPERT, D), lambda i, be, nu: (layer, be[i], 0, 0))],
        out_specs=pl.BlockSpec((MOE_BLOCK, D), lambda i, be, nu: (i, 0)),
        scratch_shapes=[pltpu.VMEM((GATHER_SLOTS, MOE_BLOCK * ROW_TILES, 128), F32),
                        pltpu.VMEM((D, D_EXPERT), BF16), pltpu.VMEM((D, D_EXPERT), BF16),
                        pltpu.VMEM((D_EXPERT, D), BF16), pltpu.SemaphoreType.DMA((GATHER_SLOTS,))],
    )
    return pl.pallas_call(
        _moe_kernel,
        grid_spec=grid_spec,
        out_shape=jax.ShapeDtypeStruct((MOE_CAP, D), BF16),
        compiler_params=_cparams(1),
        name="moe_experts",
    )(block_expert, n_used, buf_tok, buf_tok, buf_tok, h2, w_gate, w_up, w_down)


def _dispatch_plan(route, counts):
    cnt = counts[0, :N_EXPERTS].astype(jnp.int32)
    padded = (cnt + MOE_BLOCK - 1) // MOE_BLOCK * MOE_BLOCK
    pends = jnp.cumsum(padded)
    pstarts = pends - padded
    expert = route[:, 0:2].astype(jnp.int32)
    rank = route[:, 4:6].astype(jnp.int32)
    onehot = expert[:, :, None] == jnp.arange(N_EXPERTS, dtype=jnp.int32)
    pos = jnp.sum(jnp.where(onehot, pstarts, 0), axis=-1) + rank
    tok = jnp.broadcast_to(jnp.arange(N_TOK, dtype=jnp.int32)[:, None], (N_TOK, 2))
    buf_tok = jnp.zeros((MOE_CAP,), jnp.int32).at[pos.reshape(-1)].set(tok.reshape(-1))
    buf_tok = buf_tok.reshape(MOE_NBLOCKS, 1, MOE_BLOCK)
    block_start = jnp.arange(MOE_NBLOCKS, dtype=jnp.int32) * MOE_BLOCK
    block_expert = jnp.minimum(jnp.sum((pends[None, :] <= block_start[:, None]).astype(jnp.int32), axis=1),
                               N_EXPERTS - 1)
    n_used = (pends[-1] // MOE_BLOCK).reshape(1)
    return buf_tok, block_expert, n_used, pos


def _to_colmajor(t):
    c = t.shape[-1]
    t = t.reshape(BATCH, T_BATCH, c)
    lat = t[:, CTX:].reshape(BATCH, SEQ // GRID_W, GRID_W, c).swapaxes(1, 2).reshape(BATCH, SEQ, c)
    return jnp.concatenate([t[:, :CTX], lat], axis=1).reshape(N_TOK, c)


def _to_rowmajor(t):
    c = t.shape[-1]
    t = t.reshape(BATCH, T_BATCH, c)
    lat = t[:, CTX:].reshape(BATCH, GRID_W, SEQ // GRID_W, c).swapaxes(1, 2).reshape(BATCH, SEQ, c)
    return jnp.concatenate([t[:, :CTX], lat], axis=1).reshape(N_TOK, c)


def _lanes(*pieces):
    v = jnp.concatenate([jnp.asarray(p, F32).reshape(-1) for p in pieces])
    return jnp.pad(v, (0, SMALL - v.shape[0]))


def kernel(x, c, ctx, c_ctx, w_ada, b_ada, g_norm_mix, g_norm_ffn, w_in, pool_w, pool_scale, mlstm_conv_w,
           mlstm_conv_b, mlstm_gate_b, mlstm_norm_g, ssd_conv_w, ssd_conv_b, ssd_dt_bias, ssd_a_log, ssd_d,
           ssd_norm_g, w_branch, w_out, w_route_group, b_route_group, w_route_expert, b_route_expert,
           w_exp_gate, w_exp_up, w_exp_down, g_final):
    cvec = jnp.concatenate([c, c_ctx[None], jnp.zeros((3, D), F32)], axis=0)
    mods = _ada_table(cvec, w_ada, b_ada)
    xs = jnp.concatenate([ctx, x], axis=1).reshape(N_TOK, D)
    pmask, pinv = _pool_tables()
    row = lambda v: v.reshape(1, -1).astype(F32)
    w_in_t = jnp.swapaxes(w_in, 1, 2)

    def small_weight(layer):
        rows_ = jnp.concatenate([w_in_t[layer, 5 * D:5 * D + 16], w_in_t[layer, 8 * D + 16:8 * D + 48]], axis=0)
        return jnp.pad(rows_.T, ((0, 0), (0, SMALL - 48)))

    h = small = None
    out = None
    for l in range(DEPTH):
        mods3 = mods[l].reshape(8, 1, 6 * D)
        sp = np.cumsum([0, D, D, D, D, D, 16, D, D, 512, 512, 32, 3 * D])
        w_tail = jnp.concatenate([w_in_t[l, sp[6]:sp[10]], w_in_t[l, sp[11]:sp[12]]], axis=0).astype(BF16)
        w_small = small_weight(l)
        ws_hi = w_small.astype(BF16)
        ws_lo = (w_small - ws_hi.astype(F32)).astype(BF16)

        if l == 0:
            h, small = _norm_call(xs, row(g_norm_mix[l]), mods3, ws_hi, ws_lo)

        h_cm = _to_colmajor(h)
        small_cm = _to_colmajor(small)
        u_po = _mm_f32w_call(h, w_in_t, l, lambda j: 4 * j, 2, "in_proj_pool_o")
        u_tail = _mm_call(h, w_tail, "in_proj_rowmajor")
        u_cm = _mm_f32w_call(h_cm, w_in_t, l, lambda j: 1 + j, 3, "in_proj_colmajor")

        post_m = jnp.concatenate([jnp.ones((D,), F32), jnp.full((D,), DH_M ** -0.5, F32)]).reshape(1, 2 * D)
        qk = _conv_call(u_cm, lambda c: c, 2 * D, mlstm_conv_w[l], row(mlstm_conv_b[l]), post_m, "mlstm_conv")
        gate_bias = _lanes(mlstm_gate_b[l])
        gate_bwd = _lanes(jnp.zeros((12,)), jnp.ones((4,)))
        g_valc, g_cumc, g_valr, g_cumr = _gate_call(small_cm, gate_bias, jnp.ones((SMALL,), F32), gate_bwd,
                                                    False, "mlstm_gates")
        h_f = _mlstm_call(qk, u_cm, g_valc, g_cumc, g_valr, g_cumr, 0)
        h_fb = _mlstm_call(qk, u_cm, g_valc, g_cumc, g_valr, g_cumr, 1, prev=h_f)
        h_fb = _to_rowmajor(h_fb)

        xbc = _conv_call(u_tail, lambda c: c + jnp.where(c >= 2, 2, 0), 2 * D, ssd_conv_w[l],
                         row(ssd_conv_b[l]), jnp.ones((1, 2 * D), F32), "ssd_conv")
        a_neg = -jnp.exp(ssd_a_log[l].astype(F32))
        dt_bias = _lanes(jnp.zeros((16,)), ssd_dt_bias[l])
        dt_scale = _lanes(jnp.zeros((16,)), a_neg)
        dt_bwd = _lanes(jnp.zeros((32,)), jnp.ones((16,)))
        _, s_acsc, s_dtr, s_acsr = _gate_call(small, dt_bias, dt_scale, dt_bwd, True, "ssd_gates")
        y_f = _ssd_call(xbc, s_acsc, s_dtr, s_acsr, 0)
        y_fb = _ssd_call(xbc, s_acsc, s_dtr, s_acsr, 1, prev=y_f)

        w_r = jnp.pad(jnp.concatenate([w_route_group[l], w_route_expert[l]], axis=1),
                      ((0, 0), (0, SMALL - N_GROUPS_E - N_EXPERTS)))
        wr_hi = w_r.astype(BF16)
        wr_lo = (w_r - wr_hi.astype(F32)).astype(BF16)
        b_r = _lanes(b_route_group[l], b_route_expert[l]).reshape(1, SMALL)
        dsk = jnp.repeat(ssd_d[l].astype(F32), SSD_P).reshape(1, D)
        xs, h2, route, counts = _merge_call(
            xs, u_po, u_tail, h_fb, y_fb, xbc, pmask, pinv, pool_w[l].astype(BF16), row(pool_scale[l]),
            row(mlstm_norm_g[l]), dsk, row(ssd_norm_g[l]), w_branch[l].astype(BF16), w_out[l].astype(BF16),
            mods3, row(g_norm_ffn[l]), wr_hi, wr_lo, b_r)

        buf_tok, block_expert, n_used, pos = _dispatch_plan(route, counts)
        yb = _moe_call(l, block_expert, n_used, buf_tok, h2, w_exp_gate, w_exp_up, w_exp_down)
        y0 = yb[pos[:, 0]]
        y1 = yb[pos[:, 1]]

        if l + 1 < DEPTH:
            w_small_n = small_weight(l + 1)
            wsn_hi = w_small_n.astype(BF16)
            wsn_lo = (w_small_n - wsn_hi.astype(F32)).astype(BF16)
            mods3_n = mods[l + 1].reshape(8, 1, 6 * D)
            xs, h, small = _resnorm_call(xs, y0, y1, route, mods3, row(g_norm_mix[l + 1]), mods3_n,
                                         wsn_hi, wsn_lo)
        else:
            out = _final_call(xs, y0, y1, route, mods3, row(g_final))
    return out.reshape(BATCH, SEQ, D)
```

```python
import functools
import math

import jax
import jax.numpy as jnp
import numpy as np
from jax import lax
from jax.experimental import pallas as pl
from jax.experimental.pallas import tpu as pltpu

F32 = jnp.float32
BF16 = jnp.bfloat16

D = 1024
BATCH = 4
SEQ = 4096
CTX = 256
DEPTH = 2
GRID_W = 64
EPS = 1e-6

TB = 256
SEQ_BLOCKS = (SEQ + CTX) // TB
T_BATCH = SEQ + CTX
N_TOK = BATCH * T_BATCH
N_BLOCKS = N_TOK // TB

POOL_WINDOWS = (2, 4, 8, 16)
HEADS_M = 4
DH_M = 256
SSD_HEADS = 16
SSD_P = 64
SSD_GROUPS = 4
SSD_N = 128
N_GROUPS_E = 4
EPG = 8
N_EXPERTS = 32
D_EXPERT = 512
MOE_BLOCK = 256
SMALL = 128
ROW_TILES = D // 128

VMEM_LIMIT = 56 * 1024 * 1024


def _cparams(n_axes):
    return pltpu.CompilerParams(dimension_semantics=("arbitrary",) * n_axes,
                                vmem_limit_bytes=VMEM_LIMIT)


def _dot(a, b):
    return jnp.dot(a, b, preferred_element_type=F32)


def _split2(x):
    hi = x.astype(BF16)
    lo = (x - hi.astype(F32)).astype(BF16)
    return hi, lo


def _split3(x):
    hi = x.astype(BF16)
    r = x - hi.astype(F32)
    mid = r.astype(BF16)
    lo = (r - mid.astype(F32)).astype(BF16)
    return hi, mid, lo


def _sigmoid(x):
    return 0.5 * jnp.tanh(0.5 * x) + 0.5


def _silu(x):
    return x * _sigmoid(x)


def _log1pexp_negabs(x):
    return jnp.log(1.0 + jnp.exp(-jnp.abs(x)))


def _log_sigmoid(x):
    return jnp.minimum(x, 0.0) - _log1pexp_negabs(x)


def _softplus(x):
    return jnp.maximum(x, 0.0) + _log1pexp_negabs(x)


def _rms(x):
    return x * lax.rsqrt(jnp.mean(x * x, axis=-1, keepdims=True) + EPS)


def _mod_row(j):
    return jnp.where(j % SEQ_BLOCKS == 0, BATCH, j // SEQ_BLOCKS)


def _ada_kernel(c_ref, w_ref, b_ref, o_ref):
    c = c_ref[...]
    s_hi, s_lo = _split2(_silu(c))
    w_hi, w_lo = _split2(w_ref[...])
    o_ref[...] = _dot(s_hi, w_hi) + _dot(s_lo, w_hi) + _dot(s_hi, w_lo) + b_ref[...]


def _ada_table(cvec, w_ada, b_ada):
    tn = 1024
    return pl.pallas_call(
        _ada_kernel,
        grid=(DEPTH, 6 * D // tn),
        in_specs=[pl.BlockSpec((8, D), lambda l, j: (0, 0)),
                  pl.BlockSpec((None, D, tn), lambda l, j: (l, 0, j)),
                  pl.BlockSpec((None, 1, tn), lambda l, j: (l, 0, j))],
        out_specs=pl.BlockSpec((None, 8, tn), lambda l, j: (l, 0, j)),
        out_shape=jax.ShapeDtypeStruct((DEPTH, 8, 6 * D), F32),
        compiler_params=_cparams(2),
        name="ada_table",
    )(cvec, w_ada, b_ada.reshape(DEPTH, 1, 6 * D))


def _norm_mod_small(x, g_ref, sh_ref, sc_ref, wsh_ref, wsl_ref, h_ref, small_ref):
    h = _rms(x) * g_ref[...]
    h = h * (1.0 + sc_ref[...]) + sh_ref[...]
    h_hi, h_lo = _split2(h)
    h_ref[...] = h_hi
    small_ref[...] = _dot(h_hi, wsh_ref[...]) + _dot(h_lo, wsh_ref[...]) + _dot(h_hi, wsl_ref[...])


def _norm_kernel(x_ref, g_ref, sh_ref, sc_ref, wsh_ref, wsl_ref, h_ref, small_ref):
    _norm_mod_small(x_ref[...], g_ref, sh_ref, sc_ref, wsh_ref, wsl_ref, h_ref, small_ref)


def _moe_residual(x_ref, ya_ref, yb_ref, rt_ref, gate_ref):
    y = rt_ref[:, 2:3] * ya_ref[...].astype(F32) + rt_ref[:, 3:4] * yb_ref[...].astype(F32)
    return x_ref[...] + gate_ref[...] * y


def _resnorm_kernel(x_ref, ya_ref, yb_ref, rt_ref, gate_ref, g_ref, sh_ref, sc_ref, wsh_ref, wsl_ref,
                    xo_ref, h_ref, small_ref):
    x = _moe_residual(x_ref, ya_ref, yb_ref, rt_ref, gate_ref)
    xo_ref[...] = x
    _norm_mod_small(x, g_ref, sh_ref, sc_ref, wsh_ref, wsl_ref, h_ref, small_ref)


def _mod_spec(chunk):
    return pl.BlockSpec((None, 1, D), lambda j: (_mod_row(j), 0, chunk))


def _tok_spec(width=D, col=0):
    return pl.BlockSpec((TB, width), lambda j: (j, col))


def _const_spec(shape):
    nd = len(shape)
    return pl.BlockSpec(shape, lambda j: (0,) * nd)


def _norm_call(x, g, mods3, ws_hi, ws_lo):
    return pl.pallas_call(
        _norm_kernel,
        grid=(N_BLOCKS,),
        in_specs=[_tok_spec(), _const_spec((1, D)), _mod_spec(0), _mod_spec(1),
                  _const_spec((D, SMALL)), _const_spec((D, SMALL))],
        out_specs=[_tok_spec(), _tok_spec(SMALL)],
        out_shape=[jax.ShapeDtypeStruct((N_TOK, D), BF16), jax.ShapeDtypeStruct((N_TOK, SMALL), F32)],
        compiler_params=_cparams(1),
        name="norm_mod",
    )(x, g, mods3, mods3, ws_hi, ws_lo)


def _resnorm_call(x, ya, yb, route, mods3_prev, g, mods3, ws_hi, ws_lo):
    return pl.pallas_call(
        _resnorm_kernel,
        grid=(N_BLOCKS,),
        in_specs=[_tok_spec(), _tok_spec(), _tok_spec(), _tok_spec(SMALL),
                  pl.BlockSpec((None, 1, D), lambda j: (_mod_row(j), 0, 5)),
                  _const_spec((1, D)), _mod_spec(0), _mod_spec(1),
                  _const_spec((D, SMALL)), _const_spec((D, SMALL))],
        out_specs=[_tok_spec(), _tok_spec(), _tok_spec(SMALL)],
        out_shape=[jax.ShapeDtypeStruct((N_TOK, D), F32), jax.ShapeDtypeStruct((N_TOK, D), BF16),
                   jax.ShapeDtypeStruct((N_TOK, SMALL), F32)],
        compiler_params=_cparams(1),
        name="residual_norm_mod",
    )(x, ya, yb, route, mods3_prev, g, mods3, mods3, ws_hi, ws_lo)


def _final_kernel(x_ref, ya_ref, yb_ref, rt_ref, gate_ref, g_ref, o_ref):
    o_ref[...] = _rms(_moe_residual(x_ref, ya_ref, yb_ref, rt_ref, gate_ref)) * g_ref[...]


def _final_call(x, ya, yb, route, mods3, g_final):
    lat = lambda b, s: (b * SEQ_BLOCKS + 1 + s, 0)
    spec = pl.BlockSpec((TB, D), lat)
    return pl.pallas_call(
        _final_kernel,
        grid=(BATCH, SEQ // TB),
        in_specs=[spec, spec, spec, pl.BlockSpec((TB, SMALL), lat),
                  pl.BlockSpec((None, 1, D), lambda b, s: (b, 0, 5)),
                  pl.BlockSpec((1, D), lambda b, s: (0, 0))],
        out_specs=pl.BlockSpec((TB, D), lambda b, s: (b * (SEQ // TB) + s, 0)),
        out_shape=jax.ShapeDtypeStruct((BATCH * SEQ, D), F32),
        compiler_params=_cparams(2),
        name="final_norm",
    )(x, ya, yb, route, mods3, g_final)


def _dot_nt(a, bt):
    return lax.dot_general(a, bt, (((1,), (1,)), ((), ())), preferred_element_type=F32)


def _mm_kernel(h_ref, wt_ref, o_ref):
    o_ref[...] = _dot_nt(h_ref[...], wt_ref[...]).astype(o_ref.dtype)


MM_TM, MM_TN = N_TOK // 8, 1024


def _mm_call(h, wt, name):
    n = wt.shape[0]
    tm, tn = MM_TM, MM_TN
    return pl.pallas_call(
        _mm_kernel,
        grid=(n // tn, N_TOK // tm),
        in_specs=[pl.BlockSpec((tm, D), lambda j, i: (i, 0)),
                  pl.BlockSpec((tn, D), lambda j, i: (j, 0))],
        out_specs=pl.BlockSpec((tm, tn), lambda j, i: (i, j)),
        out_shape=jax.ShapeDtypeStruct((N_TOK, n), BF16),
        compiler_params=_cparams(2),
        name=name,
    )(h, wt)


def _mm_f32w_kernel(h_ref, wt_ref, o_ref, wb_ref):
    @pl.when(pl.program_id(1) == 0)
    def _():
        wb_ref[...] = wt_ref[...].astype(BF16)

    o_ref[...] = _dot_nt(h_ref[...], wb_ref[...]).astype(o_ref.dtype)


def _mm_f32w_call(h, wt_all, layer, row_block, n_tiles, name):
    tm, tn = MM_TM, MM_TN
    return pl.pallas_call(
        _mm_f32w_kernel,
        grid=(n_tiles, N_TOK // tm),
        in_specs=[pl.BlockSpec((tm, D), lambda j, i: (i, 0)),
                  pl.BlockSpec((None, tn, D), lambda j, i: (layer, row_block(j), 0))],
        out_specs=pl.BlockSpec((tm, tn), lambda j, i: (i, j)),
        out_shape=jax.ShapeDtypeStruct((N_TOK, n_tiles * tn), BF16),
        scratch_shapes=[pltpu.VMEM((tn, D), BF16)],
        compiler_params=_cparams(2),
        name=name,
    )(h, wt_all)


CONV_HALO = 16


def _conv_kernel(x_ref, w_ref, b_ref, post_ref, o_ref):
    w = w_ref[...]
    bias = b_ref[...]
    post = post_ref[...]
    row = lax.broadcasted_iota(jnp.int32, (TB, 1), 0)
    for k in range(SEQ_BLOCKS):
        r0 = k * TB
        lo = max(r0 - CONV_HALO, 0)
        hi = min(r0 + TB + CONV_HALO, T_BATCH)
        off = r0 - lo
        n = hi - lo
        ext = x_ref[lo:hi, :].astype(F32)
        xm2 = pltpu.roll(ext, 2, 0)[off:off + TB]
        xm1 = pltpu.roll(ext, 1, 0)[off:off + TB]
        x0 = ext[off:off + TB]
        xp1 = pltpu.roll(ext, n - 1, 0)[off:off + TB]
        if k in (0, 1):
            xm2 = jnp.where(row >= 2, xm2, 0.0)
            xm1 = jnp.where(row >= 1, xm1, 0.0)
        if k in (0, SEQ_BLOCKS - 1):
            xp1 = jnp.where(row <= TB - 2, xp1, 0.0)
        y = bias + w[0:1] * xm2 + w[1:2] * xm1 + w[2:3] * x0 + w[3:4] * xp1
        o_ref[r0:r0 + TB, :] = (_silu(y) * post).astype(o_ref.dtype)


def _conv_call(u, in_tile, width, w, b, post, name):
    tc = 512
    return pl.pallas_call(
        _conv_kernel,
        grid=(BATCH, width // tc),
        in_specs=[pl.BlockSpec((T_BATCH, tc), lambda bi, c: (bi, in_tile(c))),
                  pl.BlockSpec((4, tc), lambda bi, c: (0, c)),
                  pl.BlockSpec((1, tc), lambda bi, c: (0, c)),
                  pl.BlockSpec((1, tc), lambda bi, c: (0, c))],
        out_specs=pl.BlockSpec((T_BATCH, tc), lambda bi, c: (bi, c)),
        out_shape=jax.ShapeDtypeStruct((N_TOK, width), BF16),
        compiler_params=_cparams(2),
        name=name,
    )(u, w, b, post)


def _tri_dot_cols(tri, x):
    a, b, c = _split3(x)
    return _dot(tri, a) + _dot(tri, b) + _dot(tri, c)


def _tri_dot_rows(x, tri):
    a, b, c = _split3(x)
    return _dot(a, tri) + _dot(b, tri) + _dot(c, tri)


def _gate_kernel(pre_ref, pret_ref, bc_ref, br_ref, ac_ref, ar_ref, bwdc_ref, bwdr_ref, tril_ref, triu_ref,
                 valc_ref, cumc_ref, valr_ref, cumr_ref, *, ssd):
    tril = tril_ref[...]
    triu = triu_ref[...]

    def act(v):
        if ssd:
            val = _softplus(v)
            return val, val
        return v, _log_sigmoid(v)

    vc, dc = act(pre_ref[...] + bc_ref[...])
    dc = dc * ac_ref[...]
    valc_ref[...] = vc
    cumc_ref[...] = jnp.where(bwdc_ref[...] > 0.5, _tri_dot_cols(triu, dc), _tri_dot_cols(tril, dc))

    vr, dr = act(pret_ref[...] + br_ref[...])
    dr = dr * ar_ref[...]
    valr_ref[...] = vr
    cumr_ref[...] = jnp.where(bwdr_ref[...] > 0.5, _tri_dot_rows(dr, tril), _tri_dot_rows(dr, triu))


def _gate_call(small, bias, scale, bwd, ssd, name):
    small_t = small.reshape(N_BLOCKS, TB, SMALL)[:, :, :64].transpose(0, 2, 1)
    tri = np.tril(np.ones((TB, TB), np.float32))
    tril = jnp.asarray(tri, BF16)
    triu = jnp.asarray(tri.T, BF16)
    col = lambda v: v.reshape(1, SMALL).astype(F32)
    rowv = lambda v: v[:64].reshape(64, 1).astype(F32)
    cs = lambda shape: pl.BlockSpec(shape, lambda j: (0,) * len(shape))
    return pl.pallas_call(
        functools.partial(_gate_kernel, ssd=ssd),
        grid=(N_BLOCKS,),
        in_specs=[_tok_spec(SMALL), pl.BlockSpec((None, 64, TB), lambda j: (j, 0, 0)),
                  cs((1, SMALL)), cs((64, 1)), cs((1, SMALL)), cs((64, 1)), cs((1, SMALL)), cs((64, 1)),
                  cs((TB, TB)), cs((TB, TB))],
        out_specs=[_tok_spec(SMALL), _tok_spec(SMALL),
                   pl.BlockSpec((None, 64, TB), lambda j: (j, 0, 0)),
                   pl.BlockSpec((None, 64, TB), lambda j: (j, 0, 0))],
        out_shape=[jax.ShapeDtypeStruct((N_TOK, SMALL), F32), jax.ShapeDtypeStruct((N_TOK, SMALL), F32),
                   jax.ShapeDtypeStruct((N_BLOCKS, 64, TB), F32), jax.ShapeDtypeStruct((N_BLOCKS, 64, TB), F32)],
        compiler_params=_cparams(1),
        name=name,
    )(small, small_t, col(bias), rowv(bias), col(scale), rowv(scale), col(bwd), rowv(bwd), tril, triu)


def _scan_block(d):
    if d == 0:
        return lambda b, s: b * SEQ_BLOCKS + s
    return lambda b, s: b * SEQ_BLOCKS + jnp.where(s == 0, 0, SEQ_BLOCKS - s)


def _causal_mask(d):
    t = lax.broadcasted_iota(jnp.int32, (TB, TB), 0)
    s = lax.broadcasted_iota(jnp.int32, (TB, TB), 1)
    return (s <= t) if d == 0 else (s >= t)


SUB = 128


def _mlstm_kernel(q_ref, k_ref, v_ref, valc_ref, cumc_ref, valr_ref, cumr_ref, *rest, d):
    prev_ref = rest[0] if d == 1 else None
    o_ref, c_ref, n_ref, m_ref = rest[-4:]
    @pl.when(pl.program_id(1) == 0)
    def _():
        c_ref[...] = jnp.zeros_like(c_ref)
        n_ref[...] = jnp.zeros_like(n_ref)
        m_ref[...] = jnp.zeros_like(m_ref)

    t_i = lax.broadcasted_iota(jnp.int32, (SUB, SUB), 0)
    s_i = lax.broadcasted_iota(jnp.int32, (SUB, SUB), 1)
    mask = (s_i <= t_i) if d == 0 else (s_i >= t_i)
    bcast = lambda col: jnp.broadcast_to(col, (SUB, SUB))
    rep = lambda f: jnp.concatenate([f] * (DH_M // SUB), axis=1)
    order = (0, 1) if d == 0 else (1, 0)

    for h in range(HEADS_M):
        p = d * HEADS_M + h
        cs = slice(h * DH_M, (h + 1) * DH_M)
        base = jnp.zeros((1, 1), F32)
        c_st = c_ref[h]
        n_st = n_ref[h]
        m_st = m_ref[h]
        for ci, c in enumerate(order):
            r0 = c * SUB
            rows = slice(r0, r0 + SUB)
            if ci == 1:
                edge = SUB - 1 if d == 0 else SUB
                base = cumr_ref[8 + p:9 + p, edge:edge + 1]
            last = r0 + (SUB - 1 if d == 0 else 0)
            q = q_ref[rows, cs]
            k = k_ref[rows, cs]
            v = v_ref[rows, cs]
            cum_c = bcast(cumc_ref[rows, 8 + p:9 + p]) - base
            i_c = bcast(valc_ref[rows, p:p + 1])
            cum_r = cumr_ref[8 + p:9 + p, rows] - base
            i_r = valr_ref[p:p + 1, rows]
            total = cumr_ref[8 + p:9 + p, last:last + 1] - base

            dmat = jnp.where(mask, cum_c - cum_r + i_r, -jnp.inf)
            m_loc = jnp.max(dmat, axis=1, keepdims=True)
            a = jnp.exp(dmat - m_loc) * lax.dot_general(q, k, (((1,), (1,)), ((), ())),
                                                        preferred_element_type=F32)
            num_loc = _dot(a.astype(BF16), v)
            den_loc = jnp.sum(a, axis=1, keepdims=True)
            wl_log = total - cum_c + i_c
            m_kv = jnp.max(wl_log, axis=0, keepdims=True)[:, 0:1]
            wl = rep(jnp.exp(wl_log - m_kv))
            kv_loc = lax.dot_general(k, v * wl.astype(BF16), (((0,), (0,)), ((), ())),
                                     preferred_element_type=F32)
            kn_loc = jnp.sum(wl * k.astype(F32), axis=0, keepdims=True)

            g = cum_c + m_st
            m_t = jnp.maximum(g, m_loc)
            f_loc = jnp.exp(m_loc - m_t)
            f_st = jnp.exp(g - m_t)
            num = rep(f_loc) * num_loc + rep(f_st) * _dot(q, c_st.astype(BF16))
            den = f_loc * den_loc + f_st * jnp.sum(q.astype(F32) * n_st, axis=1, keepdims=True)
            inv = 1.0 / jnp.maximum(jnp.abs(den), jnp.exp(-m_t))
            h_out = num * rep(inv)
            if prev_ref is not None:
                h_out = h_out + prev_ref[rows, cs].astype(F32)
            o_ref[rows, cs] = h_out.astype(o_ref.dtype)

            m_new = jnp.maximum(total + m_st, m_kv)
            keep = jnp.exp(total + m_st - m_new)
            take = jnp.exp(m_kv - m_new)
            c_st = keep * c_st + take * kv_loc
            n_st = keep * n_st + take * kn_loc
            m_st = m_new
        c_ref[h] = c_st
        n_ref[h] = n_st
        m_ref[h] = m_st


def _mlstm_call(qk, ucm, valc, cumc, valr, cumr, d, prev=None):
    blk = _scan_block(d)
    tok = lambda col: pl.BlockSpec((TB, D), lambda b, s: (blk(b, s), col))
    small = pl.BlockSpec((TB, SMALL), lambda b, s: (blk(b, s), 0))
    rows = pl.BlockSpec((None, 64, TB), lambda b, s: (blk(b, s), 0, 0))
    extra = () if d == 0 else (prev,)
    return pl.pallas_call(
        functools.partial(_mlstm_kernel, d=d),
        grid=(BATCH, SEQ_BLOCKS),
        in_specs=[tok(0), tok(1), tok(2), small, small, rows, rows] + [tok(0)] * len(extra),
        out_specs=tok(0),
        out_shape=jax.ShapeDtypeStruct((N_TOK, D), BF16),
        scratch_shapes=[pltpu.VMEM((HEADS_M, DH_M, DH_M), F32), pltpu.VMEM((HEADS_M, 1, DH_M), F32),
                        pltpu.VMEM((HEADS_M, 1, 1), F32)],
        compiler_params=_cparams(2),
        name="mlstm_fwd" if d == 0 else "mlstm_bwd",
    )(qk, qk, ucm, valc, cumc, valr, cumr, *extra)


def _ssd_kernel(x_ref, b_ref, c_ref, acsc_ref, dtr_ref, acsr_ref, *rest, d):
    prev_ref = rest[0] if d == 1 else None
    o_ref, s_ref = rest[-2:]
    @pl.when(pl.program_id(1) == 0)
    def _():
        s_ref[...] = jnp.zeros_like(s_ref)

    t_i = lax.broadcasted_iota(jnp.int32, (SUB, SUB), 0)
    s_i = lax.broadcasted_iota(jnp.int32, (SUB, SUB), 1)
    mask = (s_i <= t_i) if d == 0 else (s_i >= t_i)
    lo = lax.broadcasted_iota(jnp.int32, (SUB, 2 * SSD_P), 1) < SSD_P
    lo1 = lax.broadcasted_iota(jnp.int32, (1, 2 * SSD_P), 1) < SSD_P
    zero_b = jnp.zeros((SUB, 2 * SSD_P), BF16)
    bcast = lambda col: jnp.broadcast_to(col, (SUB, SUB))

    base_row = jnp.zeros((1, SMALL), F32)
    for ci, c in enumerate((0, 1) if d == 0 else (1, 0)):
        r0 = c * SUB
        rows = slice(r0, r0 + SUB)
        if ci == 1:
            edge = SUB - 1 if d == 0 else SUB
            base_row = acsc_ref[edge:edge + 1, :]
        last = r0 + (SUB - 1 if d == 0 else 0)
        acs_sub = acsc_ref[rows, :]
        tot_row = acsc_ref[last:last + 1, :] - base_row
        cd_row = jnp.exp(tot_row)
        from_start = jnp.exp(acs_sub - base_row)
        for g in range(SSD_GROUPS):
            bg = b_ref[rows, g * SSD_N:(g + 1) * SSD_N]
            cg = c_ref[rows, g * SSD_N:(g + 1) * SSD_N]
            cb = lax.dot_general(cg, bg, (((1,), (1,)), ((), ())), preferred_element_type=F32)
            bg_t = jnp.transpose(bg.astype(F32)).astype(BF16)
            for j in range(2):
                tile = slice(g * 4 * SSD_P + j * 2 * SSD_P, g * 4 * SSD_P + (j + 1) * 2 * SSD_P)
                stile = slice(j * 2 * SSD_P, (j + 1) * 2 * SSD_P)
                ps = [SSD_HEADS + d * SSD_HEADS + g * 4 + j * 2 + k for k in (0, 1)]
                xt = x_ref[rows, tile]
                st = s_ref[g, :, stile]
                st_b = st.astype(BF16)
                intra, inter, to_state = [], [], []
                for p in ps:
                    cum_col = bcast(acs_sub[:, p:p + 1])
                    cum_row = acsr_ref[p:p + 1, rows]
                    dt_row = dtr_ref[p:p + 1, rows]
                    seg = jnp.where(mask, cum_col - (cum_row - jnp.log(dt_row)), -jnp.inf)
                    intra.append((cb * jnp.exp(seg)).astype(BF16))
                    inter.append(cg * bcast(from_start[:, p:p + 1]).astype(BF16))
                    w_row = dt_row * jnp.exp(tot_row[:, p:p + 1] - (cum_row - base_row[:, p:p + 1]))
                    to_state.append(bg_t * w_row.astype(BF16))
                x_pair = [jnp.where(lo, xt, zero_b), jnp.where(lo, zero_b, xt)]
                s_pair = [jnp.where(lo, st_b, zero_b), jnp.where(lo, zero_b, st_b)]
                lhs = jnp.concatenate(intra + inter, axis=1)
                y_out = _dot(lhs, jnp.concatenate(x_pair + s_pair, axis=0))
                if prev_ref is not None:
                    y_out = y_out + prev_ref[rows, tile].astype(F32)
                o_ref[rows, tile] = y_out.astype(o_ref.dtype)
                cd_sel = jnp.where(lo1, cd_row[:, ps[0]:ps[0] + 1], cd_row[:, ps[1]:ps[1] + 1])
                s_ref[g, :, stile] = cd_sel * st + _dot(jnp.concatenate(to_state, axis=1),
                                                        jnp.concatenate(x_pair, axis=0))


def _ssd_call(xbc, acsc, dtr, acsr, d, prev=None):
    blk = _scan_block(d)
    small = pl.BlockSpec((TB, SMALL), lambda b, s: (blk(b, s), 0))
    rows = pl.BlockSpec((None, 64, TB), lambda b, s: (blk(b, s), 0, 0))
    tok = pl.BlockSpec((TB, D), lambda b, s: (blk(b, s), 0))
    extra = () if d == 0 else (prev,)
    return pl.pallas_call(
        functools.partial(_ssd_kernel, d=d),
        grid=(BATCH, SEQ_BLOCKS),
        in_specs=[tok,
                  pl.BlockSpec((TB, 512), lambda b, s: (blk(b, s), 2)),
                  pl.BlockSpec((TB, 512), lambda b, s: (blk(b, s), 3)),
                  small, rows, rows] + [tok] * len(extra),
        out_specs=pl.BlockSpec((TB, D), lambda b, s: (blk(b, s), 0)),
        out_shape=jax.ShapeDtypeStruct((N_TOK, D), BF16),
        scratch_shapes=[pltpu.VMEM((SSD_GROUPS, SSD_N, 4 * SSD_P), F32)],
        compiler_params=_cparams(2),
        name="ssd_fwd" if d == 0 else "ssd_bwd",
    )(xbc, xbc, xbc, acsc, dtr, acsr, *extra)


def _pool_tables():
    masks = np.zeros((2, 4, TB, TB), np.float32)
    inv = np.zeros((2, 4, TB, 1), np.float32)
    for kind, length in ((0, GRID_W), (1, CTX)):
        for gi, w in enumerate(POOL_WINDOWS):
            for r in range(TB):
                base, c = (r // length) * length, r % length
                lo = min(max(c - w // 2, 0), length - 1)
                hi = min(max(c - w // 2 + w - 1, 0), length - 1)
                masks[kind, gi, r, base + lo:base + hi + 1] = 1.0
                inv[kind, gi, r, 0] = 1.0 / (hi - lo + 1)
    return jnp.asarray(masks, BF16), jnp.asarray(inv, F32)


ROUTE_FIELDS = 6


def _route_block(logits, ltri_ref, rt_ref, cnt_ref, cnt_scr):
    @pl.when(pl.program_id(0) == 0)
    def _():
        cnt_scr[...] = jnp.zeros_like(cnt_scr)

    lane = lax.broadcasted_iota(jnp.int32, logits.shape, 1).astype(F32)

    def top1(valid):
        v = jnp.max(jnp.where(valid, logits, -jnp.inf), axis=1, keepdims=True)
        i = jnp.min(jnp.where(valid & (logits == v), lane, float(SMALL)), axis=1, keepdims=True)
        return v, i

    is_g = lane < N_GROUPS_E
    gm, grp = top1(is_g)
    p_grp = 1.0 / jnp.sum(jnp.where(is_g, jnp.exp(logits - gm), 0.0), axis=1, keepdims=True)
    lo_lane = N_GROUPS_E + grp * EPG
    in_grp = (lane >= lo_lane) & (lane < lo_lane + EPG)
    v1, i1 = top1(in_grp)
    v2, i2 = top1(in_grp & (lane != i1))
    t = jnp.exp(v2 - v1)
    w1 = p_grp / (1.0 + t)
    w2 = p_grp * t / (1.0 + t)
    e1 = i1 - N_GROUPS_E
    e2 = i2 - N_GROUPS_E

    oh1 = lane == e1
    oh2 = lane == e2
    oh = jnp.where(oh1 | oh2, 1.0, 0.0)
    before = _dot(ltri_ref[...], oh.astype(BF16)) + cnt_scr[...]
    r1 = jnp.sum(jnp.where(oh1, before, 0.0), axis=1, keepdims=True)
    r2 = jnp.sum(jnp.where(oh2, before, 0.0), axis=1, keepdims=True)
    cnt_scr[...] = cnt_scr[...] + jnp.sum(oh, axis=0, keepdims=True)
    cnt_ref[...] = cnt_scr[...]
    rt = jnp.zeros(logits.shape, F32)
    for k, val in enumerate((e1, e2, w1, w2, r1, r2)):
        rt = jnp.where(lane == k, val, rt)
    rt_ref[...] = rt


def _merge_kernel(x_ref, pa_ref, o_ref, z_ref, mg0_ref, mg1_ref, mg2_ref, hs_ref, ys_ref, xs_ref,
                  pmask_ref, pinv_ref, poolw_ref, pscale_ref, mng_ref, dsk_ref, sng_ref, wbr_ref, wout_ref,
                  gate_ref, sh_ref, sc_ref, gffn_ref, wrh_ref, wrl_ref, brt_ref, ltri_ref,
                  xo_ref, h2_ref, rt_ref, cnt_ref, cnt_scr):
    parts = []
    for g in range(4):
        a_g = pa_ref[:, g * 256:(g + 1) * 256]
        pooled = _dot(pmask_ref[g], a_g) * pinv_ref[g] - a_g.astype(F32)
        parts.append(_dot(pooled.astype(BF16), poolw_ref[g]))
    pool = jnp.concatenate(parts, axis=1) * pscale_ref[...]

    hs = hs_ref[...].astype(F32)
    hn = jnp.concatenate([_rms(hs[:, h * DH_M:(h + 1) * DH_M]) for h in range(HEADS_M)], axis=1)
    ml = _sigmoid(o_ref[...].astype(F32)) * (hn * mng_ref[...])

    y = ys_ref[...].astype(F32) + dsk_ref[...] * xs_ref[...].astype(F32)
    sl = _rms(y * _silu(z_ref[...].astype(F32))) * sng_ref[...]

    acc = _sigmoid(mg0_ref[...].astype(F32)) * _dot(pool.astype(BF16), wbr_ref[0])
    acc = acc + _sigmoid(mg1_ref[...].astype(F32)) * _dot(ml.astype(BF16), wbr_ref[1])
    acc = acc + _sigmoid(mg2_ref[...].astype(F32)) * _dot(sl.astype(BF16), wbr_ref[2])
    xn = x_ref[...] + gate_ref[...] * _dot(acc.astype(BF16), wout_ref[...])
    xo_ref[...] = xn

    h2 = (_rms(xn) * gffn_ref[...]) * (1.0 + sc_ref[...]) + sh_ref[...]
    hi, lo = _split2(h2)
    h2f = hi.astype(F32)
    for cchunk in range(D // 128):
        h2_ref[pl.ds(cchunk, TB, stride=D // 128), :] = h2f[:, cchunk * 128:(cchunk + 1) * 128]
    logits = _dot(hi, wrh_ref[...]) + _dot(lo, wrh_ref[...]) + _dot(hi, wrl_ref[...]) + brt_ref[...]
    _route_block(logits, ltri_ref, rt_ref, cnt_ref, cnt_scr)


def _merge_call(x, u_po, u_tail, hs, ys, xbc, pmask, pinv, poolw, pscale, mng, dsk, sng, wbr, wout,
                mods3, gffn, wr_hi, wr_lo, br):
    ltri = jnp.asarray(np.tril(np.ones((TB, TB), np.float32), -1), BF16)
    kind = lambda j: jnp.where(j % SEQ_BLOCKS == 0, 1, 0)
    vec = _const_spec((1, D))
    return pl.pallas_call(
        _merge_kernel,
        grid=(N_BLOCKS,),
        in_specs=[_tok_spec(),
                  _tok_spec(D, 0), _tok_spec(D, 1), _tok_spec(D, 1),
                  _tok_spec(D, 3), _tok_spec(D, 4), _tok_spec(D, 5),
                  _tok_spec(), _tok_spec(),
                  _tok_spec(D, 0),
                  pl.BlockSpec((None, 4, TB, TB), lambda j: (kind(j), 0, 0, 0)),
                  pl.BlockSpec((None, 4, TB, 1), lambda j: (kind(j), 0, 0, 0)),
                  _const_spec((4, 256, 256)), vec, vec, vec, vec,
                  _const_spec((3, D, D)), _const_spec((D, D)),
                  _mod_spec(2), _mod_spec(3), _mod_spec(4), vec,
                  _const_spec((D, SMALL)), _const_spec((D, SMALL)), _const_spec((1, SMALL)),
                  _const_spec((TB, TB))],
        out_specs=[_tok_spec(), pl.BlockSpec((TB * ROW_TILES, 128), lambda j: (j, 0)), _tok_spec(SMALL),
                   _const_spec((1, SMALL))],
        out_shape=[jax.ShapeDtypeStruct((N_TOK, D), F32), jax.ShapeDtypeStruct((N_TOK * ROW_TILES, 128), F32),
                   jax.ShapeDtypeStruct((N_TOK, SMALL), F32), jax.ShapeDtypeStruct((1, SMALL), F32)],
        scratch_shapes=[pltpu.VMEM((1, SMALL), F32)],
        compiler_params=_cparams(1),
        name="branch_merge",
    )(x, u_po, u_po, u_tail, u_tail, u_tail, u_tail, hs, ys, xbc, pmask, pinv, poolw, pscale, mng, dsk, sng,
      wbr, wout, mods3, mods3, mods3, gffn, wr_hi, wr_lo, br, ltri)


GATHER_AHEAD = 2
GATHER_SLOTS = GATHER_AHEAD + 1
GATHER_DMA_QUEUES = 2


def _moe_kernel(be_ref, nu_ref, tok0_ref, tok1_ref, tokn_ref, h_hbm, wg_ref, wu_ref, wd_ref, o_ref,
                xbuf, wgb, wub, wdb, sem):
    i = pl.program_id(0)
    n_used = nu_ref[0]
    slot = i % GATHER_SLOTS

    def start_gather(tok_ref, s):
        for r in range(MOE_BLOCK):
            row0 = pl.multiple_of(tok_ref[0, r] * ROW_TILES, ROW_TILES)
            pltpu.make_async_copy(h_hbm.at[pl.ds(row0, ROW_TILES)],
                                  xbuf.at[s, pl.ds(r * ROW_TILES, ROW_TILES)],
                                  sem.at[s]).start(priority=r % GATHER_DMA_QUEUES)

    def wait_gather(s):
        pltpu.make_async_copy(xbuf.at[s], xbuf.at[s], sem.at[s]).wait()

    @pl.when((i == 0) & (n_used > 0))
    def _():
        start_gather(tok0_ref, 0)
        start_gather(tok1_ref, 1)

    @pl.when((i < n_used) & ((i == 0) | (be_ref[i] != be_ref[jnp.maximum(i - 1, 0)])))
    def _():
        wgb[...] = wg_ref[...].astype(BF16)
        wub[...] = wu_ref[...].astype(BF16)
        wdb[...] = wd_ref[...].astype(BF16)

    @pl.when(i < n_used)
    def _():
        wait_gather(slot)
        x = jnp.concatenate([xbuf[slot, pl.ds(c, MOE_BLOCK, stride=ROW_TILES), :] for c in range(ROW_TILES)],
                            axis=1).astype(BF16)
        start_gather(tokn_ref, (i + GATHER_AHEAD) % GATHER_SLOTS)
        gt = _dot(x, wgb[...])
        up = _dot(x, wub[...])
        act = (_silu(gt) * up).astype(BF16)
        o_ref[...] = _dot(act, wdb[...]).astype(o_ref.dtype)

    @pl.when(i == n_used - 1)
    def _():
        for ahead in range(1, GATHER_AHEAD + 1):
            wait_gather((i + ahead) % GATHER_SLOTS)

    @pl.when(i >= n_used)
    def _():
        o_ref[...] = jnp.zeros_like(o_ref)


MOE_CAP = N_TOK * 2 + N_EXPERTS * MOE_BLOCK
MOE_NBLOCKS = MOE_CAP // MOE_BLOCK


def _moe_call(layer, block_expert, n_used, buf_tok, h2, w_gate, w_up, w_down):
    grid_spec = pltpu.PrefetchScalarGridSpec(
        num_scalar_prefetch=2,
        grid=(MOE_NBLOCKS,),
        in_specs=[pl.BlockSpec((None, 1, MOE_BLOCK), lambda i, be, nu: (0, 0, 0), memory_space=pltpu.SMEM),
                  pl.BlockSpec((None, 1, MOE_BLOCK), lambda i, be, nu: (1, 0, 0), memory_space=pltpu.SMEM),
                  pl.BlockSpec((None, 1, MOE_BLOCK),
                               lambda i, be, nu: (jnp.minimum(i + GATHER_AHEAD, MOE_NBLOCKS - 1), 0, 0),
                               memory_space=pltpu.SMEM),
                  pl.BlockSpec(memory_space=pl.ANY),
                  pl.BlockSpec((None, None, D, D_EXPERT), lambda i, be, nu: (layer, be[i], 0, 0)),
                  pl.BlockSpec((None, None, D, D_EXPERT), lambda i, be, nu: (layer, be[i], 0, 0)),
                  pl.BlockSpec((None, None, D_EXPERT, D), lambda i, be, nu: (layer, be[i], 0, 0))],
        out_specs=pl.BlockSpec((MOE_BLOCK, D), lambda i, be, nu: (i, 0)),
        scratch_shapes=[pltpu.VMEM((GATHER_SLOTS, MOE_BLOCK * ROW_TILES, 128), F32),
                        pltpu.VMEM((D, D_EXPERT), BF16), pltpu.VMEM((D, D_EXPERT), BF16),
                        pltpu.VMEM((D_EXPERT, D), BF16), pltpu.SemaphoreType.DMA((GATHER_SLOTS,))],
    )
    return pl.pallas_call(
        _moe_kernel,
        grid_spec=grid_spec,
        out_shape=jax.ShapeDtypeStruct((MOE_CAP, D), BF16),
        compiler_params=_cparams(1),
        name="moe_experts",
    )(block_expert, n_used, buf_tok, buf_tok, buf_tok, h2, w_gate, w_up, w_down)


def _dispatch_plan(route, counts):
    cnt = counts[0, :N_EXPERTS].astype(jnp.int32)
    padded = (cnt + MOE_BLOCK - 1) // MOE_BLOCK * MOE_BLOCK
    pends = jnp.cumsum(padded)
    pstarts = pends - padded
    expert = route[:, 0:2].astype(jnp.int32)
    rank = route[:, 4:6].astype(jnp.int32)
    onehot = expert[:, :, None] == jnp.arange(N_EXPERTS, dtype=jnp.int32)
    pos = jnp.sum(jnp.where(onehot, pstarts, 0), axis=-1) + rank
    tok = jnp.broadcast_to(jnp.arange(N_TOK, dtype=jnp.int32)[:, None], (N_TOK, 2))
    buf_tok = jnp.zeros((MOE_CAP,), jnp.int32).at[pos.reshape(-1)].set(tok.reshape(-1))
    buf_tok = buf_tok.reshape(MOE_NBLOCKS, 1, MOE_BLOCK)
    block_start = jnp.arange(MOE_NBLOCKS, dtype=jnp.int32) * MOE_BLOCK
    block_expert = jnp.minimum(jnp.sum((pends[None, :] <= block_start[:, None]).astype(jnp.int32), axis=1),
                               N_EXPERTS - 1)
    n_used = (pends[-1] // MOE_BLOCK).reshape(1)
    return buf_tok, block_expert, n_used, pos


def _to_colmajor(t):
    c = t.shape[-1]
    t = t.reshape(BATCH, T_BATCH, c)
    lat = t[:, CTX:].reshape(BATCH, SEQ // GRID_W, GRID_W, c).swapaxes(1, 2).reshape(BATCH, SEQ, c)
    return jnp.concatenate([t[:, :CTX], lat], axis=1).reshape(N_TOK, c)


def _to_rowmajor(t):
    c = t.shape[-1]
    t = t.reshape(BATCH, T_BATCH, c)
    lat = t[:, CTX:].reshape(BATCH, GRID_W, SEQ // GRID_W, c).swapaxes(1, 2).reshape(BATCH, SEQ, c)
    return jnp.concatenate([t[:, :CTX], lat], axis=1).reshape(N_TOK, c)


def _lanes(*pieces):
    v = jnp.concatenate([jnp.asarray(p, F32).reshape(-1) for p in pieces])
    return jnp.pad(v, (0, SMALL - v.shape[0]))


def kernel(x, c, ctx, c_ctx, w_ada, b_ada, g_norm_mix, g_norm_ffn, w_in, pool_w, pool_scale, mlstm_conv_w,
           mlstm_conv_b, mlstm_gate_b, mlstm_norm_g, ssd_conv_w, ssd_conv_b, ssd_dt_bias, ssd_a_log, ssd_d,
           ssd_norm_g, w_branch, w_out, w_route_group, b_route_group, w_route_expert, b_route_expert,
           w_exp_gate, w_exp_up, w_exp_down, g_final):
    cvec = jnp.concatenate([c, c_ctx[None], jnp.zeros((3, D), F32)], axis=0)
    mods = _ada_table(cvec, w_ada, b_ada)
    xs = jnp.concatenate([ctx, x], axis=1).reshape(N_TOK, D)
    pmask, pinv = _pool_tables()
    row = lambda v: v.reshape(1, -1).astype(F32)
    w_in_t = jnp.swapaxes(w_in, 1, 2)

    def small_weight(layer):
        rows_ = jnp.concatenate([w_in_t[layer, 5 * D:5 * D + 16], w_in_t[layer, 8 * D + 16:8 * D + 48]], axis=0)
        return jnp.pad(rows_.T, ((0, 0), (0, SMALL - 48)))

    h = small = None
    out = None
    for l in range(DEPTH):
        mods3 = mods[l].reshape(8, 1, 6 * D)
        sp = np.cumsum([0, D, D, D, D, D, 16, D, D, 512, 512, 32, 3 * D])
        w_tail = jnp.concatenate([w_in_t[l, sp[6]:sp[10]], w_in_t[l, sp[11]:sp[12]]], axis=0).astype(BF16)
        w_small = small_weight(l)
        ws_hi = w_small.astype(BF16)
        ws_lo = (w_small - ws_hi.astype(F32)).astype(BF16)

        if l == 0:
            h, small = _norm_call(xs, row(g_norm_mix[l]), mods3, ws_hi, ws_lo)

        h_cm = _to_colmajor(h)
        small_cm = _to_colmajor(small)
        u_po = _mm_f32w_call(h, w_in_t, l, lambda j: 4 * j, 2, "in_proj_pool_o")
        u_tail = _mm_call(h, w_tail, "in_proj_rowmajor")
        u_cm = _mm_f32w_call(h_cm, w_in_t, l, lambda j: 1 + j, 3, "in_proj_colmajor")

        post_m = jnp.concatenate([jnp.ones((D,), F32), jnp.full((D,), DH_M ** -0.5, F32)]).reshape(1, 2 * D)
        qk = _conv_call(u_cm, lambda c: c, 2 * D, mlstm_conv_w[l], row(mlstm_conv_b[l]), post_m, "mlstm_conv")
        gate_bias = _lanes(mlstm_gate_b[l])
        gate_bwd = _lanes(jnp.zeros((12,)), jnp.ones((4,)))
        g_valc, g_cumc, g_valr, g_cumr = _gate_call(small_cm, gate_bias, jnp.ones((SMALL,), F32), gate_bwd,
                                                    False, "mlstm_gates")
        h_f = _mlstm_call(qk, u_cm, g_valc, g_cumc, g_valr, g_cumr, 0)
        h_fb = _mlstm_call(qk, u_cm, g_valc, g_cumc, g_valr, g_cumr, 1, prev=h_f)
        h_fb = _to_rowmajor(h_fb)

        xbc = _conv_call(u_tail, lambda c: c + jnp.where(c >= 2, 2, 0), 2 * D, ssd_conv_w[l],
                         row(ssd_conv_b[l]), jnp.ones((1, 2 * D), F32), "ssd_conv")
        a_neg = -jnp.exp(ssd_a_log[l].astype(F32))
        dt_bias = _lanes(jnp.zeros((16,)), ssd_dt_bias[l])
        dt_scale = _lanes(jnp.zeros((16,)), a_neg)
        dt_bwd = _lanes(jnp.zeros((32,)), jnp.ones((16,)))
        _, s_acsc, s_dtr, s_acsr = _gate_call(small, dt_bias, dt_scale, dt_bwd, True, "ssd_gates")
        y_f = _ssd_call(xbc, s_acsc, s_dtr, s_acsr, 0)
        y_fb = _ssd_call(xbc, s_acsc, s_dtr, s_acsr, 1, prev=y_f)

        w_r = jnp.pad(jnp.concatenate([w_route_group[l], w_route_expert[l]], axis=1),
                      ((0, 0), (0, SMALL - N_GROUPS_E - N_EXPERTS)))
        wr_hi = w_r.astype(BF16)
        wr_lo = (w_r - wr_hi.astype(F32)).astype(BF16)
        b_r = _lanes(b_route_group[l], b_route_expert[l]).reshape(1, SMALL)
        dsk = jnp.repeat(ssd_d[l].astype(F32), SSD_P).reshape(1, D)
        xs, h2, route, counts = _merge_call(
            xs, u_po, u_tail, h_fb, y_fb, xbc, pmask, pinv, pool_w[l].astype(BF16), row(pool_scale[l]),
            row(mlstm_norm_g[l]), dsk, row(ssd_norm_g[l]), w_branch[l].astype(BF16), w_out[l].astype(BF16),
            mods3, row(g_norm_ffn[l]), wr_hi, wr_lo, b_r)

        buf_tok, block_expert, n_used, pos = _dispatch_plan(route, counts)
        yb = _moe_call(l, block_expert, n_used, buf_tok, h2, w_exp_gate, w_exp_up, w_exp_down)
        y0 = yb[pos[:, 0]]
        y1 = yb[pos[:, 1]]

        if l + 1 < DEPTH:
            w_small_n = small_weight(l + 1)
            wsn_hi = w_small_n.astype(BF16)
            wsn_lo = (w_small_n - wsn_hi.astype(F32)).astype(BF16)
            mods3_n = mods[l + 1].reshape(8, 1, 6 * D)
            xs, h, small = _resnorm_call(xs, y0, y1, route, mods3, row(g_norm_mix[l + 1]), mods3_n,
                                         wsn_hi, wsn_lo)
        else:
            out = _final_call(xs, y0, y1, route, mods3, row(g_final))
    return out.reshape(BATCH, SEQ, D)
```

```python
import functools
import math

import jax
import jax.numpy as jnp
import numpy as np
from jax import lax
from jax.experimental import pallas as pl
from jax.experimental.pallas import tpu as pltpu

F32 = jnp.float32
BF16 = jnp.bfloat16

D = 1024
BATCH = 4
SEQ = 4096
CTX = 256
DEPTH = 2
GRID_W = 64
EPS = 1e-6

TB = 256
SEQ_BLOCKS = (SEQ + CTX) // TB
T_BATCH = SEQ + CTX
N_TOK = BATCH * T_BATCH
N_BLOCKS = N_TOK // TB

POOL_WINDOWS = (2, 4, 8, 16)
HEADS_M = 4
DH_M = 256
SSD_HEADS = 16
SSD_P = 64
SSD_GROUPS = 4
SSD_N = 128
N_GROUPS_E = 4
EPG = 8
N_EXPERTS = 32
D_EXPERT = 512
MOE_BLOCK = 256
SMALL = 128
ROW_TILES = D // 128

VMEM_LIMIT = 56 * 1024 * 1024


def _cparams(n_axes):
    return pltpu.CompilerParams(dimension_semantics=("arbitrary",) * n_axes,
                                vmem_limit_bytes=VMEM_LIMIT)


def _dot(a, b):
    return jnp.dot(a, b, preferred_element_type=F32)


def _split2(x):
    hi = x.astype(BF16)
    lo = (x - hi.astype(F32)).astype(BF16)
    return hi, lo


def _split3(x):
    hi = x.astype(BF16)
    r = x - hi.astype(F32)
    mid = r.astype(BF16)
    lo = (r - mid.astype(F32)).astype(BF16)
    return hi, mid, lo


def _sigmoid(x):
    return 0.5 * jnp.tanh(0.5 * x) + 0.5


def _silu(x):
    return x * _sigmoid(x)


def _log1pexp_negabs(x):
    return jnp.log(1.0 + jnp.exp(-jnp.abs(x)))


def _log_sigmoid(x):
    return jnp.minimum(x, 0.0) - _log1pexp_negabs(x)


def _softplus(x):
    return jnp.maximum(x, 0.0) + _log1pexp_negabs(x)


def _rms(x):
    return x * lax.rsqrt(jnp.mean(x * x, axis=-1, keepdims=True) + EPS)


def _mod_row(j):
    return jnp.where(j % SEQ_BLOCKS == 0, BATCH, j // SEQ_BLOCKS)


def _ada_kernel(c_ref, w_ref, b_ref, o_ref):
    c = c_ref[...]
    s_hi, s_lo = _split2(_silu(c))
    w_hi, w_lo = _split2(w_ref[...])
    o_ref[...] = _dot(s_hi, w_hi) + _dot(s_lo, w_hi) + _dot(s_hi, w_lo) + b_ref[...]


def _ada_table(cvec, w_ada, b_ada):
    tn = 1024
    return pl.pallas_call(
        _ada_kernel,
        grid=(DEPTH, 6 * D // tn),
        in_specs=[pl.BlockSpec((8, D), lambda l, j: (0, 0)),
                  pl.BlockSpec((None, D, tn), lambda l, j: (l, 0, j)),
                  pl.BlockSpec((None, 1, tn), lambda l, j: (l, 0, j))],
        out_specs=pl.BlockSpec((None, 8, tn), lambda l, j: (l, 0, j)),
        out_shape=jax.ShapeDtypeStruct((DEPTH, 8, 6 * D), F32),
        compiler_params=_cparams(2),
        name="ada_table",
    )(cvec, w_ada, b_ada.reshape(DEPTH, 1, 6 * D))


def _norm_mod_small(x, g_ref, sh_ref, sc_ref, wsh_ref, wsl_ref, h_ref, small_ref):
    h = _rms(x) * g_ref[...]
    h = h * (1.0 + sc_ref[...]) + sh_ref[...]
    h_hi, h_lo = _split2(h)
    h_ref[...] = h_hi
    small_ref[...] = _dot(h_hi, wsh_ref[...]) + _dot(h_lo, wsh_ref[...]) + _dot(h_hi, wsl_ref[...])


def _norm_kernel(x_ref, g_ref, sh_ref, sc_ref, wsh_ref, wsl_ref, h_ref, small_ref):
    _norm_mod_small(x_ref[...], g_ref, sh_ref, sc_ref, wsh_ref, wsl_ref, h_ref, small_ref)


def _moe_residual(x_ref, ya_ref, yb_ref, rt_ref, gate_ref):
    y = rt_ref[:, 2:3] * ya_ref[...].astype(F32) + rt_ref[:, 3:4] * yb_ref[...].astype(F32)
    return x_ref[...] + gate_ref[...] * y


def _resnorm_kernel(x_ref, ya_ref, yb_ref, rt_ref, gate_ref, g_ref, sh_ref, sc_ref, wsh_ref, wsl_ref,
                    xo_ref, h_ref, small_ref):
    x = _moe_residual(x_ref, ya_ref, yb_ref, rt_ref, gate_ref)
    xo_ref[...] = x
    _norm_mod_small(x, g_ref, sh_ref, sc_ref, wsh_ref, wsl_ref, h_ref, small_ref)


def _mod_spec(chunk):
    return pl.BlockSpec((None, 1, D), lambda j: (_mod_row(j), 0, chunk))


def _tok_spec(width=D, col=0):
    return pl.BlockSpec((TB, width), lambda j: (j, col))


def _const_spec(shape):
    nd = len(shape)
    return pl.BlockSpec(shape, lambda j: (0,) * nd)


def _norm_call(x, g, mods3, ws_hi, ws_lo):
    return pl.pallas_call(
        _norm_kernel,
        grid=(N_BLOCKS,),
        in_specs=[_tok_spec(), _const_spec((1, D)), _mod_spec(0), _mod_spec(1),
                  _const_spec((D, SMALL)), _const_spec((D, SMALL))],
        out_specs=[_tok_spec(), _tok_spec(SMALL)],
        out_shape=[jax.ShapeDtypeStruct((N_TOK, D), BF16), jax.ShapeDtypeStruct((N_TOK, SMALL), F32)],
        compiler_params=_cparams(1),
        name="norm_mod",
    )(x, g, mods3, mods3, ws_hi, ws_lo)


def _resnorm_call(x, ya, yb, route, mods3_prev, g, mods3, ws_hi, ws_lo):
    return pl.pallas_call(
        _resnorm_kernel,
        grid=(N_BLOCKS,),
        in_specs=[_tok_spec(), _tok_spec(), _tok_spec(), _tok_spec(SMALL),
                  pl.BlockSpec((None, 1, D), lambda j: (_mod_row(j), 0, 5)),
                  _const_spec((1, D)), _mod_spec(0), _mod_spec(1),
                  _const_spec((D, SMALL)), _const_spec((D, SMALL))],
        out_specs=[_tok_spec(), _tok_spec(), _tok_spec(SMALL)],
        out_shape=[jax.ShapeDtypeStruct((N_TOK, D), F32), jax.ShapeDtypeStruct((N_TOK, D), BF16),
                   jax.ShapeDtypeStruct((N_TOK, SMALL), F32)],
        compiler_params=_cparams(1),
        name="residual_norm_mod",
    )(x, ya, yb, route, mods3_prev, g, mods3, mods3, ws_hi, ws_lo)


def _final_kernel(x_ref, ya_ref, yb_ref, rt_ref, gate_ref, g_ref, o_ref):
    o_ref[...] = _rms(_moe_residual(x_ref, ya_ref, yb_ref, rt_ref, gate_ref)) * g_ref[...]


def _final_call(x, ya, yb, route, mods3, g_final):
    lat = lambda b, s: (b * SEQ_BLOCKS + 1 + s, 0)
    spec = pl.BlockSpec((TB, D), lat)
    return pl.pallas_call(
        _final_kernel,
        grid=(BATCH, SEQ // TB),
        in_specs=[spec, spec, spec, pl.BlockSpec((TB, SMALL), lat),
                  pl.BlockSpec((None, 1, D), lambda b, s: (b, 0, 5)),
                  pl.BlockSpec((1, D), lambda b, s: (0, 0))],
        out_specs=pl.BlockSpec((TB, D), lambda b, s: (b * (SEQ // TB) + s, 0)),
        out_shape=jax.ShapeDtypeStruct((BATCH * SEQ, D), F32),
        compiler_params=_cparams(2),
        name="final_norm",
    )(x, ya, yb, route, mods3, g_final)


def _dot_nt(a, bt):
    return lax.dot_general(a, bt, (((1,), (1,)), ((), ())), preferred_element_type=F32)


def _mm_kernel(h_ref, wt_ref, o_ref):
    o_ref[...] = _dot_nt(h_ref[...], wt_ref[...]).astype(o_ref.dtype)


MM_TM, MM_TN = N_TOK // 8, 1024


def _mm_call(h, wt, name):
    n = wt.shape[0]
    tm, tn = MM_TM, MM_TN
    return pl.pallas_call(
        _mm_kernel,
        grid=(n // tn, N_TOK // tm),
        in_specs=[pl.BlockSpec((tm, D), lambda j, i: (i, 0)),
                  pl.BlockSpec((tn, D), lambda j, i: (j, 0))],
        out_specs=pl.BlockSpec((tm, tn), lambda j, i: (i, j)),
        out_shape=jax.ShapeDtypeStruct((N_TOK, n), BF16),
        compiler_params=_cparams(2),
        name=name,
    )(h, wt)


def _mm_f32w_kernel(h_ref, wt_ref, o_ref, wb_ref):
    @pl.when(pl.program_id(1) == 0)
    def _():
        wb_ref[...] = wt_ref[...].astype(BF16)

    o_ref[...] = _dot_nt(h_ref[...], wb_ref[...]).astype(o_ref.dtype)


def _mm_f32w_call(h, wt_all, layer, row_block, n_tiles, name):
    tm, tn = MM_TM, MM_TN
    return pl.pallas_call(
        _mm_f32w_kernel,
        grid=(n_tiles, N_TOK // tm),
        in_specs=[pl.BlockSpec((tm, D), lambda j, i: (i, 0)),
                  pl.BlockSpec((None, tn, D), lambda j, i: (layer, row_block(j), 0))],
        out_specs=pl.BlockSpec((tm, tn), lambda j, i: (i, j)),
        out_shape=jax.ShapeDtypeStruct((N_TOK, n_tiles * tn), BF16),
        scratch_shapes=[pltpu.VMEM((tn, D), BF16)],
        compiler_params=_cparams(2),
        name=name,
    )(h, wt_all)


CONV_HALO = 16


def _conv_kernel(x_ref, w_ref, b_ref, post_ref, o_ref):
    w = w_ref[...]
    bias = b_ref[...]
    post = post_ref[...]
    row = lax.broadcasted_iota(jnp.int32, (TB, 1), 0)
    for k in range(SEQ_BLOCKS):
        r0 = k * TB
        lo = max(r0 - CONV_HALO, 0)
        hi = min(r0 + TB + CONV_HALO, T_BATCH)
        off = r0 - lo
        n = hi - lo
        ext = x_ref[lo:hi, :].astype(F32)
        xm2 = pltpu.roll(ext, 2, 0)[off:off + TB]
        xm1 = pltpu.roll(ext, 1, 0)[off:off + TB]
        x0 = ext[off:off + TB]
        xp1 = pltpu.roll(ext, n - 1, 0)[off:off + TB]
        if k in (0, 1):
            xm2 = jnp.where(row >= 2, xm2, 0.0)
            xm1 = jnp.where(row >= 1, xm1, 0.0)
        if k in (0, SEQ_BLOCKS - 1):
            xp1 = jnp.where(row <= TB - 2, xp1, 0.0)
        y = bias + w[0:1] * xm2 + w[1:2] * xm1 + w[2:3] * x0 + w[3:4] * xp1
        o_ref[r0:r0 + TB, :] = (_silu(y) * post).astype(o_ref.dtype)


def _conv_call(u, in_tile, width, w, b, post, name):
    tc = 512
    return pl.pallas_call(
        _conv_kernel,
        grid=(BATCH, width // tc),
        in_specs=[pl.BlockSpec((T_BATCH, tc), lambda bi, c: (bi, in_tile(c))),
                  pl.BlockSpec((4, tc), lambda bi, c: (0, c)),
                  pl.BlockSpec((1, tc), lambda bi, c: (0, c)),
                  pl.BlockSpec((1, tc), lambda bi, c: (0, c))],
        out_specs=pl.BlockSpec((T_BATCH, tc), lambda bi, c: (bi, c)),
        out_shape=jax.ShapeDtypeStruct((N_TOK, width), BF16),
        compiler_params=_cparams(2),
        name=name,
    )(u, w, b, post)


def _tri_dot_cols(tri, x):
    a, b, c = _split3(x)
    return _dot(tri, a) + _dot(tri, b) + _dot(tri, c)


def _tri_dot_rows(x, tri):
    a, b, c = _split3(x)
    return _dot(a, tri) + _dot(b, tri) + _dot(c, tri)


GATE_GROUP = 4


def _gate_kernel(pre_ref, pret_ref, bc_ref, br_ref, ac_ref, ar_ref, bwdc_ref, bwdr_ref, tril_ref, triu_ref,
                 valc_ref, cumc_ref, valr_ref, cumr_ref, *, ssd):
    tril = tril_ref[...]
    triu = triu_ref[...]

    def act(v):
        if ssd:
            val = _softplus(v)
            return val, val
        return v, _log_sigmoid(v)

    for k in range(GATE_GROUP):
        rows = slice(k * TB, (k + 1) * TB)
        vc, dc = act(pre_ref[rows, :] + bc_ref[...])
        dc = dc * ac_ref[...]
        valc_ref[rows, :] = vc
        cumc_ref[rows, :] = jnp.where(bwdc_ref[...] > 0.5, _tri_dot_cols(triu, dc), _tri_dot_cols(tril, dc))

        vr, dr = act(pret_ref[k] + br_ref[...])
        dr = dr * ar_ref[...]
        valr_ref[k] = vr
        cumr_ref[k] = jnp.where(bwdr_ref[...] > 0.5, _tri_dot_rows(dr, tril), _tri_dot_rows(dr, triu))


def _gate_call(small, bias, scale, bwd, ssd, name):
    small_t = small.reshape(N_BLOCKS, TB, SMALL)[:, :, :64].transpose(0, 2, 1)
    tri = np.tril(np.ones((TB, TB), np.float32))
    tril = jnp.asarray(tri, BF16)
    triu = jnp.asarray(tri.T, BF16)
    col = lambda v: v.reshape(1, SMALL).astype(F32)
    rowv = lambda v: v[:64].reshape(64, 1).astype(F32)
    cs = lambda shape: pl.BlockSpec(shape, lambda j: (0,) * len(shape))
    cols = pl.BlockSpec((GATE_GROUP * TB, SMALL), lambda j: (j, 0))
    rows = pl.BlockSpec((GATE_GROUP, 64, TB), lambda j: (j, 0, 0))
    return pl.pallas_call(
        functools.partial(_gate_kernel, ssd=ssd),
        grid=(N_BLOCKS // GATE_GROUP,),
        in_specs=[cols, rows,
                  cs((1, SMALL)), cs((64, 1)), cs((1, SMALL)), cs((64, 1)), cs((1, SMALL)), cs((64, 1)),
                  cs((TB, TB)), cs((TB, TB))],
        out_specs=[cols, cols, rows, rows],
        out_shape=[jax.ShapeDtypeStruct((N_TOK, SMALL), F32), jax.ShapeDtypeStruct((N_TOK, SMALL), F32),
                   jax.ShapeDtypeStruct((N_BLOCKS, 64, TB), F32), jax.ShapeDtypeStruct((N_BLOCKS, 64, TB), F32)],
        compiler_params=_cparams(1),
        name=name,
    )(small, small_t, col(bias), rowv(bias), col(scale), rowv(scale), col(bwd), rowv(bwd), tril, triu)


def _scan_block(d):
    if d == 0:
        return lambda b, s: b * SEQ_BLOCKS + s
    return lambda b, s: b * SEQ_BLOCKS + jnp.where(s == 0, 0, SEQ_BLOCKS - s)


def _causal_mask(d):
    t = lax.broadcasted_iota(jnp.int32, (TB, TB), 0)
    s = lax.broadcasted_iota(jnp.int32, (TB, TB), 1)
    return (s <= t) if d == 0 else (s >= t)


SUB = 128


def _mlstm_kernel(q_ref, k_ref, v_ref, valc_ref, cumc_ref, valr_ref, cumr_ref, *rest, d):
    prev_ref = rest[0] if d == 1 else None
    o_ref, c_ref, n_ref, m_ref = rest[-4:]
    @pl.when(pl.program_id(1) == 0)
    def _():
        c_ref[...] = jnp.zeros_like(c_ref)
        n_ref[...] = jnp.zeros_like(n_ref)
        m_ref[...] = jnp.zeros_like(m_ref)

    t_i = lax.broadcasted_iota(jnp.int32, (SUB, SUB), 0)
    s_i = lax.broadcasted_iota(jnp.int32, (SUB, SUB), 1)
    mask = (s_i <= t_i) if d == 0 else (s_i >= t_i)
    bcast = lambda col: jnp.broadcast_to(col, (SUB, SUB))
    rep = lambda f: jnp.concatenate([f] * (DH_M // SUB), axis=1)
    order = (0, 1) if d == 0 else (1, 0)

    for h in range(HEADS_M):
        p = d * HEADS_M + h
        cs = slice(h * DH_M, (h + 1) * DH_M)
        base = jnp.zeros((1, 1), F32)
        c_st = c_ref[h]
        n_st = n_ref[h]
        m_st = m_ref[h]
        for ci, c in enumerate(order):
            r0 = c * SUB
            rows = slice(r0, r0 + SUB)
            if ci == 1:
                edge = SUB - 1 if d == 0 else SUB
                base = cumr_ref[8 + p:9 + p, edge:edge + 1]
            last = r0 + (SUB - 1 if d == 0 else 0)
            q = q_ref[rows, cs]
            k = k_ref[rows, cs]
            v = v_ref[rows, cs]
            cum_c = bcast(cumc_ref[rows, 8 + p:9 + p]) - base
            i_c = bcast(valc_ref[rows, p:p + 1])
            cum_r = cumr_ref[8 + p:9 + p, rows] - base
            i_r = valr_ref[p:p + 1, rows]
            total = cumr_ref[8 + p:9 + p, last:last + 1] - base

            dmat = jnp.where(mask, cum_c - cum_r + i_r, -jnp.inf)
            m_loc = jnp.max(dmat, axis=1, keepdims=True)
            a = jnp.exp(dmat - m_loc) * lax.dot_general(q, k, (((1,), (1,)), ((), ())),
                                                        preferred_element_type=F32)
            num_loc = _dot(a.astype(BF16), v)
            den_loc = jnp.sum(a, axis=1, keepdims=True)
            wl_log = total - cum_c + i_c
            m_kv = jnp.max(wl_log, axis=0, keepdims=True)[:, 0:1]
            wl = rep(jnp.exp(wl_log - m_kv))
            kv_loc = lax.dot_general(k, v * wl.astype(BF16), (((0,), (0,)), ((), ())),
                                     preferred_element_type=F32)
            kn_loc = jnp.sum(wl * k.astype(F32), axis=0, keepdims=True)

            g = cum_c + m_st
            m_t = jnp.maximum(g, m_loc)
            f_loc = jnp.exp(m_loc - m_t)
            f_st = jnp.exp(g - m_t)
            num = rep(f_loc) * num_loc + rep(f_st) * _dot(q, c_st.astype(BF16))
            den = f_loc * den_loc + f_st * jnp.sum(q.astype(F32) * n_st, axis=1, keepdims=True)
            inv = 1.0 / jnp.maximum(jnp.abs(den), jnp.exp(-m_t))
            h_out = num * rep(inv)
            if prev_ref is not None:
                h_out = h_out + prev_ref[rows, cs].astype(F32)
            o_ref[rows, cs] = h_out.astype(o_ref.dtype)

            m_new = jnp.maximum(total + m_st, m_kv)
            keep = jnp.exp(total + m_st - m_new)
            take = jnp.exp(m_kv - m_new)
            c_st = keep * c_st + take * kv_loc
            n_st = keep * n_st + take * kn_loc
            m_st = m_new
        c_ref[h] = c_st
        n_ref[h] = n_st
        m_ref[h] = m_st


def _mlstm_call(qk, ucm, valc, cumc, valr, cumr, d, prev=None):
    blk = _scan_block(d)
    tok = lambda col: pl.BlockSpec((TB, D), lambda b, s: (blk(b, s), col))
    small = pl.BlockSpec((TB, SMALL), lambda b, s: (blk(b, s), 0))
    rows = pl.BlockSpec((None, 64, TB), lambda b, s: (blk(b, s), 0, 0))
    extra = () if d == 0 else (prev,)
    return pl.pallas_call(
        functools.partial(_mlstm_kernel, d=d),
        grid=(BATCH, SEQ_BLOCKS),
        in_specs=[tok(0), tok(1), tok(2), small, small, rows, rows] + [tok(0)] * len(extra),
        out_specs=tok(0),
        out_shape=jax.ShapeDtypeStruct((N_TOK, D), BF16),
        scratch_shapes=[pltpu.VMEM((HEADS_M, DH_M, DH_M), F32), pltpu.VMEM((HEADS_M, 1, DH_M), F32),
                        pltpu.VMEM((HEADS_M, 1, 1), F32)],
        compiler_params=_cparams(2),
        name="mlstm_fwd" if d == 0 else "mlstm_bwd",
    )(qk, qk, ucm, valc, cumc, valr, cumr, *extra)


def _ssd_kernel(x_ref, b_ref, c_ref, acsc_ref, dtr_ref, acsr_ref, *rest, d):
    prev_ref = rest[0] if d == 1 else None
    o_ref, s_ref = rest[-2:]
    @pl.when(pl.program_id(1) == 0)
    def _():
        s_ref[...] = jnp.zeros_like(s_ref)

    t_i = lax.broadcasted_iota(jnp.int32, (SUB, SUB), 0)
    s_i = lax.broadcasted_iota(jnp.int32, (SUB, SUB), 1)
    mask = (s_i <= t_i) if d == 0 else (s_i >= t_i)
    lo = lax.broadcasted_iota(jnp.int32, (SUB, 2 * SSD_P), 1) < SSD_P
    lo1 = lax.broadcasted_iota(jnp.int32, (1, 2 * SSD_P), 1) < SSD_P
    zero_b = jnp.zeros((SUB, 2 * SSD_P), BF16)
    bcast = lambda col: jnp.broadcast_to(col, (SUB, SUB))

    base_row = jnp.zeros((1, SMALL), F32)
    for ci, c in enumerate((0, 1) if d == 0 else (1, 0)):
        r0 = c * SUB
        rows = slice(r0, r0 + SUB)
        if ci == 1:
            edge = SUB - 1 if d == 0 else SUB
            base_row = acsc_ref[edge:edge + 1, :]
        last = r0 + (SUB - 1 if d == 0 else 0)
        acs_sub = acsc_ref[rows, :]
        tot_row = acsc_ref[last:last + 1, :] - base_row
        cd_row = jnp.exp(tot_row)
        from_start = jnp.exp(acs_sub - base_row)
        for g in range(SSD_GROUPS):
            bg = b_ref[rows, g * SSD_N:(g + 1) * SSD_N]
            cg = c_ref[rows, g * SSD_N:(g + 1) * SSD_N]
            cb = lax.dot_general(cg, bg, (((1,), (1,)), ((), ())), preferred_element_type=F32)
            bg_t = jnp.transpose(bg.astype(F32)).astype(BF16)
            for j in range(2):
                tile = slice(g * 4 * SSD_P + j * 2 * SSD_P, g * 4 * SSD_P + (j + 1) * 2 * SSD_P)
                stile = slice(j * 2 * SSD_P, (j + 1) * 2 * SSD_P)
                ps = [SSD_HEADS + d * SSD_HEADS + g * 4 + j * 2 + k for k in (0, 1)]
                xt = x_ref[rows, tile]
                st = s_ref[g, :, stile]
                st_b = st.astype(BF16)
                intra, inter, to_state = [], [], []
                for p in ps:
                    cum_col = bcast(acs_sub[:, p:p + 1])
                    cum_row = acsr_ref[p:p + 1, rows]
                    dt_row = dtr_ref[p:p + 1, rows]
                    seg = jnp.where(mask, cum_col - (cum_row - jnp.log(dt_row)), -jnp.inf)
                    intra.append((cb * jnp.exp(seg)).astype(BF16))
                    inter.append(cg * bcast(from_start[:, p:p + 1]).astype(BF16))
                    w_row = dt_row * jnp.exp(tot_row[:, p:p + 1] - (cum_row - base_row[:, p:p + 1]))
                    to_state.append(bg_t * w_row.astype(BF16))
                x_pair = [jnp.where(lo, xt, zero_b), jnp.where(lo, zero_b, xt)]
                s_pair = [jnp.where(lo, st_b, zero_b), jnp.where(lo, zero_b, st_b)]
                lhs = jnp.concatenate(intra + inter, axis=1)
                y_out = _dot(lhs, jnp.concatenate(x_pair + s_pair, axis=0))
                if prev_ref is not None:
                    y_out = y_out + prev_ref[rows, tile].astype(F32)
                o_ref[rows, tile] = y_out.astype(o_ref.dtype)
                cd_sel = jnp.where(lo1, cd_row[:, ps[0]:ps[0] + 1], cd_row[:, ps[1]:ps[1] + 1])
                s_ref[g, :, stile] = cd_sel * st + _dot(jnp.concatenate(to_state, axis=1),
                                                        jnp.concatenate(x_pair, axis=0))


def _ssd_call(xbc, acsc, dtr, acsr, d, prev=None):
    blk = _scan_block(d)
    small = pl.BlockSpec((TB, SMALL), lambda b, s: (blk(b, s), 0))
    rows = pl.BlockSpec((None, 64, TB), lambda b, s: (blk(b, s), 0, 0))
    tok = pl.BlockSpec((TB, D), lambda b, s: (blk(b, s), 0))
    extra = () if d == 0 else (prev,)
    return pl.pallas_call(
        functools.partial(_ssd_kernel, d=d),
        grid=(BATCH, SEQ_BLOCKS),
        in_specs=[tok,
                  pl.BlockSpec((TB, 512), lambda b, s: (blk(b, s), 2)),
                  pl.BlockSpec((TB, 512), lambda b, s: (blk(b, s), 3)),
                  small, rows, rows] + [tok] * len(extra),
        out_specs=pl.BlockSpec((TB, D), lambda b, s: (blk(b, s), 0)),
        out_shape=jax.ShapeDtypeStruct((N_TOK, D), BF16),
        scratch_shapes=[pltpu.VMEM((SSD_GROUPS, SSD_N, 4 * SSD_P), F32)],
        compiler_params=_cparams(2),
        name="ssd_fwd" if d == 0 else "ssd_bwd",
    )(xbc, xbc, xbc, acsc, dtr, acsr, *extra)


def _pool_tables():
    masks = np.zeros((2, 4, TB, TB), np.float32)
    inv = np.zeros((2, 4, TB, 1), np.float32)
    for kind, length in ((0, GRID_W), (1, CTX)):
        for gi, w in enumerate(POOL_WINDOWS):
            for r in range(TB):
                base, c = (r // length) * length, r % length
                lo = min(max(c - w // 2, 0), length - 1)
                hi = min(max(c - w // 2 + w - 1, 0), length - 1)
                masks[kind, gi, r, base + lo:base + hi + 1] = 1.0
                inv[kind, gi, r, 0] = 1.0 / (hi - lo + 1)
    return jnp.asarray(masks, BF16), jnp.asarray(inv, F32)


ROUTE_FIELDS = 6


def _route_block(logits, ltri_ref, rt_ref, cnt_ref, cnt_scr):
    @pl.when(pl.program_id(0) == 0)
    def _():
        cnt_scr[...] = jnp.zeros_like(cnt_scr)

    lane = lax.broadcasted_iota(jnp.int32, logits.shape, 1).astype(F32)

    def top1(valid):
        v = jnp.max(jnp.where(valid, logits, -jnp.inf), axis=1, keepdims=True)
        i = jnp.min(jnp.where(valid & (logits == v), lane, float(SMALL)), axis=1, keepdims=True)
        return v, i

    is_g = lane < N_GROUPS_E
    gm, grp = top1(is_g)
    p_grp = 1.0 / jnp.sum(jnp.where(is_g, jnp.exp(logits - gm), 0.0), axis=1, keepdims=True)
    lo_lane = N_GROUPS_E + grp * EPG
    in_grp = (lane >= lo_lane) & (lane < lo_lane + EPG)
    v1, i1 = top1(in_grp)
    v2, i2 = top1(in_grp & (lane != i1))
    t = jnp.exp(v2 - v1)
    w1 = p_grp / (1.0 + t)
    w2 = p_grp * t / (1.0 + t)
    e1 = i1 - N_GROUPS_E
    e2 = i2 - N_GROUPS_E

    oh1 = lane == e1
    oh2 = lane == e2
    oh = jnp.where(oh1 | oh2, 1.0, 0.0)
    before = _dot(ltri_ref[...], oh.astype(BF16)) + cnt_scr[...]
    r1 = jnp.sum(jnp.where(oh1, before, 0.0), axis=1, keepdims=True)
    r2 = jnp.sum(jnp.where(oh2, before, 0.0), axis=1, keepdims=True)
    cnt_scr[...] = cnt_scr[...] + jnp.sum(oh, axis=0, keepdims=True)
    cnt_ref[...] = cnt_scr[...]
    rt = jnp.zeros(logits.shape, F32)
    for k, val in enumerate((e1, e2, w1, w2, r1, r2)):
        rt = jnp.where(lane == k, val, rt)
    rt_ref[...] = rt


def _merge_kernel(x_ref, pa_ref, o_ref, z_ref, mg0_ref, mg1_ref, mg2_ref, hs_ref, ys_ref, xs_ref,
                  pmask_ref, pinv_ref, poolw_ref, pscale_ref, mng_ref, dsk_ref, sng_ref, wbr_ref, wout_ref,
                  gate_ref, sh_ref, sc_ref, gffn_ref, wrh_ref, wrl_ref, brt_ref, ltri_ref,
                  xo_ref, h2_ref, rt_ref, cnt_ref, cnt_scr):
    parts = []
    for g in range(4):
        a_g = pa_ref[:, g * 256:(g + 1) * 256]
        pooled = _dot(pmask_ref[g], a_g) * pinv_ref[g] - a_g.astype(F32)
        parts.append(_dot(pooled.astype(BF16), poolw_ref[g]))
    pool = jnp.concatenate(parts, axis=1) * pscale_ref[...]

    hs = hs_ref[...].astype(F32)
    hn = jnp.concatenate([_rms(hs[:, h * DH_M:(h + 1) * DH_M]) for h in range(HEADS_M)], axis=1)
    ml = _sigmoid(o_ref[...].astype(F32)) * (hn * mng_ref[...])

    y = ys_ref[...].astype(F32) + dsk_ref[...] * xs_ref[...].astype(F32)
    sl = _rms(y * _silu(z_ref[...].astype(F32))) * sng_ref[...]

    acc = _sigmoid(mg0_ref[...].astype(F32)) * _dot(pool.astype(BF16), wbr_ref[0])
    acc = acc + _sigmoid(mg1_ref[...].astype(F32)) * _dot(ml.astype(BF16), wbr_ref[1])
    acc = acc + _sigmoid(mg2_ref[...].astype(F32)) * _dot(sl.astype(BF16), wbr_ref[2])
    xn = x_ref[...] + gate_ref[...] * _dot(acc.astype(BF16), wout_ref[...])
    xo_ref[...] = xn

    h2 = (_rms(xn) * gffn_ref[...]) * (1.0 + sc_ref[...]) + sh_ref[...]
    hi, lo = _split2(h2)
    h2f = hi.astype(F32)
    for cchunk in range(D // 128):
        h2_ref[pl.ds(cchunk, TB, stride=D // 128), :] = h2f[:, cchunk * 128:(cchunk + 1) * 128]
    logits = _dot(hi, wrh_ref[...]) + _dot(lo, wrh_ref[...]) + _dot(hi, wrl_ref[...]) + brt_ref[...]
    _route_block(logits, ltri_ref, rt_ref, cnt_ref, cnt_scr)


def _merge_call(x, u_po, u_tail, hs, ys, xbc, pmask, pinv, poolw, pscale, mng, dsk, sng, wbr, wout,
                mods3, gffn, wr_hi, wr_lo, br):
    ltri = jnp.asarray(np.tril(np.ones((TB, TB), np.float32), -1), BF16)
    kind = lambda j: jnp.where(j % SEQ_BLOCKS == 0, 1, 0)
    vec = _const_spec((1, D))
    return pl.pallas_call(
        _merge_kernel,
        grid=(N_BLOCKS,),
        in_specs=[_tok_spec(),
                  _tok_spec(D, 0), _tok_spec(D, 1), _tok_spec(D, 1),
                  _tok_spec(D, 3), _tok_spec(D, 4), _tok_spec(D, 5),
                  _tok_spec(), _tok_spec(),
                  _tok_spec(D, 0),
                  pl.BlockSpec((None, 4, TB, TB), lambda j: (kind(j), 0, 0, 0)),
                  pl.BlockSpec((None, 4, TB, 1), lambda j: (kind(j), 0, 0, 0)),
                  _const_spec((4, 256, 256)), vec, vec, vec, vec,
                  _const_spec((3, D, D)), _const_spec((D, D)),
                  _mod_spec(2), _mod_spec(3), _mod_spec(4), vec,
                  _const_spec((D, SMALL)), _const_spec((D, SMALL)), _const_spec((1, SMALL)),
                  _const_spec((TB, TB))],
        out_specs=[_tok_spec(), pl.BlockSpec((TB * ROW_TILES, 128), lambda j: (j, 0)), _tok_spec(SMALL),
                   _const_spec((1, SMALL))],
        out_shape=[jax.ShapeDtypeStruct((N_TOK, D), F32), jax.ShapeDtypeStruct((N_TOK * ROW_TILES, 128), F32),
                   jax.ShapeDtypeStruct((N_TOK, SMALL), F32), jax.ShapeDtypeStruct((1, SMALL), F32)],
        scratch_shapes=[pltpu.VMEM((1, SMALL), F32)],
        compiler_params=_cparams(1),
        name="branch_merge",
    )(x, u_po, u_po, u_tail, u_tail, u_tail, u_tail, hs, ys, xbc, pmask, pinv, poolw, pscale, mng, dsk, sng,
      wbr, wout, mods3, mods3, mods3, gffn, wr_hi, wr_lo, br, ltri)


GATHER_AHEAD = 2
GATHER_SLOTS = GATHER_AHEAD + 1
GATHER_DMA_QUEUES = 2


def _moe_kernel(be_ref, nu_ref, tok0_ref, tok1_ref, tokn_ref, h_hbm, wg_ref, wu_ref, wd_ref, o_ref,
                xbuf, wgb, wub, wdb, sem):
    i = pl.program_id(0)
    n_used = nu_ref[0]
    slot = i % GATHER_SLOTS

    def start_gather(tok_ref, s):
        for r in range(MOE_BLOCK):
            row0 = pl.multiple_of(tok_ref[0, r] * ROW_TILES, ROW_TILES)
            pltpu.make_async_copy(h_hbm.at[pl.ds(row0, ROW_TILES)],
                                  xbuf.at[s, pl.ds(r * ROW_TILES, ROW_TILES)],
                                  sem.at[s]).start(priority=r % GATHER_DMA_QUEUES)

    def wait_gather(s):
        pltpu.make_async_copy(xbuf.at[s], xbuf.at[s], sem.at[s]).wait()

    @pl.when((i == 0) & (n_used > 0))
    def _():
        start_gather(tok0_ref, 0)
        start_gather(tok1_ref, 1)

    @pl.when((i < n_used) & ((i == 0) | (be_ref[i] != be_ref[jnp.maximum(i - 1, 0)])))
    def _():
        wgb[...] = wg_ref[...].astype(BF16)
        wub[...] = wu_ref[...].astype(BF16)
        wdb[...] = wd_ref[...].astype(BF16)

    @pl.when(i < n_used)
    def _():
        wait_gather(slot)
        x = jnp.concatenate([xbuf[slot, pl.ds(c, MOE_BLOCK, stride=ROW_TILES), :] for c in range(ROW_TILES)],
                            axis=1).astype(BF16)
        start_gather(tokn_ref, (i + GATHER_AHEAD) % GATHER_SLOTS)
        gt = _dot(x, wgb[...])
        up = _dot(x, wub[...])
        act = (_silu(gt) * up).astype(BF16)
        o_ref[...] = _dot(act, wdb[...]).astype(o_ref.dtype)

    @pl.when(i == n_used - 1)
    def _():
        for ahead in range(1, GATHER_AHEAD + 1):
            wait_gather((i + ahead) % GATHER_SLOTS)

    @pl.when(i >= n_used)
    def _():
        o_ref[...] = jnp.zeros_like(o_ref)


MOE_CAP = N_TOK * 2 + N_EXPERTS * MOE_BLOCK
MOE_NBLOCKS = MOE_CAP // MOE_BLOCK


def _moe_call(layer, block_expert, n_used, buf_tok, h2, w_gate, w_up, w_down):
    grid_spec = pltpu.PrefetchScalarGridSpec(
        num_scalar_prefetch=2,
        grid=(MOE_NBLOCKS,),
        in_specs=[pl.BlockSpec((None, 1, MOE_BLOCK), lambda i, be, nu: (0, 0, 0), memory_space=pltpu.SMEM),
                  pl.BlockSpec((None, 1, MOE_BLOCK), lambda i, be, nu: (1, 0, 0), memory_space=pltpu.SMEM),
                  pl.BlockSpec((None, 1, MOE_BLOCK),
                               lambda i, be, nu: (jnp.minimum(i + GATHER_AHEAD, MOE_NBLOCKS - 1), 0, 0),
                               memory_space=pltpu.SMEM),
                  pl.BlockSpec(memory_space=pl.ANY),
                  pl.BlockSpec((None, None, D, D_EXPERT), lambda i, be, nu: (layer, be[i], 0, 0)),
                  pl.BlockSpec((None, None, D, D_EXPERT), lambda i, be, nu: (layer, be[i], 0, 0)),
                  pl.BlockSpec((None, None, D_EXPERT, D), lambda i, be, nu: (layer, be[i], 0, 0))],
        out_specs=pl.BlockSpec((MOE_BLOCK, D), lambda i, be, nu: (i, 0)),
        scratch_shapes=[pltpu.VMEM((GATHER_SLOTS, MOE_BLOCK * ROW_TILES, 128), F32),
                        pltpu.VMEM((D, D_EXPERT), BF16), pltpu.VMEM((D, D_EXPERT), BF16),
                        pltpu.VMEM((D_EXPERT, D), BF16), pltpu.SemaphoreType.DMA((GATHER_SLOTS,))],
    )
    return pl.pallas_call(
        _moe_kernel,
        grid_spec=grid_spec,
        out_shape=jax.ShapeDtypeStruct((MOE_CAP, D), BF16),
        compiler_params=_cparams(1),
        name="moe_experts",
    )(block_expert, n_used, buf_tok, buf_tok, buf_tok, h2, w_gate, w_up, w_down)


def _dispatch_plan(route, counts):
    cnt = counts[0, :N_EXPERTS].astype(jnp.int32)
    padded = (cnt + MOE_BLOCK - 1) // MOE_BLOCK * MOE_BLOCK
    pends = jnp.cumsum(padded)
    pstarts = pends - padded
    expert = route[:, 0:2].astype(jnp.int32)
    rank = route[:, 4:6].astype(jnp.int32)
    onehot = expert[:, :, None] == jnp.arange(N_EXPERTS, dtype=jnp.int32)
    pos = jnp.sum(jnp.where(onehot, pstarts, 0), axis=-1) + rank
    tok = jnp.broadcast_to(jnp.arange(N_TOK, dtype=jnp.int32)[:, None], (N_TOK, 2))
    buf_tok = jnp.zeros((MOE_CAP,), jnp.int32).at[pos.reshape(-1)].set(tok.reshape(-1))
    buf_tok = buf_tok.reshape(MOE_NBLOCKS, 1, MOE_BLOCK)
    block_start = jnp.arange(MOE_NBLOCKS, dtype=jnp.int32) * MOE_BLOCK
    block_expert = jnp.minimum(jnp.sum((pends[None, :] <= block_start[:, None]).astype(jnp.int32), axis=1),
                               N_EXPERTS - 1)
    n_used = (pends[-1] // MOE_BLOCK).reshape(1)
    return buf_tok, block_expert, n_used, pos


def _to_colmajor(t):
    c = t.shape[-1]
    t = t.reshape(BATCH, T_BATCH, c)
    lat = t[:, CTX:].reshape(BATCH, SEQ // GRID_W, GRID_W, c).swapaxes(1, 2).reshape(BATCH, SEQ, c)
    return jnp.concatenate([t[:, :CTX], lat], axis=1).reshape(N_TOK, c)


def _to_rowmajor(t):
    c = t.shape[-1]
    t = t.reshape(BATCH, T_BATCH, c)
    lat = t[:, CTX:].reshape(BATCH, GRID_W, SEQ // GRID_W, c).swapaxes(1, 2).reshape(BATCH, SEQ, c)
    return jnp.concatenate([t[:, :CTX], lat], axis=1).reshape(N_TOK, c)


def _lanes(*pieces):
    v = jnp.concatenate([jnp.asarray(p, F32).reshape(-1) for p in pieces])
    return jnp.pad(v, (0, SMALL - v.shape[0]))


def kernel(x, c, ctx, c_ctx, w_ada, b_ada, g_norm_mix, g_norm_ffn, w_in, pool_w, pool_scale, mlstm_conv_w,
           mlstm_conv_b, mlstm_gate_b, mlstm_norm_g, ssd_conv_w, ssd_conv_b, ssd_dt_bias, ssd_a_log, ssd_d,
           ssd_norm_g, w_branch, w_out, w_route_group, b_route_group, w_route_expert, b_route_expert,
           w_exp_gate, w_exp_up, w_exp_down, g_final):
    cvec = jnp.concatenate([c, c_ctx[None], jnp.zeros((3, D), F32)], axis=0)
    mods = _ada_table(cvec, w_ada, b_ada)
    xs = jnp.concatenate([ctx, x], axis=1).reshape(N_TOK, D)
    pmask, pinv = _pool_tables()
    row = lambda v: v.reshape(1, -1).astype(F32)
    w_in_t = jnp.swapaxes(w_in, 1, 2)

    def small_weight(layer):
        rows_ = jnp.concatenate([w_in_t[layer, 5 * D:5 * D + 16], w_in_t[layer, 8 * D + 16:8 * D + 48]], axis=0)
        return jnp.pad(rows_.T, ((0, 0), (0, SMALL - 48)))

    h = small = None
    out = None
    for l in range(DEPTH):
        mods3 = mods[l].reshape(8, 1, 6 * D)
        sp = np.cumsum([0, D, D, D, D, D, 16, D, D, 512, 512, 32, 3 * D])
        w_tail = jnp.concatenate([w_in_t[l, sp[6]:sp[10]], w_in_t[l, sp[11]:sp[12]]], axis=0).astype(BF16)
        w_small = small_weight(l)
        ws_hi = w_small.astype(BF16)
        ws_lo = (w_small - ws_hi.astype(F32)).astype(BF16)

        if l == 0:
            h, small = _norm_call(xs, row(g_norm_mix[l]), mods3, ws_hi, ws_lo)

        h_cm = _to_colmajor(h)
        small_cm = _to_colmajor(small)
        u_po = _mm_f32w_call(h, w_in_t, l, lambda j: 4 * j, 2, "in_proj_pool_o")
        u_tail = _mm_call(h, w_tail, "in_proj_rowmajor")
        u_cm = _mm_f32w_call(h_cm, w_in_t, l, lambda j: 1 + j, 3, "in_proj_colmajor")

        post_m = jnp.concatenate([jnp.ones((D,), F32), jnp.full((D,), DH_M ** -0.5, F32)]).reshape(1, 2 * D)
        qk = _conv_call(u_cm, lambda c: c, 2 * D, mlstm_conv_w[l], row(mlstm_conv_b[l]), post_m, "mlstm_conv")
        gate_bias = _lanes(mlstm_gate_b[l])
        gate_bwd = _lanes(jnp.zeros((12,)), jnp.ones((4,)))
        g_valc, g_cumc, g_valr, g_cumr = _gate_call(small_cm, gate_bias, jnp.ones((SMALL,), F32), gate_bwd,
                                                    False, "mlstm_gates")
        h_f = _mlstm_call(qk, u_cm, g_valc, g_cumc, g_valr, g_cumr, 0)
        h_fb = _mlstm_call(qk, u_cm, g_valc, g_cumc, g_valr, g_cumr, 1, prev=h_f)
        h_fb = _to_rowmajor(h_fb)

        xbc = _conv_call(u_tail, lambda c: c + jnp.where(c >= 2, 2, 0), 2 * D, ssd_conv_w[l],
                         row(ssd_conv_b[l]), jnp.ones((1, 2 * D), F32), "ssd_conv")
        a_neg = -jnp.exp(ssd_a_log[l].astype(F32))
        dt_bias = _lanes(jnp.zeros((16,)), ssd_dt_bias[l])
        dt_scale = _lanes(jnp.zeros((16,)), a_neg)
        dt_bwd = _lanes(jnp.zeros((32,)), jnp.ones((16,)))
        _, s_acsc, s_dtr, s_acsr = _gate_call(small, dt_bias, dt_scale, dt_bwd, True, "ssd_gates")
        y_f = _ssd_call(xbc, s_acsc, s_dtr, s_acsr, 0)
        y_fb = _ssd_call(xbc, s_acsc, s_dtr, s_acsr, 1, prev=y_f)

        w_r = jnp.pad(jnp.concatenate([w_route_group[l], w_route_expert[l]], axis=1),
                      ((0, 0), (0, SMALL - N_GROUPS_E - N_EXPERTS)))
        wr_hi = w_r.astype(BF16)
        wr_lo = (w_r - wr_hi.astype(F32)).astype(BF16)
        b_r = _lanes(b_route_group[l], b_route_expert[l]).reshape(1, SMALL)
        dsk = jnp.repeat(ssd_d[l].astype(F32), SSD_P).reshape(1, D)
        xs, h2, route, counts = _merge_call(
            xs, u_po, u_tail, h_fb, y_fb, xbc, pmask, pinv, pool_w[l].astype(BF16), row(pool_scale[l]),
            row(mlstm_norm_g[l]), dsk, row(ssd_norm_g[l]), w_branch[l].astype(BF16), w_out[l].astype(BF16),
            mods3, row(g_norm_ffn[l]), wr_hi, wr_lo, b_r)

        buf_tok, block_expert, n_used, pos = _dispatch_plan(route, counts)
        yb = _moe_call(l, block_expert, n_used, buf_tok, h2, w_exp_gate, w_exp_up, w_exp_down)
        y0 = yb[pos[:, 0]]
        y1 = yb[pos[:, 1]]

        if l + 1 < DEPTH:
            w_small_n = small_weight(l + 1)
            wsn_hi = w_small_n.astype(BF16)
            wsn_lo = (w_small_n - wsn_hi.astype(F32)).astype(BF16)
            mods3_n = mods[l + 1].reshape(8, 1, 6 * D)
            xs, h, small = _resnorm_call(xs, y0, y1, route, mods3, row(g_norm_mix[l + 1]), mods3_n,
                                         wsn_hi, wsn_lo)
        else:
            out = _final_call(xs, y0, y1, route, mods3, row(g_final))
    return out.reshape(BATCH, SEQ, D)
```
